```python
import math
import jax
import jax.numpy as jnp
from jax import lax
import numpy as np

D_MODEL = 1024
BATCH = 16
SEQ = 4096
DEPTH = 2
DEC_BATCH = 2
DEC_SEQ = 16384
PAST_LEN = 128

MIX_W = D_MODEL
N_MIXERS = 4
GROUP_W = MIX_W // N_MIXERS
N_DIR = 2
A_HEADS = 4
A_HD = GROUP_W // A_HEADS
A_CONV = 4
RG_C = 8.0
B_HEADS = 4
B_DV = GROUP_W // B_HEADS
B_DK = B_DV // 2
B_RANK = 16
GLA_TAU = 16.0
C_HEADS = 4
C_HD = GROUP_W // C_HEADS
CHUNK = 64
D_CH = GROUP_W
D_CONV = 3
HY_ORDER = 2
HY_EMB = 33
HY_HID = 64
HY_TARGET = 1e-2
HY_FAST = 0.3
HY_SLOW = 1.5
MOE_GROUPS = 4
MOE_PER_GROUP = 4
MOE_EXPERTS = MOE_GROUPS * MOE_PER_GROUP
MOE_TOPK = 2
D_EXPERT = D_MODEL // 2
D_IN = 11 * GROUP_W + 2 * B_HEADS * B_DK + 2 * B_RANK + 2 * N_DIR * C_HEADS
EPS = 1e-6
NEG_BIG = -1e30

kernel_name = 'hymba_style_rglru_gla_mlstm_hyena_hmoe_encoder'

F32 = jnp.float32


def _rmsnorm(x, g):
    xf = x.astype(F32)
    y = xf * lax.rsqrt(jnp.mean(xf * xf, axis=-1, keepdims=True) + EPS)
    return (y * g.astype(F32)).astype(x.dtype)


def _headnorm(h, g):
    B, L, H, Dh = h.shape
    y = h * lax.rsqrt(jnp.mean(h * h, axis=-1, keepdims=True) + EPS)
    return y.reshape(B, L, H * Dh) * g.astype(F32)


def _flip(t):
    return jnp.flip(t, axis=1)


def _split_cols(z):
    sizes = (GROUP_W, GROUP_W,
             B_HEADS * B_DK, B_HEADS * B_DK, GROUP_W, GROUP_W,
             B_RANK, B_RANK,
             GROUP_W, GROUP_W, GROUP_W, GROUP_W,
             2 * N_DIR * C_HEADS,
             3 * D_CH)
    offs = np.cumsum(np.array(sizes))[:-1].tolist()
    return jnp.split(z, offs, axis=-1)


def _dwconv(u, w, b, left):
    K = w.shape[0]
    L = u.shape[1]
    w = w.astype(F32)
    up = jnp.pad(u, ((0, 0), (left, K - 1 - left), (0, 0)))
    y = up[:, 0:L] * w[0]
    for j in range(1, K):
        y = y + up[:, j:j + L] * w[j]
    return y + b.astype(F32)


def _lin_comb(e1, e2):
    a1, b1 = e1
    a2, b2 = e2
    return a1 * a2, a2 * b1 + b2


def _rglru_scan(u, w_r, b_r, w_i, b_i, lam, reverse):
    B, L, _ = u.shape
    uh = u.reshape(B, L, A_HEADS, A_HD)
    r = jax.nn.sigmoid(jnp.einsum('blhd,hde->blhe', uh, w_r.astype(F32)) + b_r.astype(F32)).reshape(B, L, GROUP_W)
    i = jax.nn.sigmoid(jnp.einsum('blhd,hde->blhe', uh, w_i.astype(F32)) + b_i.astype(F32)).reshape(B, L, GROUP_W)
    log_a = -RG_C * r * jax.nn.softplus(-lam.astype(F32))
    a = jnp.exp(log_a)
    bterm = jnp.sqrt(-jnp.expm1(2.0 * log_a)) * (i * u)
    _, h = lax.associative_scan(_lin_comb, (a, bterm), reverse=reverse, axis=1)
    return h


def _mixer_rglru(xa, ga, conv_w, conv_b, gate_w, gate_b, lam):
    u = _dwconv(xa.astype(F32), conv_w, conv_b, A_CONV // 2)
    h_f = _rglru_scan(u, gate_w[0, 0], gate_b[0, 0], gate_w[0, 1], gate_b[0, 1], lam[0], False)
    h_b = _rglru_scan(u, gate_w[1, 0], gate_b[1, 0], gate_w[1, 1], gate_b[1, 1], lam[1], True)
    return (h_f + h_b) * jax.nn.gelu(ga.astype(F32))


def _gla_causal(q, k, v, log_alpha):
    B, L, H, DK = q.shape
    DV = v.shape[-1]
    N = L // CHUNK
    q = q.reshape(B, N, CHUNK, H, DK)
    k = k.reshape(B, N, CHUNK, H, DK)
    v = v.reshape(B, N, CHUNK, H, DV)
    bcum = jnp.cumsum(log_alpha.reshape(B, N, CHUNK, H, DK), axis=2)
    b_last = bcum[:, :, -1]
    q_in = q * jnp.exp(bcum)
    k_in = k * jnp.exp(-bcum)
    k_st = k * jnp.exp(b_last[:, :, None] - bcum)
    mask = jnp.tril(jnp.ones((CHUNK, CHUNK), dtype=bool))
    att = jnp.einsum('bnthd,bnshd->bnhts', q_in, k_in)
    att = jnp.where(mask, att, 0.0)
    o_intra = jnp.einsum('bnhts,bnshe->bnthe', att, v)
    du = jnp.einsum('bnshd,bnshe->bnhde', k_st, v)

    def step(S, inp):
        dec, inc = inp
        return dec[..., None] * S + inc, S

    S0 = jnp.zeros((B, H, DK, DV), F32)
    _, S_prev = lax.scan(step, S0, (jnp.moveaxis(jnp.exp(b_last), 1, 0), jnp.moveaxis(du, 1, 0)))
    S_prev = jnp.moveaxis(S_prev, 0, 1)
    o_inter = jnp.einsum('bnthd,bnhde->bnthe', q_in, S_prev)
    return (o_intra + o_inter).reshape(B, L, H, DV)


def _mixer_gla(q, k, v, r, lr_f, lr_b, gate_w, gate_b, norm_g):
    B, L, _ = q.shape
    qh = q.astype(F32).reshape(B, L, B_HEADS, B_DK) * (B_DK ** -0.5)
    kh = k.astype(F32).reshape(B, L, B_HEADS, B_DK)
    vh = v.astype(F32).reshape(B, L, B_HEADS, B_DV)

    def log_alpha(lr, w, b):
        pre = lr.astype(F32) @ w.astype(F32) + b.astype(F32)
        return (jax.nn.log_sigmoid(pre) / GLA_TAU).reshape(B, L, B_HEADS, B_DK)

    o_f = _gla_causal(qh, kh, vh, log_alpha(lr_f, gate_w[0], gate_b[0]))
    o_b = _flip(_gla_causal(_flip(qh), _flip(kh), _flip(vh), _flip(log_alpha(lr_b, gate_w[1], gate_b[1]))))
    return _headnorm(o_f + o_b, norm_g) * jax.nn.silu(r.astype(F32))


def _mlstm_causal(q, k, v, i_pre, f_pre):
    B, L, H, D = q.shape
    N = L // CHUNK
    q = q.reshape(B, N, CHUNK, H, D)
    k = k.reshape(B, N, CHUNK, H, D)
    v = v.reshape(B, N, CHUNK, H, D)
    ig = i_pre.reshape(B, N, CHUNK, H)
    bcum = jnp.cumsum(jax.nn.log_sigmoid(f_pre).reshape(B, N, CHUNK, H), axis=2)
    b_last = bcum[:, :, -1]
    mask = jnp.tril(jnp.ones((CHUNK, CHUNK), dtype=bool))
    dmat = bcum[:, :, :, None, :] - bcum[:, :, None, :, :] + ig[:, :, None, :, :]
    dmat = jnp.where(mask[None, None, :, :, None], dmat, NEG_BIG)
    g_end = b_last[:, :, None] - bcum + ig
    g_max = jnp.max(g_end, axis=2)
    w_end = jnp.exp(g_end - g_max[:, :, None])
    u_c = jnp.einsum('bnsh,bnshd,bnshe->bnhde', w_end, k, v)
    u_n = jnp.einsum('bnsh,bnshd->bnhd', w_end, k)

    def step(carry, inp):
        S, n, m = carry
        bl, gm, us, un = inp
        m_new = jnp.maximum(bl + m, gm)
        a = jnp.exp(bl + m - m_new)
        c = jnp.exp(gm - m_new)
        S_new = a[..., None, None] * S + c[..., None, None] * us
        n_new = a[..., None] * n + c[..., None] * un
        return (S_new, n_new, m_new), (S, n, m)

    init = (jnp.zeros((B, H, D, D), F32), jnp.zeros((B, H, D), F32), jnp.full((B, H), NEG_BIG, F32))
    xs = (jnp.moveaxis(b_last, 1, 0), jnp.moveaxis(g_max, 1, 0), jnp.moveaxis(u_c, 1, 0), jnp.moveaxis(u_n, 1, 0))
    _, (S_prev, n_prev, m_prev) = lax.scan(step, init, xs)
    S_prev = jnp.moveaxis(S_prev, 0, 1)
    n_prev = jnp.moveaxis(n_prev, 0, 1)
    m_prev = jnp.moveaxis(m_prev, 0, 1)
    inter_log = bcum + m_prev[:, :, None]
    m_t = jnp.maximum(inter_log, jnp.max(dmat, axis=3))
    w_intra = jnp.exp(dmat - m_t[:, :, :, None, :])
    w_inter = jnp.exp(inter_log - m_t)
    qk = jnp.einsum('bnthd,bnshd->bntsh', q, k) * w_intra
    num = jnp.einsum('bntsh,bnshe->bnthe', qk, v) + w_inter[..., None] * jnp.einsum('bnthd,bnhde->bnthe', q, S_prev)
    den = jnp.sum(qk, axis=3) + w_inter * jnp.einsum('bnthd,bnhd->bnth', q, n_prev)
    h = num / jnp.maximum(jnp.abs(den), jnp.exp(-m_t))[..., None]
    return h.reshape(B, L, H, D)


def _mixer_mlstm(q, k, v, o, g, gate_b, norm_g):
    B, L, _ = q.shape
    qh = q.astype(F32).reshape(B, L, C_HEADS, C_HD)
    kh = k.astype(F32).reshape(B, L, C_HEADS, C_HD) * (C_HD ** -0.5)
    vh = v.astype(F32).reshape(B, L, C_HEADS, C_HD)
    gates = g.astype(F32).reshape(B, L, 2 * N_DIR, C_HEADS) + gate_b.astype(F32)
    h_f = _mlstm_causal(qh, kh, vh, gates[:, :, 0], gates[:, :, 1])
    h_b = _flip(_mlstm_causal(_flip(qh), _flip(kh), _flip(vh), _flip(gates[:, :, 2]), _flip(gates[:, :, 3])))
    return _headnorm(h_f + h_b, norm_g) * jax.nn.sigmoid(o.astype(F32))


def _hyena_filter_spectrum(L, w1, b1, freq, w2, b2, w3):
    pos = jnp.arange(L, dtype=F32)
    t = pos / max(L - 1, 1)
    bands = (HY_EMB - 1) // 2
    f = jnp.linspace(1e-4, bands - 1, bands, dtype=F32)
    ang = (2.0 * math.pi / L) * pos[:, None] * f[None, :]
    emb = jnp.concatenate([t[:, None], jnp.cos(ang), -jnp.sin(ang)], axis=-1)
    freq = freq.astype(F32)
    h = jnp.sin(freq[0] * (emb @ w1.astype(F32) + b1.astype(F32)))
    h = jnp.sin(freq[1] * (h @ w2.astype(F32) + b2.astype(F32)))
    h = (h @ w3.astype(F32)).reshape(L, HY_ORDER, N_DIR, D_CH)
    deltas = jnp.abs(jnp.linspace(math.log(HY_TARGET) / HY_SLOW, math.log(HY_TARGET) / HY_FAST, D_CH, dtype=F32))
    h = h * jnp.exp(-t[:, None] * deltas[None, :])[:, None, None, :]
    hf = h[:, :, 0]
    hb = h[:, :, 1]
    circ = jnp.concatenate([hf, jnp.zeros_like(hf[:1]), jnp.flip(hb[1:], axis=0)], axis=0)
    circ = circ / jnp.sum(jnp.abs(circ), axis=0, keepdims=True)
    return jnp.fft.rfft(circ, axis=0)


def _fft_conv(u, spec, skip):
    L = u.shape[1]
    y = jnp.fft.irfft(jnp.fft.rfft(u, n=2 * L, axis=1) * spec[None], n=2 * L, axis=1)[:, :L]
    return y + u * skip.astype(F32)


def _mixer_hyena(d_u, conv_w, conv_b, w1, b1, freq, w2, b2, w3, skip):
    u = _dwconv(d_u.astype(F32), conv_w, conv_b, D_CONV // 2)
    v, x1, x2 = jnp.split(u, 3, axis=-1)
    spec = _hyena_filter_spectrum(u.shape[1], w1, b1, freq, w2, b2, w3)
    z = x1 * _fft_conv(v, spec[:, 0], skip[0])
    z = x2 * _fft_conv(z, spec[:, 1], skip[1])
    return z


def _hier_moe(h, wg, bg, we, be, w_gate, w_up, w_down):
    B, L, D = h.shape
    ht = h.reshape(B * L, D)
    hf = ht.astype(F32)
    g_logits = hf @ wg.astype(F32) + bg.astype(F32)
    g_prob = jax.nn.softmax(g_logits, axis=-1)
    g_idx = jnp.argmax(g_logits, axis=-1)
    e_logits = (hf @ we.astype(F32) + be.astype(F32)).reshape(-1, MOE_GROUPS, MOE_PER_GROUP)
    e_in = jnp.take_along_axis(e_logits, g_idx[:, None, None], axis=1)[:, 0]
    top_v, top_i = lax.top_k(e_in, MOE_TOPK)
    w = jax.nn.softmax(top_v, axis=-1) * jnp.take_along_axis(g_prob, g_idx[:, None], axis=1)
    eid = g_idx[:, None] * MOE_PER_GROUP + top_i
    gate = jnp.sum(jax.nn.one_hot(eid, MOE_EXPERTS, dtype=F32) * w[..., None], axis=1)
    y = jnp.zeros_like(ht)
    for e in range(MOE_EXPERTS):
        a = jax.nn.silu(ht @ w_gate[e]) * (ht @ w_up[e])
        y = y + gate[:, e:e + 1].astype(ht.dtype) * (a @ w_down[e])
    return y.reshape(B, L, D)


def _layer(x, l, P):
    dt = x.dtype
    z = _rmsnorm(x, P['norm1_g'][l]) @ P['w_in'][l]
    (a_x, a_g, b_q, b_k, b_v, b_r, b_lf, b_lb, c_q, c_k, c_v, c_o, c_g, d_u) = _split_cols(z)
    y_a = _mixer_rglru(a_x, a_g, P['a_conv_w'][l], P['a_conv_b'][l], P['a_gate_w'][l], P['a_gate_b'][l], P['a_lambda'][l])
    y_b = _mixer_gla(b_q, b_k, b_v, b_r, b_lf, b_lb, P['b_gate_w'][l], P['b_gate_b'][l], P['b_norm_g'][l])
    y_c = _mixer_mlstm(c_q, c_k, c_v, c_o, c_g, P['c_gate_b'][l], P['c_norm_g'][l])
    y_d = _mixer_hyena(d_u, P['d_conv_w'][l], P['d_conv_b'][l], P['d_ffn_w1'][l], P['d_ffn_b1'][l], P['d_sin_freq'][l],
                       P['d_ffn_w2'][l], P['d_ffn_b2'][l], P['d_ffn_w3'][l], P['d_skip'][l])
    mixed = jnp.concatenate([y_a, y_b, y_c, y_d], axis=-1).astype(dt)
    x = x + mixed @ P['w_out'][l]
    x = x + _hier_moe(_rmsnorm(x, P['norm2_g'][l]), P['moe_group_w'][l], P['moe_group_b'][l], P['moe_expert_w'][l],
                      P['moe_expert_b'][l], P['moe_w_gate'][l], P['moe_w_up'][l], P['moe_w_down'][l])
    return x


def _trunk(x, P):
    for l in range(DEPTH):
        x = _layer(x, l, P)
    return _rmsnorm(x, P['final_norm_g'])


def setup_inputs(seed: int = 0) -> dict:
    keys = iter(jax.random.split(jax.random.key(seed), 48))

    def nrm(shape, scale):
        return jax.random.normal(next(keys), shape, F32) * scale

    x_prompt = nrm((BATCH, SEQ, D_MODEL), 1.0)
    x_sample = nrm((DEC_BATCH, DEC_SEQ, D_MODEL), 1.0)
    norm1_g = 1.0 + nrm((DEPTH, D_MODEL), 0.02)
    w_in = nrm((DEPTH, D_MODEL, D_IN), D_MODEL ** -0.5)
    a_conv_w = nrm((DEPTH, A_CONV, GROUP_W), A_CONV ** -0.5)
    a_conv_b = nrm((DEPTH, GROUP_W), 0.02)
    a_gate_w = nrm((DEPTH, N_DIR, 2, A_HEADS, A_HD, A_HD), A_HD ** -0.5)
    a_gate_b = nrm((DEPTH, N_DIR, 2, A_HEADS, A_HD), 0.02)
    a0 = jax.random.uniform(next(keys), (DEPTH, N_DIR, GROUP_W), F32, 0.9, 0.999)
    p = a0 ** (1.0 / RG_C)
    a_lambda = jnp.log(p) - jnp.log1p(-p)
    b_gate_w = nrm((DEPTH, N_DIR, B_RANK, B_HEADS * B_DK), B_RANK ** -0.5)
    b_gate_b = nrm((DEPTH, N_DIR, B_HEADS * B_DK), 0.02)
    b_norm_g = 1.0 + nrm((DEPTH, GROUP_W), 0.02)
    i_bias = nrm((DEPTH, N_DIR, C_HEADS), 0.1)
    f_bias = jnp.linspace(3.0, 6.0, C_HEADS, dtype=F32) + nrm((DEPTH, N_DIR, C_HEADS), 0.1)
    c_gate_b = jnp.stack([i_bias[:, 0], f_bias[:, 0], i_bias[:, 1], f_bias[:, 1]], axis=1)
    c_norm_g = 1.0 + nrm((DEPTH, GROUP_W), 0.02)
    d_conv_w = nrm((DEPTH, D_CONV, 3 * D_CH), D_CONV ** -0.5)
    d_conv_b = nrm((DEPTH, 3 * D_CH), 0.02)
    d_ffn_w1 = nrm((DEPTH, HY_EMB, HY_HID), HY_EMB ** -0.5)
    d_ffn_b1 = nrm((DEPTH, HY_HID), 0.02)
    d_sin_freq = 1.0 + nrm((DEPTH, 2, HY_HID), 0.02)
    d_ffn_w2 = nrm((DEPTH, HY_HID, HY_HID), HY_HID ** -0.5)
    d_ffn_b2 = nrm((DEPTH, HY_HID), 0.02)
    d_ffn_w3 = nrm((DEPTH, HY_HID, HY_ORDER * N_DIR * D_CH), HY_HID ** -0.5)
    d_skip = nrm((DEPTH, HY_ORDER, D_CH), 0.5)
    w_out = nrm((DEPTH, MIX_W, D_MODEL), MIX_W ** -0.5)
    norm2_g = 1.0 + nrm((DEPTH, D_MODEL), 0.02)
    moe_group_w = nrm((DEPTH, D_MODEL, MOE_GROUPS), D_MODEL ** -0.5)
    moe_group_b = nrm((DEPTH, MOE_GROUPS), 0.01)
    moe_expert_w = nrm((DEPTH, D_MODEL, MOE_EXPERTS), D_MODEL ** -0.5)
    moe_expert_b = nrm((DEPTH, MOE_EXPERTS), 0.01)
    moe_w_gate = nrm((DEPTH, MOE_EXPERTS, D_MODEL, D_EXPERT), D_MODEL ** -0.5)
    moe_w_up = nrm((DEPTH, MOE_EXPERTS, D_MODEL, D_EXPERT), D_MODEL ** -0.5)
    moe_w_down = nrm((DEPTH, MOE_EXPERTS, D_EXPERT, D_MODEL), D_EXPERT ** -0.5)
    final_norm_g = 1.0 + nrm((D_MODEL,), 0.02)
    return {'x_prompt': x_prompt, 'x_sample': x_sample, 'norm1_g': norm1_g, 'w_in': w_in,
            'a_conv_w': a_conv_w, 'a_conv_b': a_conv_b, 'a_gate_w': a_gate_w, 'a_gate_b': a_gate_b,
            'a_lambda': a_lambda, 'b_gate_w': b_gate_w, 'b_gate_b': b_gate_b, 'b_norm_g': b_norm_g,
            'c_gate_b': c_gate_b, 'c_norm_g': c_norm_g, 'd_conv_w': d_conv_w, 'd_conv_b': d_conv_b,
            'd_ffn_w1': d_ffn_w1, 'd_ffn_b1': d_ffn_b1, 'd_sin_freq': d_sin_freq, 'd_ffn_w2': d_ffn_w2,
            'd_ffn_b2': d_ffn_b2, 'd_ffn_w3': d_ffn_w3, 'd_skip': d_skip, 'w_out': w_out, 'norm2_g': norm2_g,
            'moe_group_w': moe_group_w, 'moe_group_b': moe_group_b, 'moe_expert_w': moe_expert_w,
            'moe_expert_b': moe_expert_b, 'moe_w_gate': moe_w_gate, 'moe_w_up': moe_w_up,
            'moe_w_down': moe_w_down, 'final_norm_g': final_norm_g}


def reference(x_prompt, x_sample, norm1_g, w_in, a_conv_w, a_conv_b, a_gate_w, a_gate_b, a_lambda,
              b_gate_w, b_gate_b, b_norm_g, c_gate_b, c_norm_g, d_conv_w, d_conv_b, d_ffn_w1, d_ffn_b1,
              d_sin_freq, d_ffn_w2, d_ffn_b2, d_ffn_w3, d_skip, w_out, norm2_g, moe_group_w, moe_group_b,
              moe_expert_w, moe_expert_b, moe_w_gate, moe_w_up, moe_w_down, final_norm_g):
    P = {'norm1_g': norm1_g, 'w_in': w_in, 'a_conv_w': a_conv_w, 'a_conv_b': a_conv_b,
         'a_gate_w': a_gate_w, 'a_gate_b': a_gate_b, 'a_lambda': a_lambda, 'b_gate_w': b_gate_w,
         'b_gate_b': b_gate_b, 'b_norm_g': b_norm_g, 'c_gate_b': c_gate_b, 'c_norm_g': c_norm_g,
         'd_conv_w': d_conv_w, 'd_conv_b': d_conv_b, 'd_ffn_w1': d_ffn_w1, 'd_ffn_b1': d_ffn_b1,
         'd_sin_freq': d_sin_freq, 'd_ffn_w2': d_ffn_w2, 'd_ffn_b2': d_ffn_b2, 'd_ffn_w3': d_ffn_w3,
         'd_skip': d_skip, 'w_out': w_out, 'norm2_g': norm2_g, 'moe_group_w': moe_group_w,
         'moe_group_b': moe_group_b, 'moe_expert_w': moe_expert_w, 'moe_expert_b': moe_expert_b,
         'moe_w_gate': moe_w_gate, 'moe_w_up': moe_w_up, 'moe_w_down': moe_w_down,
         'final_norm_g': final_norm_g}
    y_prompt = _trunk(x_prompt, P)
    y_sample = _trunk(x_sample, P)
    return (y_prompt, y_sample)
```

```python
import functools
import math

import jax
import jax.numpy as jnp
import numpy as np
from jax import lax
from jax.experimental import pallas as pl
from jax.experimental.pallas import tpu as pltpu

F32 = jnp.float32
BF16 = jnp.bfloat16

D_MODEL = 1024
GROUP_W = 256
N_HEADS = 4
HEAD_W = GROUP_W // N_HEADS
B_DK = 32
B_RANK = 16
CHUNK = 64
RG_C = 8.0
GLA_TAU = 16.0
HY_EMB = 33
HY_HID = 64
HY_TARGET = 1e-2
HY_FAST = 0.3
HY_SLOW = 1.5
MOE_GROUPS = 4
MOE_PER_GROUP = 4
MOE_EXPERTS = 16
D_EXPERT = 512
EPS = 1e-6
NEG_BIG = -1e30

V7X_LANES = 128
V7X_SUBLANES = 8
VMEM_LIMIT = 56 * 1024 * 1024

ZA_W, ZB_W, ZC_W, ZD_W, ZS_W = 512, 768, 1024, 768, 128
Z_SPLITS = (ZA_W, ZB_W, ZC_W, ZD_W, ZS_W)
Z_TOTAL = sum(Z_SPLITS)


def _cparams(*sem):
    return pltpu.CompilerParams(dimension_semantics=sem, vmem_limit_bytes=VMEM_LIMIT)


def _dot(a, b):
    return jnp.dot(a, b, preferred_element_type=F32)


def _dot_nt(a, b):
    return lax.dot_general(a, b, (((1,), (1,)), ((), ())), preferred_element_type=F32)


def _dot_tn(a, b):
    return lax.dot_general(a, b, (((0,), (0,)), ((), ())), preferred_element_type=F32)


def _dot_split(m_bf16, x):
    hi = x.astype(BF16)
    lo = (x - hi.astype(F32)).astype(BF16)
    return _dot(m_bf16, hi) + _dot(m_bf16, lo)


def _split_dot(x, m_bf16):
    hi = x.astype(BF16)
    lo = (x - hi.astype(F32)).astype(BF16)
    return _dot(hi, m_bf16) + _dot(lo, m_bf16)


def _sigmoid(x):
    return 1.0 / (1.0 + jnp.exp(-x))


def _log_sigmoid(x):
    return jnp.minimum(x, 0.0) - jnp.log(1.0 + jnp.exp(-jnp.abs(x)))


def _iota(shape, dim):
    return lax.broadcasted_iota(jnp.int32, shape, dim)


def _in_proj_kernel(x_ref, g_ref, w_ref, za_ref, zb_ref, zc_ref, zd_ref, zs_ref):
    x = x_ref[...]
    ms = jnp.mean(x * x, axis=-1, keepdims=True)
    xn = (x * lax.rsqrt(ms + EPS) * g_ref[...]).astype(BF16)
    off = 0
    for ref, w in zip((za_ref, zb_ref, zc_ref, zd_ref, zs_ref), Z_SPLITS):
        ref[...] = _dot(xn, w_ref[:, off:off + w])
        off += w


def _in_proj(x2d, g, w_perm, tm=256):
    t = x2d.shape[0]
    outs = tuple(jax.ShapeDtypeStruct((t, w), F32) for w in Z_SPLITS)
    return pl.pallas_call(
        _in_proj_kernel,
        grid=(t // tm,),
        in_specs=[pl.BlockSpec((tm, D_MODEL), lambda i: (i, 0)),
                  pl.BlockSpec((1, D_MODEL), lambda i: (0, 0)),
                  pl.BlockSpec((D_MODEL, Z_TOTAL), lambda i: (0, 0))],
        out_specs=tuple(pl.BlockSpec((tm, w), lambda i: (i, 0)) for w in Z_SPLITS),
        out_shape=outs,
        compiler_params=_cparams("parallel"),
        name="in_proj",
    )(x2d, g.reshape(1, D_MODEL), w_perm)


def _permute_w_in(w_in):
    sizes = (256, 256, 128, 128, 256, 256, 16, 16, 256, 256, 256, 256, 16, 768)
    offs = np.concatenate([[0], np.cumsum(sizes)])
    seg = [w_in[:, offs[i]:offs[i + 1]] for i in range(len(sizes))]
    (a_x, a_g, b_q, b_k, b_v, b_r, b_lf, b_lb, c_q, c_k, c_v, c_o, c_g, d_u) = seg
    small = jnp.concatenate([b_lf, b_lb, c_g, jnp.zeros((w_in.shape[0], ZS_W - 48), w_in.dtype)], axis=1)
    return jnp.concatenate([a_x, a_g, b_q, b_k, b_v, b_r, c_q, c_k, c_v, c_o, d_u, small], axis=1).astype(BF16)


def _halo_conv(prev8, cur, next8, w, b, lo, first, last):
    tl = cur.shape[0]
    prev8 = jnp.where(first, 0.0, prev8)
    next8 = jnp.where(last, 0.0, next8)
    ext = jnp.concatenate([prev8, cur, next8], axis=0)
    n = tl + 16
    acc = None
    for j in range(w.shape[0]):
        o = lo + j
        sh = ext if o == 0 else pltpu.roll(ext, (-o) % n, 0)
        term = sh[8:8 + tl] * w[j:j + 1, :]
        acc = term if acc is None else acc + term
    return acc + b


def _rglru_kernel(prev_ref, cur_ref, next_ref, cw_ref, cb_ref, gw_ref, gb_ref, sp_ref, out_ref,
                  a_s, b_s, h_s, p_s, carry, *, tl, reverse):
    i = pl.program_id(1)
    nt = pl.num_programs(1)
    ti = (nt - 1 - i) if reverse else i
    u = _halo_conv(prev_ref[0], cur_ref[0], next_ref[0], cw_ref[...], cb_ref[...], -2, ti == 0, ti == nt - 1)
    gates = _sigmoid(_dot(u.astype(BF16), gw_ref[...]) + gb_ref[...])
    r = gates[:, :GROUP_W]
    ig = gates[:, GROUP_W:]
    log_a = (-RG_C) * r * sp_ref[...]
    a = jnp.exp(log_a)
    bt = jnp.sqrt(1.0 - a * a) * (ig * u)
    for k in range(2):
        a_s[k] = a[:, k * V7X_LANES:(k + 1) * V7X_LANES]
        b_s[k] = bt[:, k * V7X_LANES:(k + 1) * V7X_LANES]

    @pl.when(i == 0)
    def _():
        carry[...] = jnp.zeros_like(carry)

    m = tl // 8

    def body(jj, hp):
        j = (m - 1 - jj) if reverse else jj
        out = []
        for k in range(2):
            h, p = hp[2 * k], hp[2 * k + 1]
            av = a_s[k, pl.ds(j, 8, stride=m), :]
            bv = b_s[k, pl.ds(j, 8, stride=m), :]
            h = av * h + bv
            p = av * p
            h_s[k, pl.ds(j, 8, stride=m), :] = h
            p_s[k, pl.ds(j, 8, stride=m), :] = p
            out += [h, p]
        return tuple(out)

    z8 = jnp.zeros((8, V7X_LANES), F32)
    o8 = jnp.ones((8, V7X_LANES), F32)
    ends = lax.fori_loop(0, m, body, (z8, o8, z8, o8))
    for k in range(2):
        h_end, p_end = ends[2 * k], ends[2 * k + 1]
        c = carry[k]
        cin = [None] * 8
        for seg in (range(7, -1, -1) if reverse else range(8)):
            cin[seg] = c
            c = h_end[seg:seg + 1, :] + p_end[seg:seg + 1, :] * c
        carry[k] = c
        for seg in range(8):
            rows = slice(seg * m, (seg + 1) * m)
            out_ref[0, rows, k * V7X_LANES:(k + 1) * V7X_LANES] = h_s[k, rows, :] + p_s[k, rows, :] * cin[seg]


def _rglru(za, conv_w, conv_b, gate_w_bd, gate_b, softplus_neg_lam, reverse, tl=512):
    b, l, _ = za.shape
    nt = l // tl
    r8 = tl // 8
    nb8 = l // 8

    def tmap(bi, i):
        return (nt - 1 - i) if reverse else i

    kern = functools.partial(_rglru_kernel, tl=tl, reverse=reverse)
    return pl.pallas_call(
        kern,
        grid=(b, nt),
        in_specs=[
            pl.BlockSpec((1, 8, GROUP_W), lambda bi, i: (bi, jnp.maximum(tmap(bi, i) * r8 - 1, 0), 0)),
            pl.BlockSpec((1, tl, GROUP_W), lambda bi, i: (bi, tmap(bi, i), 0)),
            pl.BlockSpec((1, 8, GROUP_W), lambda bi, i: (bi, jnp.minimum((tmap(bi, i) + 1) * r8, nb8 - 1), 0)),
            pl.BlockSpec((4, GROUP_W), lambda bi, i: (0, 0)),
            pl.BlockSpec((1, GROUP_W), lambda bi, i: (0, 0)),
            pl.BlockSpec((GROUP_W, 2 * GROUP_W), lambda bi, i: (0, 0)),
            pl.BlockSpec((1, 2 * GROUP_W), lambda bi, i: (0, 0)),
            pl.BlockSpec((1, GROUP_W), lambda bi, i: (0, 0)),
        ],
        out_specs=pl.BlockSpec((1, tl, GROUP_W), lambda bi, i: (bi, tmap(bi, i), 0)),
        out_shape=jax.ShapeDtypeStruct((b, l, GROUP_W), F32),
        scratch_shapes=[pltpu.VMEM((2, tl, V7X_LANES), F32)] * 4 + [pltpu.VMEM((2, 1, V7X_LANES), F32)],
        compiler_params=_cparams("parallel", "arbitrary"),
        name="rglru_bwd" if reverse else "rglru_fwd",
    )(za, za, za, conv_w, conv_b, gate_w_bd, gate_b, softplus_neg_lam)


def _block_diag(blocks):
    h, di, do = blocks.shape
    eye = jnp.eye(h, dtype=blocks.dtype)
    return jnp.einsum('hde,hg->hdge', blocks, eye).reshape(h * di, h * do)


def _chunk_masks(reverse):
    r = np.arange(GROUP_W)
    t = np.arange(CHUNK)
    tri = (t[None, :] >= t[:, None]) if reverse else (t[None, :] <= t[:, None])
    cmask = np.tile(tri, (1, N_HEADS))
    bd = (r[:, None] // HEAD_W) == (r[None, :] // HEAD_W)
    kd = (r[:, None] // HEAD_W) == (np.arange(N_HEADS * B_DK)[None, :] // B_DK)
    i2 = np.tile(np.eye(CHUNK, dtype=bool), (1, N_HEADS))
    hm = np.zeros((8, GROUP_W), bool)
    hm[:N_HEADS] = np.arange(N_HEADS)[:, None] == (r[None, :] // HEAD_W)
    return dict(tri=jnp.asarray(tri, BF16), cmask=jnp.asarray(cmask, F32), bd=jnp.asarray(bd, BF16),
                bdf=jnp.asarray(bd, F32), kd=jnp.asarray(kd, BF16), kdf=jnp.asarray(kd, F32),
                i2=jnp.asarray(i2, F32), hm=jnp.asarray(hm, F32))


def _const_spec(arr):
    nd = arr.ndim
    return pl.BlockSpec(arr.shape, lambda *_: (0,) * nd)


def _gla_kernel(zb_ref, zs_ref, gw_ref, gb_ref, tri_ref, kd_ref, kdf_ref, bd_ref, cmask_ref, out_ref, st_ref,
                *, tl, reverse):
    @pl.when(pl.program_id(1) == 0)
    def _():
        st_ref[...] = jnp.zeros_like(st_ref)

    la = _log_sigmoid(_dot(zs_ref[0].astype(BF16), gw_ref[...]) + gb_ref[...]) * (1.0 / GLA_TAU)
    causal = cmask_ref[...] > 0.0
    nch = tl // CHUNK
    for c in (range(nch - 1, -1, -1) if reverse else range(nch)):
        rows = slice(c * CHUNK, (c + 1) * CHUNK)
        q = zb_ref[0, rows, 0:128] * (B_DK ** -0.5)
        k = zb_ref[0, rows, 128:256]
        vb = zb_ref[0, rows, 256:512].astype(BF16)
        la_c = la[rows]
        bcum = _dot_split(tri_ref[...], la_c)
        btot = jnp.sum(la_c, axis=0, keepdims=True)
        q_in = (q * jnp.exp(bcum)).astype(BF16)
        k_in = (k * jnp.exp(-bcum)).astype(BF16)
        k_st = (k * jnp.exp(btot - bcum)).astype(BF16)
        kexp = jnp.concatenate([k_in] * N_HEADS, axis=0) * kd_ref[...]
        att = jnp.where(causal, _dot_nt(q_in, kexp), 0.0)
        vbd = jnp.concatenate([vb] * N_HEADS, axis=0) * bd_ref[...]
        st = st_ref[...]
        out_ref[0, rows, :] = _dot(att.astype(BF16), vbd) + _dot_nt(q_in, st.astype(BF16))
        st_ref[...] = st * jnp.exp(btot) + _dot_tn(vb, k_st) * kdf_ref[...]


def _gla(zb, zs, gate_w_pad, gate_b, reverse, tl=512):
    b, l, _ = zb.shape
    nt = l // tl
    mk = _chunk_masks(reverse)
    consts = (gate_w_pad, gate_b, mk['tri'], mk['kd'], mk['kdf'], mk['bd'], mk['cmask'])

    def tmap(bi, i):
        return (bi, (nt - 1 - i) if reverse else i, 0)

    return pl.pallas_call(
        functools.partial(_gla_kernel, tl=tl, reverse=reverse),
        grid=(b, nt),
        in_specs=[pl.BlockSpec((1, tl, ZB_W), tmap), pl.BlockSpec((1, tl, ZS_W), tmap)]
        + [_const_spec(c) for c in consts],
        out_specs=pl.BlockSpec((1, tl, GROUP_W), tmap),
        out_shape=jax.ShapeDtypeStruct((b, l, GROUP_W), F32),
        scratch_shapes=[pltpu.VMEM((GROUP_W, N_HEADS * B_DK), F32)],
        compiler_params=_cparams("parallel", "arbitrary"),
        name="gla_bwd" if reverse else "gla_fwd",
    )(zb, zs, *consts)


def _mlstm_kernel(zc_ref, zs_ref, e_ref, eb_ref, tri_ref, i2_ref, bd_ref, bdf_ref, cmask_ref, hm_ref, out_ref,
                  s_ref, n_ref, m_ref, *, tl, reverse):
    @pl.when(pl.program_id(1) == 0)
    def _():
        s_ref[...] = jnp.zeros_like(s_ref)
        n_ref[...] = jnp.zeros_like(n_ref)
        m_ref[...] = jnp.full_like(m_ref, NEG_BIG)

    ge = _split_dot(zs_ref[0], e_ref[...]) + eb_ref[...]
    ig_all = ge[:, :GROUP_W]
    lf_all = _log_sigmoid(ge[:, GROUP_W:])
    causal = cmask_ref[...] > 0.0
    ones_b = jnp.ones((CHUNK, GROUP_W), BF16)
    nch = tl // CHUNK
    for c in (range(nch - 1, -1, -1) if reverse else range(nch)):
        rows = slice(c * CHUNK, (c + 1) * CHUNK)
        qb = zc_ref[0, rows, 0:256].astype(BF16)
        k = zc_ref[0, rows, 256:512] * (HEAD_W ** -0.5)
        kb = k.astype(BF16)
        vb = zc_ref[0, rows, 512:768].astype(BF16)
        ig = ig_all[rows]
        lf = lf_all[rows]
        bcum = _dot_split(tri_ref[...], lf)
        blast = jnp.sum(lf, axis=0, keepdims=True)
        cc = jnp.sum((ig - bcum) * i2_ref[...], axis=0, keepdims=True)
        d2 = jnp.where(causal, bcum + cc, NEG_BIG)
        m_intra = jnp.full((CHUNK, GROUP_W), NEG_BIG, F32)
        for h in range(N_HEADS):
            hsel = hm_ref[h:h + 1, :] > 0.0
            mh = jnp.max(jnp.where(hsel, d2, NEG_BIG), axis=1, keepdims=True)
            m_intra = jnp.where(hsel, mh, m_intra)
        m_prev = m_ref[...]
        inter_log = bcum + m_prev
        m_t = jnp.maximum(inter_log, m_intra)
        w_intra = jnp.exp(d2 - m_t)
        w_inter = jnp.exp(inter_log - m_t)
        kexp = jnp.concatenate([kb] * N_HEADS, axis=0) * bd_ref[...]
        qk = (_dot_nt(qb, kexp) * w_intra).astype(BF16)
        vbd = jnp.concatenate([vb] * N_HEADS, axis=0) * bd_ref[...]
        s_prev = s_ref[...]
        n_prev = n_ref[...]
        num = _dot(qk, vbd) + w_inter * _dot(qb, s_prev.astype(BF16))
        den = _dot(qk, bd_ref[...]) + w_inter * _dot(qb, n_prev.astype(BF16))
        out_ref[0, rows, :] = num / jnp.maximum(jnp.abs(den), jnp.exp(-m_t))
        g_end = blast - bcum + ig
        g_max = jnp.max(g_end, axis=0, keepdims=True)
        kw = (k * jnp.exp(g_end - g_max)).astype(BF16)
        m_new = jnp.maximum(blast + m_prev, g_max)
        a = jnp.exp(blast + m_prev - m_new)
        cf = jnp.exp(g_max - m_new)
        s_ref[...] = a * s_prev + cf * (_dot_tn(kw, vb) * bdf_ref[...])
        n_ref[...] = a * n_prev + cf * (_dot_tn(kw, ones_b) * bdf_ref[...])
        m_ref[...] = m_new


def _mlstm(zc, zs, expand_w, expand_b, reverse, tl=512):
    b, l, _ = zc.shape
    nt = l // tl
    mk = _chunk_masks(reverse)
    consts = (expand_w, expand_b, mk['tri'], mk['i2'], mk['bd'], mk['bdf'], mk['cmask'], mk['hm'])

    def tmap(bi, i):
        return (bi, (nt - 1 - i) if reverse else i, 0)

    return pl.pallas_call(
        functools.partial(_mlstm_kernel, tl=tl, reverse=reverse),
        grid=(b, nt),
        in_specs=[pl.BlockSpec((1, tl, ZC_W), tmap), pl.BlockSpec((1, tl, ZS_W), tmap)]
        + [_const_spec(c) for c in consts],
        out_specs=pl.BlockSpec((1, tl, GROUP_W), tmap),
        out_shape=jax.ShapeDtypeStruct((b, l, GROUP_W), F32),
        scratch_shapes=[pltpu.VMEM((GROUP_W, GROUP_W), F32), pltpu.VMEM((GROUP_W, GROUP_W), F32),
                        pltpu.VMEM((1, GROUP_W), F32)],
        compiler_params=_cparams("parallel", "arbitrary"),
        name="mlstm_bwd" if reverse else "mlstm_fwd",
    )(zc, zs, *consts)


def _mixer_prep(P, l):
    out = {}
    gw, gb, lam = P['a_gate_w'][l], P['a_gate_b'][l], P['a_lambda'][l]
    for d in range(2):
        w = jnp.concatenate([_block_diag(gw[d, 0]), _block_diag(gw[d, 1])], axis=1).astype(BF16)
        bb = jnp.concatenate([gb[d, 0].reshape(1, -1), gb[d, 1].reshape(1, -1)], axis=1).astype(F32)
        nl = -lam[d].astype(F32)
        sp = (jnp.maximum(nl, 0.0) + jnp.log(1.0 + jnp.exp(-jnp.abs(nl)))).reshape(1, -1)
        out['a', d] = (w, bb, sp)
        bw = jnp.zeros((ZS_W, N_HEADS * B_DK), F32).at[B_RANK * d:B_RANK * (d + 1)].set(P['b_gate_w'][l][d])
        out['b', d] = (bw.astype(BF16), P['b_gate_b'][l][d].reshape(1, -1).astype(F32))
        e = np.zeros((ZS_W, 2 * GROUP_W), np.float32)
        for h in range(N_HEADS):
            e[32 + 8 * d + h, h * HEAD_W:(h + 1) * HEAD_W] = 1.0
            e[36 + 8 * d + h, GROUP_W + h * HEAD_W:GROUP_W + (h + 1) * HEAD_W] = 1.0
        cb = P['c_gate_b'][l].astype(F32)
        eb = jnp.concatenate([jnp.repeat(cb[2 * d], HEAD_W), jnp.repeat(cb[2 * d + 1], HEAD_W)]).reshape(1, -1)
        out['c', d] = (jnp.asarray(e, BF16), eb)
    return out


def _conv3_kernel(prev_ref, cur_ref, next_ref, w_ref, b_ref, v_ref, x1_ref, x2_ref):
    i = pl.program_id(1)
    u = _halo_conv(prev_ref[0], cur_ref[0], next_ref[0], w_ref[...], b_ref[...], -1, i == 0,
                   i == pl.num_programs(1) - 1)
    v_ref[0] = u[:, 0:GROUP_W]
    x1_ref[0] = u[:, GROUP_W:2 * GROUP_W]
    x2_ref[0] = u[:, 2 * GROUP_W:3 * GROUP_W]


def _conv3_split(zd, w, b, tl=512):
    bsz, l, _ = zd.shape
    nt, r8, nb8 = l // tl, tl // 8, l // 8
    o = jax.ShapeDtypeStruct((bsz, l, GROUP_W), F32)
    ospec = pl.BlockSpec((1, tl, GROUP_W), lambda bi, i: (bi, i, 0))
    return pl.pallas_call(
        _conv3_kernel,
        grid=(bsz, nt),
        in_specs=[pl.BlockSpec((1, 8, ZD_W), lambda bi, i: (bi, jnp.maximum(i * r8 - 1, 0), 0)),
                  pl.BlockSpec((1, tl, ZD_W), lambda bi, i: (bi, i, 0)),
                  pl.BlockSpec((1, 8, ZD_W), lambda bi, i: (bi, jnp.minimum((i + 1) * r8, nb8 - 1), 0)),
                  pl.BlockSpec((3, ZD_W), lambda bi, i: (0, 0)),
                  pl.BlockSpec((1, ZD_W), lambda bi, i: (0, 0))],
        out_specs=(ospec, ospec, ospec),
        out_shape=(o, o, o),
        compiler_params=_cparams("parallel", "parallel"),
        name="hyena_conv3",
    )(zd, zd, zd, w, b.reshape(1, ZD_W))


def _dot_hp(a, b):
    return jnp.dot(a, b, preferred_element_type=F32, precision=lax.Precision.HIGHEST)


def _filter_mlp_kernel(emb_ref, w1_ref, b1_ref, f0_ref, w2_ref, b2_ref, f1_ref, w3_ref, dl_ref, h_ref, s_ref):
    i = pl.program_id(0)
    emb = emb_ref[...]
    h = jnp.sin(f0_ref[...] * (_dot_hp(emb, w1_ref[...]) + b1_ref[...]))
    h = jnp.sin(f1_ref[...] * (_dot_hp(h, w2_ref[...]) + b2_ref[...]))
    h = _dot_hp(h, w3_ref[...]) * jnp.exp(-emb[:, 0:1] * dl_ref[...])
    h_ref[...] = h
    row = _iota(h.shape, 0) + i * h.shape[0]
    bwd_lane = (_iota(h.shape, 1) // GROUP_W) % 2 == 1
    part = jnp.sum(jnp.where(bwd_lane & (row == 0), 0.0, jnp.abs(h)), axis=0, keepdims=True)

    @pl.when(i == 0)
    def _():
        s_ref[...] = jnp.zeros_like(s_ref)

    s_ref[...] += part


def _hyena_filter(l, w1, b1, freq, w2, b2, w3, tl=512):
    pos = jnp.arange(l, dtype=F32)
    t = pos / max(l - 1, 1)
    bands = (HY_EMB - 1) // 2
    f = jnp.linspace(1e-4, bands - 1, bands, dtype=F32)
    ang = (2.0 * math.pi / l) * pos[:, None] * f[None, :]
    emb = jnp.concatenate([t[:, None], jnp.cos(ang), -jnp.sin(ang),
                           jnp.zeros((l, V7X_LANES - HY_EMB), F32)], axis=-1)
    deltas = jnp.abs(jnp.linspace(math.log(HY_TARGET) / HY_SLOW, math.log(HY_TARGET) / HY_FAST, GROUP_W, dtype=F32))
    pad = V7X_LANES - HY_HID
    w1p = jnp.zeros((V7X_LANES, V7X_LANES), F32).at[:HY_EMB, :HY_HID].set(w1.astype(F32))
    w2p = jnp.zeros((V7X_LANES, V7X_LANES), F32).at[:HY_HID, :HY_HID].set(w2.astype(F32))
    w3p = jnp.zeros((V7X_LANES, 4 * GROUP_W), F32).at[:HY_HID].set(w3.astype(F32))
    vec = lambda x: jnp.pad(x.astype(F32), (0, pad)).reshape(1, V7X_LANES)
    consts = (w1p, vec(b1), vec(freq[0]), w2p, vec(b2), vec(freq[1]), w3p, jnp.tile(deltas, 4).reshape(1, -1))
    return pl.pallas_call(
        _filter_mlp_kernel,
        grid=(l // tl,),
        in_specs=[pl.BlockSpec((tl, V7X_LANES), lambda i: (i, 0))] + [_const_spec(c) for c in consts],
        out_specs=(pl.BlockSpec((tl, 4 * GROUP_W), lambda i: (i, 0)), pl.BlockSpec((1, 4 * GROUP_W), lambda i: (0, 0))),
        out_shape=(jax.ShapeDtypeStruct((l, 4 * GROUP_W), F32), jax.ShapeDtypeStruct((1, 4 * GROUP_W), F32)),
        compiler_params=_cparams("arbitrary"),
        name="hyena_filter_mlp",
    )(emb, *consts)


def _fft_factors(l):
    return (64, 128) if l == 4096 else (2 * l // 256, 256)


@functools.lru_cache(maxsize=None)
def _fft_tables(n1, n2):
    n = n1 * n2
    k1 = np.arange(n1)[None, :, None]
    m2 = np.arange(n2)[:, None, None]

    def theta(n1_vals):
        idx = (k1 * (n1_vals[None, None, :] * n2 + m2)) % n
        return (2.0 * np.pi / n) * idx
    th = theta(np.arange(n1 // 2))
    fr, fi = np.cos(th), -np.sin(th)
    fwd = np.concatenate([np.concatenate([fr, -fi], -1), np.concatenate([fi, fr], -1)], -2)
    thf = theta(np.arange(n1))
    filt = np.concatenate([np.cos(thf), -np.sin(thf)], -2)
    cr = np.swapaxes(np.cos(th), 1, 2) / n
    ci = np.swapaxes(np.sin(th), 1, 2) / n
    inv = np.concatenate([np.concatenate([cr, -ci], -1), np.concatenate([ci, cr], -1)], -2)
    t2 = (2.0 * np.pi / n2) * ((np.arange(n2)[:, None] * np.arange(n2)[None, :]) % n2)
    gr, gi = np.cos(t2), -np.sin(t2)
    g = np.block([[gr, -gi], [gi, gr]])
    ginv = np.block([[gr, gi], [-gi, gr]])
    return tuple(jnp.asarray(a, BF16) for a in (fwd, filt, inv, g, ginv))


def _fft_stage1_kernel(zr_ref, zi_ref, t_ref, ar_ref, ai_ref, *, nb, n1):
    for j in range(nb):
        cols = slice(j * GROUP_W, (j + 1) * GROUP_W)
        rhs = jnp.concatenate([zr_ref[0, :, cols], zi_ref[0, :, cols]], axis=0).astype(BF16)
        o = _dot(t_ref[j], rhs)
        ar_ref[0, :, cols] = o[:n1].astype(BF16)
        ai_ref[0, :, cols] = o[n1:].astype(BF16)


def _fft_stage1(z, table, n1, n2, pairs, idx_r, idx_i, nb=8):
    o = jax.ShapeDtypeStruct((pairs, n1, n2 * GROUP_W), BF16)
    ospec = pl.BlockSpec((1, n1, nb * GROUP_W), lambda p, j: (p, 0, j))
    return pl.pallas_call(
        functools.partial(_fft_stage1_kernel, nb=nb, n1=n1),
        grid=(pairs, n2 // nb),
        in_specs=[pl.BlockSpec((1, n1 // 2, nb * GROUP_W), lambda p, j: (idx_r(p), 0, j)),
                  pl.BlockSpec((1, n1 // 2, nb * GROUP_W), lambda p, j: (idx_i(p), 0, j)),
                  pl.BlockSpec((nb, 2 * n1, n1), lambda p, j: (j, 0, 0))],
        out_specs=(ospec, ospec),
        out_shape=(o, o),
        compiler_params=_cparams("parallel", "parallel"),
        name="fft_stage1",
    )(z, z, table)


def _fft_spec_kernel(ar_ref, ai_ref, g_ref, s_ref, h_ref, *, kb, n2):
    for k in range(kb):
        rhs = jnp.concatenate([ar_ref[0, k], ai_ref[0, k]], axis=0)
        x = _dot(g_ref[...], rhs) * s_ref[0]
        h_ref[0, k, 0] = x[:n2]
        h_ref[0, k, 1] = x[n2:]


def _fft_filter_spectrum(ar, ai, g, inv_s, n1, n2, kb=8):
    orders = ar.shape[0]
    a4 = lambda a: a.reshape(orders, n1, n2, GROUP_W)
    aspec = pl.BlockSpec((1, kb, n2, GROUP_W), lambda o, i: (o, i, 0, 0))
    return pl.pallas_call(
        functools.partial(_fft_spec_kernel, kb=kb, n2=n2),
        grid=(orders, n1 // kb),
        in_specs=[aspec, aspec, _const_spec(g), pl.BlockSpec((1, 1, GROUP_W), lambda o, i: (o, 0, 0))],
        out_specs=pl.BlockSpec((1, kb, 2, n2, GROUP_W), lambda o, i: (o, i, 0, 0, 0)),
        out_shape=jax.ShapeDtypeStruct((orders, n1, 2, n2, GROUP_W), F32),
        compiler_params=_cparams("parallel", "parallel"),
        name="fft_filter_spectrum",
    )(a4(ar), a4(ai), g, inv_s)


def _fft_mid_kernel(ar_ref, ai_ref, g_ref, gi_ref, h_ref, br_ref, bi_ref, *, kb, n2):
    for k in range(kb):
        rhs = jnp.concatenate([ar_ref[0, k], ai_ref[0, k]], axis=0)
        x = _dot(g_ref[...], rhs)
        xr, xi = x[:n2], x[n2:]
        hr, hi = h_ref[0, k, 0], h_ref[0, k, 1]
        y = jnp.concatenate([xr * hr - xi * hi, xr * hi + xi * hr], axis=0).astype(BF16)
        z = _dot(gi_ref[...], y)
        br_ref[0, k] = z[:n2].astype(BF16)
        bi_ref[0, k] = z[n2:].astype(BF16)


def _fft_mid(ar, ai, g, ginv, spec, order, n1, n2, kb=8):
    pairs = ar.shape[0]
    a4 = lambda a: a.reshape(pairs, n1, n2, GROUP_W)
    aspec = pl.BlockSpec((1, kb, n2, GROUP_W), lambda i, p: (p, i, 0, 0))
    o = jax.ShapeDtypeStruct((pairs, n1, n2, GROUP_W), BF16)
    br, bi = pl.pallas_call(
        functools.partial(_fft_mid_kernel, kb=kb, n2=n2),
        grid=(n1 // kb, pairs),
        in_specs=[aspec, aspec, _const_spec(g), _const_spec(ginv),
                  pl.BlockSpec((1, kb, 2, n2, GROUP_W), lambda i, p: (order, i, 0, 0, 0))],
        out_specs=(aspec, aspec),
        out_shape=(o, o),
        compiler_params=_cparams("parallel", "parallel"),
        name="fft_mid",
    )(a4(ar), a4(ai), g, ginv, spec)
    return br.reshape(pairs, n1, n2 * GROUP_W), bi.reshape(pairs, n1, n2 * GROUP_W)


def _fft_stage3_kernel(br_ref, bi_ref, t_ref, ur_ref, ui_ref, gr_ref, gi_ref, sk_ref, y_ref, *, nb, n1):
    h = n1 // 2
    for j in range(nb):
        cols = slice(j * GROUP_W, (j + 1) * GROUP_W)
        rhs = jnp.concatenate([br_ref[0, :, cols], bi_ref[0, :, cols]], axis=0)
        o = _dot(t_ref[j], rhs)
        sk = sk_ref[...]
        y_ref[0, 0, :, cols] = gr_ref[0, :, cols] * (o[:h] + ur_ref[0, :, cols] * sk)
        y_ref[0, 1, :, cols] = gi_ref[0, :, cols] * (o[h:] + ui_ref[0, :, cols] * sk)


def _fft_stage3(br, bi, table, u, gate, skip, n1, n2, nb=8):
    pairs = br.shape[0]
    bspec = pl.BlockSpec((1, n1, nb * GROUP_W), lambda p, j: (p, 0, j))
    even = pl.BlockSpec((1, n1 // 2, nb * GROUP_W), lambda p, j: (2 * p, 0, j))
    odd = pl.BlockSpec((1, n1 // 2, nb * GROUP_W), lambda p, j: (2 * p + 1, 0, j))
    y = pl.pallas_call(
        functools.partial(_fft_stage3_kernel, nb=nb, n1=n1),
        grid=(pairs, n2 // nb),
        in_specs=[bspec, bspec, pl.BlockSpec((nb, n1, 2 * n1), lambda p, j: (j, 0, 0)),
                  even, odd, even, odd, pl.BlockSpec((1, GROUP_W), lambda p, j: (0, 0))],
        out_specs=pl.BlockSpec((1, 2, n1 // 2, nb * GROUP_W), lambda p, j: (p, 0, 0, j)),
        out_shape=jax.ShapeDtypeStruct((pairs, 2, n1 // 2, n2 * GROUP_W), F32),
        compiler_params=_cparams("parallel", "parallel"),
        name="fft_stage3",
    )(br, bi, table, u, u, gate, gate, skip.reshape(1, GROUP_W).astype(F32))
    return y.reshape(2 * pairs, n1 // 2, n2 * GROUP_W)


def _hyena_spectrum(l, w1, b1, freq, w2, b2, w3):
    n1, n2 = _fft_factors(l)
    _, t_filt, _, g, _ = _fft_tables(n1, n2)
    hf, sums = _hyena_filter(l, w1, b1, freq, w2, b2, w3)
    hf = hf.reshape(l, 2, 2, GROUP_W)
    fwd = jnp.moveaxis(hf[:, :, 0], 1, 0)
    bwd = jnp.moveaxis(hf[:, :, 1], 1, 0)
    second = jnp.concatenate([jnp.zeros_like(bwd[:, :1]), jnp.flip(bwd[:, 1:], axis=1)], axis=1)
    z = jnp.concatenate([fwd, second], axis=0).reshape(4, n1 // 2, n2 * GROUP_W)
    ar, ai = _fft_stage1(z, t_filt, n1, n2, 2, lambda p: p, lambda p: p + 2)
    s = sums.reshape(2, 2, GROUP_W)
    inv_s = (1.0 / (s[:, 0] + s[:, 1])).reshape(2, 1, GROUP_W)
    return _fft_filter_spectrum(ar, ai, g, inv_s, n1, n2)


def _hyena_long_conv(u, gate, spec, order, skip):
    b, l, _ = u.shape
    n1, n2 = _fft_factors(l)
    t_fwd, _, t_inv, g, ginv = _fft_tables(n1, n2)
    u2 = u.reshape(b, n1 // 2, n2 * GROUP_W)
    ar, ai = _fft_stage1(u2, t_fwd, n1, n2, b // 2, lambda p: 2 * p, lambda p: 2 * p + 1)
    br, bi = _fft_mid(ar, ai, g, ginv, spec, order, n1, n2)
    y = _fft_stage3(br, bi, t_inv, u2, gate.reshape(b, n1 // 2, n2 * GROUP_W), skip, n1, n2)
    return y.reshape(b, l, GROUP_W)


def _mixer_hyena(zd, P, l, spec):
    v, x1, x2 = _conv3_split(zd, P['d_conv_w'][l].astype(F32), P['d_conv_b'][l].astype(F32))
    z = _hyena_long_conv(v, x1, spec, 0, P['d_skip'][l][0])
    return _hyena_long_conv(z, x2, spec, 1, P['d_skip'][l][1])


def _gelu_tanh(x):
    return 0.5 * x * (1.0 + jnp.tanh(math.sqrt(2.0 / math.pi) * (x + 0.044715 * (x * x * x))))


def _head_rms(h, bd):
    ss = _dot((h * h).astype(BF16), bd) * (1.0 / HEAD_W)
    return h * lax.rsqrt(ss + EPS)


def _out_proj_kernel(x_ref, af_ref, ab_ref, ag_ref, bf_ref, bb_ref, br_ref, cf_ref, cb_ref, co_ref, yd_ref,
                     w_ref, bg_ref, cg_ref, bd_ref, n2_ref, rwh_ref, rwl_ref, rb_ref,
                     xo_ref, hn_ref, lg_ref):
    bd = bd_ref[...]
    ya = (af_ref[...] + ab_ref[...]) * _gelu_tanh(ag_ref[...])
    r = br_ref[...]
    yb = _head_rms(bf_ref[...] + bb_ref[...], bd) * bg_ref[...] * (r * _sigmoid(r))
    yc = _head_rms(cf_ref[...] + cb_ref[...], bd) * cg_ref[...] * _sigmoid(co_ref[...])
    mixed = jnp.concatenate([ya, yb, yc, yd_ref[...]], axis=-1).astype(BF16)
    x = x_ref[...] + _dot(mixed, w_ref[...])
    xo_ref[...] = x
    hn = x * lax.rsqrt(jnp.mean(x * x, axis=-1, keepdims=True) + EPS) * n2_ref[...]
    hn_ref[...] = hn.astype(BF16)
    hi = hn.astype(BF16)
    lo = (hn - hi.astype(F32)).astype(BF16)
    lg_ref[...] = _dot(hi, rwh_ref[...]) + _dot(lo, rwh_ref[...]) + _dot(hi, rwl_ref[...]) + rb_ref[...]


def _out_proj(x2d, af, ab, za, bf, bb, zb, cf, cb, zc, yd, w_out, b_norm_g, c_norm_g, norm2_g, rw, rb, tm=256):
    t = x2d.shape[0]
    bd = _chunk_masks(False)['bd']
    rwh = rw.astype(BF16)
    rwl = (rw - rwh.astype(F32)).astype(BF16)
    row = lambda w, j=0: pl.BlockSpec((tm, w), lambda i: (i, j))
    consts = (w_out, b_norm_g.reshape(1, -1).astype(F32), c_norm_g.reshape(1, -1).astype(F32), bd,
              norm2_g.reshape(1, -1).astype(F32), rwh, rwl, rb)
    return pl.pallas_call(
        _out_proj_kernel,
        grid=(t // tm,),
        in_specs=[row(D_MODEL), row(GROUP_W), row(GROUP_W), row(GROUP_W, 1), row(GROUP_W), row(GROUP_W),
                  row(GROUP_W, 2), row(GROUP_W), row(GROUP_W), row(GROUP_W, 3), row(GROUP_W)]
        + [_const_spec(c) for c in consts],
        out_specs=(row(D_MODEL), row(D_MODEL), row(V7X_LANES)),
        out_shape=(jax.ShapeDtypeStruct((t, D_MODEL), F32), jax.ShapeDtypeStruct((t, D_MODEL), BF16),
                   jax.ShapeDtypeStruct((t, V7X_LANES), F32)),
        compiler_params=_cparams("parallel"),
        name="out_proj",
    )(x2d, af, ab, za, bf, bb, zb, cf, cb, zc, yd, *consts)


def _route(logits):
    lane = _iota(logits.shape, 1).astype(F32)
    is_g = lane < MOE_GROUPS
    gl = jnp.where(is_g, logits, NEG_BIG)
    gmax = jnp.max(gl, axis=1, keepdims=True)
    gidx = jnp.min(jnp.where(gl == gmax, lane, 1e9), axis=1, keepdims=True)
    gprob = 1.0 / jnp.sum(jnp.where(is_g, jnp.exp(gl - gmax), 0.0), axis=1, keepdims=True)
    lo = MOE_GROUPS + MOE_PER_GROUP * gidx
    el = jnp.where((lane >= lo) & (lane < lo + MOE_PER_GROUP), logits, NEG_BIG)
    v1 = jnp.max(el, axis=1, keepdims=True)
    i1 = jnp.min(jnp.where(el == v1, lane, 1e9), axis=1, keepdims=True)
    el2 = jnp.where(lane == i1, NEG_BIG, el)
    v2 = jnp.max(el2, axis=1, keepdims=True)
    i2 = jnp.min(jnp.where(el2 == v2, lane, 1e9), axis=1, keepdims=True)
    e21 = jnp.exp(v2 - v1)
    p1 = 1.0 / (1.0 + e21)
    return jnp.where(lane == i1, p1 * gprob, 0.0) + jnp.where(lane == i2, e21 * p1 * gprob, 0.0)


def _moe_dense_kernel(x_ref, hn_ref, lg_ref, wg_ref, wu_ref, wd_ref, fg_ref, o_ref, acc_ref, gate_ref, *, final):
    e = pl.program_id(1)

    @pl.when(e == 0)
    def _():
        acc_ref[...] = x_ref[...]
        gate_ref[...] = _route(lg_ref[...])

    hn = hn_ref[...]
    g = _dot(hn, wg_ref[0])
    a = (g * _sigmoid(g)) * _dot(hn, wu_ref[0])
    y = _dot(a.astype(BF16), wd_ref[0])
    lane = _iota(gate_ref.shape, 1)
    ge = jnp.sum(jnp.where(lane == MOE_GROUPS + e, gate_ref[...], 0.0), axis=1, keepdims=True)
    acc_ref[...] += ge * y

    @pl.when(e == MOE_EXPERTS - 1)
    def _():
        x = acc_ref[...]
        if final:
            x = x * lax.rsqrt(jnp.mean(x * x, axis=-1, keepdims=True) + EPS) * fg_ref[...]
        o_ref[...] = x


def _moe_dense(x2d, hn, logits, w_gate, w_up, w_down, final_g, final, tm=512):
    t = x2d.shape[0]
    row = lambda w: pl.BlockSpec((tm, w), lambda i, e: (i, 0))
    return pl.pallas_call(
        functools.partial(_moe_dense_kernel, final=final),
        grid=(t // tm, MOE_EXPERTS),
        in_specs=[row(D_MODEL), row(D_MODEL), row(V7X_LANES),
                  pl.BlockSpec((1, D_MODEL, D_EXPERT), lambda i, e: (e, 0, 0)),
                  pl.BlockSpec((1, D_MODEL, D_EXPERT), lambda i, e: (e, 0, 0)),
                  pl.BlockSpec((1, D_EXPERT, D_MODEL), lambda i, e: (e, 0, 0)),
                  pl.BlockSpec((1, D_MODEL), lambda i, e: (0, 0))],
        out_specs=row(D_MODEL),
        out_shape=jax.ShapeDtypeStruct((t, D_MODEL), F32),
        scratch_shapes=[pltpu.VMEM((tm, D_MODEL), F32), pltpu.VMEM((tm, V7X_LANES), F32)],
        compiler_params=_cparams("parallel", "arbitrary"),
        name="moe_dense",
    )(x2d, hn, logits, w_gate, w_up, w_down, final_g.reshape(1, -1).astype(F32))


def _layer(x2d, bsz, seq, l, P, W, spec, final):
    za, zb, zc, zd, zs = _in_proj(x2d, P['norm1_g'][l].astype(F32), W['w_in'][l])
    r3 = lambda a: a.reshape(bsz, seq, a.shape[-1])
    r2 = lambda a: a.reshape(bsz * seq, a.shape[-1])
    mp = W['mix'][l]
    cw, cb = P['a_conv_w'][l].astype(F32), P['a_conv_b'][l].reshape(1, -1).astype(F32)
    za3, zb3, zc3, zs3 = r3(za), r3(zb), r3(zc), r3(zs)
    af = _rglru(za3, cw, cb, *mp['a', 0], False)
    ab = _rglru(za3, cw, cb, *mp['a', 1], True)
    bf = _gla(zb3, zs3, *mp['b', 0], False)
    bb = _gla(zb3, zs3, *mp['b', 1], True)
    cf = _mlstm(zc3, zs3, *mp['c', 0], False)
    cbk = _mlstm(zc3, zs3, *mp['c', 1], True)
    yd = _mixer_hyena(r3(zd), P, l, spec)
    x_new, hn, logits = _out_proj(x2d, r2(af), r2(ab), za, r2(bf), r2(bb), zb, r2(cf), r2(cbk), zc, r2(yd),
                                  W['w_out'][l], P['b_norm_g'][l], P['c_norm_g'][l], P['norm2_g'][l],
                                  W['router_w'][l], W['router_b'][l])
    return _moe_dense(x_new, hn, logits, W['w_gate'][l], W['w_up'][l], W['w_down'][l], P['final_norm_g'], final)


def _trunk(x, P, W):
    bsz, seq, _ = x.shape
    depth = P['w_in'].shape[0]
    x2d = x.reshape(bsz * seq, D_MODEL)
    for l in range(depth):
        spec = _hyena_spectrum(seq, P['d_ffn_w1'][l], P['d_ffn_b1'][l], P['d_sin_freq'][l], P['d_ffn_w2'][l],
                               P['d_ffn_b2'][l], P['d_ffn_w3'][l])
        x2d = _layer(x2d, bsz, seq, l, P, W, spec, l == depth - 1)
    return x2d.reshape(bsz, seq, D_MODEL)


def _prep_weights(P):
    depth = P['w_in'].shape[0]
    W = {'w_in': [_permute_w_in(P['w_in'][l]) for l in range(depth)],
         'w_out': [P['w_out'][l].astype(BF16) for l in range(depth)],
         'mix': [_mixer_prep(P, l) for l in range(depth)],
         'w_gate': [P['moe_w_gate'][l].astype(BF16) for l in range(depth)],
         'w_up': [P['moe_w_up'][l].astype(BF16) for l in range(depth)],
         'w_down': [P['moe_w_down'][l].astype(BF16) for l in range(depth)],
         'router_w': [], 'router_b': []}
    for l in range(depth):
        rw = jnp.concatenate([P['moe_group_w'][l], P['moe_expert_w'][l]], axis=1).astype(F32)
        rb = jnp.concatenate([P['moe_group_b'][l], P['moe_expert_b'][l]]).astype(F32)
        padc = V7X_LANES - rw.shape[1]
        W['router_w'].append(jnp.pad(rw, ((0, 0), (0, padc))))
        W['router_b'].append(jnp.pad(rb, (0, padc)).reshape(1, V7X_LANES))
    return W


def kernel(x_prompt, x_sample, norm1_g, w_in, a_conv_w, a_conv_b, a_gate_w, a_gate_b, a_lambda, b_gate_w, b_gate_b, b_norm_g, c_gate_b, c_norm_g, d_conv_w, d_conv_b, d_ffn_w1, d_ffn_b1, d_sin_freq, d_ffn_w2, d_ffn_b2, d_ffn_w3, d_skip, w_out, norm2_g, moe_group_w, moe_group_b, moe_expert_w, moe_expert_b, moe_w_gate, moe_w_up, moe_w_down, final_norm_g):
    P = {'norm1_g': norm1_g, 'w_in': w_in, 'a_conv_w': a_conv_w, 'a_conv_b': a_conv_b,
         'a_gate_w': a_gate_w, 'a_gate_b': a_gate_b, 'a_lambda': a_lambda, 'b_gate_w': b_gate_w,
         'b_gate_b': b_gate_b, 'b_norm_g': b_norm_g, 'c_gate_b': c_gate_b, 'c_norm_g': c_norm_g,
         'd_conv_w': d_conv_w, 'd_conv_b': d_conv_b, 'd_ffn_w1': d_ffn_w1, 'd_ffn_b1': d_ffn_b1,
         'd_sin_freq': d_sin_freq, 'd_ffn_w2': d_ffn_w2, 'd_ffn_b2': d_ffn_b2, 'd_ffn_w3': d_ffn_w3,
         'd_skip': d_skip, 'w_out': w_out, 'norm2_g': norm2_g, 'moe_group_w': moe_group_w,
         'moe_group_b': moe_group_b, 'moe_expert_w': moe_expert_w, 'moe_expert_b': moe_expert_b,
         'moe_w_gate': moe_w_gate, 'moe_w_up': moe_w_up, 'moe_w_down': moe_w_down,
         'final_norm_g': final_norm_g}
    W = _prep_weights(P)
    return (_trunk(x_prompt, P, W), _trunk(x_sample, P, W))
```

```python
import functools
import math

import jax
import jax.numpy as jnp
import numpy as np
from jax import lax
from jax.experimental import pallas as pl
from jax.experimental.pallas import tpu as pltpu

F32 = jnp.float32
BF16 = jnp.bfloat16

D_MODEL = 1024
GROUP_W = 256
N_HEADS = 4
HEAD_W = GROUP_W // N_HEADS
B_DK = 32
B_RANK = 16
CHUNK = 64
RG_C = 8.0
RG_SEGMENTS = 16
RG_UNROLL = 4
GLA_TAU = 16.0
HY_EMB = 33
HY_HID = 64
HY_TARGET = 1e-2
HY_FAST = 0.3
HY_SLOW = 1.5
MOE_GROUPS = 4
MOE_PER_GROUP = 4
MOE_EXPERTS = 16
D_EXPERT = 512
EPS = 1e-6
NEG_BIG = -1e30

V7X_LANES = 128
V7X_SUBLANES = 8
VMEM_LIMIT = 56 * 1024 * 1024

ZA_W, ZB_W, ZC_W, ZD_W, ZS_W = 512, 768, 1024, 768, 128
Z_SPLITS = (ZA_W, ZB_W, ZC_W, ZD_W, ZS_W)
Z_TOTAL = sum(Z_SPLITS)


def _cparams(*sem):
    return pltpu.CompilerParams(dimension_semantics=sem, vmem_limit_bytes=VMEM_LIMIT)


def _dot(a, b):
    return jnp.dot(a, b, preferred_element_type=F32)


def _dot_nt(a, b):
    return lax.dot_general(a, b, (((1,), (1,)), ((), ())), preferred_element_type=F32)


def _dot_tn(a, b):
    return lax.dot_general(a, b, (((0,), (0,)), ((), ())), preferred_element_type=F32)


def _dot_split(m_bf16, x):
    hi = x.astype(BF16)
    lo = (x - hi.astype(F32)).astype(BF16)
    return _dot(m_bf16, hi) + _dot(m_bf16, lo)


def _split_dot(x, m_bf16):
    hi = x.astype(BF16)
    lo = (x - hi.astype(F32)).astype(BF16)
    return _dot(hi, m_bf16) + _dot(lo, m_bf16)


def _sigmoid(x):
    return 1.0 / (1.0 + jnp.exp(-x))


def _log_sigmoid(x):
    return jnp.minimum(x, 0.0) - jnp.log(1.0 + jnp.exp(-jnp.abs(x)))


def _iota(shape, dim):
    return lax.broadcasted_iota(jnp.int32, shape, dim)


def _in_proj_kernel(x_ref, g_ref, w_ref, za_ref, zb_ref, zc_ref, zd_ref, zs_ref):
    x = x_ref[...]
    ms = jnp.mean(x * x, axis=-1, keepdims=True)
    xn = (x * lax.rsqrt(ms + EPS) * g_ref[...]).astype(BF16)
    off = 0
    for ref, w in zip((za_ref, zb_ref, zc_ref, zd_ref, zs_ref), Z_SPLITS):
        ref[...] = _dot(xn, w_ref[:, off:off + w])
        off += w


def _in_proj(x2d, g, w_perm, tm=256):
    t = x2d.shape[0]
    outs = tuple(jax.ShapeDtypeStruct((t, w), F32) for w in Z_SPLITS)
    return pl.pallas_call(
        _in_proj_kernel,
        grid=(t // tm,),
        in_specs=[pl.BlockSpec((tm, D_MODEL), lambda i: (i, 0)),
                  pl.BlockSpec((1, D_MODEL), lambda i: (0, 0)),
                  pl.BlockSpec((D_MODEL, Z_TOTAL), lambda i: (0, 0))],
        out_specs=tuple(pl.BlockSpec((tm, w), lambda i: (i, 0)) for w in Z_SPLITS),
        out_shape=outs,
        compiler_params=_cparams("parallel"),
        name="in_proj",
    )(x2d, g.reshape(1, D_MODEL), w_perm)


def _permute_w_in(w_in):
    sizes = (256, 256, 128, 128, 256, 256, 16, 16, 256, 256, 256, 256, 16, 768)
    offs = np.concatenate([[0], np.cumsum(sizes)])
    seg = [w_in[:, offs[i]:offs[i + 1]] for i in range(len(sizes))]
    (a_x, a_g, b_q, b_k, b_v, b_r, b_lf, b_lb, c_q, c_k, c_v, c_o, c_g, d_u) = seg
    small = jnp.concatenate([b_lf, b_lb, c_g, jnp.zeros((w_in.shape[0], ZS_W - 48), w_in.dtype)], axis=1)
    return jnp.concatenate([a_x, a_g, b_q, b_k, b_v, b_r, c_q, c_k, c_v, c_o, d_u, small], axis=1).astype(BF16)


def _halo_conv(prev8, cur, next8, w, b, lo, first, last):
    tl = cur.shape[0]
    prev8 = jnp.where(first, 0.0, prev8)
    next8 = jnp.where(last, 0.0, next8)
    ext = jnp.concatenate([prev8, cur, next8], axis=0)
    n = tl + 16
    acc = None
    for j in range(w.shape[0]):
        o = lo + j
        sh = ext if o == 0 else pltpu.roll(ext, (-o) % n, 0)
        term = sh[8:8 + tl] * w[j:j + 1, :]
        acc = term if acc is None else acc + term
    return acc + b


def _rglru_kernel(prev_ref, cur_ref, next_ref, cw_ref, cb_ref, gw_ref, gb_ref, sp_ref, out_ref,
                  a_s, b_s, h_s, p_s, carry, *, tl, reverse):
    i = pl.program_id(1)
    nt = pl.num_programs(1)
    ti = (nt - 1 - i) if reverse else i
    u = _halo_conv(prev_ref[0], cur_ref[0], next_ref[0], cw_ref[...], cb_ref[...], -2, ti == 0, ti == nt - 1)
    gates = _sigmoid(_dot(u.astype(BF16), gw_ref[...]) + gb_ref[...])
    r = gates[:, :GROUP_W]
    ig = gates[:, GROUP_W:]
    log_a = (-RG_C) * r * sp_ref[...]
    a = jnp.exp(log_a)
    bt = jnp.sqrt(1.0 - a * a) * (ig * u)
    for k in range(2):
        a_s[k] = a[:, k * V7X_LANES:(k + 1) * V7X_LANES]
        b_s[k] = bt[:, k * V7X_LANES:(k + 1) * V7X_LANES]

    @pl.when(i == 0)
    def _():
        carry[...] = jnp.zeros_like(carry)

    nseg = RG_SEGMENTS
    m = tl // nseg

    def body(jj, hp):
        j = (m - 1 - jj) if reverse else jj
        out = []
        for k in range(2):
            h, p = hp[2 * k], hp[2 * k + 1]
            av = a_s[k, pl.ds(j, nseg, stride=m), :]
            bv = b_s[k, pl.ds(j, nseg, stride=m), :]
            h = av * h + bv
            p = av * p
            h_s[k, pl.ds(j, nseg, stride=m), :] = h
            p_s[k, pl.ds(j, nseg, stride=m), :] = p
            out += [h, p]
        return tuple(out)

    z8 = jnp.zeros((nseg, V7X_LANES), F32)
    o8 = jnp.ones((nseg, V7X_LANES), F32)
    ends = lax.fori_loop(0, m, body, (z8, o8, z8, o8), unroll=RG_UNROLL)
    for k in range(2):
        h_end, p_end = ends[2 * k], ends[2 * k + 1]
        c = carry[k]
        cin = [None] * nseg
        for seg in (range(nseg - 1, -1, -1) if reverse else range(nseg)):
            cin[seg] = c
            c = h_end[seg:seg + 1, :] + p_end[seg:seg + 1, :] * c
        carry[k] = c
        for seg in range(nseg):
            rows = slice(seg * m, (seg + 1) * m)
            out_ref[0, rows, k * V7X_LANES:(k + 1) * V7X_LANES] = h_s[k, rows, :] + p_s[k, rows, :] * cin[seg]


def _rglru(za, conv_w, conv_b, gate_w_bd, gate_b, softplus_neg_lam, reverse, tl=512):
    b, l, _ = za.shape
    nt = l // tl
    r8 = tl // 8
    nb8 = l // 8

    def tmap(bi, i):
        return (nt - 1 - i) if reverse else i

    kern = functools.partial(_rglru_kernel, tl=tl, reverse=reverse)
    return pl.pallas_call(
        kern,
        grid=(b, nt),
        in_specs=[
            pl.BlockSpec((1, 8, GROUP_W), lambda bi, i: (bi, jnp.maximum(tmap(bi, i) * r8 - 1, 0), 0)),
            pl.BlockSpec((1, tl, GROUP_W), lambda bi, i: (bi, tmap(bi, i), 0)),
            pl.BlockSpec((1, 8, GROUP_W), lambda bi, i: (bi, jnp.minimum((tmap(bi, i) + 1) * r8, nb8 - 1), 0)),
            pl.BlockSpec((4, GROUP_W), lambda bi, i: (0, 0)),
            pl.BlockSpec((1, GROUP_W), lambda bi, i: (0, 0)),
            pl.BlockSpec((GROUP_W, 2 * GROUP_W), lambda bi, i: (0, 0)),
            pl.BlockSpec((1, 2 * GROUP_W), lambda bi, i: (0, 0)),
            pl.BlockSpec((1, GROUP_W), lambda bi, i: (0, 0)),
        ],
        out_specs=pl.BlockSpec((1, tl, GROUP_W), lambda bi, i: (bi, tmap(bi, i), 0)),
        out_shape=jax.ShapeDtypeStruct((b, l, GROUP_W), F32),
        scratch_shapes=[pltpu.VMEM((2, tl, V7X_LANES), F32)] * 4 + [pltpu.VMEM((2, 1, V7X_LANES), F32)],
        compiler_params=_cparams("parallel", "arbitrary"),
        name="rglru_bwd" if reverse else "rglru_fwd",
    )(za, za, za, conv_w, conv_b, gate_w_bd, gate_b, softplus_neg_lam)


def _block_diag(blocks):
    h, di, do = blocks.shape
    eye = jnp.eye(h, dtype=blocks.dtype)
    return jnp.einsum('hde,hg->hdge', blocks, eye).reshape(h * di, h * do)


def _chunk_masks(reverse):
    r = np.arange(GROUP_W)
    t = np.arange(CHUNK)
    tri = (t[None, :] >= t[:, None]) if reverse else (t[None, :] <= t[:, None])
    cmask = np.tile(tri, (1, N_HEADS))
    bd = (r[:, None] // HEAD_W) == (r[None, :] // HEAD_W)
    kd = (r[:, None] // HEAD_W) == (np.arange(N_HEADS * B_DK)[None, :] // B_DK)
    i2 = np.tile(np.eye(CHUNK, dtype=bool), (1, N_HEADS))
    hm = np.zeros((8, GROUP_W), bool)
    hm[:N_HEADS] = np.arange(N_HEADS)[:, None] == (r[None, :] // HEAD_W)
    return dict(tri=jnp.asarray(tri, BF16), cmask=jnp.asarray(cmask, F32), bd=jnp.asarray(bd, BF16),
                bdf=jnp.asarray(bd, F32), kd=jnp.asarray(kd, BF16), kdf=jnp.asarray(kd, F32),
                i2=jnp.asarray(i2, F32), hm=jnp.asarray(hm, F32))


def _const_spec(arr):
    nd = arr.ndim
    return pl.BlockSpec(arr.shape, lambda *_: (0,) * nd)


def _gla_kernel(zb_ref, zs_ref, gw_ref, gb_ref, tri_ref, kd_ref, kdf_ref, bd_ref, cmask_ref, out_ref, st_ref,
                *, tl, reverse):
    @pl.when(pl.program_id(1) == 0)
    def _():
        st_ref[...] = jnp.zeros_like(st_ref)

    la = _log_sigmoid(_dot(zs_ref[0].astype(BF16), gw_ref[...]) + gb_ref[...]) * (1.0 / GLA_TAU)
    causal = cmask_ref[...] > 0.0
    nch = tl // CHUNK
    for c in (range(nch - 1, -1, -1) if reverse else range(nch)):
        rows = slice(c * CHUNK, (c + 1) * CHUNK)
        q = zb_ref[0, rows, 0:128] * (B_DK ** -0.5)
        k = zb_ref[0, rows, 128:256]
        vb = zb_ref[0, rows, 256:512].astype(BF16)
        la_c = la[rows]
        bcum = _dot_split(tri_ref[...], la_c)
        btot = jnp.sum(la_c, axis=0, keepdims=True)
        q_in = (q * jnp.exp(bcum)).astype(BF16)
        k_in = (k * jnp.exp(-bcum)).astype(BF16)
        k_st = (k * jnp.exp(btot - bcum)).astype(BF16)
        kexp = jnp.concatenate([k_in] * N_HEADS, axis=0) * kd_ref[...]
        att = jnp.where(causal, _dot_nt(q_in, kexp), 0.0)
        vbd = jnp.concatenate([vb] * N_HEADS, axis=0) * bd_ref[...]
        st = st_ref[...]
        out_ref[0, rows, :] = _dot(att.astype(BF16), vbd) + _dot_nt(q_in, st.astype(BF16))
        st_ref[...] = st * jnp.exp(btot) + _dot_tn(vb, k_st) * kdf_ref[...]


def _gla(zb, zs, gate_w_pad, gate_b, reverse, tl=512):
    b, l, _ = zb.shape
    nt = l // tl
    mk = _chunk_masks(reverse)
    consts = (gate_w_pad, gate_b, mk['tri'], mk['kd'], mk['kdf'], mk['bd'], mk['cmask'])

    def tmap(bi, i):
        return (bi, (nt - 1 - i) if reverse else i, 0)

    return pl.pallas_call(
        functools.partial(_gla_kernel, tl=tl, reverse=reverse),
        grid=(b, nt),
        in_specs=[pl.BlockSpec((1, tl, ZB_W), tmap), pl.BlockSpec((1, tl, ZS_W), tmap)]
        + [_const_spec(c) for c in consts],
        out_specs=pl.BlockSpec((1, tl, GROUP_W), tmap),
        out_shape=jax.ShapeDtypeStruct((b, l, GROUP_W), F32),
        scratch_shapes=[pltpu.VMEM((GROUP_W, N_HEADS * B_DK), F32)],
        compiler_params=_cparams("parallel", "arbitrary"),
        name="gla_bwd" if reverse else "gla_fwd",
    )(zb, zs, *consts)


def _mlstm_kernel(zc_ref, zs_ref, e_ref, eb_ref, tri_ref, i2_ref, bd_ref, bdf_ref, cmask_ref, hm_ref, out_ref,
                  s_ref, n_ref, m_ref, *, tl, reverse):
    @pl.when(pl.program_id(1) == 0)
    def _():
        s_ref[...] = jnp.zeros_like(s_ref)
        n_ref[...] = jnp.zeros_like(n_ref)
        m_ref[...] = jnp.full_like(m_ref, NEG_BIG)

    ge = _split_dot(zs_ref[0], e_ref[...]) + eb_ref[...]
    ig_all = ge[:, :GROUP_W]
    lf_all = _log_sigmoid(ge[:, GROUP_W:])
    causal = cmask_ref[...] > 0.0
    ones_b = jnp.ones((CHUNK, GROUP_W), BF16)
    nch = tl // CHUNK
    for c in (range(nch - 1, -1, -1) if reverse else range(nch)):
        rows = slice(c * CHUNK, (c + 1) * CHUNK)
        qb = zc_ref[0, rows, 0:256].astype(BF16)
        k = zc_ref[0, rows, 256:512] * (HEAD_W ** -0.5)
        kb = k.astype(BF16)
        vb = zc_ref[0, rows, 512:768].astype(BF16)
        ig = ig_all[rows]
        lf = lf_all[rows]
        bcum = _dot_split(tri_ref[...], lf)
        blast = jnp.sum(lf, axis=0, keepdims=True)
        cc = jnp.sum((ig - bcum) * i2_ref[...], axis=0, keepdims=True)
        d2 = jnp.where(causal, bcum + cc, NEG_BIG)
        m_intra = jnp.full((CHUNK, GROUP_W), NEG_BIG, F32)
        for h in range(N_HEADS):
            hsel = hm_ref[h:h + 1, :] > 0.0
            mh = jnp.max(jnp.where(hsel, d2, NEG_BIG), axis=1, keepdims=True)
            m_intra = jnp.where(hsel, mh, m_intra)
        m_prev = m_ref[...]
        inter_log = bcum + m_prev
        m_t = jnp.maximum(inter_log, m_intra)
        w_intra = jnp.exp(d2 - m_t)
        w_inter = jnp.exp(inter_log - m_t)
        kexp = jnp.concatenate([kb] * N_HEADS, axis=0) * bd_ref[...]
        qk = (_dot_nt(qb, kexp) * w_intra).astype(BF16)
        vbd = jnp.concatenate([vb] * N_HEADS, axis=0) * bd_ref[...]
        s_prev = s_ref[...]
        n_prev = n_ref[...]
        num = _dot(qk, vbd) + w_inter * _dot(qb, s_prev.astype(BF16))
        den = _dot(qk, bd_ref[...]) + w_inter * _dot(qb, n_prev.astype(BF16))
        out_ref[0, rows, :] = num / jnp.maximum(jnp.abs(den), jnp.exp(-m_t))
        g_end = blast - bcum + ig
        g_max = jnp.max(g_end, axis=0, keepdims=True)
        kw = (k * jnp.exp(g_end - g_max)).astype(BF16)
        m_new = jnp.maximum(blast + m_prev, g_max)
        a = jnp.exp(blast + m_prev - m_new)
        cf = jnp.exp(g_max - m_new)
        s_ref[...] = a * s_prev + cf * (_dot_tn(kw, vb) * bdf_ref[...])
        n_ref[...] = a * n_prev + cf * (_dot_tn(kw, ones_b) * bdf_ref[...])
        m_ref[...] = m_new


def _mlstm(zc, zs, expand_w, expand_b, reverse, tl=512):
    b, l, _ = zc.shape
    nt = l // tl
    mk = _chunk_masks(reverse)
    consts = (expand_w, expand_b, mk['tri'], mk['i2'], mk['bd'], mk['bdf'], mk['cmask'], mk['hm'])

    def tmap(bi, i):
        return (bi, (nt - 1 - i) if reverse else i, 0)

    return pl.pallas_call(
        functools.partial(_mlstm_kernel, tl=tl, reverse=reverse),
        grid=(b, nt),
        in_specs=[pl.BlockSpec((1, tl, ZC_W), tmap), pl.BlockSpec((1, tl, ZS_W), tmap)]
        + [_const_spec(c) for c in consts],
        out_specs=pl.BlockSpec((1, tl, GROUP_W), tmap),
        out_shape=jax.ShapeDtypeStruct((b, l, GROUP_W), F32),
        scratch_shapes=[pltpu.VMEM((GROUP_W, GROUP_W), F32), pltpu.VMEM((GROUP_W, GROUP_W), F32),
                        pltpu.VMEM((1, GROUP_W), F32)],
        compiler_params=_cparams("parallel", "arbitrary"),
        name="mlstm_bwd" if reverse else "mlstm_fwd",
    )(zc, zs, *consts)


def _mixer_prep(P, l):
    out = {}
    gw, gb, lam = P['a_gate_w'][l], P['a_gate_b'][l], P['a_lambda'][l]
    for d in range(2):
        w = jnp.concatenate([_block_diag(gw[d, 0]), _block_diag(gw[d, 1])], axis=1).astype(BF16)
        bb = jnp.concatenate([gb[d, 0].reshape(1, -1), gb[d, 1].reshape(1, -1)], axis=1).astype(F32)
        nl = -lam[d].astype(F32)
        sp = (jnp.maximum(nl, 0.0) + jnp.log(1.0 + jnp.exp(-jnp.abs(nl)))).reshape(1, -1)
        out['a', d] = (w, bb, sp)
        bw = jnp.zeros((ZS_W, N_HEADS * B_DK), F32).at[B_RANK * d:B_RANK * (d + 1)].set(P['b_gate_w'][l][d])
        out['b', d] = (bw.astype(BF16), P['b_gate_b'][l][d].reshape(1, -1).astype(F32))
        e = np.zeros((ZS_W, 2 * GROUP_W), np.float32)
        for h in range(N_HEADS):
            e[32 + 8 * d + h, h * HEAD_W:(h + 1) * HEAD_W] = 1.0
            e[36 + 8 * d + h, GROUP_W + h * HEAD_W:GROUP_W + (h + 1) * HEAD_W] = 1.0
        cb = P['c_gate_b'][l].astype(F32)
        eb = jnp.concatenate([jnp.repeat(cb[2 * d], HEAD_W), jnp.repeat(cb[2 * d + 1], HEAD_W)]).reshape(1, -1)
        out['c', d] = (jnp.asarray(e, BF16), eb)
    return out


def _conv3_kernel(prev_ref, cur_ref, next_ref, w_ref, b_ref, v_ref, x1_ref, x2_ref):
    i = pl.program_id(1)
    u = _halo_conv(prev_ref[0], cur_ref[0], next_ref[0], w_ref[...], b_ref[...], -1, i == 0,
                   i == pl.num_programs(1) - 1)
    v_ref[0] = u[:, 0:GROUP_W]
    x1_ref[0] = u[:, GROUP_W:2 * GROUP_W]
    x2_ref[0] = u[:, 2 * GROUP_W:3 * GROUP_W]


def _conv3_split(zd, w, b, tl=512):
    bsz, l, _ = zd.shape
    nt, r8, nb8 = l // tl, tl // 8, l // 8
    o = jax.ShapeDtypeStruct((bsz, l, GROUP_W), F32)
    ospec = pl.BlockSpec((1, tl, GROUP_W), lambda bi, i: (bi, i, 0))
    return pl.pallas_call(
        _conv3_kernel,
        grid=(bsz, nt),
        in_specs=[pl.BlockSpec((1, 8, ZD_W), lambda bi, i: (bi, jnp.maximum(i * r8 - 1, 0), 0)),
                  pl.BlockSpec((1, tl, ZD_W), lambda bi, i: (bi, i, 0)),
                  pl.BlockSpec((1, 8, ZD_W), lambda bi, i: (bi, jnp.minimum((i + 1) * r8, nb8 - 1), 0)),
                  pl.BlockSpec((3, ZD_W), lambda bi, i: (0, 0)),
                  pl.BlockSpec((1, ZD_W), lambda bi, i: (0, 0))],
        out_specs=(ospec, ospec, ospec),
        out_shape=(o, o, o),
        compiler_params=_cparams("parallel", "parallel"),
        name="hyena_conv3",
    )(zd, zd, zd, w, b.reshape(1, ZD_W))


def _dot_hp(a, b):
    return jnp.dot(a, b, preferred_element_type=F32, precision=lax.Precision.HIGHEST)


def _filter_mlp_kernel(emb_ref, embr_ref, w1_ref, b1_ref, f0_ref, w2_ref, b2_ref, f1_ref, w3f_ref, w3b_ref, dl_ref,
                       fwd_ref, sec_ref, s_ref):
    i = pl.program_id(0)

    def mlp(emb, w3):
        h = jnp.sin(f0_ref[...] * (_dot_hp(emb, w1_ref[...]) + b1_ref[...]))
        h = jnp.sin(f1_ref[...] * (_dot_hp(h, w2_ref[...]) + b2_ref[...]))
        return _dot_hp(h, w3) * jnp.exp(-emb[:, 0:1] * dl_ref[...])

    hf = mlp(emb_ref[...], w3f_ref[...])
    hb = mlp(embr_ref[...], w3b_ref[...])
    row = _iota(hb.shape, 0) + i * hb.shape[0]
    hb = jnp.where(row == 0, 0.0, hb)
    for o in range(2):
        fwd_ref[o] = hf[:, o * GROUP_W:(o + 1) * GROUP_W]
        sec_ref[o] = hb[:, o * GROUP_W:(o + 1) * GROUP_W]

    @pl.when(i == 0)
    def _():
        s_ref[...] = jnp.zeros_like(s_ref)

    s_ref[...] += jnp.sum(jnp.abs(hf) + jnp.abs(hb), axis=0, keepdims=True)


def _hyena_filter(l, w1, b1, freq, w2, b2, w3, tl=512):
    bands = (HY_EMB - 1) // 2
    f = jnp.linspace(1e-4, bands - 1, bands, dtype=F32)

    def embed(pos):
        t = pos / max(l - 1, 1)
        ang = (2.0 * math.pi / l) * pos[:, None] * f[None, :]
        return jnp.concatenate([t[:, None], jnp.cos(ang), -jnp.sin(ang),
                                jnp.zeros((l, V7X_LANES - HY_EMB), F32)], axis=-1)

    pos = jnp.arange(l, dtype=F32)
    emb, emb_rev = embed(pos), embed(l - pos)
    deltas = jnp.abs(jnp.linspace(math.log(HY_TARGET) / HY_SLOW, math.log(HY_TARGET) / HY_FAST, GROUP_W, dtype=F32))
    pad = V7X_LANES - HY_HID
    w1p = jnp.zeros((V7X_LANES, V7X_LANES), F32).at[:HY_EMB, :HY_HID].set(w1.astype(F32))
    w2p = jnp.zeros((V7X_LANES, V7X_LANES), F32).at[:HY_HID, :HY_HID].set(w2.astype(F32))
    w3r = w3.astype(F32).reshape(HY_HID, 2, 2, GROUP_W)
    w3p = lambda d: jnp.zeros((V7X_LANES, 2 * GROUP_W), F32).at[:HY_HID].set(w3r[:, :, d].reshape(HY_HID, -1))
    vec = lambda x: jnp.pad(x.astype(F32), (0, pad)).reshape(1, V7X_LANES)
    consts = (w1p, vec(b1), vec(freq[0]), w2p, vec(b2), vec(freq[1]), w3p(0), w3p(1),
              jnp.tile(deltas, 2).reshape(1, -1))
    half = jax.ShapeDtypeStruct((2, l, GROUP_W), F32)
    hspec = pl.BlockSpec((2, tl, GROUP_W), lambda i: (0, i, 0))
    espec = pl.BlockSpec((tl, V7X_LANES), lambda i: (i, 0))
    return pl.pallas_call(
        _filter_mlp_kernel,
        grid=(l // tl,),
        in_specs=[espec, espec] + [_const_spec(c) for c in consts],
        out_specs=(hspec, hspec, pl.BlockSpec((1, 2 * GROUP_W), lambda i: (0, 0))),
        out_shape=(half, half, jax.ShapeDtypeStruct((1, 2 * GROUP_W), F32)),
        compiler_params=_cparams("arbitrary"),
        name="hyena_filter_mlp",
    )(emb, emb_rev, *consts)


def _fft_factors(l):
    return (64, 128) if l == 4096 else (2 * l // 256, 256)


@functools.lru_cache(maxsize=None)
def _fft_tables(n1, n2):
    n = n1 * n2
    k1 = np.arange(n1)[None, :, None]
    m2 = np.arange(n2)[:, None, None]

    def theta(n1_vals):
        idx = (k1 * (n1_vals[None, None, :] * n2 + m2)) % n
        return (2.0 * np.pi / n) * idx
    th = theta(np.arange(n1 // 2))
    fr, fi = np.cos(th), -np.sin(th)
    fwd = np.concatenate([np.concatenate([fr, -fi], -1), np.concatenate([fi, fr], -1)], -2)
    thf = theta(np.arange(n1))
    filt = np.concatenate([np.cos(thf), -np.sin(thf)], -2)
    cr = np.swapaxes(np.cos(th), 1, 2) / n
    ci = np.swapaxes(np.sin(th), 1, 2) / n
    inv = np.concatenate([np.concatenate([cr, -ci], -1), np.concatenate([ci, cr], -1)], -2)
    t2 = (2.0 * np.pi / n2) * ((np.arange(n2)[:, None] * np.arange(n2)[None, :]) % n2)
    gr, gi = np.cos(t2), -np.sin(t2)
    g = np.block([[gr, -gi], [gi, gr]])
    ginv = np.block([[gr, gi], [-gi, gr]])
    return tuple(jnp.asarray(a, BF16) for a in (fwd, filt, inv, g, ginv))


def _fft_stage1_kernel(zr_ref, zi_ref, t_ref, ar_ref, ai_ref, *, nb, n1):
    for j in range(nb):
        cols = slice(j * GROUP_W, (j + 1) * GROUP_W)
        rhs = jnp.concatenate([zr_ref[0, :, cols], zi_ref[0, :, cols]], axis=0).astype(BF16)
        o = _dot(t_ref[j], rhs)
        ar_ref[0, :, cols] = o[:n1].astype(BF16)
        ai_ref[0, :, cols] = o[n1:].astype(BF16)


def _fft_stage1(zr, zi, table, n1, n2, pairs, idx_r, idx_i, nb=8):
    o = jax.ShapeDtypeStruct((pairs, n1, n2 * GROUP_W), BF16)
    ospec = pl.BlockSpec((1, n1, nb * GROUP_W), lambda p, j: (p, 0, j))
    return pl.pallas_call(
        functools.partial(_fft_stage1_kernel, nb=nb, n1=n1),
        grid=(pairs, n2 // nb),
        in_specs=[pl.BlockSpec((1, n1 // 2, nb * GROUP_W), lambda p, j: (idx_r(p), 0, j)),
                  pl.BlockSpec((1, n1 // 2, nb * GROUP_W), lambda p, j: (idx_i(p), 0, j)),
                  pl.BlockSpec((nb, 2 * n1, n1), lambda p, j: (j, 0, 0))],
        out_specs=(ospec, ospec),
        out_shape=(o, o),
        compiler_params=_cparams("parallel", "parallel"),
        name="fft_stage1",
    )(zr, zi, table)


def _fft_spec_kernel(ar_ref, ai_ref, g_ref, s_ref, h_ref, *, kb, n2):
    for k in range(kb):
        rhs = jnp.concatenate([ar_ref[0, k], ai_ref[0, k]], axis=0)
        x = _dot(g_ref[...], rhs) * s_ref[0]
        h_ref[0, k, 0] = x[:n2]
        h_ref[0, k, 1] = x[n2:]


def _fft_filter_spectrum(ar, ai, g, inv_s, n1, n2, kb=8):
    orders = ar.shape[0]
    a4 = lambda a: a.reshape(orders, n1, n2, GROUP_W)
    aspec = pl.BlockSpec((1, kb, n2, GROUP_W), lambda o, i: (o, i, 0, 0))
    return pl.pallas_call(
        functools.partial(_fft_spec_kernel, kb=kb, n2=n2),
        grid=(orders, n1 // kb),
        in_specs=[aspec, aspec, _const_spec(g), pl.BlockSpec((1, 1, GROUP_W), lambda o, i: (o, 0, 0))],
        out_specs=pl.BlockSpec((1, kb, 2, n2, GROUP_W), lambda o, i: (o, i, 0, 0, 0)),
        out_shape=jax.ShapeDtypeStruct((orders, n1, 2, n2, GROUP_W), F32),
        compiler_params=_cparams("parallel", "parallel"),
        name="fft_filter_spectrum",
    )(a4(ar), a4(ai), g, inv_s)


def _fft_mid_kernel(ar_ref, ai_ref, g_ref, gi_ref, h_ref, br_ref, bi_ref, *, kb, n2):
    for k in range(kb):
        rhs = jnp.concatenate([ar_ref[0, k], ai_ref[0, k]], axis=0)
        x = _dot(g_ref[...], rhs)
        xr, xi = x[:n2], x[n2:]
        hr, hi = h_ref[0, k, 0], h_ref[0, k, 1]
        y = jnp.concatenate([xr * hr - xi * hi, xr * hi + xi * hr], axis=0).astype(BF16)
        z = _dot(gi_ref[...], y)
        br_ref[0, k] = z[:n2].astype(BF16)
        bi_ref[0, k] = z[n2:].astype(BF16)


def _fft_mid(ar, ai, g, ginv, spec, order, n1, n2, kb=8):
    pairs = ar.shape[0]
    a4 = lambda a: a.reshape(pairs, n1, n2, GROUP_W)
    aspec = pl.BlockSpec((1, kb, n2, GROUP_W), lambda i, p: (p, i, 0, 0))
    o = jax.ShapeDtypeStruct((pairs, n1, n2, GROUP_W), BF16)
    br, bi = pl.pallas_call(
        functools.partial(_fft_mid_kernel, kb=kb, n2=n2),
        grid=(n1 // kb, pairs),
        in_specs=[aspec, aspec, _const_spec(g), _const_spec(ginv),
                  pl.BlockSpec((1, kb, 2, n2, GROUP_W), lambda i, p: (order, i, 0, 0, 0))],
        out_specs=(aspec, aspec),
        out_shape=(o, o),
        compiler_params=_cparams("parallel", "parallel"),
        name="fft_mid",
    )(a4(ar), a4(ai), g, ginv, spec)
    return br.reshape(pairs, n1, n2 * GROUP_W), bi.reshape(pairs, n1, n2 * GROUP_W)


def _fft_stage3_kernel(br_ref, bi_ref, t_ref, ur_ref, ui_ref, gr_ref, gi_ref, sk_ref, y_ref, *, nb, n1):
    h = n1 // 2
    for j in range(nb):
        cols = slice(j * GROUP_W, (j + 1) * GROUP_W)
        rhs = jnp.concatenate([br_ref[0, :, cols], bi_ref[0, :, cols]], axis=0)
        o = _dot(t_ref[j], rhs)
        sk = sk_ref[...]
        y_ref[0, 0, :, cols] = gr_ref[0, :, cols] * (o[:h] + ur_ref[0, :, cols] * sk)
        y_ref[0, 1, :, cols] = gi_ref[0, :, cols] * (o[h:] + ui_ref[0, :, cols] * sk)


def _fft_stage3(br, bi, table, u, gate, skip, n1, n2, nb=8):
    pairs = br.shape[0]
    bspec = pl.BlockSpec((1, n1, nb * GROUP_W), lambda p, j: (p, 0, j))
    even = pl.BlockSpec((1, n1 // 2, nb * GROUP_W), lambda p, j: (2 * p, 0, j))
    odd = pl.BlockSpec((1, n1 // 2, nb * GROUP_W), lambda p, j: (2 * p + 1, 0, j))
    y = pl.pallas_call(
        functools.partial(_fft_stage3_kernel, nb=nb, n1=n1),
        grid=(pairs, n2 // nb),
        in_specs=[bspec, bspec, pl.BlockSpec((nb, n1, 2 * n1), lambda p, j: (j, 0, 0)),
                  even, odd, even, odd, pl.BlockSpec((1, GROUP_W), lambda p, j: (0, 0))],
        out_specs=pl.BlockSpec((1, 2, n1 // 2, nb * GROUP_W), lambda p, j: (p, 0, 0, j)),
        out_shape=jax.ShapeDtypeStruct((pairs, 2, n1 // 2, n2 * GROUP_W), F32),
        compiler_params=_cparams("parallel", "parallel"),
        name="fft_stage3",
    )(br, bi, table, u, u, gate, gate, skip.reshape(1, GROUP_W).astype(F32))
    return y.reshape(2 * pairs, n1 // 2, n2 * GROUP_W)


def _hyena_spectrum(l, w1, b1, freq, w2, b2, w3):
    n1, n2 = _fft_factors(l)
    _, t_filt, _, g, _ = _fft_tables(n1, n2)
    first, second, sums = _hyena_filter(l, w1, b1, freq, w2, b2, w3)
    shape = (2, n1 // 2, n2 * GROUP_W)
    ar, ai = _fft_stage1(first.reshape(shape), second.reshape(shape), t_filt, n1, n2, 2, lambda p: p, lambda p: p)
    inv_s = (1.0 / sums).reshape(2, 1, GROUP_W)
    return _fft_filter_spectrum(ar, ai, g, inv_s, n1, n2)


def _hyena_long_conv(u, gate, spec, order, skip):
    b, l, _ = u.shape
    n1, n2 = _fft_factors(l)
    t_fwd, _, t_inv, g, ginv = _fft_tables(n1, n2)
    u2 = u.reshape(b, n1 // 2, n2 * GROUP_W)
    ar, ai = _fft_stage1(u2, u2, t_fwd, n1, n2, b // 2, lambda p: 2 * p, lambda p: 2 * p + 1)
    br, bi = _fft_mid(ar, ai, g, ginv, spec, order, n1, n2)
    y = _fft_stage3(br, bi, t_inv, u2, gate.reshape(b, n1 // 2, n2 * GROUP_W), skip, n1, n2)
    return y.reshape(b, l, GROUP_W)


def _mixer_hyena(zd, P, l, spec):
    v, x1, x2 = _conv3_split(zd, P['d_conv_w'][l].astype(F32), P['d_conv_b'][l].astype(F32))
    z = _hyena_long_conv(v, x1, spec, 0, P['d_skip'][l][0])
    return _hyena_long_conv(z, x2, spec, 1, P['d_skip'][l][1])


def _gelu_tanh(x):
    return 0.5 * x * (1.0 + jnp.tanh(math.sqrt(2.0 / math.pi) * (x + 0.044715 * (x * x * x))))


def _head_rms(h, bd):
    ss = _dot((h * h).astype(BF16), bd) * (1.0 / HEAD_W)
    return h * lax.rsqrt(ss + EPS)


def _out_proj_kernel(x_ref, af_ref, ab_ref, ag_ref, bf_ref, bb_ref, br_ref, cf_ref, cb_ref, co_ref, yd_ref,
                     w_ref, bg_ref, cg_ref, bd_ref, n2_ref, rwh_ref, rwl_ref, rb_ref,
                     xo_ref, hn_ref, lg_ref):
    bd = bd_ref[...]
    ya = (af_ref[...] + ab_ref[...]) * _gelu_tanh(ag_ref[...])
    r = br_ref[...]
    yb = _head_rms(bf_ref[...] + bb_ref[...], bd) * bg_ref[...] * (r * _sigmoid(r))
    yc = _head_rms(cf_ref[...] + cb_ref[...], bd) * cg_ref[...] * _sigmoid(co_ref[...])
    mixed = jnp.concatenate([ya, yb, yc, yd_ref[...]], axis=-1).astype(BF16)
    x = x_ref[...] + _dot(mixed, w_ref[...])
    xo_ref[...] = x
    hn = x * lax.rsqrt(jnp.mean(x * x, axis=-1, keepdims=True) + EPS) * n2_ref[...]
    hn_ref[...] = hn.astype(BF16)
    hi = hn.astype(BF16)
    lo = (hn - hi.astype(F32)).astype(BF16)
    lg_ref[...] = _dot(hi, rwh_ref[...]) + _dot(lo, rwh_ref[...]) + _dot(hi, rwl_ref[...]) + rb_ref[...]


def _out_proj(x2d, af, ab, za, bf, bb, zb, cf, cb, zc, yd, w_out, b_norm_g, c_norm_g, norm2_g, rw, rb, tm=256):
    t = x2d.shape[0]
    bd = _chunk_masks(False)['bd']
    rwh = rw.astype(BF16)
    rwl = (rw - rwh.astype(F32)).astype(BF16)
    row = lambda w, j=0: pl.BlockSpec((tm, w), lambda i: (i, j))
    consts = (w_out, b_norm_g.reshape(1, -1).astype(F32), c_norm_g.reshape(1, -1).astype(F32), bd,
              norm2_g.reshape(1, -1).astype(F32), rwh, rwl, rb)
    return pl.pallas_call(
        _out_proj_kernel,
        grid=(t // tm,),
        in_specs=[row(D_MODEL), row(GROUP_W), row(GROUP_W), row(GROUP_W, 1), row(GROUP_W), row(GROUP_W),
                  row(GROUP_W, 2), row(GROUP_W), row(GROUP_W), row(GROUP_W, 3), row(GROUP_W)]
        + [_const_spec(c) for c in consts],
        out_specs=(row(D_MODEL), row(D_MODEL), row(V7X_LANES)),
        out_shape=(jax.ShapeDtypeStruct((t, D_MODEL), F32), jax.ShapeDtypeStruct((t, D_MODEL), BF16),
                   jax.ShapeDtypeStruct((t, V7X_LANES), F32)),
        compiler_params=_cparams("parallel"),
        name="out_proj",
    )(x2d, af, ab, za, bf, bb, zb, cf, cb, zc, yd, *consts)


def _route(logits):
    lane = _iota(logits.shape, 1).astype(F32)
    is_g = lane < MOE_GROUPS
    gl = jnp.where(is_g, logits, NEG_BIG)
    gmax = jnp.max(gl, axis=1, keepdims=True)
    gidx = jnp.min(jnp.where(gl == gmax, lane, 1e9), axis=1, keepdims=True)
    gprob = 1.0 / jnp.sum(jnp.where(is_g, jnp.exp(gl - gmax), 0.0), axis=1, keepdims=True)
    lo = MOE_GROUPS + MOE_PER_GROUP * gidx
    el = jnp.where((lane >= lo) & (lane < lo + MOE_PER_GROUP), logits, NEG_BIG)
    v1 = jnp.max(el, axis=1, keepdims=True)
    i1 = jnp.min(jnp.where(el == v1, lane, 1e9), axis=1, keepdims=True)
    el2 = jnp.where(lane == i1, NEG_BIG, el)
    v2 = jnp.max(el2, axis=1, keepdims=True)
    i2 = jnp.min(jnp.where(el2 == v2, lane, 1e9), axis=1, keepdims=True)
    e21 = jnp.exp(v2 - v1)
    p1 = 1.0 / (1.0 + e21)
    return jnp.where(lane == i1, p1 * gprob, 0.0) + jnp.where(lane == i2, e21 * p1 * gprob, 0.0), gidx


MOE_TB = 1024
MOE_RT = 128
MOE_SB = MOE_TB + MOE_GROUPS * MOE_RT


def _moe_kernel(x_ref, hn_ref, lg_ref, wg_ref, wu_ref, wd_ref, fg_ref, o_ref,
                xs_ref, ys_ref, gs_ref, pos_ref, meta_ref, *, final):
    e = pl.program_id(1)
    grp = e // MOE_PER_GROUP
    tb, sb, rt = MOE_TB, MOE_SB, MOE_RT

    @pl.when(e == 0)
    def _():
        gates, gidx = _route(lg_ref[...])
        lane = _iota((tb, V7X_LANES), 1).astype(F32)
        ohg = jnp.where(lane == gidx, 1.0, 0.0)
        tril = jnp.where(_iota((tb, tb), 1) < _iota((tb, tb), 0), 1.0, 0.0).astype(BF16)
        rank = jnp.sum(_dot(tril, ohg.astype(BF16)) * ohg, axis=1, keepdims=True)
        cnt = jnp.sum(ohg, axis=0, keepdims=True)
        pc = jnp.ceil(cnt * (1.0 / rt)) * rt
        lane1 = _iota((1, V7X_LANES), 1)
        off = jnp.zeros((1, V7X_LANES), F32)
        run = jnp.zeros((1, 1), F32)
        for g in range(MOE_GROUPS):
            off = jnp.where(lane1 == g, run, off)
            run = run + jnp.sum(jnp.where(lane1 == g, pc, 0.0), axis=1, keepdims=True)
        pos = jnp.sum(ohg * off, axis=1, keepdims=True) + rank
        pos_ref[...] = jnp.broadcast_to(pos, (tb, V7X_LANES))
        meta_ref[0:1, :] = off
        meta_ref[1:2, :] = pc
        hi = jnp.floor(pos * (1.0 / 64.0))
        lo = pos - 64.0 * hi
        pm = jnp.where(lane == 0.0, hi, jnp.where(lane == 1.0, lo, 0.0)).astype(BF16)
        sel = jnp.where(_iota((8, V7X_LANES), 0) == _iota((8, V7X_LANES), 1), 1.0, 0.0).astype(BF16)
        rows8 = _dot_nt(sel, pm)
        pos_row = 64.0 * rows8[0:1, :] + rows8[1:2, :]
        p = jnp.where(_iota((sb, tb), 0).astype(F32) == pos_row, 1.0, 0.0).astype(BF16)
        xs_ref[...] = _dot(p, hn_ref[...]).astype(BF16)
        gh = gates.astype(BF16)
        gl = (gates - gh.astype(F32)).astype(BF16)
        gs_ref[...] = _dot(p, gh) + _dot(p, gl)
        ys_ref[...] = jnp.zeros_like(ys_ref)

    lane1 = _iota((1, V7X_LANES), 1)
    start = jnp.sum(jnp.where(lane1 == grp, meta_ref[0:1, :], 0.0)).astype(jnp.int32)
    rows = jnp.sum(jnp.where(lane1 == grp, meta_ref[1:2, :], 0.0)).astype(jnp.int32)
    lane_t = _iota((rt, V7X_LANES), 1)

    def tile(i, carry):
        r0 = pl.multiple_of(start + i * rt, rt)
        xt = xs_ref[pl.ds(r0, rt), :]
        g = _dot(xt, wg_ref[0])
        a = (g * _sigmoid(g)) * _dot(xt, wu_ref[0])
        y = _dot(a.astype(BF16), wd_ref[0])
        w = jnp.sum(jnp.where(lane_t == MOE_GROUPS + e, gs_ref[pl.ds(r0, rt), :], 0.0), axis=1, keepdims=True)
        ys_ref[pl.ds(r0, rt), :] += w * y
        return carry

    lax.fori_loop(0, rows // rt, tile, 0)

    @pl.when(e == MOE_EXPERTS - 1)
    def _():
        pt = jnp.where(_iota((tb, sb), 1).astype(F32) == pos_ref[:, 0:1], 1.0, 0.0).astype(BF16)
        x = x_ref[...] + _dot(pt, ys_ref[...].astype(BF16))
        if final:
            x = x * lax.rsqrt(jnp.mean(x * x, axis=-1, keepdims=True) + EPS) * fg_ref[...]
        o_ref[...] = x


def _moe(x2d, hn, logits, w_gate, w_up, w_down, final_g, final):
    t = x2d.shape[0]
    tm = MOE_TB
    row = lambda w: pl.BlockSpec((tm, w), lambda i, e: (i, 0))
    return pl.pallas_call(
        functools.partial(_moe_kernel, final=final),
        grid=(t // tm, MOE_EXPERTS),
        in_specs=[row(D_MODEL), row(D_MODEL), row(V7X_LANES),
                  pl.BlockSpec((1, D_MODEL, D_EXPERT), lambda i, e: (e, 0, 0)),
                  pl.BlockSpec((1, D_MODEL, D_EXPERT), lambda i, e: (e, 0, 0)),
                  pl.BlockSpec((1, D_EXPERT, D_MODEL), lambda i, e: (e, 0, 0)),
                  pl.BlockSpec((1, D_MODEL), lambda i, e: (0, 0))],
        out_specs=row(D_MODEL),
        out_shape=jax.ShapeDtypeStruct((t, D_MODEL), F32),
        scratch_shapes=[pltpu.VMEM((MOE_SB, D_MODEL), BF16), pltpu.VMEM((MOE_SB, D_MODEL), F32),
                        pltpu.VMEM((MOE_SB, V7X_LANES), F32), pltpu.VMEM((tm, V7X_LANES), F32),
                        pltpu.VMEM((8, V7X_LANES), F32)],
        compiler_params=_cparams("parallel", "arbitrary"),
        name="moe",
    )(x2d, hn, logits, w_gate, w_up, w_down, final_g.reshape(1, -1).astype(F32))


def _layer(x2d, bsz, seq, l, P, W, spec, final):
    za, zb, zc, zd, zs = _in_proj(x2d, P['norm1_g'][l].astype(F32), W['w_in'][l])
    r3 = lambda a: a.reshape(bsz, seq, a.shape[-1])
    r2 = lambda a: a.reshape(bsz * seq, a.shape[-1])
    mp = W['mix'][l]
    cw, cb = P['a_conv_w'][l].astype(F32), P['a_conv_b'][l].reshape(1, -1).astype(F32)
    za3, zb3, zc3, zs3 = r3(za), r3(zb), r3(zc), r3(zs)
    af = _rglru(za3, cw, cb, *mp['a', 0], False)
    ab = _rglru(za3, cw, cb, *mp['a', 1], True)
    bf = _gla(zb3, zs3, *mp['b', 0], False)
    bb = _gla(zb3, zs3, *mp['b', 1], True)
    cf = _mlstm(zc3, zs3, *mp['c', 0], False)
    cbk = _mlstm(zc3, zs3, *mp['c', 1], True)
    yd = _mixer_hyena(r3(zd), P, l, spec)
    x_new, hn, logits = _out_proj(x2d, r2(af), r2(ab), za, r2(bf), r2(bb), zb, r2(cf), r2(cbk), zc, r2(yd),
                                  W['w_out'][l], P['b_norm_g'][l], P['c_norm_g'][l], P['norm2_g'][l],
                                  W['router_w'][l], W['router_b'][l])
    return _moe(x_new, hn, logits, W['w_gate'][l], W['w_up'][l], W['w_down'][l], P['final_norm_g'], final)


def _trunk(x, P, W):
    bsz, seq, _ = x.shape
    depth = P['w_in'].shape[0]
    x2d = x.reshape(bsz * seq, D_MODEL)
    for l in range(depth):
        spec = _hyena_spectrum(seq, P['d_ffn_w1'][l], P['d_ffn_b1'][l], P['d_sin_freq'][l], P['d_ffn_w2'][l],
                               P['d_ffn_b2'][l], P['d_ffn_w3'][l])
        x2d = _layer(x2d, bsz, seq, l, P, W, spec, l == depth - 1)
    return x2d.reshape(bsz, seq, D_MODEL)


def _prep_weights(P):
    depth = P['w_in'].shape[0]
    W = {'w_in': [_permute_w_in(P['w_in'][l]) for l in range(depth)],
         'w_out': [P['w_out'][l].astype(BF16) for l in range(depth)],
         'mix': [_mixer_prep(P, l) for l in range(depth)],
         'w_gate': [P['moe_w_gate'][l].astype(BF16) for l in range(depth)],
         'w_up': [P['moe_w_up'][l].astype(BF16) for l in range(depth)],
         'w_down': [P['moe_w_down'][l].astype(BF16) for l in range(depth)],
         'router_w': [], 'router_b': []}
    for l in range(depth):
        rw = jnp.concatenate([P['moe_group_w'][l], P['moe_expert_w'][l]], axis=1).astype(F32)
        rb = jnp.concatenate([P['moe_group_b'][l], P['moe_expert_b'][l]]).astype(F32)
        padc = V7X_LANES - rw.shape[1]
        W['router_w'].append(jnp.pad(rw, ((0, 0), (0, padc))))
        W['router_b'].append(jnp.pad(rb, (0, padc)).reshape(1, V7X_LANES))
    return W


def kernel(x_prompt, x_sample, norm1_g, w_in, a_conv_w, a_conv_b, a_gate_w, a_gate_b, a_lambda, b_gate_w, b_gate_b, b_norm_g, c_gate_b, c_norm_g, d_conv_w, d_conv_b, d_ffn_w1, d_ffn_b1, d_sin_freq, d_ffn_w2, d_ffn_b2, d_ffn_w3, d_skip, w_out, norm2_g, moe_group_w, moe_group_b, moe_expert_w, moe_expert_b, moe_w_gate, moe_w_up, moe_w_down, final_norm_g):
    P = {'norm1_g': norm1_g, 'w_in': w_in, 'a_conv_w': a_conv_w, 'a_conv_b': a_conv_b,
         'a_gate_w': a_gate_w, 'a_gate_b': a_gate_b, 'a_lambda': a_lambda, 'b_gate_w': b_gate_w,
         'b_gate_b': b_gate_b, 'b_norm_g': b_norm_g, 'c_gate_b': c_gate_b, 'c_norm_g': c_norm_g,
         'd_conv_w': d_conv_w, 'd_conv_b': d_conv_b, 'd_ffn_w1': d_ffn_w1, 'd_ffn_b1': d_ffn_b1,
         'd_sin_freq': d_sin_freq, 'd_ffn_w2': d_ffn_w2, 'd_ffn_b2': d_ffn_b2, 'd_ffn_w3': d_ffn_w3,
         'd_skip': d_skip, 'w_out': w_out, 'norm2_g': norm2_g, 'moe_group_w': moe_group_w,
         'moe_group_b': moe_group_b, 'moe_expert_w': moe_expert_w, 'moe_expert_b': moe_expert_b,
         'moe_w_gate': moe_w_gate, 'moe_w_up': moe_w_up, 'moe_w_down': moe_w_down,
         'final_norm_g': final_norm_g}
    W = _prep_weights(P)
    return (_trunk(x_prompt, P, W), _trunk(x_sample, P, W))
```

```python
import functools
import math

import jax
import jax.numpy as jnp
import numpy as np
from jax import lax
from jax.experimental import pallas as pl
from jax.experimental.pallas import tpu as pltpu

F32 = jnp.float32
BF16 = jnp.bfloat16

D_MODEL = 1024
GROUP_W = 256
N_HEADS = 4
HEAD_W = GROUP_W // N_HEADS
B_DK = 32
B_RANK = 16
CHUNK = 64
RG_C = 8.0
RG_SEGMENTS = 16
RG_UNROLL = 4
GLA_TAU = 16.0
HY_EMB = 33
HY_HID = 64
HY_TARGET = 1e-2
HY_FAST = 0.3
HY_SLOW = 1.5
MOE_GROUPS = 4
MOE_PER_GROUP = 4
MOE_EXPERTS = 16
D_EXPERT = 512
EPS = 1e-6
NEG_BIG = -1e30

V7X_LANES = 128
V7X_SUBLANES = 8
VMEM_LIMIT = 56 * 1024 * 1024

ZA_W, ZB_W, ZC_W, ZD_W, ZS_W = 512, 768, 1024, 768, 128
Z_SPLITS = (ZA_W, ZB_W, ZC_W, ZD_W, ZS_W)
Z_TOTAL = sum(Z_SPLITS)


def _cparams(*sem):
    return pltpu.CompilerParams(dimension_semantics=sem, vmem_limit_bytes=VMEM_LIMIT)


def _dot(a, b):
    return jnp.dot(a, b, preferred_element_type=F32)


def _dot_nt(a, b):
    return lax.dot_general(a, b, (((1,), (1,)), ((), ())), preferred_element_type=F32)


def _dot_tn(a, b):
    return lax.dot_general(a, b, (((0,), (0,)), ((), ())), preferred_element_type=F32)


def _dot_split(m_bf16, x):
    hi = x.astype(BF16)
    lo = (x - hi.astype(F32)).astype(BF16)
    return _dot(m_bf16, hi) + _dot(m_bf16, lo)


def _split_dot(x, m_bf16):
    hi = x.astype(BF16)
    lo = (x - hi.astype(F32)).astype(BF16)
    return _dot(hi, m_bf16) + _dot(lo, m_bf16)


def _sigmoid(x):
    return 1.0 / (1.0 + jnp.exp(-x))


def _log_sigmoid(x):
    return jnp.minimum(x, 0.0) - jnp.log(1.0 + jnp.exp(-jnp.abs(x)))


def _iota(shape, dim):
    return lax.broadcasted_iota(jnp.int32, shape, dim)


def _in_proj_kernel(x_ref, g_ref, w_ref, za_ref, zb_ref, zc_ref, zd_ref, zs_ref):
    x = x_ref[...]
    ms = jnp.mean(x * x, axis=-1, keepdims=True)
    xn = (x * lax.rsqrt(ms + EPS) * g_ref[...]).astype(BF16)
    off = 0
    for ref, w in zip((za_ref, zb_ref, zc_ref, zd_ref, zs_ref), Z_SPLITS):
        ref[...] = _dot(xn, w_ref[:, off:off + w]).astype(ref.dtype)
        off += w


def _in_proj(x2d, g, w_perm, tm=256):
    t = x2d.shape[0]
    outs = tuple(jax.ShapeDtypeStruct((t, w), F32 if w == ZS_W else BF16) for w in Z_SPLITS)
    return pl.pallas_call(
        _in_proj_kernel,
        grid=(t // tm,),
        in_specs=[pl.BlockSpec((tm, D_MODEL), lambda i: (i, 0)),
                  pl.BlockSpec((1, D_MODEL), lambda i: (0, 0)),
                  pl.BlockSpec((D_MODEL, Z_TOTAL), lambda i: (0, 0))],
        out_specs=tuple(pl.BlockSpec((tm, w), lambda i: (i, 0)) for w in Z_SPLITS),
        out_shape=outs,
        compiler_params=_cparams("parallel"),
        name="in_proj",
    )(x2d, g.reshape(1, D_MODEL), w_perm)


def _permute_w_in(w_in):
    sizes = (256, 256, 128, 128, 256, 256, 16, 16, 256, 256, 256, 256, 16, 768)
    offs = np.concatenate([[0], np.cumsum(sizes)])
    seg = [w_in[:, offs[i]:offs[i + 1]] for i in range(len(sizes))]
    (a_x, a_g, b_q, b_k, b_v, b_r, b_lf, b_lb, c_q, c_k, c_v, c_o, c_g, d_u) = seg
    small = jnp.concatenate([b_lf, b_lb, c_g, jnp.zeros((w_in.shape[0], ZS_W - 48), w_in.dtype)], axis=1)
    return jnp.concatenate([a_x, a_g, b_q, b_k, b_v, b_r, c_q, c_k, c_v, c_o, d_u, small], axis=1).astype(BF16)


HALO = 16


def _strided_pitch(n):
    assert n % V7X_SUBLANES == 0
    return n if (n // V7X_SUBLANES) % 2 == 1 else n + V7X_SUBLANES


def _halo_conv(prev, cur, nxt, w, b, lo, first, last):
    tl = cur.shape[0]
    prev = jnp.where(first, 0.0, prev.astype(F32))
    nxt = jnp.where(last, 0.0, nxt.astype(F32))
    ext = jnp.concatenate([prev, cur.astype(F32), nxt], axis=0)
    n = tl + 2 * HALO
    acc = None
    for j in range(w.shape[0]):
        o = lo + j
        sh = ext if o == 0 else pltpu.roll(ext, (-o) % n, 0)
        term = sh[HALO:HALO + tl] * w[j:j + 1, :]
        acc = term if acc is None else acc + term
    return acc + b


def _rglru_kernel(prev_ref, cur_ref, next_ref, cw_ref, cb_ref, gw_ref, gb_ref, sp_ref, out_ref,
                  a_s, b_s, h_s, p_s, carry, *, tl, reverse):
    i = pl.program_id(1)
    nt = pl.num_programs(1)
    ti = (nt - 1 - i) if reverse else i
    u = _halo_conv(prev_ref[0], cur_ref[0], next_ref[0], cw_ref[...], cb_ref[...], -2, ti == 0, ti == nt - 1)
    gates = _sigmoid(_dot(u.astype(BF16), gw_ref[...]) + gb_ref[...])
    r = gates[:, :GROUP_W]
    ig = gates[:, GROUP_W:]
    log_a = (-RG_C) * r * sp_ref[...]
    a = jnp.exp(log_a)
    bt = jnp.sqrt(1.0 - a * a) * (ig * u)
    nseg = RG_SEGMENTS
    m = tl // nseg
    mp, sp = _strided_pitch(m), _strided_pitch(nseg)
    for k in range(2):
        for seg in range(nseg):
            a_s[k, seg * mp:seg * mp + m, :] = a[seg * m:(seg + 1) * m, k * V7X_LANES:(k + 1) * V7X_LANES]
            b_s[k, seg * mp:seg * mp + m, :] = bt[seg * m:(seg + 1) * m, k * V7X_LANES:(k + 1) * V7X_LANES]

    @pl.when(i == 0)
    def _():
        carry[...] = jnp.zeros_like(carry)

    def body(jj, hp):
        j = (m - 1 - jj) if reverse else jj
        out = []
        for k in range(2):
            h, p = hp[2 * k], hp[2 * k + 1]
            av = a_s[k, pl.ds(j, nseg, stride=mp), :]
            bv = b_s[k, pl.ds(j, nseg, stride=mp), :]
            h = av * h + bv
            p = av * p
            rows = pl.ds(pl.multiple_of(j * sp, V7X_SUBLANES), nseg)
            h_s[k, rows, :] = h
            p_s[k, rows, :] = p
            out += [h, p]
        return tuple(out)

    z8 = jnp.zeros((nseg, V7X_LANES), F32)
    o8 = jnp.ones((nseg, V7X_LANES), F32)
    ends = lax.fori_loop(0, m, body, (z8, o8, z8, o8), unroll=RG_UNROLL)
    for k in range(2):
        h_end, p_end = ends[2 * k], ends[2 * k + 1]
        c = carry[k]
        cin = [None] * nseg
        for seg in (range(nseg - 1, -1, -1) if reverse else range(nseg)):
            cin[seg] = c
            c = h_end[seg:seg + 1, :] + p_end[seg:seg + 1, :] * c
        carry[k] = c
        for seg in range(nseg):
            hv = h_s[k, pl.ds(seg, m, stride=sp), :]
            pv = p_s[k, pl.ds(seg, m, stride=sp), :]
            out_ref[0, seg * m:(seg + 1) * m, k * V7X_LANES:(k + 1) * V7X_LANES] = (hv + pv * cin[seg]).astype(out_ref.dtype)


def _rglru(za, conv_w, conv_b, gate_w_bd, gate_b, softplus_neg_lam, reverse, tl=512):
    b, l, _ = za.shape
    nt = l // tl
    rh = tl // HALO
    nbh = l // HALO

    def tmap(bi, i):
        return (nt - 1 - i) if reverse else i

    kern = functools.partial(_rglru_kernel, tl=tl, reverse=reverse)
    return pl.pallas_call(
        kern,
        grid=(b, nt),
        in_specs=[
            pl.BlockSpec((1, HALO, GROUP_W), lambda bi, i: (bi, jnp.maximum(tmap(bi, i) * rh - 1, 0), 0)),
            pl.BlockSpec((1, tl, GROUP_W), lambda bi, i: (bi, tmap(bi, i), 0)),
            pl.BlockSpec((1, HALO, GROUP_W), lambda bi, i: (bi, jnp.minimum((tmap(bi, i) + 1) * rh, nbh - 1), 0)),
            pl.BlockSpec((4, GROUP_W), lambda bi, i: (0, 0)),
            pl.BlockSpec((1, GROUP_W), lambda bi, i: (0, 0)),
            pl.BlockSpec((GROUP_W, 2 * GROUP_W), lambda bi, i: (0, 0)),
            pl.BlockSpec((1, 2 * GROUP_W), lambda bi, i: (0, 0)),
            pl.BlockSpec((1, GROUP_W), lambda bi, i: (0, 0)),
        ],
        out_specs=pl.BlockSpec((1, tl, GROUP_W), lambda bi, i: (bi, tmap(bi, i), 0)),
        out_shape=jax.ShapeDtypeStruct((b, l, GROUP_W), BF16),
        scratch_shapes=[pltpu.VMEM((2, RG_SEGMENTS * _strided_pitch(tl // RG_SEGMENTS), V7X_LANES), F32)] * 2
        + [pltpu.VMEM((2, (tl // RG_SEGMENTS) * _strided_pitch(RG_SEGMENTS), V7X_LANES), F32)] * 2
        + [pltpu.VMEM((2, 1, V7X_LANES), F32)],
        compiler_params=_cparams("parallel", "arbitrary"),
        name="rglru_bwd" if reverse else "rglru_fwd",
    )(za, za, za, conv_w, conv_b, gate_w_bd, gate_b, softplus_neg_lam)


def _block_diag(blocks):
    h, di, do = blocks.shape
    eye = jnp.eye(h, dtype=blocks.dtype)
    return jnp.einsum('hde,hg->hdge', blocks, eye).reshape(h * di, h * do)


def _chunk_masks(reverse):
    r = np.arange(GROUP_W)
    t = np.arange(CHUNK)
    tri = (t[None, :] >= t[:, None]) if reverse else (t[None, :] <= t[:, None])
    cmask = np.tile(tri, (1, N_HEADS))
    bd = (r[:, None] // HEAD_W) == (r[None, :] // HEAD_W)
    kd = (r[:, None] // HEAD_W) == (np.arange(N_HEADS * B_DK)[None, :] // B_DK)
    i2 = np.tile(np.eye(CHUNK, dtype=bool), (1, N_HEADS))
    hm = np.zeros((8, GROUP_W), bool)
    hm[:N_HEADS] = np.arange(N_HEADS)[:, None] == (r[None, :] // HEAD_W)
    return dict(tri=jnp.asarray(tri, BF16), cmask=jnp.asarray(cmask, F32), bd=jnp.asarray(bd, BF16),
                kd=jnp.asarray(kd, BF16), kdf=jnp.asarray(kd, F32),
                i2=jnp.asarray(i2, F32), hm=jnp.asarray(hm, F32))


def _const_spec(arr):
    nd = arr.ndim
    return pl.BlockSpec(arr.shape, lambda *_: (0,) * nd)


def _gla_kernel(zb_ref, zs_ref, gw_ref, gb_ref, tri_ref, kd_ref, kdf_ref, bd_ref, cmask_ref, out_ref, st_ref,
                *, tl, reverse):
    @pl.when(pl.program_id(1) == 0)
    def _():
        st_ref[...] = jnp.zeros_like(st_ref)

    la = _log_sigmoid(_dot(zs_ref[0].astype(BF16), gw_ref[...]) + gb_ref[...]) * (1.0 / GLA_TAU)
    causal = cmask_ref[...] > 0.0
    nch = tl // CHUNK
    for c in (range(nch - 1, -1, -1) if reverse else range(nch)):
        rows = slice(c * CHUNK, (c + 1) * CHUNK)
        q = zb_ref[0, rows, 0:128].astype(F32) * (B_DK ** -0.5)
        k = zb_ref[0, rows, 128:256].astype(F32)
        vb = zb_ref[0, rows, 256:512]
        la_c = la[rows]
        bcum = _dot_split(tri_ref[...], la_c)
        btot = jnp.sum(la_c, axis=0, keepdims=True)
        q_in = (q * jnp.exp(bcum)).astype(BF16)
        k_in = (k * jnp.exp(-bcum)).astype(BF16)
        k_st = (k * jnp.exp(btot - bcum)).astype(BF16)
        kexp = jnp.concatenate([k_in] * N_HEADS, axis=0) * kd_ref[...]
        att = jnp.where(causal, _dot_nt(q_in, kexp), 0.0)
        vbd = jnp.concatenate([vb] * N_HEADS, axis=0) * bd_ref[...]
        st = st_ref[...]
        out_ref[0, rows, :] = (_dot(att.astype(BF16), vbd) + _dot_nt(q_in, st.astype(BF16))).astype(out_ref.dtype)
        st_ref[...] = st * jnp.exp(btot) + _dot_tn(vb, k_st) * kdf_ref[...]


def _gla(zb, zs, gate_w_pad, gate_b, reverse, tl=512):
    b, l, _ = zb.shape
    nt = l // tl
    mk = _chunk_masks(reverse)
    consts = (gate_w_pad, gate_b, mk['tri'], mk['kd'], mk['kdf'], mk['bd'], mk['cmask'])

    def tmap(bi, i):
        return (bi, (nt - 1 - i) if reverse else i, 0)

    return pl.pallas_call(
        functools.partial(_gla_kernel, tl=tl, reverse=reverse),
        grid=(b, nt),
        in_specs=[pl.BlockSpec((1, tl, ZB_W), tmap), pl.BlockSpec((1, tl, ZS_W), tmap)]
        + [_const_spec(c) for c in consts],
        out_specs=pl.BlockSpec((1, tl, GROUP_W), tmap),
        out_shape=jax.ShapeDtypeStruct((b, l, GROUP_W), BF16),
        scratch_shapes=[pltpu.VMEM((GROUP_W, N_HEADS * B_DK), F32)],
        compiler_params=_cparams("parallel", "arbitrary"),
        name="gla_bwd" if reverse else "gla_fwd",
    )(zb, zs, *consts)


def _mlstm_kernel(zc_ref, zs_ref, e_ref, eb_ref, tri_ref, i2_ref, bd_ref, cmask_ref, hm_ref, out_ref,
                  s_ref, n_ref, m_ref, *, tl, reverse):
    @pl.when(pl.program_id(1) == 0)
    def _():
        s_ref[...] = jnp.zeros_like(s_ref)
        n_ref[...] = jnp.zeros_like(n_ref)
        m_ref[...] = jnp.full_like(m_ref, NEG_BIG)

    ge = _split_dot(zs_ref[0], e_ref[...]) + eb_ref[...]
    ig_all = ge[:, :GROUP_W]
    lf_all = _log_sigmoid(ge[:, GROUP_W:])
    causal = cmask_ref[...] > 0.0
    ones_b = jnp.ones((CHUNK, GROUP_W), BF16)
    nch = tl // CHUNK
    for c in (range(nch - 1, -1, -1) if reverse else range(nch)):
        rows = slice(c * CHUNK, (c + 1) * CHUNK)
        qb = zc_ref[0, rows, 0:256]
        k = zc_ref[0, rows, 256:512].astype(F32) * (HEAD_W ** -0.5)
        kb = k.astype(BF16)
        vb = zc_ref[0, rows, 512:768]
        ig = ig_all[rows]
        lf = lf_all[rows]
        bcum = _dot_split(tri_ref[...], lf)
        blast = jnp.sum(lf, axis=0, keepdims=True)
        cc = jnp.sum((ig - bcum) * i2_ref[...], axis=0, keepdims=True)
        d2 = jnp.where(causal, bcum + cc, NEG_BIG)
        m_intra = jnp.full((CHUNK, GROUP_W), NEG_BIG, F32)
        for h in range(N_HEADS):
            hsel = hm_ref[h:h + 1, :] > 0.0
            mh = jnp.max(jnp.where(hsel, d2, NEG_BIG), axis=1, keepdims=True)
            m_intra = jnp.where(hsel, mh, m_intra)
        m_prev = m_ref[...]
        inter_log = bcum + m_prev
        m_t = jnp.maximum(inter_log, m_intra)
        w_intra = jnp.exp(d2 - m_t)
        w_inter = jnp.exp(inter_log - m_t)
        kexp = jnp.concatenate([kb] * N_HEADS, axis=0) * bd_ref[...]
        qk = (_dot_nt(qb, kexp) * w_intra).astype(BF16)
        vbd = jnp.concatenate([vb] * N_HEADS, axis=0) * bd_ref[...]
        s_prev = s_ref[...]
        n_prev = n_ref[...]
        sbd = jnp.concatenate([s_prev.astype(BF16)] * N_HEADS, axis=0) * bd_ref[...]
        nbd = jnp.concatenate([n_prev.astype(BF16)] * N_HEADS, axis=0) * bd_ref[...]
        num = _dot(qk, vbd) + w_inter * _dot(qb, sbd)
        den = _dot(qk, bd_ref[...]) + w_inter * _dot(qb, nbd)
        out_ref[0, rows, :] = (num / jnp.maximum(jnp.abs(den), jnp.exp(-m_t))).astype(out_ref.dtype)
        g_end = blast - bcum + ig
        g_max = jnp.max(g_end, axis=0, keepdims=True)
        kw = (k * jnp.exp(g_end - g_max)).astype(BF16)
        m_new = jnp.maximum(blast + m_prev, g_max)
        a = jnp.exp(blast + m_prev - m_new)
        cf = jnp.exp(g_max - m_new)
        uc_full = _dot_tn(kw, vb)
        un_full = _dot_tn(kw, ones_b)
        uc = jnp.zeros((HEAD_W, GROUP_W), F32)
        un = jnp.zeros((HEAD_W, GROUP_W), F32)
        for h in range(N_HEADS):
            hsel = hm_ref[h:h + 1, :] > 0.0
            blk = slice(h * HEAD_W, (h + 1) * HEAD_W)
            uc = jnp.where(hsel, uc_full[blk], uc)
            un = jnp.where(hsel, un_full[blk], un)
        s_ref[...] = a * s_prev + cf * uc
        n_ref[...] = a * n_prev + cf * un
        m_ref[...] = m_new


def _mlstm(zc, zs, expand_w, expand_b, reverse, tl=512):
    b, l, _ = zc.shape
    nt = l // tl
    mk = _chunk_masks(reverse)
    consts = (expand_w, expand_b, mk['tri'], mk['i2'], mk['bd'], mk['cmask'], mk['hm'])

    def tmap(bi, i):
        return (bi, (nt - 1 - i) if reverse else i, 0)

    return pl.pallas_call(
        functools.partial(_mlstm_kernel, tl=tl, reverse=reverse),
        grid=(b, nt),
        in_specs=[pl.BlockSpec((1, tl, ZC_W), tmap), pl.BlockSpec((1, tl, ZS_W), tmap)]
        + [_const_spec(c) for c in consts],
        out_specs=pl.BlockSpec((1, tl, GROUP_W), tmap),
        out_shape=jax.ShapeDtypeStruct((b, l, GROUP_W), BF16),
        scratch_shapes=[pltpu.VMEM((HEAD_W, GROUP_W), F32), pltpu.VMEM((HEAD_W, GROUP_W), F32),
                        pltpu.VMEM((1, GROUP_W), F32)],
        compiler_params=_cparams("parallel", "arbitrary"),
        name="mlstm_bwd" if reverse else "mlstm_fwd",
    )(zc, zs, *consts)


def _mixer_prep(P, l):
    out = {}
    gw, gb, lam = P['a_gate_w'][l], P['a_gate_b'][l], P['a_lambda'][l]
    for d in range(2):
        w = jnp.concatenate([_block_diag(gw[d, 0]), _block_diag(gw[d, 1])], axis=1).astype(BF16)
        bb = jnp.concatenate([gb[d, 0].reshape(1, -1), gb[d, 1].reshape(1, -1)], axis=1).astype(F32)
        nl = -lam[d].astype(F32)
        sp = (jnp.maximum(nl, 0.0) + jnp.log(1.0 + jnp.exp(-jnp.abs(nl)))).reshape(1, -1)
        out['a', d] = (w, bb, sp)
        bw = jnp.zeros((ZS_W, N_HEADS * B_DK), F32).at[B_RANK * d:B_RANK * (d + 1)].set(P['b_gate_w'][l][d])
        out['b', d] = (bw.astype(BF16), P['b_gate_b'][l][d].reshape(1, -1).astype(F32))
        e = np.zeros((ZS_W, 2 * GROUP_W), np.float32)
        for h in range(N_HEADS):
            e[32 + 8 * d + h, h * HEAD_W:(h + 1) * HEAD_W] = 1.0
            e[36 + 8 * d + h, GROUP_W + h * HEAD_W:GROUP_W + (h + 1) * HEAD_W] = 1.0
        cb = P['c_gate_b'][l].astype(F32)
        eb = jnp.concatenate([jnp.repeat(cb[2 * d], HEAD_W), jnp.repeat(cb[2 * d + 1], HEAD_W)]).reshape(1, -1)
        out['c', d] = (jnp.asarray(e, BF16), eb)
    return out


G_ROWS = 8
G_ROWS_BF16 = 16


def _slab(k):
    return slice(k * V7X_LANES, (k + 1) * V7X_LANES)


def _scatter_rows(scr, base, val, rows, pitch, blk0=0):
    for s in range(val.shape[1] // V7X_LANES):
        for b in range(val.shape[0] // rows):
            r0 = (blk0 + b) * pitch
            scr[base + s, r0:r0 + rows, :] = val[b * rows:(b + 1) * rows, _slab(s)]


def _gather_n2_major(scr, outs, n2, pitch):
    nsl = GROUP_W // V7X_LANES

    def body(j, carry):
        for a, (ref, lead) in enumerate(outs):
            for s in range(nsl):
                ref[lead, j, :, _slab(s)] = scr[a * nsl + s, pl.ds(j, G_ROWS, stride=pitch), :]
        return carry

    lax.fori_loop(0, n2, body, 0, unroll=4)


def _conv3_kernel(prev_ref, cur_ref, next_ref, w_ref, b_ref, v_ref, x1_ref, x2_ref, scr, *, n2, pitch):
    i = pl.program_id(1)
    u = _halo_conv(prev_ref[0], cur_ref[0], next_ref[0], w_ref[...], b_ref[...], -1, i == 0,
                   i == pl.num_programs(1) - 1)
    nsl = GROUP_W // V7X_LANES
    for a in range(3):
        _scatter_rows(scr, a * nsl, u[:, a * GROUP_W:(a + 1) * GROUP_W], n2, pitch)
    _gather_n2_major(scr, ((v_ref, 0), (x1_ref, 0), (x2_ref, 0)), n2, pitch)


def _conv3_split(zd, w, b, n1, n2):
    bsz, l, _ = zd.shape
    tl = G_ROWS * n2
    pitch = _strided_pitch(n2)
    nt, rh, nbh = l // tl, tl // HALO, l // HALO
    o = jax.ShapeDtypeStruct((bsz, n2, n1 // 2, GROUP_W), F32)
    ospec = pl.BlockSpec((1, n2, G_ROWS, GROUP_W), lambda bi, i: (bi, 0, i, 0))
    return pl.pallas_call(
        functools.partial(_conv3_kernel, n2=n2, pitch=pitch),
        grid=(bsz, nt),
        in_specs=[pl.BlockSpec((1, HALO, ZD_W), lambda bi, i: (bi, jnp.maximum(i * rh - 1, 0), 0)),
                  pl.BlockSpec((1, tl, ZD_W), lambda bi, i: (bi, i, 0)),
                  pl.BlockSpec((1, HALO, ZD_W), lambda bi, i: (bi, jnp.minimum((i + 1) * rh, nbh - 1), 0)),
                  pl.BlockSpec((3, ZD_W), lambda bi, i: (0, 0)),
                  pl.BlockSpec((1, ZD_W), lambda bi, i: (0, 0))],
        out_specs=(ospec, ospec, ospec),
        out_shape=(o, o, o),
        scratch_shapes=[pltpu.VMEM((3 * GROUP_W // V7X_LANES, G_ROWS * pitch, V7X_LANES), F32)],
        compiler_params=_cparams("parallel", "parallel"),
        name="hyena_conv3",
    )(zd, zd, zd, w, b.reshape(1, ZD_W))


def _dot_hp(a, b):
    return jnp.dot(a, b, preferred_element_type=F32, precision=lax.Precision.HIGHEST)


FILTER_SUB = 512


def _filter_mlp_kernel(emb_ref, w1_ref, b1_ref, f0_ref, w2_ref, b2_ref, f1_ref, w3_ref, dl_ref,
                       fwd_ref, sec_ref, s_ref, scr, *, n2, pitch):
    i = pl.program_id(0)
    tl = emb_ref.shape[0]
    nsl = GROUP_W // V7X_LANES

    @pl.when(i == 0)
    def _():
        s_ref[...] = jnp.zeros_like(s_ref)

    for sub in range(tl // FILTER_SUB):
        emb = emb_ref[sub * FILTER_SUB:(sub + 1) * FILTER_SUB, :]
        h = jnp.sin(f0_ref[...] * (_dot_hp(emb, w1_ref[...]) + b1_ref[...]))
        h = jnp.sin(f1_ref[...] * (_dot_hp(h, w2_ref[...]) + b2_ref[...]))
        h = _dot_hp(h, w3_ref[...])
        dl = dl_ref[...]
        hf = h[:, :2 * GROUP_W] * jnp.exp(-emb[:, 0:1] * dl)
        hb = h[:, 2 * GROUP_W:] * jnp.exp(-emb[:, HY_EMB:HY_EMB + 1] * dl)
        if sub == 0:
            row = _iota(hb.shape, 0) + i * tl
            hb = jnp.where(row == 0, 0.0, hb)
        s_ref[...] += jnp.sum(jnp.abs(hf) + jnp.abs(hb), axis=0, keepdims=True)
        blk0 = sub * FILTER_SUB // n2
        for o in range(2):
            _scatter_rows(scr, o * nsl, hf[:, o * GROUP_W:(o + 1) * GROUP_W], n2, pitch, blk0)
            _scatter_rows(scr, (2 + o) * nsl, hb[:, o * GROUP_W:(o + 1) * GROUP_W], n2, pitch, blk0)
    _gather_n2_major(scr, ((fwd_ref, 0), (fwd_ref, 1), (sec_ref, 0), (sec_ref, 1)), n2, pitch)


def _hyena_filter(l, w1, b1, freq, w2, b2, w3, n1, n2):
    tl = G_ROWS * n2
    bands = (HY_EMB - 1) // 2
    f = jnp.linspace(1e-4, bands - 1, bands, dtype=F32)

    def embed(pos):
        t = pos / max(l - 1, 1)
        ang = (2.0 * math.pi / l) * pos[:, None] * f[None, :]
        return jnp.concatenate([t[:, None], jnp.cos(ang), -jnp.sin(ang)], axis=-1)

    pos = jnp.arange(l, dtype=F32)
    emb = jnp.concatenate([embed(pos), embed(l - pos), jnp.zeros((l, V7X_LANES - 2 * HY_EMB), F32)], axis=-1)
    deltas = jnp.abs(jnp.linspace(math.log(HY_TARGET) / HY_SLOW, math.log(HY_TARGET) / HY_FAST, GROUP_W, dtype=F32))
    z = lambda r, c: jnp.zeros((r, c), F32)
    w1f, w2f = w1.astype(F32), w2.astype(F32)
    w3r = w3.astype(F32).reshape(HY_HID, 2, 2, GROUP_W)
    w3d = lambda d: w3r[:, :, d].reshape(HY_HID, 2 * GROUP_W)
    w1p = z(V7X_LANES, V7X_LANES).at[:HY_EMB, :HY_HID].set(w1f).at[HY_EMB:2 * HY_EMB, HY_HID:].set(w1f)
    w2p = z(V7X_LANES, V7X_LANES).at[:HY_HID, :HY_HID].set(w2f).at[HY_HID:, HY_HID:].set(w2f)
    w3p = z(V7X_LANES, 4 * GROUP_W).at[:HY_HID, :2 * GROUP_W].set(w3d(0)).at[HY_HID:, 2 * GROUP_W:].set(w3d(1))
    vec = lambda x: jnp.tile(x.astype(F32), 2).reshape(1, V7X_LANES)
    consts = (w1p, vec(b1), vec(freq[0]), w2p, vec(b2), vec(freq[1]), w3p, jnp.tile(deltas, 2).reshape(1, -1))
    pitch = _strided_pitch(n2)
    half = jax.ShapeDtypeStruct((2, n2, n1 // 2, GROUP_W), F32)
    hspec = pl.BlockSpec((2, n2, G_ROWS, GROUP_W), lambda i: (0, 0, i, 0))
    return pl.pallas_call(
        functools.partial(_filter_mlp_kernel, n2=n2, pitch=pitch),
        grid=(l // tl,),
        in_specs=[pl.BlockSpec((tl, V7X_LANES), lambda i: (i, 0))] + [_const_spec(c) for c in consts],
        out_specs=(hspec, hspec, pl.BlockSpec((1, 2 * GROUP_W), lambda i: (0, 0))),
        out_shape=(half, half, jax.ShapeDtypeStruct((1, 2 * GROUP_W), F32)),
        scratch_shapes=[pltpu.VMEM((4 * GROUP_W // V7X_LANES, G_ROWS * pitch, V7X_LANES), F32)],
        compiler_params=_cparams("arbitrary"),
        name="hyena_filter_mlp",
    )(emb, *consts)


def _fft_factors(l):
    return (64, 128) if l == 4096 else (2 * l // 256, 256)


@functools.lru_cache(maxsize=None)
def _fft_tables(n1, n2):
    n = n1 * n2
    k1 = np.arange(n1)[None, :, None]
    m2 = np.arange(n2)[:, None, None]

    def theta(n1_vals):
        idx = (k1 * (n1_vals[None, None, :] * n2 + m2)) % n
        return (2.0 * np.pi / n) * idx
    th = theta(np.arange(n1 // 2))
    fr, fi = np.cos(th), -np.sin(th)
    fwd = np.concatenate([np.concatenate([fr, -fi], -1), np.concatenate([fi, fr], -1)], -2)
    thf = theta(np.arange(n1))
    filt = np.concatenate([np.cos(thf), -np.sin(thf)], -2)
    cr = np.swapaxes(np.cos(th), 1, 2) / n
    ci = np.swapaxes(np.sin(th), 1, 2) / n
    inv = np.concatenate([np.concatenate([cr, -ci], -1), np.concatenate([ci, cr], -1)], -2)
    t2 = (2.0 * np.pi / n2) * ((np.arange(n2)[:, None] * np.arange(n2)[None, :]) % n2)
    gr, gi = np.cos(t2), -np.sin(t2)
    g = np.block([[gr, -gi], [gi, gr]])
    ginv = np.block([[gr, gi], [-gi, gr]])
    return tuple(jnp.asarray(a, BF16) for a in (fwd, filt, inv, g, ginv))


def _regroup_rows(scr, n_rows, pitch, store):
    def body(r, carry):
        for s in range(GROUP_W // V7X_LANES):
            store(r, s, scr[s, pl.ds(r, G_ROWS_BF16, stride=pitch), :])
        return carry

    lax.fori_loop(0, n_rows, body, 0, unroll=4)


def _fft_stage1_kernel(zr_ref, zi_ref, t_ref, a_ref, scr, *, n1, pitch):
    for j in range(G_ROWS_BF16):
        rhs = jnp.concatenate([zr_ref[0, j], zi_ref[0, j]], axis=0).astype(BF16)
        _scatter_rows(scr, 0, _dot(t_ref[j], rhs), 2 * n1, pitch, j)

    def store(r, s, rows):
        a_ref[0, r, :, _slab(s)] = rows.astype(BF16)

    _regroup_rows(scr, 2 * n1, pitch, store)


def _fft_stage1(zr, zi, table, n1, n2, pairs, idx_r, idx_i):
    nb = G_ROWS_BF16
    pitch = _strided_pitch(2 * n1)
    return pl.pallas_call(
        functools.partial(_fft_stage1_kernel, n1=n1, pitch=pitch),
        grid=(pairs, n2 // nb),
        in_specs=[pl.BlockSpec((1, nb, n1 // 2, GROUP_W), lambda p, j: (idx_r(p), j, 0, 0)),
                  pl.BlockSpec((1, nb, n1 // 2, GROUP_W), lambda p, j: (idx_i(p), j, 0, 0)),
                  pl.BlockSpec((nb, 2 * n1, n1), lambda p, j: (j, 0, 0))],
        out_specs=pl.BlockSpec((1, 2 * n1, nb, GROUP_W), lambda p, j: (p, 0, j, 0)),
        out_shape=jax.ShapeDtypeStruct((pairs, 2 * n1, n2, GROUP_W), BF16),
        scratch_shapes=[pltpu.VMEM((GROUP_W // V7X_LANES, nb * pitch, V7X_LANES), F32)],
        compiler_params=_cparams("parallel", "parallel"),
        name="fft_stage1",
    )(zr, zi, table)


def _a_specs(n1, n2, kb, index):
    def spec(part):
        def imap(i, p):
            pp, kk = index(i, p)
            return (pp, part * (n1 // kb) + kk, 0, 0)
        return pl.BlockSpec((1, kb, n2, GROUP_W), imap)
    return [spec(0), spec(1)]


def _fft_spec_kernel(ar_ref, ai_ref, g_ref, s_ref, h_ref, *, kb, n2):
    for k in range(kb):
        rhs = jnp.concatenate([ar_ref[0, k], ai_ref[0, k]], axis=0)
        x = _dot(g_ref[...], rhs) * s_ref[0]
        h_ref[0, k, 0] = x[:n2]
        h_ref[0, k, 1] = x[n2:]


def _fft_filter_spectrum(a, g, inv_s, n1, n2, kb=8):
    orders = a.shape[0]
    return pl.pallas_call(
        functools.partial(_fft_spec_kernel, kb=kb, n2=n2),
        grid=(orders, n1 // kb),
        in_specs=_a_specs(n1, n2, kb, lambda o, i: (o, i))
        + [_const_spec(g), pl.BlockSpec((1, 1, GROUP_W), lambda o, i: (o, 0, 0))],
        out_specs=pl.BlockSpec((1, kb, 2, n2, GROUP_W), lambda o, i: (o, i, 0, 0, 0)),
        out_shape=jax.ShapeDtypeStruct((orders, n1, 2, n2, GROUP_W), F32),
        compiler_params=_cparams("parallel", "parallel"),
        name="fft_filter_spectrum",
    )(a, a, g, inv_s)


def _fft_mid_kernel(ar_ref, ai_ref, g_ref, gi_ref, h_ref, b_ref, scr, *, n2, pitch):
    for k in range(G_ROWS_BF16):
        rhs = jnp.concatenate([ar_ref[0, k], ai_ref[0, k]], axis=0)
        x = _dot(g_ref[...], rhs)
        xr, xi = x[:n2], x[n2:]
        hr, hi = h_ref[0, k, 0], h_ref[0, k, 1]
        y = jnp.concatenate([xr * hr - xi * hi, xr * hi + xi * hr], axis=0).astype(BF16)
        _scatter_rows(scr, 0, _dot(gi_ref[...], y), 2 * n2, pitch, k)

    def store(r, s, rows):
        b_ref[0, r, :, _slab(s)] = rows.astype(BF16)

    _regroup_rows(scr, 2 * n2, pitch, store)


def _fft_mid(a, g, ginv, spec, order, n1, n2):
    pairs = a.shape[0]
    kb = G_ROWS_BF16
    pitch = _strided_pitch(2 * n2)
    return pl.pallas_call(
        functools.partial(_fft_mid_kernel, n2=n2, pitch=pitch),
        grid=(n1 // kb, pairs),
        in_specs=_a_specs(n1, n2, kb, lambda i, p: (p, i)) + [_const_spec(g), _const_spec(ginv),
                  pl.BlockSpec((1, kb, 2, n2, GROUP_W), lambda i, p: (order, i, 0, 0, 0))],
        out_specs=pl.BlockSpec((1, 2 * n2, kb, GROUP_W), lambda i, p: (p, 0, i, 0)),
        out_shape=jax.ShapeDtypeStruct((pairs, 2 * n2, n1, GROUP_W), BF16),
        scratch_shapes=[pltpu.VMEM((GROUP_W // V7X_LANES, kb * pitch, V7X_LANES), F32)],
        compiler_params=_cparams("parallel", "parallel"),
        name="fft_mid",
    )(a, a, g, ginv, spec)


def _fft_stage3_kernel(br_ref, bi_ref, t_ref, ur_ref, ui_ref, gr_ref, gi_ref, sk_ref, y_ref, *scr,
                       n1, pitch, time_major):
    h = n1 // 2
    sk = sk_ref[...]
    for j in range(G_ROWS_BF16):
        rhs = jnp.concatenate([br_ref[0, j], bi_ref[0, j]], axis=0)
        o = _dot(t_ref[j], rhs)
        yr = gr_ref[0, j] * (o[:h] + ur_ref[0, j] * sk)
        yi = gi_ref[0, j] * (o[h:] + ui_ref[0, j] * sk)
        if time_major:
            _scatter_rows(scr[0], 0, yr, h, pitch, j)
            _scatter_rows(scr[1], 0, yi, h, pitch, j)
        else:
            y_ref[0, 0, j] = yr
            y_ref[0, 1, j] = yi
    if time_major:
        for part in range(2):
            def store(r, s, rows, part=part):
                y_ref[0, part, r, :, _slab(s)] = rows.astype(y_ref.dtype)
            _regroup_rows(scr[part], h, pitch, store)


def _fft_stage3(b, table, u, gate, skip, n1, n2, time_major):
    pairs = b.shape[0]
    nb = G_ROWS_BF16
    h = n1 // 2
    pitch = _strided_pitch(h)
    bspec = lambda part: pl.BlockSpec((1, nb, n1, GROUP_W), lambda p, j: (p, part * (n2 // nb) + j, 0, 0))
    even = pl.BlockSpec((1, nb, h, GROUP_W), lambda p, j: (2 * p, j, 0, 0))
    odd = pl.BlockSpec((1, nb, h, GROUP_W), lambda p, j: (2 * p + 1, j, 0, 0))
    if time_major:
        out_spec = pl.BlockSpec((1, 2, h, nb, GROUP_W), lambda p, j: (p, 0, 0, j, 0))
        out_shape = jax.ShapeDtypeStruct((pairs, 2, h, n2, GROUP_W), BF16)
        scratch = [pltpu.VMEM((GROUP_W // V7X_LANES, nb * pitch, V7X_LANES), F32)] * 2
    else:
        out_spec = pl.BlockSpec((1, 2, nb, h, GROUP_W), lambda p, j: (p, 0, j, 0, 0))
        out_shape = jax.ShapeDtypeStruct((pairs, 2, n2, h, GROUP_W), F32)
        scratch = []
    y = pl.pallas_call(
        functools.partial(_fft_stage3_kernel, n1=n1, pitch=pitch, time_major=time_major),
        grid=(pairs, n2 // nb),
        in_specs=[bspec(0), bspec(1), pl.BlockSpec((nb, n1, 2 * n1), lambda p, j: (j, 0, 0)),
                  even, odd, even, odd, pl.BlockSpec((1, GROUP_W), lambda p, j: (0, 0))],
        out_specs=out_spec,
        out_shape=out_shape,
        scratch_shapes=scratch,
        compiler_params=_cparams("parallel", "parallel"),
        name="fft_stage3_out" if time_major else "fft_stage3",
    )(b, b, table, u, u, gate, gate, skip.reshape(1, GROUP_W).astype(F32))
    return y.reshape(2 * pairs, h * n2, GROUP_W) if time_major else y.reshape(2 * pairs, n2, h, GROUP_W)


def _hyena_spectrum(l, w1, b1, freq, w2, b2, w3):
    n1, n2 = _fft_factors(l)
    _, t_filt, _, g, _ = _fft_tables(n1, n2)
    first, second, sums = _hyena_filter(l, w1, b1, freq, w2, b2, w3, n1, n2)
    a = _fft_stage1(first, second, t_filt, n1, n2, 2, lambda p: p, lambda p: p)
    inv_s = (1.0 / sums).reshape(2, 1, GROUP_W)
    return _fft_filter_spectrum(a, g, inv_s, n1, n2)


def _hyena_long_conv(u, gate, spec, order, skip, n1, n2, time_major):
    t_fwd, _, t_inv, g, ginv = _fft_tables(n1, n2)
    a = _fft_stage1(u, u, t_fwd, n1, n2, u.shape[0] // 2, lambda p: 2 * p, lambda p: 2 * p + 1)
    b = _fft_mid(a, g, ginv, spec, order, n1, n2)
    return _fft_stage3(b, t_inv, u, gate, skip, n1, n2, time_major)


def _mixer_hyena(zd, P, l, spec):
    n1, n2 = _fft_factors(zd.shape[1])
    v, x1, x2 = _conv3_split(zd, P['d_conv_w'][l].astype(F32), P['d_conv_b'][l].astype(F32), n1, n2)
    z = _hyena_long_conv(v, x1, spec, 0, P['d_skip'][l][0], n1, n2, False)
    return _hyena_long_conv(z, x2, spec, 1, P['d_skip'][l][1], n1, n2, True)


def _gelu_tanh(x):
    return 0.5 * x * (1.0 + jnp.tanh(math.sqrt(2.0 / math.pi) * (x + 0.044715 * (x * x * x))))


def _head_rms(h, bd):
    ss = _dot((h * h).astype(BF16), bd) * (1.0 / HEAD_W)
    return h * lax.rsqrt(ss + EPS)


def _out_proj_kernel(x_ref, af_ref, ab_ref, ag_ref, bf_ref, bb_ref, br_ref, cf_ref, cb_ref, co_ref, yd_ref,
                     w_ref, bg_ref, cg_ref, bd_ref, n2_ref, rwh_ref, rwl_ref, rb_ref,
                     xo_ref, hn_ref, lg_ref):
    bd = bd_ref[...]
    f = lambda ref: ref[...].astype(F32)
    ya = (f(af_ref) + f(ab_ref)) * _gelu_tanh(f(ag_ref))
    r = f(br_ref)
    yb = _head_rms(f(bf_ref) + f(bb_ref), bd) * bg_ref[...] * (r * _sigmoid(r))
    yc = _head_rms(f(cf_ref) + f(cb_ref), bd) * cg_ref[...] * _sigmoid(f(co_ref))
    mixed = jnp.concatenate([ya.astype(BF16), yb.astype(BF16), yc.astype(BF16), yd_ref[...]], axis=-1)
    x = x_ref[...] + _dot(mixed, w_ref[...])
    xo_ref[...] = x
    hn = x * lax.rsqrt(jnp.mean(x * x, axis=-1, keepdims=True) + EPS) * n2_ref[...]
    hn_ref[...] = hn.astype(BF16)
    hi = hn.astype(BF16)
    lo = (hn - hi.astype(F32)).astype(BF16)
    lg_ref[...] = _dot(hi, rwh_ref[...]) + _dot(lo, rwh_ref[...]) + _dot(hi, rwl_ref[...]) + rb_ref[...]


def _out_proj(x2d, af, ab, za, bf, bb, zb, cf, cb, zc, yd, w_out, b_norm_g, c_norm_g, norm2_g, rw, rb, tm=256):
    t = x2d.shape[0]
    bd = _chunk_masks(False)['bd']
    rwh = rw.astype(BF16)
    rwl = (rw - rwh.astype(F32)).astype(BF16)
    row = lambda w, j=0: pl.BlockSpec((tm, w), lambda i: (i, j))
    consts = (w_out, b_norm_g.reshape(1, -1).astype(F32), c_norm_g.reshape(1, -1).astype(F32), bd,
              norm2_g.reshape(1, -1).astype(F32), rwh, rwl, rb)
    return pl.pallas_call(
        _out_proj_kernel,
        grid=(t // tm,),
        in_specs=[row(D_MODEL), row(GROUP_W), row(GROUP_W), row(GROUP_W, 1), row(GROUP_W), row(GROUP_W),
                  row(GROUP_W, 2), row(GROUP_W), row(GROUP_W), row(GROUP_W, 3), row(GROUP_W)]
        + [_const_spec(c) for c in consts],
        out_specs=(row(D_MODEL), row(D_MODEL), row(V7X_LANES)),
        out_shape=(jax.ShapeDtypeStruct((t, D_MODEL), F32), jax.ShapeDtypeStruct((t, D_MODEL), BF16),
                   jax.ShapeDtypeStruct((t, V7X_LANES), F32)),
        compiler_params=_cparams("parallel"),
        name="out_proj",
    )(x2d, af, ab, za, bf, bb, zb, cf, cb, zc, yd, *consts)


def _route(logits):
    lane = _iota(logits.shape, 1).astype(F32)
    is_g = lane < MOE_GROUPS
    gl = jnp.where(is_g, logits, NEG_BIG)
    gmax = jnp.max(gl, axis=1, keepdims=True)
    gidx = jnp.min(jnp.where(gl == gmax, lane, 1e9), axis=1, keepdims=True)
    gprob = 1.0 / jnp.sum(jnp.where(is_g, jnp.exp(gl - gmax), 0.0), axis=1, keepdims=True)
    lo = MOE_GROUPS + MOE_PER_GROUP * gidx
    el = jnp.where((lane >= lo) & (lane < lo + MOE_PER_GROUP), logits, NEG_BIG)
    v1 = jnp.max(el, axis=1, keepdims=True)
    i1 = jnp.min(jnp.where(el == v1, lane, 1e9), axis=1, keepdims=True)
    el2 = jnp.where(lane == i1, NEG_BIG, el)
    v2 = jnp.max(el2, axis=1, keepdims=True)
    i2 = jnp.min(jnp.where(el2 == v2, lane, 1e9), axis=1, keepdims=True)
    e21 = jnp.exp(v2 - v1)
    p1 = 1.0 / (1.0 + e21)
    return jnp.where(lane == i1, p1 * gprob, 0.0) + jnp.where(lane == i2, e21 * p1 * gprob, 0.0), gidx


MOE_TB = 1024
MOE_RT = 128
MOE_SB = MOE_TB + MOE_GROUPS * MOE_RT


def _moe_kernel(x_ref, hn_ref, lg_ref, tril_ref, wg_ref, wu_ref, wd_ref, fg_ref, o_ref,
                xs_ref, ys_ref, gs_ref, pos_ref, meta_ref, *, final):
    e = pl.program_id(1)
    grp = e // MOE_PER_GROUP
    tb, sb, rt = MOE_TB, MOE_SB, MOE_RT

    @pl.when(e == 0)
    def _():
        gates, gidx = _route(lg_ref[...])
        lane = _iota((tb, V7X_LANES), 1).astype(F32)
        ohg = jnp.where(lane == gidx, 1.0, 0.0)
        rank = jnp.sum(_dot(tril_ref[...], ohg.astype(BF16)) * ohg, axis=1, keepdims=True)
        cnt = jnp.sum(ohg, axis=0, keepdims=True)
        pc = jnp.ceil(cnt * (1.0 / rt)) * rt
        lane1 = _iota((1, V7X_LANES), 1)
        off = jnp.zeros((1, V7X_LANES), F32)
        run = jnp.zeros((1, 1), F32)
        for g in range(MOE_GROUPS):
            off = jnp.where(lane1 == g, run, off)
            run = run + jnp.sum(jnp.where(lane1 == g, pc, 0.0), axis=1, keepdims=True)
        pos = jnp.sum(ohg * off, axis=1, keepdims=True) + rank
        pos_ref[...] = jnp.broadcast_to(pos, (tb, V7X_LANES))
        meta_ref[0:1, :] = off
        meta_ref[1:2, :] = pc
        hi = jnp.floor(pos * (1.0 / 64.0))
        lo = pos - 64.0 * hi
        pm = jnp.where(lane == 0.0, hi, jnp.where(lane == 1.0, lo, 0.0)).astype(BF16)
        sel = jnp.where(_iota((8, V7X_LANES), 0) == _iota((8, V7X_LANES), 1), 1.0, 0.0).astype(BF16)
        rows8 = _dot_nt(sel, pm)
        pos_row = 64.0 * rows8[0:1, :] + rows8[1:2, :]
        p = jnp.where(_iota((sb, tb), 0).astype(F32) == pos_row, 1.0, 0.0).astype(BF16)
        xs_ref[...] = _dot(p, hn_ref[...]).astype(BF16)
        gh = gates.astype(BF16)
        gl = (gates - gh.astype(F32)).astype(BF16)
        gs_ref[...] = _dot(p, gh) + _dot(p, gl)
        ys_ref[...] = jnp.zeros_like(ys_ref)

    lane1 = _iota((1, V7X_LANES), 1)
    start = jnp.sum(jnp.where(lane1 == grp, meta_ref[0:1, :], 0.0)).astype(jnp.int32)
    rows = jnp.sum(jnp.where(lane1 == grp, meta_ref[1:2, :], 0.0)).astype(jnp.int32)
    lane_t = _iota((rt, V7X_LANES), 1)

    def tile(i, carry):
        r0 = pl.multiple_of(start + i * rt, rt)
        xt = xs_ref[pl.ds(r0, rt), :]
        g = _dot(xt, wg_ref[0])
        a = (g * _sigmoid(g)) * _dot(xt, wu_ref[0])
        y = _dot(a.astype(BF16), wd_ref[0])
        w = jnp.sum(jnp.where(lane_t == MOE_GROUPS + e, gs_ref[pl.ds(r0, rt), :], 0.0), axis=1, keepdims=True)
        ys_ref[pl.ds(r0, rt), :] += w * y
        return carry

    lax.fori_loop(0, rows // rt, tile, 0)

    @pl.when(e == MOE_EXPERTS - 1)
    def _():
        pt = jnp.where(_iota((tb, sb), 1).astype(F32) == pos_ref[:, 0:1], 1.0, 0.0).astype(BF16)
        x = x_ref[...] + _dot(pt, ys_ref[...].astype(BF16))
        if final:
            x = x * lax.rsqrt(jnp.mean(x * x, axis=-1, keepdims=True) + EPS) * fg_ref[...]
        o_ref[...] = x


def _moe(x2d, hn, logits, w_gate, w_up, w_down, final_g, final):
    t = x2d.shape[0]
    tm = MOE_TB
    tril = jnp.asarray(np.tril(np.ones((tm, tm), np.float32), -1), BF16)
    row = lambda w: pl.BlockSpec((tm, w), lambda i, e: (i, 0))
    return pl.pallas_call(
        functools.partial(_moe_kernel, final=final),
        grid=(t // tm, MOE_EXPERTS),
        in_specs=[row(D_MODEL), row(D_MODEL), row(V7X_LANES), _const_spec(tril),
                  pl.BlockSpec((1, D_MODEL, D_EXPERT), lambda i, e: (e, 0, 0)),
                  pl.BlockSpec((1, D_MODEL, D_EXPERT), lambda i, e: (e, 0, 0)),
                  pl.BlockSpec((1, D_EXPERT, D_MODEL), lambda i, e: (e, 0, 0)),
                  pl.BlockSpec((1, D_MODEL), lambda i, e: (0, 0))],
        out_specs=row(D_MODEL),
        out_shape=jax.ShapeDtypeStruct((t, D_MODEL), F32),
        scratch_shapes=[pltpu.VMEM((MOE_SB, D_MODEL), BF16), pltpu.VMEM((MOE_SB, D_MODEL), F32),
                        pltpu.VMEM((MOE_SB, V7X_LANES), F32), pltpu.VMEM((tm, V7X_LANES), F32),
                        pltpu.VMEM((8, V7X_LANES), F32)],
        compiler_params=_cparams("parallel", "arbitrary"),
        name="moe",
    )(x2d, hn, logits, tril, w_gate, w_up, w_down, final_g.reshape(1, -1).astype(F32))


def _layer(x2d, bsz, seq, l, P, W, spec, final):
    za, zb, zc, zd, zs = _in_proj(x2d, P['norm1_g'][l].astype(F32), W['w_in'][l])
    r3 = lambda a: a.reshape(bsz, seq, a.shape[-1])
    r2 = lambda a: a.reshape(bsz * seq, a.shape[-1])
    mp = W['mix'][l]
    cw, cb = P['a_conv_w'][l].astype(F32), P['a_conv_b'][l].reshape(1, -1).astype(F32)
    za3, zb3, zc3, zs3 = r3(za), r3(zb), r3(zc), r3(zs)
    af = _rglru(za3, cw, cb, *mp['a', 0], False)
    ab = _rglru(za3, cw, cb, *mp['a', 1], True)
    bf = _gla(zb3, zs3, *mp['b', 0], False)
    bb = _gla(zb3, zs3, *mp['b', 1], True)
    cf = _mlstm(zc3, zs3, *mp['c', 0], False)
    cbk = _mlstm(zc3, zs3, *mp['c', 1], True)
    yd = _mixer_hyena(r3(zd), P, l, spec)
    x_new, hn, logits = _out_proj(x2d, r2(af), r2(ab), za, r2(bf), r2(bb), zb, r2(cf), r2(cbk), zc, r2(yd),
                                  W['w_out'][l], P['b_norm_g'][l], P['c_norm_g'][l], P['norm2_g'][l],
                                  W['router_w'][l], W['router_b'][l])
    return _moe(x_new, hn, logits, W['w_gate'][l], W['w_up'][l], W['w_down'][l], P['final_norm_g'], final)


def _trunk(x, P, W):
    bsz, seq, _ = x.shape
    depth = P['w_in'].shape[0]
    x2d = x.reshape(bsz * seq, D_MODEL)
    for l in range(depth):
        spec = _hyena_spectrum(seq, P['d_ffn_w1'][l], P['d_ffn_b1'][l], P['d_sin_freq'][l], P['d_ffn_w2'][l],
                               P['d_ffn_b2'][l], P['d_ffn_w3'][l])
        x2d = _layer(x2d, bsz, seq, l, P, W, spec, l == depth - 1)
    return x2d.reshape(bsz, seq, D_MODEL)


def _prep_weights(P):
    depth = P['w_in'].shape[0]
    W = {'w_in': [_permute_w_in(P['w_in'][l]) for l in range(depth)],
         'w_out': [P['w_out'][l].astype(BF16) for l in range(depth)],
         'mix': [_mixer_prep(P, l) for l in range(depth)],
         'w_gate': [P['moe_w_gate'][l].astype(BF16) for l in range(depth)],
         'w_up': [P['moe_w_up'][l].astype(BF16) for l in range(depth)],
         'w_down': [P['moe_w_down'][l].astype(BF16) for l in range(depth)],
         'router_w': [], 'router_b': []}
    for l in range(depth):
        rw = jnp.concatenate([P['moe_group_w'][l], P['moe_expert_w'][l]], axis=1).astype(F32)
        rb = jnp.concatenate([P['moe_group_b'][l], P['moe_expert_b'][l]]).astype(F32)
        padc = V7X_LANES - rw.shape[1]
        W['router_w'].append(jnp.pad(rw, ((0, 0), (0, padc))))
        W['router_b'].append(jnp.pad(rb, (0, padc)).reshape(1, V7X_LANES))
    return W


def kernel(x_prompt, x_sample, norm1_g, w_in, a_conv_w, a_conv_b, a_gate_w, a_gate_b, a_lambda, b_gate_w, b_gate_b, b_norm_g, c_gate_b, c_norm_g, d_conv_w, d_conv_b, d_ffn_w1, d_ffn_b1, d_sin_freq, d_ffn_w2, d_ffn_b2, d_ffn_w3, d_skip, w_out, norm2_g, moe_group_w, moe_group_b, moe_expert_w, moe_expert_b, moe_w_gate, moe_w_up, moe_w_down, final_norm_g):
    P = {'norm1_g': norm1_g, 'w_in': w_in, 'a_conv_w': a_conv_w, 'a_conv_b': a_conv_b,
         'a_gate_w': a_gate_w, 'a_gate_b': a_gate_b, 'a_lambda': a_lambda, 'b_gate_w': b_gate_w,
         'b_gate_b': b_gate_b, 'b_norm_g': b_norm_g, 'c_gate_b': c_gate_b, 'c_norm_g': c_norm_g,
         'd_conv_w': d_conv_w, 'd_conv_b': d_conv_b, 'd_ffn_w1': d_ffn_w1, 'd_ffn_b1': d_ffn_b1,
         'd_sin_freq': d_sin_freq, 'd_ffn_w2': d_ffn_w2, 'd_ffn_b2': d_ffn_b2, 'd_ffn_w3': d_ffn_w3,
         'd_skip': d_skip, 'w_out': w_out, 'norm2_g': norm2_g, 'moe_group_w': moe_group_w,
         'moe_group_b': moe_group_b, 'moe_expert_w': moe_expert_w, 'moe_expert_b': moe_expert_b,
         'moe_w_gate': moe_w_gate, 'moe_w_up': moe_w_up, 'moe_w_down': moe_w_down,
         'final_norm_g': final_norm_g}
    W = _prep_weights(P)
    return (_trunk(x_prompt, P, W), _trunk(x_sample, P, W))
```

```python
import functools
import math

import jax
import jax.numpy as jnp
import numpy as np
from jax import lax
from jax.experimental import pallas as pl
from jax.experimental.pallas import tpu as pltpu

F32 = jnp.float32
BF16 = jnp.bfloat16

D_MODEL = 1024
GROUP_W = 256
N_HEADS = 4
HEAD_W = GROUP_W // N_HEADS
B_DK = 32
B_RANK = 16
CHUNK = 64
RG_C = 8.0
RG_SEGMENTS = 16
RG_UNROLL = 4
GLA_TAU = 16.0
HY_EMB = 33
HY_HID = 64
HY_TARGET = 1e-2
HY_FAST = 0.3
HY_SLOW = 1.5
MOE_GROUPS = 4
MOE_PER_GROUP = 4
MOE_EXPERTS = 16
D_EXPERT = 512
EPS = 1e-6
NEG_BIG = -1e30

V7X_LANES = 128
V7X_SUBLANES = 8
VMEM_LIMIT = 56 * 1024 * 1024

ZA_W, ZB_W, ZC_W, ZD_W, ZS_W = 512, 768, 1024, 768, 128
Z_SPLITS = (ZA_W, ZB_W, ZC_W, ZD_W, ZS_W)
Z_TOTAL = sum(Z_SPLITS)


def _cparams(*sem):
    return pltpu.CompilerParams(dimension_semantics=sem, vmem_limit_bytes=VMEM_LIMIT)


def _dot(a, b):
    return jnp.dot(a, b, preferred_element_type=F32)


def _dot_nt(a, b):
    return lax.dot_general(a, b, (((1,), (1,)), ((), ())), preferred_element_type=F32)


def _dot_tn(a, b):
    return lax.dot_general(a, b, (((0,), (0,)), ((), ())), preferred_element_type=F32)


def _dot_split(m_bf16, x):
    hi = x.astype(BF16)
    lo = (x - hi.astype(F32)).astype(BF16)
    return _dot(m_bf16, hi) + _dot(m_bf16, lo)


def _split_dot(x, m_bf16):
    hi = x.astype(BF16)
    lo = (x - hi.astype(F32)).astype(BF16)
    return _dot(hi, m_bf16) + _dot(lo, m_bf16)


def _sigmoid(x):
    return 1.0 / (1.0 + jnp.exp(-x))


def _log_sigmoid(x):
    return jnp.minimum(x, 0.0) - jnp.log(1.0 + jnp.exp(-jnp.abs(x)))


def _iota(shape, dim):
    return lax.broadcasted_iota(jnp.int32, shape, dim)


def _in_proj_kernel(*refs, add):
    if add:
        x_ref, m_ref, g_ref, w_ref, xo_ref, za_ref, zb_ref, zc_ref, zd_ref, zs_ref = refs
        x = x_ref[...] + m_ref[...].astype(F32)
        xo_ref[...] = x
    else:
        x_ref, g_ref, w_ref, za_ref, zb_ref, zc_ref, zd_ref, zs_ref = refs
        x = x_ref[...]
    ms = jnp.mean(x * x, axis=-1, keepdims=True)
    xn = (x * lax.rsqrt(ms + EPS) * g_ref[...]).astype(BF16)
    off = 0
    for ref, w in zip((za_ref, zb_ref, zc_ref, zd_ref, zs_ref), Z_SPLITS):
        ref[...] = _dot(xn, w_ref[:, off:off + w]).astype(ref.dtype)
        off += w


def _in_proj(x2d, moe_prev, g, w_perm, tm=256):
    t = x2d.shape[0]
    add = moe_prev is not None
    row = lambda w: pl.BlockSpec((tm, w), lambda i: (i, 0))
    outs = tuple(jax.ShapeDtypeStruct((t, w), F32 if w == ZS_W else BF16) for w in Z_SPLITS)
    out_specs = tuple(row(w) for w in Z_SPLITS)
    consts = (g.reshape(1, D_MODEL), w_perm)
    res = pl.pallas_call(
        functools.partial(_in_proj_kernel, add=add),
        grid=(t // tm,),
        in_specs=[row(D_MODEL)] * (2 if add else 1) + [_const_spec(c) for c in consts],
        out_specs=((row(D_MODEL),) if add else ()) + out_specs,
        out_shape=((jax.ShapeDtypeStruct((t, D_MODEL), F32),) if add else ()) + outs,
        compiler_params=_cparams("parallel"),
        name="in_proj",
    )(*((x2d, moe_prev) if add else (x2d,)), *consts)
    return tuple(res) if add else (x2d,) + tuple(res)


def _final_norm_kernel(x_ref, m_ref, g_ref, o_ref):
    x = x_ref[...] + m_ref[...].astype(F32)
    o_ref[...] = x * lax.rsqrt(jnp.mean(x * x, axis=-1, keepdims=True) + EPS) * g_ref[...]


def _final_norm(x2d, moe_prev, g, tm=512):
    t = x2d.shape[0]
    row = pl.BlockSpec((tm, D_MODEL), lambda i: (i, 0))
    return pl.pallas_call(
        _final_norm_kernel,
        grid=(t // tm,),
        in_specs=[row, row, pl.BlockSpec((1, D_MODEL), lambda i: (0, 0))],
        out_specs=row,
        out_shape=jax.ShapeDtypeStruct((t, D_MODEL), F32),
        compiler_params=_cparams("parallel"),
        name="final_norm",
    )(x2d, moe_prev, g.reshape(1, D_MODEL).astype(F32))


def _permute_w_in(w_in):
    sizes = (256, 256, 128, 128, 256, 256, 16, 16, 256, 256, 256, 256, 16, 768)
    offs = np.concatenate([[0], np.cumsum(sizes)])
    seg = [w_in[:, offs[i]:offs[i + 1]] for i in range(len(sizes))]
    (a_x, a_g, b_q, b_k, b_v, b_r, b_lf, b_lb, c_q, c_k, c_v, c_o, c_g, d_u) = seg
    small = jnp.concatenate([b_lf, b_lb, c_g, jnp.zeros((w_in.shape[0], ZS_W - 48), w_in.dtype)], axis=1)
    return jnp.concatenate([a_x, a_g, b_q, b_k, b_v, b_r, c_q, c_k, c_v, c_o, d_u, small], axis=1).astype(BF16)


HALO = 16


def _strided_pitch(n):
    assert n % V7X_SUBLANES == 0
    return n if (n // V7X_SUBLANES) % 2 == 1 else n + V7X_SUBLANES


def _halo_conv(prev, cur, nxt, w, b, lo, first, last):
    tl = cur.shape[0]
    prev = jnp.where(first, 0.0, prev.astype(F32))
    nxt = jnp.where(last, 0.0, nxt.astype(F32))
    ext = jnp.concatenate([prev, cur.astype(F32), nxt], axis=0)
    n = tl + 2 * HALO
    acc = None
    for j in range(w.shape[0]):
        o = lo + j
        sh = ext if o == 0 else pltpu.roll(ext, (-o) % n, 0)
        term = sh[HALO:HALO + tl] * w[j:j + 1, :]
        acc = term if acc is None else acc + term
    return acc + b


def _rglru_kernel(prev_ref, cur_ref, next_ref, cw_ref, cb_ref, gw_ref, gb_ref, sp_ref, out_ref,
                  a_s, b_s, h_s, p_s, carry, *, tl, reverse):
    i = pl.program_id(1)
    nt = pl.num_programs(1)
    ti = (nt - 1 - i) if reverse else i
    u = _halo_conv(prev_ref[0], cur_ref[0], next_ref[0], cw_ref[...], cb_ref[...], -2, ti == 0, ti == nt - 1)
    gates = _sigmoid(_dot(u.astype(BF16), gw_ref[...]) + gb_ref[...])
    r = gates[:, :GROUP_W]
    ig = gates[:, GROUP_W:]
    log_a = (-RG_C) * r * sp_ref[...]
    a = jnp.exp(log_a)
    bt = jnp.sqrt(1.0 - a * a) * (ig * u)
    nseg = RG_SEGMENTS
    m = tl // nseg
    mp, sp = _strided_pitch(m), _strided_pitch(nseg)
    for k in range(2):
        for seg in range(nseg):
            a_s[k, seg * mp:seg * mp + m, :] = a[seg * m:(seg + 1) * m, k * V7X_LANES:(k + 1) * V7X_LANES]
            b_s[k, seg * mp:seg * mp + m, :] = bt[seg * m:(seg + 1) * m, k * V7X_LANES:(k + 1) * V7X_LANES]

    @pl.when(i == 0)
    def _():
        carry[...] = jnp.zeros_like(carry)

    def body(jj, hp):
        j = (m - 1 - jj) if reverse else jj
        out = []
        for k in range(2):
            h, p = hp[2 * k], hp[2 * k + 1]
            av = a_s[k, pl.ds(j, nseg, stride=mp), :]
            bv = b_s[k, pl.ds(j, nseg, stride=mp), :]
            h = av * h + bv
            p = av * p
            rows = pl.ds(pl.multiple_of(j * sp, V7X_SUBLANES), nseg)
            h_s[k, rows, :] = h
            p_s[k, rows, :] = p
            out += [h, p]
        return tuple(out)

    z8 = jnp.zeros((nseg, V7X_LANES), F32)
    o8 = jnp.ones((nseg, V7X_LANES), F32)
    ends = lax.fori_loop(0, m, body, (z8, o8, z8, o8), unroll=RG_UNROLL)
    for k in range(2):
        h_end, p_end = ends[2 * k], ends[2 * k + 1]
        c = carry[k]
        cin = [None] * nseg
        for seg in (range(nseg - 1, -1, -1) if reverse else range(nseg)):
            cin[seg] = c
            c = h_end[seg:seg + 1, :] + p_end[seg:seg + 1, :] * c
        carry[k] = c
        for seg in range(nseg):
            hv = h_s[k, pl.ds(seg, m, stride=sp), :]
            pv = p_s[k, pl.ds(seg, m, stride=sp), :]
            out_ref[0, seg * m:(seg + 1) * m, k * V7X_LANES:(k + 1) * V7X_LANES] = (hv + pv * cin[seg]).astype(out_ref.dtype)


def _rglru(za, conv_w, conv_b, gate_w_bd, gate_b, softplus_neg_lam, reverse, tl=512):
    b, l, _ = za.shape
    nt = l // tl
    rh = tl // HALO
    nbh = l // HALO

    def tmap(bi, i):
        return (nt - 1 - i) if reverse else i

    kern = functools.partial(_rglru_kernel, tl=tl, reverse=reverse)
    return pl.pallas_call(
        kern,
        grid=(b, nt),
        in_specs=[
            pl.BlockSpec((1, HALO, GROUP_W), lambda bi, i: (bi, jnp.maximum(tmap(bi, i) * rh - 1, 0), 0)),
            pl.BlockSpec((1, tl, GROUP_W), lambda bi, i: (bi, tmap(bi, i), 0)),
            pl.BlockSpec((1, HALO, GROUP_W), lambda bi, i: (bi, jnp.minimum((tmap(bi, i) + 1) * rh, nbh - 1), 0)),
            pl.BlockSpec((4, GROUP_W), lambda bi, i: (0, 0)),
            pl.BlockSpec((1, GROUP_W), lambda bi, i: (0, 0)),
            pl.BlockSpec((GROUP_W, 2 * GROUP_W), lambda bi, i: (0, 0)),
            pl.BlockSpec((1, 2 * GROUP_W), lambda bi, i: (0, 0)),
            pl.BlockSpec((1, GROUP_W), lambda bi, i: (0, 0)),
        ],
        out_specs=pl.BlockSpec((1, tl, GROUP_W), lambda bi, i: (bi, tmap(bi, i), 0)),
        out_shape=jax.ShapeDtypeStruct((b, l, GROUP_W), BF16),
        scratch_shapes=[pltpu.VMEM((2, RG_SEGMENTS * _strided_pitch(tl // RG_SEGMENTS), V7X_LANES), F32)] * 2
        + [pltpu.VMEM((2, (tl // RG_SEGMENTS) * _strided_pitch(RG_SEGMENTS), V7X_LANES), F32)] * 2
        + [pltpu.VMEM((2, 1, V7X_LANES), F32)],
        compiler_params=_cparams("parallel", "arbitrary"),
        name="rglru_bwd" if reverse else "rglru_fwd",
    )(za, za, za, conv_w, conv_b, gate_w_bd, gate_b, softplus_neg_lam)


def _block_diag(blocks):
    h, di, do = blocks.shape
    eye = jnp.eye(h, dtype=blocks.dtype)
    return jnp.einsum('hde,hg->hdge', blocks, eye).reshape(h * di, h * do)


def _chunk_masks(reverse):
    r = np.arange(GROUP_W)
    t = np.arange(CHUNK)
    tri = (t[None, :] >= t[:, None]) if reverse else (t[None, :] <= t[:, None])
    cmask = np.tile(tri, (1, N_HEADS))
    bd = (r[:, None] // HEAD_W) == (r[None, :] // HEAD_W)
    kd = (r[:, None] // HEAD_W) == (np.arange(N_HEADS * B_DK)[None, :] // B_DK)
    i2 = np.tile(np.eye(CHUNK, dtype=bool), (1, N_HEADS))
    hm = np.zeros((8, GROUP_W), bool)
    hm[:N_HEADS] = np.arange(N_HEADS)[:, None] == (r[None, :] // HEAD_W)
    return dict(tri=jnp.asarray(tri, BF16), cmask=jnp.asarray(cmask, F32), bd=jnp.asarray(bd, BF16),
                kd=jnp.asarray(kd, BF16), kdf=jnp.asarray(kd, F32),
                i2=jnp.asarray(i2, F32), hm=jnp.asarray(hm, F32))


def _const_spec(arr):
    nd = arr.ndim
    return pl.BlockSpec(arr.shape, lambda *_: (0,) * nd)


def _gla_kernel(zb_ref, zs_ref, gw_ref, gb_ref, tri_ref, kd_ref, kdf_ref, bd_ref, cmask_ref, out_ref, st_ref,
                *, tl, reverse):
    @pl.when(pl.program_id(1) == 0)
    def _():
        st_ref[...] = jnp.zeros_like(st_ref)

    la = _log_sigmoid(_dot(zs_ref[0].astype(BF16), gw_ref[...]) + gb_ref[...]) * (1.0 / GLA_TAU)
    causal = cmask_ref[...] > 0.0
    nch = tl // CHUNK
    for c in (range(nch - 1, -1, -1) if reverse else range(nch)):
        rows = slice(c * CHUNK, (c + 1) * CHUNK)
        q = zb_ref[0, rows, 0:128].astype(F32) * (B_DK ** -0.5)
        k = zb_ref[0, rows, 128:256].astype(F32)
        vb = zb_ref[0, rows, 256:512]
        la_c = la[rows]
        bcum = _dot_split(tri_ref[...], la_c)
        btot = jnp.sum(la_c, axis=0, keepdims=True)
        q_in = (q * jnp.exp(bcum)).astype(BF16)
        k_in = (k * jnp.exp(-bcum)).astype(BF16)
        k_st = (k * jnp.exp(btot - bcum)).astype(BF16)
        kexp = jnp.concatenate([k_in] * N_HEADS, axis=0) * kd_ref[...]
        att = jnp.where(causal, _dot_nt(q_in, kexp), 0.0)
        vbd = jnp.concatenate([vb] * N_HEADS, axis=0) * bd_ref[...]
        st = st_ref[...]
        out_ref[0, rows, :] = (_dot(att.astype(BF16), vbd) + _dot_nt(q_in, st.astype(BF16))).astype(out_ref.dtype)
        st_ref[...] = st * jnp.exp(btot) + _dot_tn(vb, k_st) * kdf_ref[...]


def _gla(zb, zs, gate_w_pad, gate_b, reverse, tl=512):
    b, l, _ = zb.shape
    nt = l // tl
    mk = _chunk_masks(reverse)
    consts = (gate_w_pad, gate_b, mk['tri'], mk['kd'], mk['kdf'], mk['bd'], mk['cmask'])

    def tmap(bi, i):
        return (bi, (nt - 1 - i) if reverse else i, 0)

    return pl.pallas_call(
        functools.partial(_gla_kernel, tl=tl, reverse=reverse),
        grid=(b, nt),
        in_specs=[pl.BlockSpec((1, tl, ZB_W), tmap), pl.BlockSpec((1, tl, ZS_W), tmap)]
        + [_const_spec(c) for c in consts],
        out_specs=pl.BlockSpec((1, tl, GROUP_W), tmap),
        out_shape=jax.ShapeDtypeStruct((b, l, GROUP_W), BF16),
        scratch_shapes=[pltpu.VMEM((GROUP_W, N_HEADS * B_DK), F32)],
        compiler_params=_cparams("parallel", "arbitrary"),
        name="gla_bwd" if reverse else "gla_fwd",
    )(zb, zs, *consts)


def _mlstm_kernel(zc_ref, zs_ref, e_ref, eb_ref, tri_ref, i2_ref, bd_ref, cmask_ref, hm_ref, out_ref,
                  s_ref, n_ref, m_ref, *, tl, reverse):
    @pl.when(pl.program_id(1) == 0)
    def _():
        s_ref[...] = jnp.zeros_like(s_ref)
        n_ref[...] = jnp.zeros_like(n_ref)
        m_ref[...] = jnp.full_like(m_ref, NEG_BIG)

    ge = _split_dot(zs_ref[0], e_ref[...]) + eb_ref[...]
    ig_all = ge[:, :GROUP_W]
    lf_all = _log_sigmoid(ge[:, GROUP_W:])
    causal = cmask_ref[...] > 0.0
    ones_b = jnp.ones((CHUNK, GROUP_W), BF16)
    nch = tl // CHUNK
    for c in (range(nch - 1, -1, -1) if reverse else range(nch)):
        rows = slice(c * CHUNK, (c + 1) * CHUNK)
        qb = zc_ref[0, rows, 0:256]
        k = zc_ref[0, rows, 256:512].astype(F32) * (HEAD_W ** -0.5)
        kb = k.astype(BF16)
        vb = zc_ref[0, rows, 512:768]
        ig = ig_all[rows]
        lf = lf_all[rows]
        bcum = _dot_split(tri_ref[...], lf)
        blast = jnp.sum(lf, axis=0, keepdims=True)
        cc = jnp.sum((ig - bcum) * i2_ref[...], axis=0, keepdims=True)
        d2 = jnp.where(causal, bcum + cc, NEG_BIG)
        m_intra = jnp.full((CHUNK, GROUP_W), NEG_BIG, F32)
        for h in range(N_HEADS):
            hsel = hm_ref[h:h + 1, :] > 0.0
            mh = jnp.max(jnp.where(hsel, d2, NEG_BIG), axis=1, keepdims=True)
            m_intra = jnp.where(hsel, mh, m_intra)
        m_prev = m_ref[...]
        inter_log = bcum + m_prev
        m_t = jnp.maximum(inter_log, m_intra)
        w_intra = jnp.exp(d2 - m_t)
        w_inter = jnp.exp(inter_log - m_t)
        kexp = jnp.concatenate([kb] * N_HEADS, axis=0) * bd_ref[...]
        qk = (_dot_nt(qb, kexp) * w_intra).astype(BF16)
        vbd = jnp.concatenate([vb] * N_HEADS, axis=0) * bd_ref[...]
        s_prev = s_ref[...]
        n_prev = n_ref[...]
        sbd = jnp.concatenate([s_prev.astype(BF16)] * N_HEADS, axis=0) * bd_ref[...]
        nbd = jnp.concatenate([n_prev.astype(BF16)] * N_HEADS, axis=0) * bd_ref[...]
        num = _dot(qk, vbd) + w_inter * _dot(qb, sbd)
        den = _dot(qk, bd_ref[...]) + w_inter * _dot(qb, nbd)
        out_ref[0, rows, :] = (num / jnp.maximum(jnp.abs(den), jnp.exp(-m_t))).astype(out_ref.dtype)
        g_end = blast - bcum + ig
        g_max = jnp.max(g_end, axis=0, keepdims=True)
        kw = (k * jnp.exp(g_end - g_max)).astype(BF16)
        m_new = jnp.maximum(blast + m_prev, g_max)
        a = jnp.exp(blast + m_prev - m_new)
        cf = jnp.exp(g_max - m_new)
        uc_full = _dot_tn(kw, vb)
        un_full = _dot_tn(kw, ones_b)
        uc = jnp.zeros((HEAD_W, GROUP_W), F32)
        un = jnp.zeros((HEAD_W, GROUP_W), F32)
        for h in range(N_HEADS):
            hsel = hm_ref[h:h + 1, :] > 0.0
            blk = slice(h * HEAD_W, (h + 1) * HEAD_W)
            uc = jnp.where(hsel, uc_full[blk], uc)
            un = jnp.where(hsel, un_full[blk], un)
        s_ref[...] = a * s_prev + cf * uc
        n_ref[...] = a * n_prev + cf * un
        m_ref[...] = m_new


def _mlstm(zc, zs, expand_w, expand_b, reverse, tl=512):
    b, l, _ = zc.shape
    nt = l // tl
    mk = _chunk_masks(reverse)
    consts = (expand_w, expand_b, mk['tri'], mk['i2'], mk['bd'], mk['cmask'], mk['hm'])

    def tmap(bi, i):
        return (bi, (nt - 1 - i) if reverse else i, 0)

    return pl.pallas_call(
        functools.partial(_mlstm_kernel, tl=tl, reverse=reverse),
        grid=(b, nt),
        in_specs=[pl.BlockSpec((1, tl, ZC_W), tmap), pl.BlockSpec((1, tl, ZS_W), tmap)]
        + [_const_spec(c) for c in consts],
        out_specs=pl.BlockSpec((1, tl, GROUP_W), tmap),
        out_shape=jax.ShapeDtypeStruct((b, l, GROUP_W), BF16),
        scratch_shapes=[pltpu.VMEM((HEAD_W, GROUP_W), F32), pltpu.VMEM((HEAD_W, GROUP_W), F32),
                        pltpu.VMEM((1, GROUP_W), F32)],
        compiler_params=_cparams("parallel", "arbitrary"),
        name="mlstm_bwd" if reverse else "mlstm_fwd",
    )(zc, zs, *consts)


def _mixer_prep(P, l):
    out = {}
    gw, gb, lam = P['a_gate_w'][l], P['a_gate_b'][l], P['a_lambda'][l]
    for d in range(2):
        w = jnp.concatenate([_block_diag(gw[d, 0]), _block_diag(gw[d, 1])], axis=1).astype(BF16)
        bb = jnp.concatenate([gb[d, 0].reshape(1, -1), gb[d, 1].reshape(1, -1)], axis=1).astype(F32)
        nl = -lam[d].astype(F32)
        sp = (jnp.maximum(nl, 0.0) + jnp.log(1.0 + jnp.exp(-jnp.abs(nl)))).reshape(1, -1)
        out['a', d] = (w, bb, sp)
        bw = jnp.zeros((ZS_W, N_HEADS * B_DK), F32).at[B_RANK * d:B_RANK * (d + 1)].set(P['b_gate_w'][l][d])
        out['b', d] = (bw.astype(BF16), P['b_gate_b'][l][d].reshape(1, -1).astype(F32))
        e = np.zeros((ZS_W, 2 * GROUP_W), np.float32)
        for h in range(N_HEADS):
            e[32 + 8 * d + h, h * HEAD_W:(h + 1) * HEAD_W] = 1.0
            e[36 + 8 * d + h, GROUP_W + h * HEAD_W:GROUP_W + (h + 1) * HEAD_W] = 1.0
        cb = P['c_gate_b'][l].astype(F32)
        eb = jnp.concatenate([jnp.repeat(cb[2 * d], HEAD_W), jnp.repeat(cb[2 * d + 1], HEAD_W)]).reshape(1, -1)
        out['c', d] = (jnp.asarray(e, BF16), eb)
    return out


G_ROWS = 8
G_ROWS_BF16 = 16


def _slab(k):
    return slice(k * V7X_LANES, (k + 1) * V7X_LANES)


def _scatter_rows(scr, base, val, rows, pitch, blk0=0):
    for s in range(val.shape[1] // V7X_LANES):
        for b in range(val.shape[0] // rows):
            r0 = (blk0 + b) * pitch
            scr[base + s, r0:r0 + rows, :] = val[b * rows:(b + 1) * rows, _slab(s)]


def _gather_n2_major(scr, outs, n2, pitch):
    nsl = GROUP_W // V7X_LANES

    def body(j, carry):
        for a, (ref, lead) in enumerate(outs):
            for s in range(nsl):
                ref[lead, j, :, _slab(s)] = scr[a * nsl + s, pl.ds(j, G_ROWS, stride=pitch), :]
        return carry

    lax.fori_loop(0, n2, body, 0, unroll=4)


def _conv3_kernel(prev_ref, cur_ref, next_ref, w_ref, b_ref, v_ref, x1_ref, x2_ref, scr, *, n2, pitch):
    i = pl.program_id(1)
    u = _halo_conv(prev_ref[0], cur_ref[0], next_ref[0], w_ref[...], b_ref[...], -1, i == 0,
                   i == pl.num_programs(1) - 1)
    nsl = GROUP_W // V7X_LANES
    for a in range(3):
        _scatter_rows(scr, a * nsl, u[:, a * GROUP_W:(a + 1) * GROUP_W], n2, pitch)
    _gather_n2_major(scr, ((v_ref, 0), (x1_ref, 0), (x2_ref, 0)), n2, pitch)


def _conv3_split(zd, w, b, n1, n2):
    bsz, l, _ = zd.shape
    tl = G_ROWS * n2
    pitch = _strided_pitch(n2)
    nt, rh, nbh = l // tl, tl // HALO, l // HALO
    o = jax.ShapeDtypeStruct((bsz, n2, n1 // 2, GROUP_W), F32)
    ospec = pl.BlockSpec((1, n2, G_ROWS, GROUP_W), lambda bi, i: (bi, 0, i, 0))
    return pl.pallas_call(
        functools.partial(_conv3_kernel, n2=n2, pitch=pitch),
        grid=(bsz, nt),
        in_specs=[pl.BlockSpec((1, HALO, ZD_W), lambda bi, i: (bi, jnp.maximum(i * rh - 1, 0), 0)),
                  pl.BlockSpec((1, tl, ZD_W), lambda bi, i: (bi, i, 0)),
                  pl.BlockSpec((1, HALO, ZD_W), lambda bi, i: (bi, jnp.minimum((i + 1) * rh, nbh - 1), 0)),
                  pl.BlockSpec((3, ZD_W), lambda bi, i: (0, 0)),
                  pl.BlockSpec((1, ZD_W), lambda bi, i: (0, 0))],
        out_specs=(ospec, ospec, ospec),
        out_shape=(o, o, o),
        scratch_shapes=[pltpu.VMEM((3 * GROUP_W // V7X_LANES, G_ROWS * pitch, V7X_LANES), F32)],
        compiler_params=_cparams("parallel", "parallel"),
        name="hyena_conv3",
    )(zd, zd, zd, w, b.reshape(1, ZD_W))


def _dot_hp(a, b):
    return jnp.dot(a, b, preferred_element_type=F32, precision=lax.Precision.HIGHEST)


FILTER_SUB = 512


def _filter_mlp_kernel(emb_ref, w1_ref, b1_ref, f0_ref, w2_ref, b2_ref, f1_ref, w3_ref, dl_ref,
                       fwd_ref, sec_ref, s_ref, scr, *, n2, pitch):
    i = pl.program_id(0)
    tl = emb_ref.shape[0]
    nsl = GROUP_W // V7X_LANES

    @pl.when(i == 0)
    def _():
        s_ref[...] = jnp.zeros_like(s_ref)

    for sub in range(tl // FILTER_SUB):
        emb = emb_ref[sub * FILTER_SUB:(sub + 1) * FILTER_SUB, :]
        h = jnp.sin(f0_ref[...] * (_dot_hp(emb, w1_ref[...]) + b1_ref[...]))
        h = jnp.sin(f1_ref[...] * (_dot_hp(h, w2_ref[...]) + b2_ref[...]))
        h = _dot_hp(h, w3_ref[...])
        dl = dl_ref[...]
        hf = h[:, :2 * GROUP_W] * jnp.exp(-emb[:, 0:1] * dl)
        hb = h[:, 2 * GROUP_W:] * jnp.exp(-emb[:, HY_EMB:HY_EMB + 1] * dl)
        if sub == 0:
            row = _iota(hb.shape, 0) + i * tl
            hb = jnp.where(row == 0, 0.0, hb)
        s_ref[...] += jnp.sum(jnp.abs(hf) + jnp.abs(hb), axis=0, keepdims=True)
        blk0 = sub * FILTER_SUB // n2
        for o in range(2):
            _scatter_rows(scr, o * nsl, hf[:, o * GROUP_W:(o + 1) * GROUP_W], n2, pitch, blk0)
            _scatter_rows(scr, (2 + o) * nsl, hb[:, o * GROUP_W:(o + 1) * GROUP_W], n2, pitch, blk0)
    _gather_n2_major(scr, ((fwd_ref, 0), (fwd_ref, 1), (sec_ref, 0), (sec_ref, 1)), n2, pitch)


def _hyena_filter(l, w1, b1, freq, w2, b2, w3, n1, n2):
    tl = G_ROWS * n2
    bands = (HY_EMB - 1) // 2
    f = jnp.linspace(1e-4, bands - 1, bands, dtype=F32)

    def embed(pos):
        t = pos / max(l - 1, 1)
        ang = (2.0 * math.pi / l) * pos[:, None] * f[None, :]
        return jnp.concatenate([t[:, None], jnp.cos(ang), -jnp.sin(ang)], axis=-1)

    pos = jnp.arange(l, dtype=F32)
    emb = jnp.concatenate([embed(pos), embed(l - pos), jnp.zeros((l, V7X_LANES - 2 * HY_EMB), F32)], axis=-1)
    deltas = jnp.abs(jnp.linspace(math.log(HY_TARGET) / HY_SLOW, math.log(HY_TARGET) / HY_FAST, GROUP_W, dtype=F32))
    z = lambda r, c: jnp.zeros((r, c), F32)
    w1f, w2f = w1.astype(F32), w2.astype(F32)
    w3r = w3.astype(F32).reshape(HY_HID, 2, 2, GROUP_W)
    w3d = lambda d: w3r[:, :, d].reshape(HY_HID, 2 * GROUP_W)
    w1p = z(V7X_LANES, V7X_LANES).at[:HY_EMB, :HY_HID].set(w1f).at[HY_EMB:2 * HY_EMB, HY_HID:].set(w1f)
    w2p = z(V7X_LANES, V7X_LANES).at[:HY_HID, :HY_HID].set(w2f).at[HY_HID:, HY_HID:].set(w2f)
    w3p = z(V7X_LANES, 4 * GROUP_W).at[:HY_HID, :2 * GROUP_W].set(w3d(0)).at[HY_HID:, 2 * GROUP_W:].set(w3d(1))
    vec = lambda x: jnp.tile(x.astype(F32), 2).reshape(1, V7X_LANES)
    consts = (w1p, vec(b1), vec(freq[0]), w2p, vec(b2), vec(freq[1]), w3p, jnp.tile(deltas, 2).reshape(1, -1))
    pitch = _strided_pitch(n2)
    half = jax.ShapeDtypeStruct((2, n2, n1 // 2, GROUP_W), F32)
    hspec = pl.BlockSpec((2, n2, G_ROWS, GROUP_W), lambda i: (0, 0, i, 0))
    return pl.pallas_call(
        functools.partial(_filter_mlp_kernel, n2=n2, pitch=pitch),
        grid=(l // tl,),
        in_specs=[pl.BlockSpec((tl, V7X_LANES), lambda i: (i, 0))] + [_const_spec(c) for c in consts],
        out_specs=(hspec, hspec, pl.BlockSpec((1, 2 * GROUP_W), lambda i: (0, 0))),
        out_shape=(half, half, jax.ShapeDtypeStruct((1, 2 * GROUP_W), F32)),
        scratch_shapes=[pltpu.VMEM((4 * GROUP_W // V7X_LANES, G_ROWS * pitch, V7X_LANES), F32)],
        compiler_params=_cparams("arbitrary"),
        name="hyena_filter_mlp",
    )(emb, *consts)


def _fft_factors(l):
    return (64, 128) if l == 4096 else (2 * l // 256, 256)


@functools.lru_cache(maxsize=None)
def _fft_tables(n1, n2):
    n = n1 * n2
    k1 = np.arange(n1)[None, :, None]
    m2 = np.arange(n2)[:, None, None]

    def theta(n1_vals):
        idx = (k1 * (n1_vals[None, None, :] * n2 + m2)) % n
        return (2.0 * np.pi / n) * idx
    th = theta(np.arange(n1 // 2))
    fr, fi = np.cos(th), -np.sin(th)
    fwd = np.concatenate([np.concatenate([fr, -fi], -1), np.concatenate([fi, fr], -1)], -2)
    thf = theta(np.arange(n1))
    filt = np.concatenate([np.cos(thf), -np.sin(thf)], -2)
    cr = np.swapaxes(np.cos(th), 1, 2) / n
    ci = np.swapaxes(np.sin(th), 1, 2) / n
    inv = np.concatenate([np.concatenate([cr, -ci], -1), np.concatenate([ci, cr], -1)], -2)
    t2 = (2.0 * np.pi / n2) * ((np.arange(n2)[:, None] * np.arange(n2)[None, :]) % n2)
    gr, gi = np.cos(t2), -np.sin(t2)
    g = np.block([[gr, -gi], [gi, gr]])
    ginv = np.block([[gr, gi], [-gi, gr]])
    return tuple(jnp.asarray(a, BF16) for a in (fwd, filt, inv, g, ginv))


def _regroup_rows(scr, n_rows, pitch, store):
    def body(r, carry):
        for s in range(GROUP_W // V7X_LANES):
            store(r, s, scr[s, pl.ds(r, G_ROWS_BF16, stride=pitch), :])
        return carry

    lax.fori_loop(0, n_rows, body, 0, unroll=4)


def _fft_stage1_kernel(zr_ref, zi_ref, t_ref, a_ref, scr, *, n1, pitch):
    for j in range(G_ROWS_BF16):
        rhs = jnp.concatenate([zr_ref[0, j], zi_ref[0, j]], axis=0).astype(BF16)
        _scatter_rows(scr, 0, _dot(t_ref[j], rhs), 2 * n1, pitch, j)

    def store(r, s, rows):
        a_ref[0, r, :, _slab(s)] = rows.astype(BF16)

    _regroup_rows(scr, 2 * n1, pitch, store)


def _fft_stage1(zr, zi, table, n1, n2, pairs, idx_r, idx_i):
    nb = G_ROWS_BF16
    pitch = _strided_pitch(2 * n1)
    return pl.pallas_call(
        functools.partial(_fft_stage1_kernel, n1=n1, pitch=pitch),
        grid=(pairs, n2 // nb),
        in_specs=[pl.BlockSpec((1, nb, n1 // 2, GROUP_W), lambda p, j: (idx_r(p), j, 0, 0)),
                  pl.BlockSpec((1, nb, n1 // 2, GROUP_W), lambda p, j: (idx_i(p), j, 0, 0)),
                  pl.BlockSpec((nb, 2 * n1, n1), lambda p, j: (j, 0, 0))],
        out_specs=pl.BlockSpec((1, 2 * n1, nb, GROUP_W), lambda p, j: (p, 0, j, 0)),
        out_shape=jax.ShapeDtypeStruct((pairs, 2 * n1, n2, GROUP_W), BF16),
        scratch_shapes=[pltpu.VMEM((GROUP_W // V7X_LANES, nb * pitch, V7X_LANES), F32)],
        compiler_params=_cparams("parallel", "parallel"),
        name="fft_stage1",
    )(zr, zi, table)


def _a_specs(n1, n2, kb, index):
    def spec(part):
        def imap(i, p):
            pp, kk = index(i, p)
            return (pp, part * (n1 // kb) + kk, 0, 0)
        return pl.BlockSpec((1, kb, n2, GROUP_W), imap)
    return [spec(0), spec(1)]


def _fft_spec_kernel(ar_ref, ai_ref, g_ref, s_ref, h_ref, *, kb, n2):
    for k in range(kb):
        rhs = jnp.concatenate([ar_ref[0, k], ai_ref[0, k]], axis=0)
        x = _dot(g_ref[...], rhs) * s_ref[0]
        h_ref[0, k, 0] = x[:n2]
        h_ref[0, k, 1] = x[n2:]


def _fft_filter_spectrum(a, g, inv_s, n1, n2, kb=8):
    orders = a.shape[0]
    return pl.pallas_call(
        functools.partial(_fft_spec_kernel, kb=kb, n2=n2),
        grid=(orders, n1 // kb),
        in_specs=_a_specs(n1, n2, kb, lambda o, i: (o, i))
        + [_const_spec(g), pl.BlockSpec((1, 1, GROUP_W), lambda o, i: (o, 0, 0))],
        out_specs=pl.BlockSpec((1, kb, 2, n2, GROUP_W), lambda o, i: (o, i, 0, 0, 0)),
        out_shape=jax.ShapeDtypeStruct((orders, n1, 2, n2, GROUP_W), F32),
        compiler_params=_cparams("parallel", "parallel"),
        name="fft_filter_spectrum",
    )(a, a, g, inv_s)


def _fft_mid_kernel(ar_ref, ai_ref, g_ref, gi_ref, h_ref, b_ref, scr, *, n2, pitch):
    for k in range(G_ROWS_BF16):
        rhs = jnp.concatenate([ar_ref[0, k], ai_ref[0, k]], axis=0)
        x = _dot(g_ref[...], rhs)
        xr, xi = x[:n2], x[n2:]
        hr, hi = h_ref[0, k, 0], h_ref[0, k, 1]
        y = jnp.concatenate([xr * hr - xi * hi, xr * hi + xi * hr], axis=0).astype(BF16)
        _scatter_rows(scr, 0, _dot(gi_ref[...], y), 2 * n2, pitch, k)

    def store(r, s, rows):
        b_ref[0, r, :, _slab(s)] = rows.astype(BF16)

    _regroup_rows(scr, 2 * n2, pitch, store)


def _fft_mid(a, g, ginv, spec, order, n1, n2):
    pairs = a.shape[0]
    kb = G_ROWS_BF16
    pitch = _strided_pitch(2 * n2)
    return pl.pallas_call(
        functools.partial(_fft_mid_kernel, n2=n2, pitch=pitch),
        grid=(n1 // kb, pairs),
        in_specs=_a_specs(n1, n2, kb, lambda i, p: (p, i)) + [_const_spec(g), _const_spec(ginv),
                  pl.BlockSpec((1, kb, 2, n2, GROUP_W), lambda i, p: (order, i, 0, 0, 0))],
        out_specs=pl.BlockSpec((1, 2 * n2, kb, GROUP_W), lambda i, p: (p, 0, i, 0)),
        out_shape=jax.ShapeDtypeStruct((pairs, 2 * n2, n1, GROUP_W), BF16),
        scratch_shapes=[pltpu.VMEM((GROUP_W // V7X_LANES, kb * pitch, V7X_LANES), F32)],
        compiler_params=_cparams("parallel", "parallel"),
        name="fft_mid",
    )(a, a, g, ginv, spec)


def _fft_stage3_kernel(br_ref, bi_ref, t_ref, ur_ref, ui_ref, gr_ref, gi_ref, sk_ref, y_ref, *scr,
                       n1, pitch, time_major):
    h = n1 // 2
    sk = sk_ref[...]
    for j in range(G_ROWS_BF16):
        rhs = jnp.concatenate([br_ref[0, j], bi_ref[0, j]], axis=0)
        o = _dot(t_ref[j], rhs)
        yr = gr_ref[0, j] * (o[:h] + ur_ref[0, j] * sk)
        yi = gi_ref[0, j] * (o[h:] + ui_ref[0, j] * sk)
        if time_major:
            _scatter_rows(scr[0], 0, yr, h, pitch, j)
            _scatter_rows(scr[1], 0, yi, h, pitch, j)
        else:
            y_ref[0, 0, j] = yr
            y_ref[0, 1, j] = yi
    if time_major:
        for part in range(2):
            def store(r, s, rows, part=part):
                y_ref[0, part, r, :, _slab(s)] = rows.astype(y_ref.dtype)
            _regroup_rows(scr[part], h, pitch, store)


def _fft_stage3(b, table, u, gate, skip, n1, n2, time_major):
    pairs = b.shape[0]
    nb = G_ROWS_BF16
    h = n1 // 2
    pitch = _strided_pitch(h)
    bspec = lambda part: pl.BlockSpec((1, nb, n1, GROUP_W), lambda p, j: (p, part * (n2 // nb) + j, 0, 0))
    even = pl.BlockSpec((1, nb, h, GROUP_W), lambda p, j: (2 * p, j, 0, 0))
    odd = pl.BlockSpec((1, nb, h, GROUP_W), lambda p, j: (2 * p + 1, j, 0, 0))
    if time_major:
        out_spec = pl.BlockSpec((1, 2, h, nb, GROUP_W), lambda p, j: (p, 0, 0, j, 0))
        out_shape = jax.ShapeDtypeStruct((pairs, 2, h, n2, GROUP_W), BF16)
        scratch = [pltpu.VMEM((GROUP_W // V7X_LANES, nb * pitch, V7X_LANES), F32)] * 2
    else:
        out_spec = pl.BlockSpec((1, 2, nb, h, GROUP_W), lambda p, j: (p, 0, j, 0, 0))
        out_shape = jax.ShapeDtypeStruct((pairs, 2, n2, h, GROUP_W), F32)
        scratch = []
    y = pl.pallas_call(
        functools.partial(_fft_stage3_kernel, n1=n1, pitch=pitch, time_major=time_major),
        grid=(pairs, n2 // nb),
        in_specs=[bspec(0), bspec(1), pl.BlockSpec((nb, n1, 2 * n1), lambda p, j: (j, 0, 0)),
                  even, odd, even, odd, pl.BlockSpec((1, GROUP_W), lambda p, j: (0, 0))],
        out_specs=out_spec,
        out_shape=out_shape,
        scratch_shapes=scratch,
        compiler_params=_cparams("parallel", "parallel"),
        name="fft_stage3_out" if time_major else "fft_stage3",
    )(b, b, table, u, u, gate, gate, skip.reshape(1, GROUP_W).astype(F32))
    return y.reshape(2 * pairs, h * n2, GROUP_W) if time_major else y.reshape(2 * pairs, n2, h, GROUP_W)


def _hyena_spectrum(l, w1, b1, freq, w2, b2, w3):
    n1, n2 = _fft_factors(l)
    _, t_filt, _, g, _ = _fft_tables(n1, n2)
    first, second, sums = _hyena_filter(l, w1, b1, freq, w2, b2, w3, n1, n2)
    a = _fft_stage1(first, second, t_filt, n1, n2, 2, lambda p: p, lambda p: p)
    inv_s = (1.0 / sums).reshape(2, 1, GROUP_W)
    return _fft_filter_spectrum(a, g, inv_s, n1, n2)


def _hyena_long_conv(u, gate, spec, order, skip, n1, n2, time_major):
    t_fwd, _, t_inv, g, ginv = _fft_tables(n1, n2)
    a = _fft_stage1(u, u, t_fwd, n1, n2, u.shape[0] // 2, lambda p: 2 * p, lambda p: 2 * p + 1)
    b = _fft_mid(a, g, ginv, spec, order, n1, n2)
    return _fft_stage3(b, t_inv, u, gate, skip, n1, n2, time_major)


def _mixer_hyena(zd, P, l, spec):
    n1, n2 = _fft_factors(zd.shape[1])
    v, x1, x2 = _conv3_split(zd, P['d_conv_w'][l].astype(F32), P['d_conv_b'][l].astype(F32), n1, n2)
    z = _hyena_long_conv(v, x1, spec, 0, P['d_skip'][l][0], n1, n2, False)
    return _hyena_long_conv(z, x2, spec, 1, P['d_skip'][l][1], n1, n2, True)


def _gelu_tanh(x):
    return 0.5 * x * (1.0 + jnp.tanh(math.sqrt(2.0 / math.pi) * (x + 0.044715 * (x * x * x))))


def _head_rms(h, bd):
    ss = _dot((h * h).astype(BF16), bd) * (1.0 / HEAD_W)
    return h * lax.rsqrt(ss + EPS)


def _out_proj_kernel(x_ref, af_ref, ab_ref, ag_ref, bf_ref, bb_ref, br_ref, cf_ref, cb_ref, co_ref, yd_ref,
                     w_ref, bg_ref, cg_ref, bd_ref, n2_ref, rwh_ref, rwl_ref, rb_ref,
                     xo_ref, hn_ref, lg_ref):
    bd = bd_ref[...]
    f = lambda ref: ref[...].astype(F32)
    ya = (f(af_ref) + f(ab_ref)) * _gelu_tanh(f(ag_ref))
    r = f(br_ref)
    yb = _head_rms(f(bf_ref) + f(bb_ref), bd) * bg_ref[...] * (r * _sigmoid(r))
    yc = _head_rms(f(cf_ref) + f(cb_ref), bd) * cg_ref[...] * _sigmoid(f(co_ref))
    mixed = jnp.concatenate([ya.astype(BF16), yb.astype(BF16), yc.astype(BF16), yd_ref[...]], axis=-1)
    x = x_ref[...] + _dot(mixed, w_ref[...])
    xo_ref[...] = x
    hn = x * lax.rsqrt(jnp.mean(x * x, axis=-1, keepdims=True) + EPS) * n2_ref[...]
    hn_ref[...] = hn.astype(BF16)
    hi = hn.astype(BF16)
    lo = (hn - hi.astype(F32)).astype(BF16)
    lg_ref[...] = _dot(hi, rwh_ref[...]) + _dot(lo, rwh_ref[...]) + _dot(hi, rwl_ref[...]) + rb_ref[...]


def _out_proj(x2d, af, ab, za, bf, bb, zb, cf, cb, zc, yd, w_out, b_norm_g, c_norm_g, norm2_g, rw, rb, tm=256):
    t = x2d.shape[0]
    bd = _chunk_masks(False)['bd']
    rwh = rw.astype(BF16)
    rwl = (rw - rwh.astype(F32)).astype(BF16)
    row = lambda w, j=0: pl.BlockSpec((tm, w), lambda i: (i, j))
    consts = (w_out, b_norm_g.reshape(1, -1).astype(F32), c_norm_g.reshape(1, -1).astype(F32), bd,
              norm2_g.reshape(1, -1).astype(F32), rwh, rwl, rb)
    return pl.pallas_call(
        _out_proj_kernel,
        grid=(t // tm,),
        in_specs=[row(D_MODEL), row(GROUP_W), row(GROUP_W), row(GROUP_W, 1), row(GROUP_W), row(GROUP_W),
                  row(GROUP_W, 2), row(GROUP_W), row(GROUP_W), row(GROUP_W, 3), row(GROUP_W)]
        + [_const_spec(c) for c in consts],
        out_specs=(row(D_MODEL), row(D_MODEL), row(V7X_LANES)),
        out_shape=(jax.ShapeDtypeStruct((t, D_MODEL), F32), jax.ShapeDtypeStruct((t, D_MODEL), BF16),
                   jax.ShapeDtypeStruct((t, V7X_LANES), F32)),
        compiler_params=_cparams("parallel"),
        name="out_proj",
    )(x2d, af, ab, za, bf, bb, zb, cf, cb, zc, yd, *consts)


def _route(logits):
    lane = _iota(logits.shape, 1).astype(F32)
    is_g = lane < MOE_GROUPS
    gl = jnp.where(is_g, logits, NEG_BIG)
    gmax = jnp.max(gl, axis=1, keepdims=True)
    gidx = jnp.min(jnp.where(gl == gmax, lane, 1e9), axis=1, keepdims=True)
    gprob = 1.0 / jnp.sum(jnp.where(is_g, jnp.exp(gl - gmax), 0.0), axis=1, keepdims=True)
    lo = MOE_GROUPS + MOE_PER_GROUP * gidx
    el = jnp.where((lane >= lo) & (lane < lo + MOE_PER_GROUP), logits, NEG_BIG)
    v1 = jnp.max(el, axis=1, keepdims=True)
    i1 = jnp.min(jnp.where(el == v1, lane, 1e9), axis=1, keepdims=True)
    el2 = jnp.where(lane == i1, NEG_BIG, el)
    v2 = jnp.max(el2, axis=1, keepdims=True)
    i2 = jnp.min(jnp.where(el2 == v2, lane, 1e9), axis=1, keepdims=True)
    e21 = jnp.exp(v2 - v1)
    p1 = 1.0 / (1.0 + e21)
    return jnp.where(lane == i1, p1 * gprob, 0.0) + jnp.where(lane == i2, e21 * p1 * gprob, 0.0), gidx


MOE_TB = 1024
MOE_RT = 128
MOE_SB = MOE_TB + MOE_GROUPS * MOE_RT


def _moe_kernel(hn_ref, lg_ref, wg_ref, wu_ref, wd_ref, o_ref, xs_ref, ys_ref, gs_ref, pos_ref, meta_ref):
    grp = pl.program_id(1)
    tb, sb, rt = MOE_TB, MOE_SB, MOE_RT

    @pl.when(grp == 0)
    def _():
        gates, gidx = _route(lg_ref[...])
        lane = _iota((tb, V7X_LANES), 1).astype(F32)
        ohg = jnp.where(lane == gidx, 1.0, 0.0)
        tril = jnp.where(_iota((tb, tb), 1) < _iota((tb, tb), 0), 1.0, 0.0).astype(BF16)
        rank = jnp.sum(_dot(tril, ohg.astype(BF16)) * ohg, axis=1, keepdims=True)
        cnt = jnp.sum(ohg, axis=0, keepdims=True)
        pc = jnp.ceil(cnt * (1.0 / rt)) * rt
        lane1 = _iota((1, V7X_LANES), 1)
        off = jnp.zeros((1, V7X_LANES), F32)
        run = jnp.zeros((1, 1), F32)
        for g in range(MOE_GROUPS):
            off = jnp.where(lane1 == g, run, off)
            run = run + jnp.sum(jnp.where(lane1 == g, pc, 0.0), axis=1, keepdims=True)
        pos = jnp.sum(ohg * off, axis=1, keepdims=True) + rank
        pos_ref[...] = jnp.broadcast_to(pos, (tb, V7X_LANES))
        meta_ref[0:1, :] = off
        meta_ref[1:2, :] = pc
        hi = jnp.floor(pos * (1.0 / 64.0))
        lo = pos - 64.0 * hi
        pm = jnp.where(lane == 0.0, hi, jnp.where(lane == 1.0, lo, 0.0)).astype(BF16)
        sel = jnp.where(_iota((8, V7X_LANES), 0) == _iota((8, V7X_LANES), 1), 1.0, 0.0).astype(BF16)
        rows8 = _dot_nt(sel, pm)
        pos_row = 64.0 * rows8[0:1, :] + rows8[1:2, :]
        p = jnp.where(_iota((sb, tb), 0).astype(F32) == pos_row, 1.0, 0.0).astype(BF16)
        xs_ref[...] = _dot(p, hn_ref[...]).astype(BF16)
        gh = gates.astype(BF16)
        gl = (gates - gh.astype(F32)).astype(BF16)
        gsorted = _dot(p, jnp.concatenate([gh, gl], axis=1))
        gs_ref[...] = gsorted[:, :V7X_LANES] + gsorted[:, V7X_LANES:]
        ys_ref[...] = jnp.zeros_like(ys_ref)

    lane1 = _iota((1, V7X_LANES), 1)
    start = jnp.sum(jnp.where(lane1 == grp, meta_ref[0:1, :], 0.0)).astype(jnp.int32)
    rows = jnp.sum(jnp.where(lane1 == grp, meta_ref[1:2, :], 0.0)).astype(jnp.int32)

    def experts(r0, nrows):
        xt = xs_ref[pl.ds(r0, nrows), :]
        gsl = gs_ref[pl.ds(r0, nrows), :]
        lane_t = _iota((nrows, V7X_LANES), 1)
        acc = None
        for j in range(MOE_PER_GROUP):
            g = _dot(xt, wg_ref[j])
            a = (g * _sigmoid(g)) * _dot(xt, wu_ref[j])
            y = _dot(a.astype(BF16), wd_ref[j])
            w = jnp.sum(jnp.where(lane_t == MOE_GROUPS + MOE_PER_GROUP * grp + j, gsl, 0.0), axis=1, keepdims=True)
            acc = w * y if acc is None else acc + w * y
        ys_ref[pl.ds(r0, nrows), :] = acc.astype(BF16)

    ntiles = rows // rt

    def pair(i, carry):
        experts(pl.multiple_of(start + 2 * i * rt, rt), 2 * rt)
        return carry

    lax.fori_loop(0, ntiles // 2, pair, 0)

    @pl.when(ntiles % 2 == 1)
    def _():
        experts(pl.multiple_of(start + (ntiles - 1) * rt, rt), rt)

    @pl.when(grp == MOE_GROUPS - 1)
    def _():
        pt = jnp.where(_iota((tb, sb), 1).astype(F32) == pos_ref[:, 0:1], 1.0, 0.0).astype(BF16)
        o_ref[...] = _dot(pt, ys_ref[...]).astype(o_ref.dtype)


def _moe(hn, logits, w_gate, w_up, w_down):
    t = hn.shape[0]
    tm = MOE_TB
    row = lambda w: pl.BlockSpec((tm, w), lambda i, g: (i, 0))
    wspec = lambda a, b: pl.BlockSpec((MOE_PER_GROUP, a, b), lambda i, g: (g, 0, 0))
    return pl.pallas_call(
        _moe_kernel,
        grid=(t // tm, MOE_GROUPS),
        in_specs=[row(D_MODEL), row(V7X_LANES), wspec(D_MODEL, D_EXPERT), wspec(D_MODEL, D_EXPERT),
                  wspec(D_EXPERT, D_MODEL)],
        out_specs=row(D_MODEL),
        out_shape=jax.ShapeDtypeStruct((t, D_MODEL), BF16),
        scratch_shapes=[pltpu.VMEM((MOE_SB, D_MODEL), BF16), pltpu.VMEM((MOE_SB, D_MODEL), BF16),
                        pltpu.VMEM((MOE_SB, V7X_LANES), F32), pltpu.VMEM((tm, V7X_LANES), F32),
                        pltpu.VMEM((8, V7X_LANES), F32)],
        compiler_params=_cparams("parallel", "arbitrary"),
        name="moe",
    )(hn, logits, w_gate, w_up, w_down)


def _layer(x2d, moe_prev, bsz, seq, l, P, W, spec):
    x2d, za, zb, zc, zd, zs = _in_proj(x2d, moe_prev, P['norm1_g'][l].astype(F32), W['w_in'][l])
    r3 = lambda a: a.reshape(bsz, seq, a.shape[-1])
    r2 = lambda a: a.reshape(bsz * seq, a.shape[-1])
    mp = W['mix'][l]
    cw, cb = P['a_conv_w'][l].astype(F32), P['a_conv_b'][l].reshape(1, -1).astype(F32)
    za3, zb3, zc3, zs3 = r3(za), r3(zb), r3(zc), r3(zs)
    af = _rglru(za3, cw, cb, *mp['a', 0], False)
    ab = _rglru(za3, cw, cb, *mp['a', 1], True)
    bf = _gla(zb3, zs3, *mp['b', 0], False)
    bb = _gla(zb3, zs3, *mp['b', 1], True)
    cf = _mlstm(zc3, zs3, *mp['c', 0], False)
    cbk = _mlstm(zc3, zs3, *mp['c', 1], True)
    yd = _mixer_hyena(r3(zd), P, l, spec)
    x_new, hn, logits = _out_proj(x2d, r2(af), r2(ab), za, r2(bf), r2(bb), zb, r2(cf), r2(cbk), zc, r2(yd),
                                  W['w_out'][l], P['b_norm_g'][l], P['c_norm_g'][l], P['norm2_g'][l],
                                  W['router_w'][l], W['router_b'][l])
    return x_new, _moe(hn, logits, W['w_gate'][l], W['w_up'][l], W['w_down'][l])


def _trunk(x, P, W):
    bsz, seq, _ = x.shape
    depth = P['w_in'].shape[0]
    x2d = x.reshape(bsz * seq, D_MODEL)
    moe_term = None
    for l in range(depth):
        spec = _hyena_spectrum(seq, P['d_ffn_w1'][l], P['d_ffn_b1'][l], P['d_sin_freq'][l], P['d_ffn_w2'][l],
                               P['d_ffn_b2'][l], P['d_ffn_w3'][l])
        x2d, moe_term = _layer(x2d, moe_term, bsz, seq, l, P, W, spec)
    return _final_norm(x2d, moe_term, P['final_norm_g']).reshape(bsz, seq, D_MODEL)


def _prep_weights(P):
    depth = P['w_in'].shape[0]
    W = {'w_in': [_permute_w_in(P['w_in'][l]) for l in range(depth)],
         'w_out': [P['w_out'][l].astype(BF16) for l in range(depth)],
         'mix': [_mixer_prep(P, l) for l in range(depth)],
         'w_gate': [P['moe_w_gate'][l].astype(BF16) for l in range(depth)],
         'w_up': [P['moe_w_up'][l].astype(BF16) for l in range(depth)],
         'w_down': [P['moe_w_down'][l].astype(BF16) for l in range(depth)],
         'router_w': [], 'router_b': []}
    for l in range(depth):
        rw = jnp.concatenate([P['moe_group_w'][l], P['moe_expert_w'][l]], axis=1).astype(F32)
        rb = jnp.concatenate([P['moe_group_b'][l], P['moe_expert_b'][l]]).astype(F32)
        padc = V7X_LANES - rw.shape[1]
        W['router_w'].append(jnp.pad(rw, ((0, 0), (0, padc))))
        W['router_b'].append(jnp.pad(rb, (0, padc)).reshape(1, V7X_LANES))
    return W


def kernel(x_prompt, x_sample, norm1_g, w_in, a_conv_w, a_conv_b, a_gate_w, a_gate_b, a_lambda, b_gate_w, b_gate_b, b_norm_g, c_gate_b, c_norm_g, d_conv_w, d_conv_b, d_ffn_w1, d_ffn_b1, d_sin_freq, d_ffn_w2, d_ffn_b2, d_ffn_w3, d_skip, w_out, norm2_g, moe_group_w, moe_group_b, moe_expert_w, moe_expert_b, moe_w_gate, moe_w_up, moe_w_down, final_norm_g):
    P = {'norm1_g': norm1_g, 'w_in': w_in, 'a_conv_w': a_conv_w, 'a_conv_b': a_conv_b,
         'a_gate_w': a_gate_w, 'a_gate_b': a_gate_b, 'a_lambda': a_lambda, 'b_gate_w': b_gate_w,
         'b_gate_b': b_gate_b, 'b_norm_g': b_norm_g, 'c_gate_b': c_gate_b, 'c_norm_g': c_norm_g,
         'd_conv_w': d_conv_w, 'd_conv_b': d_conv_b, 'd_ffn_w1': d_ffn_w1, 'd_ffn_b1': d_ffn_b1,
         'd_sin_freq': d_sin_freq, 'd_ffn_w2': d_ffn_w2, 'd_ffn_b2': d_ffn_b2, 'd_ffn_w3': d_ffn_w3,
         'd_skip': d_skip, 'w_out': w_out, 'norm2_g': norm2_g, 'moe_group_w': moe_group_w,
         'moe_group_b': moe_group_b, 'moe_expert_w': moe_expert_w, 'moe_expert_b': moe_expert_b,
         'moe_w_gate': moe_w_gate, 'moe_w_up': moe_w_up, 'moe_w_down': moe_w_down,
         'final_norm_g': final_norm_g}
    W = _prep_weights(P)
    return (_trunk(x_prompt, P, W), _trunk(x_sample, P, W))
```

```python
import functools
import math

import jax
import jax.numpy as jnp
import numpy as np
from jax import lax
from jax.experimental import pallas as pl
from jax.experimental.pallas import tpu as pltpu

F32 = jnp.float32
BF16 = jnp.bfloat16

D_MODEL = 1024
GROUP_W = 256
N_HEADS = 4
HEAD_W = GROUP_W // N_HEADS
B_DK = 32
B_RANK = 16
CHUNK = 64
RG_C = 8.0
RG_SEGMENTS = 16
RG_UNROLL = 4
GLA_TAU = 16.0
HY_EMB = 33
HY_HID = 64
HY_TARGET = 1e-2
HY_FAST = 0.3
HY_SLOW = 1.5
MOE_GROUPS = 4
MOE_PER_GROUP = 4
MOE_EXPERTS = 16
D_EXPERT = 512
EPS = 1e-6
NEG_BIG = -1e30

V7X_LANES = 128
V7X_SUBLANES = 8
VMEM_LIMIT = 56 * 1024 * 1024

ZA_W, ZB_W, ZC_W, ZD_W, ZS_W = 512, 768, 1024, 768, 128
Z_SPLITS = (ZA_W, ZB_W, ZC_W, ZD_W, ZS_W)
Z_TOTAL = sum(Z_SPLITS)


def _cparams(*sem):
    return pltpu.CompilerParams(dimension_semantics=sem, vmem_limit_bytes=VMEM_LIMIT)


def _dot(a, b):
    return jnp.dot(a, b, preferred_element_type=F32)


def _dot_nt(a, b):
    return lax.dot_general(a, b, (((1,), (1,)), ((), ())), preferred_element_type=F32)


def _dot_tn(a, b):
    return lax.dot_general(a, b, (((0,), (0,)), ((), ())), preferred_element_type=F32)


def _dot_split(m_bf16, x):
    hi = x.astype(BF16)
    lo = (x - hi.astype(F32)).astype(BF16)
    return _dot(m_bf16, hi) + _dot(m_bf16, lo)


def _split_dot(x, m_bf16):
    hi = x.astype(BF16)
    lo = (x - hi.astype(F32)).astype(BF16)
    return _dot(hi, m_bf16) + _dot(lo, m_bf16)


def _sigmoid(x):
    return 1.0 / (1.0 + jnp.exp(-x))


def _log_sigmoid(x):
    return jnp.minimum(x, 0.0) - jnp.log(1.0 + jnp.exp(-jnp.abs(x)))


def _iota(shape, dim):
    return lax.broadcasted_iota(jnp.int32, shape, dim)


def _in_proj_kernel(*refs, add):
    if add:
        x_ref, m_ref, g_ref, w_ref, xo_ref, za_ref, zb_ref, zc_ref, zd_ref, zs_ref = refs
        x = x_ref[...] + m_ref[...].astype(F32)
        xo_ref[...] = x
    else:
        x_ref, g_ref, w_ref, za_ref, zb_ref, zc_ref, zd_ref, zs_ref = refs
        x = x_ref[...]
    ms = jnp.mean(x * x, axis=-1, keepdims=True)
    xn = (x * lax.rsqrt(ms + EPS) * g_ref[...]).astype(BF16)
    off = 0
    for ref, w in zip((za_ref, zb_ref, zc_ref, zd_ref, zs_ref), Z_SPLITS):
        ref[...] = _dot(xn, w_ref[:, off:off + w]).astype(ref.dtype)
        off += w


def _in_proj(x2d, moe_prev, g, w_perm, tm=512):
    t = x2d.shape[0]
    add = moe_prev is not None
    row = lambda w: pl.BlockSpec((tm, w), lambda i: (i, 0))
    outs = tuple(jax.ShapeDtypeStruct((t, w), F32 if w == ZS_W else BF16) for w in Z_SPLITS)
    out_specs = tuple(row(w) for w in Z_SPLITS)
    consts = (g.reshape(1, D_MODEL), w_perm)
    res = pl.pallas_call(
        functools.partial(_in_proj_kernel, add=add),
        grid=(t // tm,),
        in_specs=[row(D_MODEL)] * (2 if add else 1) + [_const_spec(c) for c in consts],
        out_specs=((row(D_MODEL),) if add else ()) + out_specs,
        out_shape=((jax.ShapeDtypeStruct((t, D_MODEL), F32),) if add else ()) + outs,
        compiler_params=_cparams("parallel"),
        name="in_proj",
    )(*((x2d, moe_prev) if add else (x2d,)), *consts)
    return tuple(res) if add else (x2d,) + tuple(res)


def _final_norm_kernel(x_ref, m_ref, g_ref, o_ref):
    x = x_ref[...] + m_ref[...].astype(F32)
    o_ref[...] = x * lax.rsqrt(jnp.mean(x * x, axis=-1, keepdims=True) + EPS) * g_ref[...]


def _final_norm(x2d, moe_prev, g, tm=512):
    t = x2d.shape[0]
    row = pl.BlockSpec((tm, D_MODEL), lambda i: (i, 0))
    return pl.pallas_call(
        _final_norm_kernel,
        grid=(t // tm,),
        in_specs=[row, row, pl.BlockSpec((1, D_MODEL), lambda i: (0, 0))],
        out_specs=row,
        out_shape=jax.ShapeDtypeStruct((t, D_MODEL), F32),
        compiler_params=_cparams("parallel"),
        name="final_norm",
    )(x2d, moe_prev, g.reshape(1, D_MODEL).astype(F32))


def _permute_w_in(w_in):
    sizes = (256, 256, 128, 128, 256, 256, 16, 16, 256, 256, 256, 256, 16, 768)
    offs = np.concatenate([[0], np.cumsum(sizes)])
    seg = [w_in[:, offs[i]:offs[i + 1]] for i in range(len(sizes))]
    (a_x, a_g, b_q, b_k, b_v, b_r, b_lf, b_lb, c_q, c_k, c_v, c_o, c_g, d_u) = seg
    small = jnp.concatenate([b_lf, b_lb, c_g, jnp.zeros((w_in.shape[0], ZS_W - 48), w_in.dtype)], axis=1)
    return jnp.concatenate([a_x, a_g, b_q, b_k, b_v, b_r, c_q, c_k, c_v, c_o, d_u, small], axis=1).astype(BF16)


HALO = 16


def _strided_pitch(n):
    assert n % V7X_SUBLANES == 0
    return n if (n // V7X_SUBLANES) % 2 == 1 else n + V7X_SUBLANES


def _halo_conv(prev, cur, nxt, w, b, lo, first, last):
    tl = cur.shape[0]
    prev = jnp.where(first, 0.0, prev.astype(F32))
    nxt = jnp.where(last, 0.0, nxt.astype(F32))
    ext = jnp.concatenate([prev, cur.astype(F32), nxt], axis=0)
    n = tl + 2 * HALO
    acc = None
    for j in range(w.shape[0]):
        o = lo + j
        sh = ext if o == 0 else pltpu.roll(ext, (-o) % n, 0)
        term = sh[HALO:HALO + tl] * w[j:j + 1, :]
        acc = term if acc is None else acc + term
    return acc + b


def _rglru_kernel(prev_ref, cur_ref, next_ref, cw_ref, cb_ref, gw_ref, gb_ref, sp_ref, out_ref,
                  a_s, b_s, h_s, p_s, carry, *, tl, reverse):
    i = pl.program_id(1)
    nt = pl.num_programs(1)
    ti = (nt - 1 - i) if reverse else i
    u = _halo_conv(prev_ref[0], cur_ref[0], next_ref[0], cw_ref[...], cb_ref[...], -2, ti == 0, ti == nt - 1)
    gates = _sigmoid(_dot(u.astype(BF16), gw_ref[...]) + gb_ref[...])
    r = gates[:, :GROUP_W]
    ig = gates[:, GROUP_W:]
    log_a = (-RG_C) * r * sp_ref[...]
    a = jnp.exp(log_a)
    bt = jnp.sqrt(1.0 - a * a) * (ig * u)
    nseg = RG_SEGMENTS
    m = tl // nseg
    mp, sp = _strided_pitch(m), _strided_pitch(nseg)
    for k in range(2):
        for seg in range(nseg):
            a_s[k, seg * mp:seg * mp + m, :] = a[seg * m:(seg + 1) * m, k * V7X_LANES:(k + 1) * V7X_LANES]
            b_s[k, seg * mp:seg * mp + m, :] = bt[seg * m:(seg + 1) * m, k * V7X_LANES:(k + 1) * V7X_LANES]

    @pl.when(i == 0)
    def _():
        carry[...] = jnp.zeros_like(carry)

    def body(jj, hp):
        j = (m - 1 - jj) if reverse else jj
        out = []
        for k in range(2):
            h, p = hp[2 * k], hp[2 * k + 1]
            av = a_s[k, pl.ds(j, nseg, stride=mp), :]
            bv = b_s[k, pl.ds(j, nseg, stride=mp), :]
            h = av * h + bv
            p = av * p
            rows = pl.ds(pl.multiple_of(j * sp, V7X_SUBLANES), nseg)
            h_s[k, rows, :] = h
            p_s[k, rows, :] = p
            out += [h, p]
        return tuple(out)

    z8 = jnp.zeros((nseg, V7X_LANES), F32)
    o8 = jnp.ones((nseg, V7X_LANES), F32)
    ends = lax.fori_loop(0, m, body, (z8, o8, z8, o8), unroll=RG_UNROLL)
    for k in range(2):
        h_end, p_end = ends[2 * k], ends[2 * k + 1]
        c = carry[k]
        cin = [None] * nseg
        for seg in (range(nseg - 1, -1, -1) if reverse else range(nseg)):
            cin[seg] = c
            c = h_end[seg:seg + 1, :] + p_end[seg:seg + 1, :] * c
        carry[k] = c
        for seg in range(nseg):
            hv = h_s[k, pl.ds(seg, m, stride=sp), :]
            pv = p_s[k, pl.ds(seg, m, stride=sp), :]
            out_ref[0, seg * m:(seg + 1) * m, k * V7X_LANES:(k + 1) * V7X_LANES] = (hv + pv * cin[seg]).astype(out_ref.dtype)


def _rglru(za, conv_w, conv_b, gate_w_bd, gate_b, softplus_neg_lam, reverse, tl=512):
    b, l, _ = za.shape
    nt = l // tl
    rh = tl // HALO
    nbh = l // HALO

    def tmap(bi, i):
        return (nt - 1 - i) if reverse else i

    kern = functools.partial(_rglru_kernel, tl=tl, reverse=reverse)
    return pl.pallas_call(
        kern,
        grid=(b, nt),
        in_specs=[
            pl.BlockSpec((1, HALO, GROUP_W), lambda bi, i: (bi, jnp.maximum(tmap(bi, i) * rh - 1, 0), 0)),
            pl.BlockSpec((1, tl, GROUP_W), lambda bi, i: (bi, tmap(bi, i), 0)),
            pl.BlockSpec((1, HALO, GROUP_W), lambda bi, i: (bi, jnp.minimum((tmap(bi, i) + 1) * rh, nbh - 1), 0)),
            pl.BlockSpec((4, GROUP_W), lambda bi, i: (0, 0)),
            pl.BlockSpec((1, GROUP_W), lambda bi, i: (0, 0)),
            pl.BlockSpec((GROUP_W, 2 * GROUP_W), lambda bi, i: (0, 0)),
            pl.BlockSpec((1, 2 * GROUP_W), lambda bi, i: (0, 0)),
            pl.BlockSpec((1, GROUP_W), lambda bi, i: (0, 0)),
        ],
        out_specs=pl.BlockSpec((1, tl, GROUP_W), lambda bi, i: (bi, tmap(bi, i), 0)),
        out_shape=jax.ShapeDtypeStruct((b, l, GROUP_W), BF16),
        scratch_shapes=[pltpu.VMEM((2, RG_SEGMENTS * _strided_pitch(tl // RG_SEGMENTS), V7X_LANES), F32)] * 2
        + [pltpu.VMEM((2, (tl // RG_SEGMENTS) * _strided_pitch(RG_SEGMENTS), V7X_LANES), F32)] * 2
        + [pltpu.VMEM((2, 1, V7X_LANES), F32)],
        compiler_params=_cparams("parallel", "arbitrary"),
        name="rglru_bwd" if reverse else "rglru_fwd",
    )(za, za, za, conv_w, conv_b, gate_w_bd, gate_b, softplus_neg_lam)


def _block_diag(blocks):
    h, di, do = blocks.shape
    eye = jnp.eye(h, dtype=blocks.dtype)
    return jnp.einsum('hde,hg->hdge', blocks, eye).reshape(h * di, h * do)


def _chunk_masks(reverse):
    r = np.arange(GROUP_W)
    t = np.arange(CHUNK)
    tri = (t[None, :] >= t[:, None]) if reverse else (t[None, :] <= t[:, None])
    cmask = np.tile(tri, (1, N_HEADS))
    bd = (r[:, None] // HEAD_W) == (r[None, :] // HEAD_W)
    kd = (r[:, None] // HEAD_W) == (np.arange(N_HEADS * B_DK)[None, :] // B_DK)
    i2 = np.tile(np.eye(CHUNK, dtype=bool), (1, N_HEADS))
    hm = np.zeros((8, GROUP_W), bool)
    hm[:N_HEADS] = np.arange(N_HEADS)[:, None] == (r[None, :] // HEAD_W)
    return dict(tri=jnp.asarray(tri, BF16), cmask=jnp.asarray(cmask, F32), bd=jnp.asarray(bd, BF16),
                kd=jnp.asarray(kd, BF16), kdf=jnp.asarray(kd, F32),
                i2=jnp.asarray(i2, F32), hm=jnp.asarray(hm, F32))


def _const_spec(arr):
    nd = arr.ndim
    return pl.BlockSpec(arr.shape, lambda *_: (0,) * nd)


def _gla_chunk(zb_ref, la, tri_ref, kd_ref, kdf_ref, bd_ref, causal, out_ref, st_ref, c):
    rows = slice(c * CHUNK, (c + 1) * CHUNK)
    q = zb_ref[0, rows, 0:128].astype(F32) * (B_DK ** -0.5)
    k = zb_ref[0, rows, 128:256].astype(F32)
    vb = zb_ref[0, rows, 256:512]
    la_c = la[rows]
    bcum = _dot_split(tri_ref[...], la_c)
    btot = jnp.sum(la_c, axis=0, keepdims=True)
    q_in = (q * jnp.exp(bcum)).astype(BF16)
    k_in = (k * jnp.exp(-bcum)).astype(BF16)
    k_st = (k * jnp.exp(btot - bcum)).astype(BF16)
    kexp = jnp.concatenate([k_in] * N_HEADS, axis=0) * kd_ref[...]
    att = jnp.where(causal, _dot_nt(q_in, kexp), 0.0)
    vbd = jnp.concatenate([vb] * N_HEADS, axis=0) * bd_ref[...]
    st = st_ref[...]
    out_ref[0, rows, :] = (_dot(att.astype(BF16), vbd) + _dot_nt(q_in, st.astype(BF16))).astype(out_ref.dtype)
    st_ref[...] = st * jnp.exp(btot) + _dot_tn(vb, k_st) * kdf_ref[...]


def _gla_kernel(zbf_ref, zsf_ref, zbb_ref, zsb_ref, gwf_ref, gbf_ref, gwb_ref, gbb_ref, trif_ref, trib_ref,
                kd_ref, kdf_ref, bd_ref, cmf_ref, cmb_ref, outf_ref, outb_ref, stf_ref, stb_ref, *, tl):
    @pl.when(pl.program_id(1) == 0)
    def _():
        stf_ref[...] = jnp.zeros_like(stf_ref)
        stb_ref[...] = jnp.zeros_like(stb_ref)

    log_alpha = lambda zs_ref, gw_ref, gb_ref: _log_sigmoid(
        _dot(zs_ref[0].astype(BF16), gw_ref[...]) + gb_ref[...]) * (1.0 / GLA_TAU)
    laf = log_alpha(zsf_ref, gwf_ref, gbf_ref)
    lab = log_alpha(zsb_ref, gwb_ref, gbb_ref)
    causal_f = cmf_ref[...] > 0.0
    causal_b = cmb_ref[...] > 0.0
    nch = tl // CHUNK
    for c in range(nch):
        _gla_chunk(zbf_ref, laf, trif_ref, kd_ref, kdf_ref, bd_ref, causal_f, outf_ref, stf_ref, c)
        _gla_chunk(zbb_ref, lab, trib_ref, kd_ref, kdf_ref, bd_ref, causal_b, outb_ref, stb_ref, nch - 1 - c)


def _gla(zb, zs, prep_f, prep_b, tl=512):
    b, l, _ = zb.shape
    nt = l // tl
    mf, mb = _chunk_masks(False), _chunk_masks(True)
    consts = (*prep_f, *prep_b, mf['tri'], mb['tri'], mf['kd'], mf['kdf'], mf['bd'], mf['cmask'], mb['cmask'])
    fmap = lambda bi, i: (bi, i, 0)
    bmap = lambda bi, i: (bi, nt - 1 - i, 0)
    o = jax.ShapeDtypeStruct((b, l, GROUP_W), BF16)
    st = pltpu.VMEM((GROUP_W, N_HEADS * B_DK), F32)
    return pl.pallas_call(
        functools.partial(_gla_kernel, tl=tl),
        grid=(b, nt),
        in_specs=[pl.BlockSpec((1, tl, ZB_W), fmap), pl.BlockSpec((1, tl, ZS_W), fmap),
                  pl.BlockSpec((1, tl, ZB_W), bmap), pl.BlockSpec((1, tl, ZS_W), bmap)]
        + [_const_spec(c) for c in consts],
        out_specs=(pl.BlockSpec((1, tl, GROUP_W), fmap), pl.BlockSpec((1, tl, GROUP_W), bmap)),
        out_shape=(o, o),
        scratch_shapes=[st, st],
        compiler_params=_cparams("parallel", "arbitrary"),
        name="gla",
    )(zb, zs, zb, zs, *consts)


def _mlstm_chunk(zc_ref, ig_all, lf_all, tri_ref, i2_ref, bd_ref, causal, hm_ref, out_ref, s_ref, n_ref, m_ref, c):
    rows = slice(c * CHUNK, (c + 1) * CHUNK)
    qb = zc_ref[0, rows, 0:256]
    k = zc_ref[0, rows, 256:512].astype(F32) * (HEAD_W ** -0.5)
    kb = k.astype(BF16)
    vb = zc_ref[0, rows, 512:768]
    ig = ig_all[rows]
    lf = lf_all[rows]
    bcum = _dot_split(tri_ref[...], lf)
    blast = jnp.sum(lf, axis=0, keepdims=True)
    cc = jnp.sum((ig - bcum) * i2_ref[...], axis=0, keepdims=True)
    d2 = jnp.where(causal, bcum + cc, NEG_BIG)
    m_intra = jnp.full((CHUNK, GROUP_W), NEG_BIG, F32)
    for h in range(N_HEADS):
        hsel = hm_ref[h:h + 1, :] > 0.0
        mh = jnp.max(jnp.where(hsel, d2, NEG_BIG), axis=1, keepdims=True)
        m_intra = jnp.where(hsel, mh, m_intra)
    m_prev = m_ref[...]
    inter_log = bcum + m_prev
    m_t = jnp.maximum(inter_log, m_intra)
    w_intra = jnp.exp(d2 - m_t)
    w_inter = jnp.exp(inter_log - m_t)
    kexp = jnp.concatenate([kb] * N_HEADS, axis=0) * bd_ref[...]
    qk = (_dot_nt(qb, kexp) * w_intra).astype(BF16)
    vbd = jnp.concatenate([vb] * N_HEADS, axis=0) * bd_ref[...]
    s_prev = s_ref[...]
    n_prev = n_ref[...]
    sbd = jnp.concatenate([s_prev.astype(BF16)] * N_HEADS, axis=0) * bd_ref[...]
    nbd = jnp.concatenate([n_prev.astype(BF16)] * N_HEADS, axis=0) * bd_ref[...]
    num = _dot(qk, vbd) + w_inter * _dot(qb, sbd)
    den = _dot(qk, bd_ref[...]) + w_inter * _dot(qb, nbd)
    out_ref[0, rows, :] = (num / jnp.maximum(jnp.abs(den), jnp.exp(-m_t))).astype(out_ref.dtype)
    g_end = blast - bcum + ig
    g_max = jnp.max(g_end, axis=0, keepdims=True)
    kw = (k * jnp.exp(g_end - g_max)).astype(BF16)
    m_new = jnp.maximum(blast + m_prev, g_max)
    a = jnp.exp(blast + m_prev - m_new)
    cf = jnp.exp(g_max - m_new)
    uc_full = _dot_tn(kw, vb)
    un_full = _dot_tn(kw, jnp.ones((CHUNK, GROUP_W), BF16))
    uc = jnp.zeros((HEAD_W, GROUP_W), F32)
    un = jnp.zeros((HEAD_W, GROUP_W), F32)
    for h in range(N_HEADS):
        hsel = hm_ref[h:h + 1, :] > 0.0
        blk = slice(h * HEAD_W, (h + 1) * HEAD_W)
        uc = jnp.where(hsel, uc_full[blk], uc)
        un = jnp.where(hsel, un_full[blk], un)
    s_ref[...] = a * s_prev + cf * uc
    n_ref[...] = a * n_prev + cf * un
    m_ref[...] = m_new


def _mlstm_kernel(zcf_ref, zsf_ref, zcb_ref, zsb_ref, ef_ref, ebf_ref, eb_ref, ebb_ref, trif_ref, trib_ref,
                  i2_ref, bd_ref, cmf_ref, cmb_ref, hm_ref, outf_ref, outb_ref,
                  sf_ref, nf_ref, mf_ref, sb_ref, nb_ref, mb_ref, *, tl):
    @pl.when(pl.program_id(1) == 0)
    def _():
        for s_ref, n_ref, m_ref in ((sf_ref, nf_ref, mf_ref), (sb_ref, nb_ref, mb_ref)):
            s_ref[...] = jnp.zeros_like(s_ref)
            n_ref[...] = jnp.zeros_like(n_ref)
            m_ref[...] = jnp.full_like(m_ref, NEG_BIG)

    def gates(zs_ref, e_ref, eb_ref):
        ge = _split_dot(zs_ref[0], e_ref[...]) + eb_ref[...]
        return ge[:, :GROUP_W], _log_sigmoid(ge[:, GROUP_W:])

    igf, lff = gates(zsf_ref, ef_ref, ebf_ref)
    igb, lfb = gates(zsb_ref, eb_ref, ebb_ref)
    causal_f = cmf_ref[...] > 0.0
    causal_b = cmb_ref[...] > 0.0
    nch = tl // CHUNK
    for c in range(nch):
        _mlstm_chunk(zcf_ref, igf, lff, trif_ref, i2_ref, bd_ref, causal_f, hm_ref, outf_ref,
                     sf_ref, nf_ref, mf_ref, c)
        _mlstm_chunk(zcb_ref, igb, lfb, trib_ref, i2_ref, bd_ref, causal_b, hm_ref, outb_ref,
                     sb_ref, nb_ref, mb_ref, nch - 1 - c)


def _mlstm(zc, zs, prep_f, prep_b, tl=512):
    b, l, _ = zc.shape
    nt = l // tl
    mf, mb = _chunk_masks(False), _chunk_masks(True)
    consts = (*prep_f, *prep_b, mf['tri'], mb['tri'], mf['i2'], mf['bd'], mf['cmask'], mb['cmask'], mf['hm'])
    fmap = lambda bi, i: (bi, i, 0)
    bmap = lambda bi, i: (bi, nt - 1 - i, 0)
    o = jax.ShapeDtypeStruct((b, l, GROUP_W), BF16)
    state = [pltpu.VMEM((HEAD_W, GROUP_W), F32), pltpu.VMEM((HEAD_W, GROUP_W), F32), pltpu.VMEM((1, GROUP_W), F32)]
    return pl.pallas_call(
        functools.partial(_mlstm_kernel, tl=tl),
        grid=(b, nt),
        in_specs=[pl.BlockSpec((1, tl, ZC_W), fmap), pl.BlockSpec((1, tl, ZS_W), fmap),
                  pl.BlockSpec((1, tl, ZC_W), bmap), pl.BlockSpec((1, tl, ZS_W), bmap)]
        + [_const_spec(c) for c in consts],
        out_specs=(pl.BlockSpec((1, tl, GROUP_W), fmap), pl.BlockSpec((1, tl, GROUP_W), bmap)),
        out_shape=(o, o),
        scratch_shapes=state + state,
        compiler_params=_cparams("parallel", "arbitrary"),
        name="mlstm",
    )(zc, zs, zc, zs, *consts)


def _mixer_prep(P, l):
    out = {}
    gw, gb, lam = P['a_gate_w'][l], P['a_gate_b'][l], P['a_lambda'][l]
    for d in range(2):
        w = jnp.concatenate([_block_diag(gw[d, 0]), _block_diag(gw[d, 1])], axis=1).astype(BF16)
        bb = jnp.concatenate([gb[d, 0].reshape(1, -1), gb[d, 1].reshape(1, -1)], axis=1).astype(F32)
        nl = -lam[d].astype(F32)
        sp = (jnp.maximum(nl, 0.0) + jnp.log(1.0 + jnp.exp(-jnp.abs(nl)))).reshape(1, -1)
        out['a', d] = (w, bb, sp)
        bw = jnp.zeros((ZS_W, N_HEADS * B_DK), F32).at[B_RANK * d:B_RANK * (d + 1)].set(P['b_gate_w'][l][d])
        out['b', d] = (bw.astype(BF16), P['b_gate_b'][l][d].reshape(1, -1).astype(F32))
        e = np.zeros((ZS_W, 2 * GROUP_W), np.float32)
        for h in range(N_HEADS):
            e[32 + 8 * d + h, h * HEAD_W:(h + 1) * HEAD_W] = 1.0
            e[36 + 8 * d + h, GROUP_W + h * HEAD_W:GROUP_W + (h + 1) * HEAD_W] = 1.0
        cb = P['c_gate_b'][l].astype(F32)
        eb = jnp.concatenate([jnp.repeat(cb[2 * d], HEAD_W), jnp.repeat(cb[2 * d + 1], HEAD_W)]).reshape(1, -1)
        out['c', d] = (jnp.asarray(e, BF16), eb)
    return out


G_ROWS = 8
G_ROWS_BF16 = 16


def _slab(k):
    return slice(k * V7X_LANES, (k + 1) * V7X_LANES)


def _scatter_rows(scr, base, val, rows, pitch, blk0=0):
    for s in range(val.shape[1] // V7X_LANES):
        for b in range(val.shape[0] // rows):
            r0 = (blk0 + b) * pitch
            scr[base + s, r0:r0 + rows, :] = val[b * rows:(b + 1) * rows, _slab(s)]


def _gather_n2_major(scr, outs, n2, pitch):
    nsl = GROUP_W // V7X_LANES

    def body(j, carry):
        for a, (ref, lead) in enumerate(outs):
            for s in range(nsl):
                ref[lead, j, :, _slab(s)] = scr[a * nsl + s, pl.ds(j, G_ROWS, stride=pitch), :]
        return carry

    lax.fori_loop(0, n2, body, 0, unroll=4)


def _conv3_kernel(prev_ref, cur_ref, next_ref, w_ref, b_ref, v_ref, x1_ref, x2_ref, scr, *, n2, pitch):
    i = pl.program_id(1)
    u = _halo_conv(prev_ref[0], cur_ref[0], next_ref[0], w_ref[...], b_ref[...], -1, i == 0,
                   i == pl.num_programs(1) - 1)
    nsl = GROUP_W // V7X_LANES
    for a in range(3):
        _scatter_rows(scr, a * nsl, u[:, a * GROUP_W:(a + 1) * GROUP_W], n2, pitch)
    _gather_n2_major(scr, ((v_ref, 0), (x1_ref, 0), (x2_ref, 0)), n2, pitch)


def _conv3_split(zd, w, b, n1, n2):
    bsz, l, _ = zd.shape
    tl = G_ROWS * n2
    pitch = _strided_pitch(n2)
    nt, rh, nbh = l // tl, tl // HALO, l // HALO
    o = jax.ShapeDtypeStruct((bsz, n2, n1 // 2, GROUP_W), F32)
    ospec = pl.BlockSpec((1, n2, G_ROWS, GROUP_W), lambda bi, i: (bi, 0, i, 0))
    return pl.pallas_call(
        functools.partial(_conv3_kernel, n2=n2, pitch=pitch),
        grid=(bsz, nt),
        in_specs=[pl.BlockSpec((1, HALO, ZD_W), lambda bi, i: (bi, jnp.maximum(i * rh - 1, 0), 0)),
                  pl.BlockSpec((1, tl, ZD_W), lambda bi, i: (bi, i, 0)),
                  pl.BlockSpec((1, HALO, ZD_W), lambda bi, i: (bi, jnp.minimum((i + 1) * rh, nbh - 1), 0)),
                  pl.BlockSpec((3, ZD_W), lambda bi, i: (0, 0)),
                  pl.BlockSpec((1, ZD_W), lambda bi, i: (0, 0))],
        out_specs=(ospec, ospec, ospec),
        out_shape=(o, o, o),
        scratch_shapes=[pltpu.VMEM((3 * GROUP_W // V7X_LANES, G_ROWS * pitch, V7X_LANES), F32)],
        compiler_params=_cparams("parallel", "parallel"),
        name="hyena_conv3",
    )(zd, zd, zd, w, b.reshape(1, ZD_W))


def _dot_hp(a, b):
    return jnp.dot(a, b, preferred_element_type=F32, precision=lax.Precision.HIGHEST)


FILTER_SUB = 512


def _filter_mlp_kernel(emb_ref, w1_ref, b1_ref, f0_ref, w2_ref, b2_ref, f1_ref, w3_ref, dl_ref,
                       fwd_ref, sec_ref, s_ref, scr, *, n2, pitch):
    i = pl.program_id(0)
    tl = emb_ref.shape[0]
    nsl = GROUP_W // V7X_LANES

    @pl.when(i == 0)
    def _():
        s_ref[...] = jnp.zeros_like(s_ref)

    for sub in range(tl // FILTER_SUB):
        emb = emb_ref[sub * FILTER_SUB:(sub + 1) * FILTER_SUB, :]
        h = jnp.sin(f0_ref[...] * (_dot_hp(emb, w1_ref[...]) + b1_ref[...]))
        h = jnp.sin(f1_ref[...] * (_dot_hp(h, w2_ref[...]) + b2_ref[...]))
        h = _dot_hp(h, w3_ref[...])
        dl = dl_ref[...]
        hf = h[:, :2 * GROUP_W] * jnp.exp(-emb[:, 0:1] * dl)
        hb = h[:, 2 * GROUP_W:] * jnp.exp(-emb[:, HY_EMB:HY_EMB + 1] * dl)
        if sub == 0:
            row = _iota(hb.shape, 0) + i * tl
            hb = jnp.where(row == 0, 0.0, hb)
        s_ref[...] += jnp.sum(jnp.abs(hf) + jnp.abs(hb), axis=0, keepdims=True)
        blk0 = sub * FILTER_SUB // n2
        for o in range(2):
            _scatter_rows(scr, o * nsl, hf[:, o * GROUP_W:(o + 1) * GROUP_W], n2, pitch, blk0)
            _scatter_rows(scr, (2 + o) * nsl, hb[:, o * GROUP_W:(o + 1) * GROUP_W], n2, pitch, blk0)
    _gather_n2_major(scr, ((fwd_ref, 0), (fwd_ref, 1), (sec_ref, 0), (sec_ref, 1)), n2, pitch)


def _hyena_filter(l, w1, b1, freq, w2, b2, w3, n1, n2):
    tl = G_ROWS * n2
    bands = (HY_EMB - 1) // 2
    f = jnp.linspace(1e-4, bands - 1, bands, dtype=F32)

    def embed(pos):
        t = pos / max(l - 1, 1)
        ang = (2.0 * math.pi / l) * pos[:, None] * f[None, :]
        return jnp.concatenate([t[:, None], jnp.cos(ang), -jnp.sin(ang)], axis=-1)

    pos = jnp.arange(l, dtype=F32)
    emb = jnp.concatenate([embed(pos), embed(l - pos), jnp.zeros((l, V7X_LANES - 2 * HY_EMB), F32)], axis=-1)
    deltas = jnp.abs(jnp.linspace(math.log(HY_TARGET) / HY_SLOW, math.log(HY_TARGET) / HY_FAST, GROUP_W, dtype=F32))
    z = lambda r, c: jnp.zeros((r, c), F32)
    w1f, w2f = w1.astype(F32), w2.astype(F32)
    w3r = w3.astype(F32).reshape(HY_HID, 2, 2, GROUP_W)
    w3d = lambda d: w3r[:, :, d].reshape(HY_HID, 2 * GROUP_W)
    w1p = z(V7X_LANES, V7X_LANES).at[:HY_EMB, :HY_HID].set(w1f).at[HY_EMB:2 * HY_EMB, HY_HID:].set(w1f)
    w2p = z(V7X_LANES, V7X_LANES).at[:HY_HID, :HY_HID].set(w2f).at[HY_HID:, HY_HID:].set(w2f)
    w3p = z(V7X_LANES, 4 * GROUP_W).at[:HY_HID, :2 * GROUP_W].set(w3d(0)).at[HY_HID:, 2 * GROUP_W:].set(w3d(1))
    vec = lambda x: jnp.tile(x.astype(F32), 2).reshape(1, V7X_LANES)
    consts = (w1p, vec(b1), vec(freq[0]), w2p, vec(b2), vec(freq[1]), w3p, jnp.tile(deltas, 2).reshape(1, -1))
    pitch = _strided_pitch(n2)
    half = jax.ShapeDtypeStruct((2, n2, n1 // 2, GROUP_W), F32)
    hspec = pl.BlockSpec((2, n2, G_ROWS, GROUP_W), lambda i: (0, 0, i, 0))
    return pl.pallas_call(
        functools.partial(_filter_mlp_kernel, n2=n2, pitch=pitch),
        grid=(l // tl,),
        in_specs=[pl.BlockSpec((tl, V7X_LANES), lambda i: (i, 0))] + [_const_spec(c) for c in consts],
        out_specs=(hspec, hspec, pl.BlockSpec((1, 2 * GROUP_W), lambda i: (0, 0))),
        out_shape=(half, half, jax.ShapeDtypeStruct((1, 2 * GROUP_W), F32)),
        scratch_shapes=[pltpu.VMEM((4 * GROUP_W // V7X_LANES, G_ROWS * pitch, V7X_LANES), F32)],
        compiler_params=_cparams("arbitrary"),
        name="hyena_filter_mlp",
    )(emb, *consts)


def _fft_factors(l):
    return (64, 128) if l == 4096 else (2 * l // 256, 256)


@functools.lru_cache(maxsize=None)
def _fft_tables(n1, n2):
    n = n1 * n2
    k1 = np.arange(n1)[None, :, None]
    m2 = np.arange(n2)[:, None, None]

    def theta(n1_vals):
        idx = (k1 * (n1_vals[None, None, :] * n2 + m2)) % n
        return (2.0 * np.pi / n) * idx
    th = theta(np.arange(n1 // 2))
    fr, fi = np.cos(th), -np.sin(th)
    fwd = np.concatenate([np.concatenate([fr, -fi], -1), np.concatenate([fi, fr], -1)], -2)
    thf = theta(np.arange(n1))
    filt = np.concatenate([np.cos(thf), -np.sin(thf)], -2)
    cr = np.swapaxes(np.cos(th), 1, 2) / n
    ci = np.swapaxes(np.sin(th), 1, 2) / n
    inv = np.concatenate([np.concatenate([cr, -ci], -1), np.concatenate([ci, cr], -1)], -2)
    t2 = (2.0 * np.pi / n2) * ((np.arange(n2)[:, None] * np.arange(n2)[None, :]) % n2)
    gr, gi = np.cos(t2), -np.sin(t2)
    g = np.block([[gr, -gi], [gi, gr]])
    ginv = np.block([[gr, gi], [-gi, gr]])
    return tuple(jnp.asarray(a, BF16) for a in (fwd, filt, inv, g, ginv))


def _regroup_rows(scr, n_rows, pitch, store):
    def body(r, carry):
        for s in range(GROUP_W // V7X_LANES):
            store(r, s, scr[s, pl.ds(r, G_ROWS_BF16, stride=pitch), :])
        return carry

    lax.fori_loop(0, n_rows, body, 0, unroll=4)


def _fft_stage1_kernel(zr_ref, zi_ref, t_ref, a_ref, scr, *, n1, pitch):
    for j in range(G_ROWS_BF16):
        rhs = jnp.concatenate([zr_ref[0, j], zi_ref[0, j]], axis=0).astype(BF16)
        _scatter_rows(scr, 0, _dot(t_ref[j], rhs), 2 * n1, pitch, j)

    def store(r, s, rows):
        a_ref[0, r, :, _slab(s)] = rows.astype(BF16)

    _regroup_rows(scr, 2 * n1, pitch, store)


def _fft_stage1(zr, zi, table, n1, n2, pairs, idx_r, idx_i):
    nb = G_ROWS_BF16
    pitch = _strided_pitch(2 * n1)
    return pl.pallas_call(
        functools.partial(_fft_stage1_kernel, n1=n1, pitch=pitch),
        grid=(pairs, n2 // nb),
        in_specs=[pl.BlockSpec((1, nb, n1 // 2, GROUP_W), lambda p, j: (idx_r(p), j, 0, 0)),
                  pl.BlockSpec((1, nb, n1 // 2, GROUP_W), lambda p, j: (idx_i(p), j, 0, 0)),
                  pl.BlockSpec((nb, 2 * n1, n1), lambda p, j: (j, 0, 0))],
        out_specs=pl.BlockSpec((1, 2 * n1, nb, GROUP_W), lambda p, j: (p, 0, j, 0)),
        out_shape=jax.ShapeDtypeStruct((pairs, 2 * n1, n2, GROUP_W), BF16),
        scratch_shapes=[pltpu.VMEM((GROUP_W // V7X_LANES, nb * pitch, V7X_LANES), F32)],
        compiler_params=_cparams("parallel", "parallel"),
        name="fft_stage1",
    )(zr, zi, table)


def _a_specs(n1, n2, kb, index):
    def spec(part):
        def imap(i, p):
            pp, kk = index(i, p)
            return (pp, part * (n1 // kb) + kk, 0, 0)
        return pl.BlockSpec((1, kb, n2, GROUP_W), imap)
    return [spec(0), spec(1)]


def _fft_spec_kernel(ar_ref, ai_ref, g_ref, s_ref, h_ref, *, kb, n2):
    for k in range(kb):
        rhs = jnp.concatenate([ar_ref[0, k], ai_ref[0, k]], axis=0)
        x = _dot(g_ref[...], rhs) * s_ref[0]
        h_ref[0, k, 0] = x[:n2]
        h_ref[0, k, 1] = x[n2:]


def _fft_filter_spectrum(a, g, inv_s, n1, n2, kb=8):
    orders = a.shape[0]
    return pl.pallas_call(
        functools.partial(_fft_spec_kernel, kb=kb, n2=n2),
        grid=(orders, n1 // kb),
        in_specs=_a_specs(n1, n2, kb, lambda o, i: (o, i))
        + [_const_spec(g), pl.BlockSpec((1, 1, GROUP_W), lambda o, i: (o, 0, 0))],
        out_specs=pl.BlockSpec((1, kb, 2, n2, GROUP_W), lambda o, i: (o, i, 0, 0, 0)),
        out_shape=jax.ShapeDtypeStruct((orders, n1, 2, n2, GROUP_W), F32),
        compiler_params=_cparams("parallel", "parallel"),
        name="fft_filter_spectrum",
    )(a, a, g, inv_s)


def _fft_mid_kernel(ar_ref, ai_ref, g_ref, gi_ref, h_ref, b_ref, scr, *, n2, pitch):
    for k in range(G_ROWS_BF16):
        rhs = jnp.concatenate([ar_ref[0, k], ai_ref[0, k]], axis=0)
        x = _dot(g_ref[...], rhs)
        xr, xi = x[:n2], x[n2:]
        hr, hi = h_ref[0, k, 0], h_ref[0, k, 1]
        y = jnp.concatenate([xr * hr - xi * hi, xr * hi + xi * hr], axis=0).astype(BF16)
        _scatter_rows(scr, 0, _dot(gi_ref[...], y), 2 * n2, pitch, k)

    def store(r, s, rows):
        b_ref[0, r, :, _slab(s)] = rows.astype(BF16)

    _regroup_rows(scr, 2 * n2, pitch, store)


def _fft_mid(a, g, ginv, spec, order, n1, n2):
    pairs = a.shape[0]
    kb = G_ROWS_BF16
    pitch = _strided_pitch(2 * n2)
    return pl.pallas_call(
        functools.partial(_fft_mid_kernel, n2=n2, pitch=pitch),
        grid=(n1 // kb, pairs),
        in_specs=_a_specs(n1, n2, kb, lambda i, p: (p, i)) + [_const_spec(g), _const_spec(ginv),
                  pl.BlockSpec((1, kb, 2, n2, GROUP_W), lambda i, p: (order, i, 0, 0, 0))],
        out_specs=pl.BlockSpec((1, 2 * n2, kb, GROUP_W), lambda i, p: (p, 0, i, 0)),
        out_shape=jax.ShapeDtypeStruct((pairs, 2 * n2, n1, GROUP_W), BF16),
        scratch_shapes=[pltpu.VMEM((GROUP_W // V7X_LANES, kb * pitch, V7X_LANES), F32)],
        compiler_params=_cparams("parallel", "parallel"),
        name="fft_mid",
    )(a, a, g, ginv, spec)


def _fft_stage3_kernel(br_ref, bi_ref, t_ref, ur_ref, ui_ref, gr_ref, gi_ref, sk_ref, y_ref, *scr,
                       n1, pitch, time_major):
    h = n1 // 2
    sk = sk_ref[...]
    for j in range(G_ROWS_BF16):
        rhs = jnp.concatenate([br_ref[0, j], bi_ref[0, j]], axis=0)
        o = _dot(t_ref[j], rhs)
        yr = gr_ref[0, j] * (o[:h] + ur_ref[0, j] * sk)
        yi = gi_ref[0, j] * (o[h:] + ui_ref[0, j] * sk)
        if time_major:
            _scatter_rows(scr[0], 0, yr, h, pitch, j)
            _scatter_rows(scr[1], 0, yi, h, pitch, j)
        else:
            y_ref[0, 0, j] = yr
            y_ref[0, 1, j] = yi
    if time_major:
        for part in range(2):
            def store(r, s, rows, part=part):
                y_ref[0, part, r, :, _slab(s)] = rows.astype(y_ref.dtype)
            _regroup_rows(scr[part], h, pitch, store)


def _fft_stage3(b, table, u, gate, skip, n1, n2, time_major):
    pairs = b.shape[0]
    nb = G_ROWS_BF16
    h = n1 // 2
    pitch = _strided_pitch(h)
    bspec = lambda part: pl.BlockSpec((1, nb, n1, GROUP_W), lambda p, j: (p, part * (n2 // nb) + j, 0, 0))
    even = pl.BlockSpec((1, nb, h, GROUP_W), lambda p, j: (2 * p, j, 0, 0))
    odd = pl.BlockSpec((1, nb, h, GROUP_W), lambda p, j: (2 * p + 1, j, 0, 0))
    if time_major:
        out_spec = pl.BlockSpec((1, 2, h, nb, GROUP_W), lambda p, j: (p, 0, 0, j, 0))
        out_shape = jax.ShapeDtypeStruct((pairs, 2, h, n2, GROUP_W), BF16)
        scratch = [pltpu.VMEM((GROUP_W // V7X_LANES, nb * pitch, V7X_LANES), F32)] * 2
    else:
        out_spec = pl.BlockSpec((1, 2, nb, h, GROUP_W), lambda p, j: (p, 0, j, 0, 0))
        out_shape = jax.ShapeDtypeStruct((pairs, 2, n2, h, GROUP_W), F32)
        scratch = []
    y = pl.pallas_call(
        functools.partial(_fft_stage3_kernel, n1=n1, pitch=pitch, time_major=time_major),
        grid=(pairs, n2 // nb),
        in_specs=[bspec(0), bspec(1), pl.BlockSpec((nb, n1, 2 * n1), lambda p, j: (j, 0, 0)),
                  even, odd, even, odd, pl.BlockSpec((1, GROUP_W), lambda p, j: (0, 0))],
        out_specs=out_spec,
        out_shape=out_shape,
        scratch_shapes=scratch,
        compiler_params=_cparams("parallel", "parallel"),
        name="fft_stage3_out" if time_major else "fft_stage3",
    )(b, b, table, u, u, gate, gate, skip.reshape(1, GROUP_W).astype(F32))
    return y.reshape(2 * pairs, h * n2, GROUP_W) if time_major else y.reshape(2 * pairs, n2, h, GROUP_W)


def _hyena_spectrum(l, w1, b1, freq, w2, b2, w3):
    n1, n2 = _fft_factors(l)
    _, t_filt, _, g, _ = _fft_tables(n1, n2)
    first, second, sums = _hyena_filter(l, w1, b1, freq, w2, b2, w3, n1, n2)
    a = _fft_stage1(first, second, t_filt, n1, n2, 2, lambda p: p, lambda p: p)
    inv_s = (1.0 / sums).reshape(2, 1, GROUP_W)
    return _fft_filter_spectrum(a, g, inv_s, n1, n2)


def _hyena_long_conv(u, gate, spec, order, skip, n1, n2, time_major):
    t_fwd, _, t_inv, g, ginv = _fft_tables(n1, n2)
    a = _fft_stage1(u, u, t_fwd, n1, n2, u.shape[0] // 2, lambda p: 2 * p, lambda p: 2 * p + 1)
    b = _fft_mid(a, g, ginv, spec, order, n1, n2)
    return _fft_stage3(b, t_inv, u, gate, skip, n1, n2, time_major)


def _mixer_hyena(zd, P, l, spec):
    n1, n2 = _fft_factors(zd.shape[1])
    v, x1, x2 = _conv3_split(zd, P['d_conv_w'][l].astype(F32), P['d_conv_b'][l].astype(F32), n1, n2)
    z = _hyena_long_conv(v, x1, spec, 0, P['d_skip'][l][0], n1, n2, False)
    return _hyena_long_conv(z, x2, spec, 1, P['d_skip'][l][1], n1, n2, True)


def _gelu_tanh(x):
    return 0.5 * x * (1.0 + jnp.tanh(math.sqrt(2.0 / math.pi) * (x + 0.044715 * (x * x * x))))


def _head_rms(h, bd):
    ss = _dot((h * h).astype(BF16), bd) * (1.0 / HEAD_W)
    return h * lax.rsqrt(ss + EPS)


def _out_proj_kernel(x_ref, af_ref, ab_ref, ag_ref, bf_ref, bb_ref, br_ref, cf_ref, cb_ref, co_ref, yd_ref,
                     w_ref, bg_ref, cg_ref, bd_ref, n2_ref, rwh_ref, rwl_ref, rb_ref,
                     xo_ref, hn_ref, lg_ref):
    bd = bd_ref[...]
    f = lambda ref: ref[...].astype(F32)
    ya = (f(af_ref) + f(ab_ref)) * _gelu_tanh(f(ag_ref))
    r = f(br_ref)
    yb = _head_rms(f(bf_ref) + f(bb_ref), bd) * bg_ref[...] * (r * _sigmoid(r))
    yc = _head_rms(f(cf_ref) + f(cb_ref), bd) * cg_ref[...] * _sigmoid(f(co_ref))
    mixed = jnp.concatenate([ya.astype(BF16), yb.astype(BF16), yc.astype(BF16), yd_ref[...]], axis=-1)
    x = x_ref[...] + _dot(mixed, w_ref[...])
    xo_ref[...] = x
    hn = x * lax.rsqrt(jnp.mean(x * x, axis=-1, keepdims=True) + EPS) * n2_ref[...]
    hn_ref[...] = hn.astype(BF16)
    hi = hn.astype(BF16)
    lo = (hn - hi.astype(F32)).astype(BF16)
    lg_ref[...] = _dot(hi, rwh_ref[...]) + _dot(lo, rwh_ref[...]) + _dot(hi, rwl_ref[...]) + rb_ref[...]


def _out_proj(x2d, af, ab, za, bf, bb, zb, cf, cb, zc, yd, w_out, b_norm_g, c_norm_g, norm2_g, rw, rb, tm=512):
    t = x2d.shape[0]
    bd = _chunk_masks(False)['bd']
    rwh = rw.astype(BF16)
    rwl = (rw - rwh.astype(F32)).astype(BF16)
    row = lambda w, j=0: pl.BlockSpec((tm, w), lambda i: (i, j))
    consts = (w_out, b_norm_g.reshape(1, -1).astype(F32), c_norm_g.reshape(1, -1).astype(F32), bd,
              norm2_g.reshape(1, -1).astype(F32), rwh, rwl, rb)
    return pl.pallas_call(
        _out_proj_kernel,
        grid=(t // tm,),
        in_specs=[row(D_MODEL), row(GROUP_W), row(GROUP_W), row(GROUP_W, 1), row(GROUP_W), row(GROUP_W),
                  row(GROUP_W, 2), row(GROUP_W), row(GROUP_W), row(GROUP_W, 3), row(GROUP_W)]
        + [_const_spec(c) for c in consts],
        out_specs=(row(D_MODEL), row(D_MODEL), row(V7X_LANES)),
        out_shape=(jax.ShapeDtypeStruct((t, D_MODEL), F32), jax.ShapeDtypeStruct((t, D_MODEL), BF16),
                   jax.ShapeDtypeStruct((t, V7X_LANES), F32)),
        compiler_params=_cparams("parallel"),
        name="out_proj",
    )(x2d, af, ab, za, bf, bb, zb, cf, cb, zc, yd, *consts)


def _route(logits):
    lane = _iota(logits.shape, 1).astype(F32)
    is_g = lane < MOE_GROUPS
    gl = jnp.where(is_g, logits, NEG_BIG)
    gmax = jnp.max(gl, axis=1, keepdims=True)
    gidx = jnp.min(jnp.where(gl == gmax, lane, 1e9), axis=1, keepdims=True)
    gprob = 1.0 / jnp.sum(jnp.where(is_g, jnp.exp(gl - gmax), 0.0), axis=1, keepdims=True)
    lo = MOE_GROUPS + MOE_PER_GROUP * gidx
    el = jnp.where((lane >= lo) & (lane < lo + MOE_PER_GROUP), logits, NEG_BIG)
    v1 = jnp.max(el, axis=1, keepdims=True)
    i1 = jnp.min(jnp.where(el == v1, lane, 1e9), axis=1, keepdims=True)
    el2 = jnp.where(lane == i1, NEG_BIG, el)
    v2 = jnp.max(el2, axis=1, keepdims=True)
    i2 = jnp.min(jnp.where(el2 == v2, lane, 1e9), axis=1, keepdims=True)
    e21 = jnp.exp(v2 - v1)
    p1 = 1.0 / (1.0 + e21)
    return jnp.where(lane == i1, p1 * gprob, 0.0) + jnp.where(lane == i2, e21 * p1 * gprob, 0.0), gidx


MOE_TB = 1024
MOE_RT = 128
MOE_SB = MOE_TB + MOE_GROUPS * MOE_RT


def _moe_kernel(hn_ref, lg_ref, wg_ref, wu_ref, wd_ref, o_ref, xs_ref, ys_ref, gs_ref, pos_ref, meta_ref):
    grp = pl.program_id(1)
    tb, sb, rt = MOE_TB, MOE_SB, MOE_RT

    @pl.when(grp == 0)
    def _():
        gates, gidx = _route(lg_ref[...])
        lane = _iota((tb, V7X_LANES), 1).astype(F32)
        ohg = jnp.where(lane == gidx, 1.0, 0.0)
        tril = jnp.where(_iota((tb, tb), 1) < _iota((tb, tb), 0), 1.0, 0.0).astype(BF16)
        rank = jnp.sum(_dot(tril, ohg.astype(BF16)) * ohg, axis=1, keepdims=True)
        cnt = jnp.sum(ohg, axis=0, keepdims=True)
        pc = jnp.ceil(cnt * (1.0 / rt)) * rt
        lane1 = _iota((1, V7X_LANES), 1)
        off = jnp.zeros((1, V7X_LANES), F32)
        run = jnp.zeros((1, 1), F32)
        for g in range(MOE_GROUPS):
            off = jnp.where(lane1 == g, run, off)
            run = run + jnp.sum(jnp.where(lane1 == g, pc, 0.0), axis=1, keepdims=True)
        pos = jnp.sum(ohg * off, axis=1, keepdims=True) + rank
        pos_ref[...] = jnp.broadcast_to(pos, (tb, V7X_LANES))
        meta_ref[0:1, :] = off
        meta_ref[1:2, :] = pc
        hi = jnp.floor(pos * (1.0 / 64.0))
        lo = pos - 64.0 * hi
        pm = jnp.where(lane == 0.0, hi, jnp.where(lane == 1.0, lo, 0.0)).astype(BF16)
        sel = jnp.where(_iota((8, V7X_LANES), 0) == _iota((8, V7X_LANES), 1), 1.0, 0.0).astype(BF16)
        rows8 = _dot_nt(sel, pm)
        pos_row = 64.0 * rows8[0:1, :] + rows8[1:2, :]
        p = jnp.where(_iota((sb, tb), 0).astype(F32) == pos_row, 1.0, 0.0).astype(BF16)
        xs_ref[...] = _dot(p, hn_ref[...]).astype(BF16)
        gh = gates.astype(BF16)
        gl = (gates - gh.astype(F32)).astype(BF16)
        gsorted = _dot(p, jnp.concatenate([gh, gl], axis=1))
        gs_ref[...] = gsorted[:, :V7X_LANES] + gsorted[:, V7X_LANES:]
        ys_ref[...] = jnp.zeros_like(ys_ref)

    lane1 = _iota((1, V7X_LANES), 1)
    start = jnp.sum(jnp.where(lane1 == grp, meta_ref[0:1, :], 0.0)).astype(jnp.int32)
    rows = jnp.sum(jnp.where(lane1 == grp, meta_ref[1:2, :], 0.0)).astype(jnp.int32)

    def experts(r0, nrows):
        xt = xs_ref[pl.ds(r0, nrows), :]
        gsl = gs_ref[pl.ds(r0, nrows), :]
        lane_t = _iota((nrows, V7X_LANES), 1)
        acc = None
        for j in range(MOE_PER_GROUP):
            g = _dot(xt, wg_ref[j])
            a = (g * _sigmoid(g)) * _dot(xt, wu_ref[j])
            y = _dot(a.astype(BF16), wd_ref[j])
            w = jnp.sum(jnp.where(lane_t == MOE_GROUPS + MOE_PER_GROUP * grp + j, gsl, 0.0), axis=1, keepdims=True)
            acc = w * y if acc is None else acc + w * y
        ys_ref[pl.ds(r0, nrows), :] = acc.astype(BF16)

    ntiles = rows // rt

    def pair(i, carry):
        experts(pl.multiple_of(start + 2 * i * rt, rt), 2 * rt)
        return carry

    lax.fori_loop(0, ntiles // 2, pair, 0)

    @pl.when(ntiles % 2 == 1)
    def _():
        experts(pl.multiple_of(start + (ntiles - 1) * rt, rt), rt)

    @pl.when(grp == MOE_GROUPS - 1)
    def _():
        pt = jnp.where(_iota((tb, sb), 1).astype(F32) == pos_ref[:, 0:1], 1.0, 0.0).astype(BF16)
        o_ref[...] = _dot(pt, ys_ref[...]).astype(o_ref.dtype)


def _moe(hn, logits, w_gate, w_up, w_down):
    t = hn.shape[0]
    tm = MOE_TB
    row = lambda w: pl.BlockSpec((tm, w), lambda i, g: (i, 0))
    wspec = lambda a, b: pl.BlockSpec((MOE_PER_GROUP, a, b), lambda i, g: (g, 0, 0))
    return pl.pallas_call(
        _moe_kernel,
        grid=(t // tm, MOE_GROUPS),
        in_specs=[row(D_MODEL), row(V7X_LANES), wspec(D_MODEL, D_EXPERT), wspec(D_MODEL, D_EXPERT),
                  wspec(D_EXPERT, D_MODEL)],
        out_specs=row(D_MODEL),
        out_shape=jax.ShapeDtypeStruct((t, D_MODEL), BF16),
        scratch_shapes=[pltpu.VMEM((MOE_SB, D_MODEL), BF16), pltpu.VMEM((MOE_SB, D_MODEL), BF16),
                        pltpu.VMEM((MOE_SB, V7X_LANES), F32), pltpu.VMEM((tm, V7X_LANES), F32),
                        pltpu.VMEM((8, V7X_LANES), F32)],
        compiler_params=_cparams("parallel", "arbitrary"),
        name="moe",
    )(hn, logits, w_gate, w_up, w_down)


def _layer(x2d, moe_prev, bsz, seq, l, P, W, spec):
    x2d, za, zb, zc, zd, zs = _in_proj(x2d, moe_prev, P['norm1_g'][l].astype(F32), W['w_in'][l])
    r3 = lambda a: a.reshape(bsz, seq, a.shape[-1])
    r2 = lambda a: a.reshape(bsz * seq, a.shape[-1])
    mp = W['mix'][l]
    cw, cb = P['a_conv_w'][l].astype(F32), P['a_conv_b'][l].reshape(1, -1).astype(F32)
    za3, zb3, zc3, zs3 = r3(za), r3(zb), r3(zc), r3(zs)
    af = _rglru(za3, cw, cb, *mp['a', 0], False)
    ab = _rglru(za3, cw, cb, *mp['a', 1], True)
    bf, bb = _gla(zb3, zs3, mp['b', 0], mp['b', 1])
    cf, cbk = _mlstm(zc3, zs3, mp['c', 0], mp['c', 1])
    yd = _mixer_hyena(r3(zd), P, l, spec)
    x_new, hn, logits = _out_proj(x2d, r2(af), r2(ab), za, r2(bf), r2(bb), zb, r2(cf), r2(cbk), zc, r2(yd),
                                  W['w_out'][l], P['b_norm_g'][l], P['c_norm_g'][l], P['norm2_g'][l],
                                  W['router_w'][l], W['router_b'][l])
    return x_new, _moe(hn, logits, W['w_gate'][l], W['w_up'][l], W['w_down'][l])


def _trunk(x, P, W):
    bsz, seq, _ = x.shape
    depth = P['w_in'].shape[0]
    x2d = x.reshape(bsz * seq, D_MODEL)
    moe_term = None
    for l in range(depth):
        spec = _hyena_spectrum(seq, P['d_ffn_w1'][l], P['d_ffn_b1'][l], P['d_sin_freq'][l], P['d_ffn_w2'][l],
                               P['d_ffn_b2'][l], P['d_ffn_w3'][l])
        x2d, moe_term = _layer(x2d, moe_term, bsz, seq, l, P, W, spec)
    return _final_norm(x2d, moe_term, P['final_norm_g']).reshape(bsz, seq, D_MODEL)


def _prep_weights(P):
    depth = P['w_in'].shape[0]
    W = {'w_in': [_permute_w_in(P['w_in'][l]) for l in range(depth)],
         'w_out': [P['w_out'][l].astype(BF16) for l in range(depth)],
         'mix': [_mixer_prep(P, l) for l in range(depth)],
         'w_gate': [P['moe_w_gate'][l].astype(BF16) for l in range(depth)],
         'w_up': [P['moe_w_up'][l].astype(BF16) for l in range(depth)],
         'w_down': [P['moe_w_down'][l].astype(BF16) for l in range(depth)],
         'router_w': [], 'router_b': []}
    for l in range(depth):
        rw = jnp.concatenate([P['moe_group_w'][l], P['moe_expert_w'][l]], axis=1).astype(F32)
        rb = jnp.concatenate([P['moe_group_b'][l], P['moe_expert_b'][l]]).astype(F32)
        padc = V7X_LANES - rw.shape[1]
        W['router_w'].append(jnp.pad(rw, ((0, 0), (0, padc))))
        W['router_b'].append(jnp.pad(rb, (0, padc)).reshape(1, V7X_LANES))
    return W


def kernel(x_prompt, x_sample, norm1_g, w_in, a_conv_w, a_conv_b, a_gate_w, a_gate_b, a_lambda, b_gate_w, b_gate_b, b_norm_g, c_gate_b, c_norm_g, d_conv_w, d_conv_b, d_ffn_w1, d_ffn_b1, d_sin_freq, d_ffn_w2, d_ffn_b2, d_ffn_w3, d_skip, w_out, norm2_g, moe_group_w, moe_group_b, moe_expert_w, moe_expert_b, moe_w_gate, moe_w_up, moe_w_down, final_norm_g):
    P = {'norm1_g': norm1_g, 'w_in': w_in, 'a_conv_w': a_conv_w, 'a_conv_b': a_conv_b,
         'a_gate_w': a_gate_w, 'a_gate_b': a_gate_b, 'a_lambda': a_lambda, 'b_gate_w': b_gate_w,
         'b_gate_b': b_gate_b, 'b_norm_g': b_norm_g, 'c_gate_b': c_gate_b, 'c_norm_g': c_norm_g,
         'd_conv_w': d_conv_w, 'd_conv_b': d_conv_b, 'd_ffn_w1': d_ffn_w1, 'd_ffn_b1': d_ffn_b1,
         'd_sin_freq': d_sin_freq, 'd_ffn_w2': d_ffn_w2, 'd_ffn_b2': d_ffn_b2, 'd_ffn_w3': d_ffn_w3,
         'd_skip': d_skip, 'w_out': w_out, 'norm2_g': norm2_g, 'moe_group_w': moe_group_w,
         'moe_group_b': moe_group_b, 'moe_expert_w': moe_expert_w, 'moe_expert_b': moe_expert_b,
         'moe_w_gate': moe_w_gate, 'moe_w_up': moe_w_up, 'moe_w_down': moe_w_down,
         'final_norm_g': final_norm_g}
    W = _prep_weights(P)
    return (_trunk(x_prompt, P, W), _trunk(x_sample, P, W))
```

```python
import functools
import math

import jax
import jax.numpy as jnp
import numpy as np
from jax import lax
from jax.experimental import pallas as pl
from jax.experimental.pallas import tpu as pltpu

F32 = jnp.float32
BF16 = jnp.bfloat16

D_MODEL = 1024
GROUP_W = 256
N_HEADS = 4
HEAD_W = GROUP_W // N_HEADS
B_DK = 32
B_RANK = 16
CHUNK = 64
RG_C = 8.0
RG_SEGMENTS = 16
RG_UNROLL = 4
GLA_TAU = 16.0
HY_EMB = 33
HY_HID = 64
HY_TARGET = 1e-2
HY_FAST = 0.3
HY_SLOW = 1.5
MOE_GROUPS = 4
MOE_PER_GROUP = 4
MOE_EXPERTS = 16
D_EXPERT = 512
EPS = 1e-6
NEG_BIG = -1e30

V7X_LANES = 128
V7X_SUBLANES = 8
VMEM_LIMIT = 56 * 1024 * 1024

ZA_W, ZB_W, ZC_W, ZD_W, ZS_W = 512, 768, 1024, 768, 128
Z_SPLITS = (ZA_W, ZB_W, ZC_W, ZD_W, ZS_W)
Z_TOTAL = sum(Z_SPLITS)


def _cparams(*sem):
    return pltpu.CompilerParams(dimension_semantics=sem, vmem_limit_bytes=VMEM_LIMIT)


def _dot(a, b):
    return jnp.dot(a, b, preferred_element_type=F32)


def _dot_nt(a, b):
    return lax.dot_general(a, b, (((1,), (1,)), ((), ())), preferred_element_type=F32)


def _dot_tn(a, b):
    return lax.dot_general(a, b, (((0,), (0,)), ((), ())), preferred_element_type=F32)


def _dot_split(m_bf16, x):
    hi = x.astype(BF16)
    lo = (x - hi.astype(F32)).astype(BF16)
    return _dot(m_bf16, hi) + _dot(m_bf16, lo)


def _split_dot(x, m_bf16):
    hi = x.astype(BF16)
    lo = (x - hi.astype(F32)).astype(BF16)
    return _dot(hi, m_bf16) + _dot(lo, m_bf16)


def _sigmoid(x):
    return 1.0 / (1.0 + jnp.exp(-x))


def _log_sigmoid(x):
    return jnp.minimum(x, 0.0) - jnp.log(1.0 + jnp.exp(-jnp.abs(x)))


def _iota(shape, dim):
    return lax.broadcasted_iota(jnp.int32, shape, dim)


def _in_proj_kernel(*refs, add):
    if add:
        x_ref, m_ref, g_ref, w_ref, xo_ref, za_ref, zb_ref, zc_ref, zd_ref, zs_ref = refs
        x = x_ref[...] + m_ref[...].astype(F32)
        xo_ref[...] = x
    else:
        x_ref, g_ref, w_ref, za_ref, zb_ref, zc_ref, zd_ref, zs_ref = refs
        x = x_ref[...]
    ms = jnp.mean(x * x, axis=-1, keepdims=True)
    xn = (x * lax.rsqrt(ms + EPS) * g_ref[...]).astype(BF16)
    off = 0
    for ref, w in zip((za_ref, zb_ref, zc_ref, zd_ref, zs_ref), Z_SPLITS):
        ref[...] = _dot(xn, w_ref[:, off:off + w]).astype(ref.dtype)
        off += w


def _in_proj(x2d, moe_prev, g, w_perm, tm=512):
    t = x2d.shape[0]
    add = moe_prev is not None
    row = lambda w: pl.BlockSpec((tm, w), lambda i: (i, 0))
    outs = tuple(jax.ShapeDtypeStruct((t, w), F32 if w == ZS_W else BF16) for w in Z_SPLITS)
    out_specs = tuple(row(w) for w in Z_SPLITS)
    consts = (g.reshape(1, D_MODEL), w_perm)
    res = pl.pallas_call(
        functools.partial(_in_proj_kernel, add=add),
        grid=(t // tm,),
        in_specs=[row(D_MODEL)] * (2 if add else 1) + [_const_spec(c) for c in consts],
        out_specs=((row(D_MODEL),) if add else ()) + out_specs,
        out_shape=((jax.ShapeDtypeStruct((t, D_MODEL), F32),) if add else ()) + outs,
        compiler_params=_cparams("parallel"),
        name="in_proj",
    )(*((x2d, moe_prev) if add else (x2d,)), *consts)
    return tuple(res) if add else (x2d,) + tuple(res)


def _final_norm_kernel(x_ref, m_ref, g_ref, o_ref):
    x = x_ref[...] + m_ref[...].astype(F32)
    o_ref[...] = x * lax.rsqrt(jnp.mean(x * x, axis=-1, keepdims=True) + EPS) * g_ref[...]


def _final_norm(x2d, moe_prev, g, tm=512):
    t = x2d.shape[0]
    row = pl.BlockSpec((tm, D_MODEL), lambda i: (i, 0))
    return pl.pallas_call(
        _final_norm_kernel,
        grid=(t // tm,),
        in_specs=[row, row, pl.BlockSpec((1, D_MODEL), lambda i: (0, 0))],
        out_specs=row,
        out_shape=jax.ShapeDtypeStruct((t, D_MODEL), F32),
        compiler_params=_cparams("parallel"),
        name="final_norm",
    )(x2d, moe_prev, g.reshape(1, D_MODEL).astype(F32))


def _permute_w_in(w_in):
    sizes = (256, 256, 128, 128, 256, 256, 16, 16, 256, 256, 256, 256, 16, 768)
    offs = np.concatenate([[0], np.cumsum(sizes)])
    seg = [w_in[:, offs[i]:offs[i + 1]] for i in range(len(sizes))]
    (a_x, a_g, b_q, b_k, b_v, b_r, b_lf, b_lb, c_q, c_k, c_v, c_o, c_g, d_u) = seg
    small = jnp.concatenate([b_lf, b_lb, c_g, jnp.zeros((w_in.shape[0], ZS_W - 48), w_in.dtype)], axis=1)
    return jnp.concatenate([a_x, a_g, b_q, b_k, b_v, b_r, c_q, c_k, c_v, c_o, d_u, small], axis=1).astype(BF16)


HALO = 16


def _strided_pitch(n):
    assert n % V7X_SUBLANES == 0
    return n if (n // V7X_SUBLANES) % 2 == 1 else n + V7X_SUBLANES


def _halo_conv(prev, cur, nxt, w, b, lo, first, last):
    tl = cur.shape[0]
    prev = jnp.where(first, 0.0, prev.astype(F32))
    nxt = jnp.where(last, 0.0, nxt.astype(F32))
    ext = jnp.concatenate([prev, cur.astype(F32), nxt], axis=0)
    n = tl + 2 * HALO
    acc = None
    for j in range(w.shape[0]):
        o = lo + j
        sh = ext if o == 0 else pltpu.roll(ext, (-o) % n, 0)
        term = sh[HALO:HALO + tl] * w[j:j + 1, :]
        acc = term if acc is None else acc + term
    return acc + b


def _rglru_kernel(prev_ref, cur_ref, next_ref, cw_ref, cb_ref, gw_ref, gb_ref, sp_ref, out_ref,
                  a_s, b_s, h_s, p_s, carry, *, tl, reverse):
    i = pl.program_id(1)
    nt = pl.num_programs(1)
    ti = (nt - 1 - i) if reverse else i
    u = _halo_conv(prev_ref[0], cur_ref[0], next_ref[0], cw_ref[...], cb_ref[...], -2, ti == 0, ti == nt - 1)
    gates = _sigmoid(_dot(u.astype(BF16), gw_ref[...]) + gb_ref[...])
    r = gates[:, :GROUP_W]
    ig = gates[:, GROUP_W:]
    log_a = (-RG_C) * r * sp_ref[...]
    a = jnp.exp(log_a)
    bt = jnp.sqrt(1.0 - a * a) * (ig * u)
    nseg = RG_SEGMENTS
    m = tl // nseg
    mp, sp = _strided_pitch(m), _strided_pitch(nseg)
    for k in range(2):
        for seg in range(nseg):
            a_s[k, seg * mp:seg * mp + m, :] = a[seg * m:(seg + 1) * m, k * V7X_LANES:(k + 1) * V7X_LANES]
            b_s[k, seg * mp:seg * mp + m, :] = bt[seg * m:(seg + 1) * m, k * V7X_LANES:(k + 1) * V7X_LANES]

    @pl.when(i == 0)
    def _():
        carry[...] = jnp.zeros_like(carry)

    def body(jj, hp):
        j = (m - 1 - jj) if reverse else jj
        out = []
        for k in range(2):
            h, p = hp[2 * k], hp[2 * k + 1]
            av = a_s[k, pl.ds(j, nseg, stride=mp), :]
            bv = b_s[k, pl.ds(j, nseg, stride=mp), :]
            h = av * h + bv
            p = av * p
            rows = pl.ds(pl.multiple_of(j * sp, V7X_SUBLANES), nseg)
            h_s[k, rows, :] = h
            p_s[k, rows, :] = p
            out += [h, p]
        return tuple(out)

    z8 = jnp.zeros((nseg, V7X_LANES), F32)
    o8 = jnp.ones((nseg, V7X_LANES), F32)
    ends = lax.fori_loop(0, m, body, (z8, o8, z8, o8), unroll=RG_UNROLL)
    for k in range(2):
        h_end, p_end = ends[2 * k], ends[2 * k + 1]
        c = carry[k]
        cin = [None] * nseg
        for seg in (range(nseg - 1, -1, -1) if reverse else range(nseg)):
            cin[seg] = c
            c = h_end[seg:seg + 1, :] + p_end[seg:seg + 1, :] * c
        carry[k] = c
        for seg in range(nseg):
            hv = h_s[k, pl.ds(seg, m, stride=sp), :]
            pv = p_s[k, pl.ds(seg, m, stride=sp), :]
            out_ref[0, seg * m:(seg + 1) * m, k * V7X_LANES:(k + 1) * V7X_LANES] = (hv + pv * cin[seg]).astype(out_ref.dtype)


def _rglru(za, conv_w, conv_b, gate_w_bd, gate_b, softplus_neg_lam, reverse, tl=512):
    b, l, _ = za.shape
    nt = l // tl
    rh = tl // HALO
    nbh = l // HALO

    def tmap(bi, i):
        return (nt - 1 - i) if reverse else i

    kern = functools.partial(_rglru_kernel, tl=tl, reverse=reverse)
    return pl.pallas_call(
        kern,
        grid=(b, nt),
        in_specs=[
            pl.BlockSpec((1, HALO, GROUP_W), lambda bi, i: (bi, jnp.maximum(tmap(bi, i) * rh - 1, 0), 0)),
            pl.BlockSpec((1, tl, GROUP_W), lambda bi, i: (bi, tmap(bi, i), 0)),
            pl.BlockSpec((1, HALO, GROUP_W), lambda bi, i: (bi, jnp.minimum((tmap(bi, i) + 1) * rh, nbh - 1), 0)),
            pl.BlockSpec((4, GROUP_W), lambda bi, i: (0, 0)),
            pl.BlockSpec((1, GROUP_W), lambda bi, i: (0, 0)),
            pl.BlockSpec((GROUP_W, 2 * GROUP_W), lambda bi, i: (0, 0)),
            pl.BlockSpec((1, 2 * GROUP_W), lambda bi, i: (0, 0)),
            pl.BlockSpec((1, GROUP_W), lambda bi, i: (0, 0)),
        ],
        out_specs=pl.BlockSpec((1, tl, GROUP_W), lambda bi, i: (bi, tmap(bi, i), 0)),
        out_shape=jax.ShapeDtypeStruct((b, l, GROUP_W), BF16),
        scratch_shapes=[pltpu.VMEM((2, RG_SEGMENTS * _strided_pitch(tl // RG_SEGMENTS), V7X_LANES), F32)] * 2
        + [pltpu.VMEM((2, (tl // RG_SEGMENTS) * _strided_pitch(RG_SEGMENTS), V7X_LANES), F32)] * 2
        + [pltpu.VMEM((2, 1, V7X_LANES), F32)],
        compiler_params=_cparams("parallel", "arbitrary"),
        name="rglru_bwd" if reverse else "rglru_fwd",
    )(za, za, za, conv_w, conv_b, gate_w_bd, gate_b, softplus_neg_lam)


def _block_diag(blocks):
    h, di, do = blocks.shape
    eye = jnp.eye(h, dtype=blocks.dtype)
    return jnp.einsum('hde,hg->hdge', blocks, eye).reshape(h * di, h * do)


def _chunk_masks(reverse):
    r = np.arange(GROUP_W)
    t = np.arange(CHUNK)
    tri = (t[None, :] >= t[:, None]) if reverse else (t[None, :] <= t[:, None])
    cmask = np.tile(tri, (1, N_HEADS))
    bd = (r[:, None] // HEAD_W) == (r[None, :] // HEAD_W)
    kd = (r[:, None] // HEAD_W) == (np.arange(N_HEADS * B_DK)[None, :] // B_DK)
    i2 = np.tile(np.eye(CHUNK, dtype=bool), (1, N_HEADS))
    hm = np.zeros((8, GROUP_W), bool)
    hm[:N_HEADS] = np.arange(N_HEADS)[:, None] == (r[None, :] // HEAD_W)
    return dict(tri=jnp.asarray(tri, BF16), cmask=jnp.asarray(cmask, F32), bd=jnp.asarray(bd, BF16),
                kd=jnp.asarray(kd, BF16), kdf=jnp.asarray(kd, F32),
                i2=jnp.asarray(i2, F32), hm=jnp.asarray(hm, F32))


def _const_spec(arr):
    nd = arr.ndim
    return pl.BlockSpec(arr.shape, lambda *_: (0,) * nd)


def _gla_chunk(zb_ref, la, tri_ref, kd_ref, kdf_ref, bd_ref, causal, out_ref, st_ref, c):
    rows = slice(c * CHUNK, (c + 1) * CHUNK)
    q = zb_ref[0, rows, 0:128].astype(F32) * (B_DK ** -0.5)
    k = zb_ref[0, rows, 128:256].astype(F32)
    vb = zb_ref[0, rows, 256:512]
    la_c = la[rows]
    bcum = _dot_split(tri_ref[...], la_c)
    btot = jnp.sum(la_c, axis=0, keepdims=True)
    q_in = (q * jnp.exp(bcum)).astype(BF16)
    k_in = (k * jnp.exp(-bcum)).astype(BF16)
    k_st = (k * jnp.exp(btot - bcum)).astype(BF16)
    kexp = jnp.concatenate([k_in] * N_HEADS, axis=0) * kd_ref[...]
    att = jnp.where(causal, _dot_nt(q_in, kexp), 0.0)
    vbd = jnp.concatenate([vb] * N_HEADS, axis=0) * bd_ref[...]
    st = st_ref[...]
    out_ref[0, rows, :] = (_dot(att.astype(BF16), vbd) + _dot_nt(q_in, st.astype(BF16))).astype(out_ref.dtype)
    st_ref[...] = st * jnp.exp(btot) + _dot_tn(vb, k_st) * kdf_ref[...]


def _gla_kernel(zbf_ref, zsf_ref, zbb_ref, zsb_ref, gwf_ref, gbf_ref, gwb_ref, gbb_ref, trif_ref, trib_ref,
                kd_ref, kdf_ref, bd_ref, cmf_ref, cmb_ref, outf_ref, outb_ref, stf_ref, stb_ref, *, tl):
    @pl.when(pl.program_id(1) == 0)
    def _():
        stf_ref[...] = jnp.zeros_like(stf_ref)
        stb_ref[...] = jnp.zeros_like(stb_ref)

    log_alpha = lambda zs_ref, gw_ref, gb_ref: _log_sigmoid(
        _dot(zs_ref[0].astype(BF16), gw_ref[...]) + gb_ref[...]) * (1.0 / GLA_TAU)
    laf = log_alpha(zsf_ref, gwf_ref, gbf_ref)
    lab = log_alpha(zsb_ref, gwb_ref, gbb_ref)
    causal_f = cmf_ref[...] > 0.0
    causal_b = cmb_ref[...] > 0.0
    nch = tl // CHUNK
    for c in range(nch):
        _gla_chunk(zbf_ref, laf, trif_ref, kd_ref, kdf_ref, bd_ref, causal_f, outf_ref, stf_ref, c)
        _gla_chunk(zbb_ref, lab, trib_ref, kd_ref, kdf_ref, bd_ref, causal_b, outb_ref, stb_ref, nch - 1 - c)


def _gla(zb, zs, prep_f, prep_b, tl=512):
    b, l, _ = zb.shape
    nt = l // tl
    mf, mb = _chunk_masks(False), _chunk_masks(True)
    consts = (*prep_f, *prep_b, mf['tri'], mb['tri'], mf['kd'], mf['kdf'], mf['bd'], mf['cmask'], mb['cmask'])
    fmap = lambda bi, i: (bi, i, 0)
    bmap = lambda bi, i: (bi, nt - 1 - i, 0)
    o = jax.ShapeDtypeStruct((b, l, GROUP_W), BF16)
    st = pltpu.VMEM((GROUP_W, N_HEADS * B_DK), F32)
    return pl.pallas_call(
        functools.partial(_gla_kernel, tl=tl),
        grid=(b, nt),
        in_specs=[pl.BlockSpec((1, tl, ZB_W), fmap), pl.BlockSpec((1, tl, ZS_W), fmap),
                  pl.BlockSpec((1, tl, ZB_W), bmap), pl.BlockSpec((1, tl, ZS_W), bmap)]
        + [_const_spec(c) for c in consts],
        out_specs=(pl.BlockSpec((1, tl, GROUP_W), fmap), pl.BlockSpec((1, tl, GROUP_W), bmap)),
        out_shape=(o, o),
        scratch_shapes=[st, st],
        compiler_params=_cparams("parallel", "arbitrary"),
        name="gla",
    )(zb, zs, zb, zs, *consts)


def _mlstm_chunk(zc_ref, ig_all, lf_all, tri_ref, i2_ref, bd_ref, causal, hm_ref, out_ref, s_ref, n_ref, m_ref, c):
    rows = slice(c * CHUNK, (c + 1) * CHUNK)
    qb = zc_ref[0, rows, 0:256]
    k = zc_ref[0, rows, 256:512].astype(F32) * (HEAD_W ** -0.5)
    kb = k.astype(BF16)
    vb = zc_ref[0, rows, 512:768]
    ig = ig_all[rows]
    lf = lf_all[rows]
    bcum = _dot_split(tri_ref[...], lf)
    blast = jnp.sum(lf, axis=0, keepdims=True)
    cc = jnp.sum((ig - bcum) * i2_ref[...], axis=0, keepdims=True)
    d2 = jnp.where(causal, bcum + cc, NEG_BIG)
    m_intra = jnp.full((CHUNK, GROUP_W), NEG_BIG, F32)
    for h in range(N_HEADS):
        hsel = hm_ref[h:h + 1, :] > 0.0
        mh = jnp.max(jnp.where(hsel, d2, NEG_BIG), axis=1, keepdims=True)
        m_intra = jnp.where(hsel, mh, m_intra)
    m_prev = m_ref[...]
    inter_log = bcum + m_prev
    m_t = jnp.maximum(inter_log, m_intra)
    w_intra = jnp.exp(d2 - m_t)
    w_inter = jnp.exp(inter_log - m_t)
    kexp = jnp.concatenate([kb] * N_HEADS, axis=0) * bd_ref[...]
    qk = (_dot_nt(qb, kexp) * w_intra).astype(BF16)
    vbd = jnp.concatenate([vb] * N_HEADS, axis=0) * bd_ref[...]
    s_prev = s_ref[...]
    n_prev = n_ref[...]
    sbd = jnp.concatenate([s_prev.astype(BF16)] * N_HEADS, axis=0) * bd_ref[...]
    nbd = jnp.concatenate([n_prev.astype(BF16)] * N_HEADS, axis=0) * bd_ref[...]
    num = _dot(qk, vbd) + w_inter * _dot(qb, sbd)
    den = _dot(qk, bd_ref[...]) + w_inter * _dot(qb, nbd)
    out_ref[0, rows, :] = (num / jnp.maximum(jnp.abs(den), jnp.exp(-m_t))).astype(out_ref.dtype)
    g_end = blast - bcum + ig
    g_max = jnp.max(g_end, axis=0, keepdims=True)
    kw = (k * jnp.exp(g_end - g_max)).astype(BF16)
    m_new = jnp.maximum(blast + m_prev, g_max)
    a = jnp.exp(blast + m_prev - m_new)
    cf = jnp.exp(g_max - m_new)
    uc_full = _dot_tn(kw, vb)
    un_full = _dot_tn(kw, jnp.ones((CHUNK, GROUP_W), BF16))
    uc = jnp.zeros((HEAD_W, GROUP_W), F32)
    un = jnp.zeros((HEAD_W, GROUP_W), F32)
    for h in range(N_HEADS):
        hsel = hm_ref[h:h + 1, :] > 0.0
        blk = slice(h * HEAD_W, (h + 1) * HEAD_W)
        uc = jnp.where(hsel, uc_full[blk], uc)
        un = jnp.where(hsel, un_full[blk], un)
    s_ref[...] = a * s_prev + cf * uc
    n_ref[...] = a * n_prev + cf * un
    m_ref[...] = m_new


def _mlstm_kernel(zcf_ref, zsf_ref, zcb_ref, zsb_ref, ef_ref, ebf_ref, eb_ref, ebb_ref, trif_ref, trib_ref,
                  i2_ref, bd_ref, cmf_ref, cmb_ref, hm_ref, outf_ref, outb_ref,
                  sf_ref, nf_ref, mf_ref, sb_ref, nb_ref, mb_ref, *, tl):
    @pl.when(pl.program_id(1) == 0)
    def _():
        for s_ref, n_ref, m_ref in ((sf_ref, nf_ref, mf_ref), (sb_ref, nb_ref, mb_ref)):
            s_ref[...] = jnp.zeros_like(s_ref)
            n_ref[...] = jnp.zeros_like(n_ref)
            m_ref[...] = jnp.full_like(m_ref, NEG_BIG)

    def gates(zs_ref, e_ref, eb_ref):
        ge = _split_dot(zs_ref[0], e_ref[...]) + eb_ref[...]
        return ge[:, :GROUP_W], _log_sigmoid(ge[:, GROUP_W:])

    igf, lff = gates(zsf_ref, ef_ref, ebf_ref)
    igb, lfb = gates(zsb_ref, eb_ref, ebb_ref)
    causal_f = cmf_ref[...] > 0.0
    causal_b = cmb_ref[...] > 0.0
    nch = tl // CHUNK
    for c in range(nch):
        _mlstm_chunk(zcf_ref, igf, lff, trif_ref, i2_ref, bd_ref, causal_f, hm_ref, outf_ref,
                     sf_ref, nf_ref, mf_ref, c)
        _mlstm_chunk(zcb_ref, igb, lfb, trib_ref, i2_ref, bd_ref, causal_b, hm_ref, outb_ref,
                     sb_ref, nb_ref, mb_ref, nch - 1 - c)


def _mlstm(zc, zs, prep_f, prep_b, tl=512):
    b, l, _ = zc.shape
    nt = l // tl
    mf, mb = _chunk_masks(False), _chunk_masks(True)
    consts = (*prep_f, *prep_b, mf['tri'], mb['tri'], mf['i2'], mf['bd'], mf['cmask'], mb['cmask'], mf['hm'])
    fmap = lambda bi, i: (bi, i, 0)
    bmap = lambda bi, i: (bi, nt - 1 - i, 0)
    o = jax.ShapeDtypeStruct((b, l, GROUP_W), BF16)
    state = [pltpu.VMEM((HEAD_W, GROUP_W), F32), pltpu.VMEM((HEAD_W, GROUP_W), F32), pltpu.VMEM((1, GROUP_W), F32)]
    return pl.pallas_call(
        functools.partial(_mlstm_kernel, tl=tl),
        grid=(b, nt),
        in_specs=[pl.BlockSpec((1, tl, ZC_W), fmap), pl.BlockSpec((1, tl, ZS_W), fmap),
                  pl.BlockSpec((1, tl, ZC_W), bmap), pl.BlockSpec((1, tl, ZS_W), bmap)]
        + [_const_spec(c) for c in consts],
        out_specs=(pl.BlockSpec((1, tl, GROUP_W), fmap), pl.BlockSpec((1, tl, GROUP_W), bmap)),
        out_shape=(o, o),
        scratch_shapes=state + state,
        compiler_params=_cparams("parallel", "arbitrary"),
        name="mlstm",
    )(zc, zs, zc, zs, *consts)


def _mixer_prep(P, l):
    out = {}
    gw, gb, lam = P['a_gate_w'][l], P['a_gate_b'][l], P['a_lambda'][l]
    for d in range(2):
        w = jnp.concatenate([_block_diag(gw[d, 0]), _block_diag(gw[d, 1])], axis=1).astype(BF16)
        bb = jnp.concatenate([gb[d, 0].reshape(1, -1), gb[d, 1].reshape(1, -1)], axis=1).astype(F32)
        nl = -lam[d].astype(F32)
        sp = (jnp.maximum(nl, 0.0) + jnp.log(1.0 + jnp.exp(-jnp.abs(nl)))).reshape(1, -1)
        out['a', d] = (w, bb, sp)
        bw = jnp.zeros((ZS_W, N_HEADS * B_DK), F32).at[B_RANK * d:B_RANK * (d + 1)].set(P['b_gate_w'][l][d])
        out['b', d] = (bw.astype(BF16), P['b_gate_b'][l][d].reshape(1, -1).astype(F32))
        e = np.zeros((ZS_W, 2 * GROUP_W), np.float32)
        for h in range(N_HEADS):
            e[32 + 8 * d + h, h * HEAD_W:(h + 1) * HEAD_W] = 1.0
            e[36 + 8 * d + h, GROUP_W + h * HEAD_W:GROUP_W + (h + 1) * HEAD_W] = 1.0
        cb = P['c_gate_b'][l].astype(F32)
        eb = jnp.concatenate([jnp.repeat(cb[2 * d], HEAD_W), jnp.repeat(cb[2 * d + 1], HEAD_W)]).reshape(1, -1)
        out['c', d] = (jnp.asarray(e, BF16), eb)
    return out


G_ROWS = 8
G_ROWS_BF16 = 16


def _slab(k):
    return slice(k * V7X_LANES, (k + 1) * V7X_LANES)


def _scatter_rows(scr, base, val, rows, pitch, blk0=0):
    for s in range(val.shape[1] // V7X_LANES):
        for b in range(val.shape[0] // rows):
            r0 = (blk0 + b) * pitch
            scr[base + s, r0:r0 + rows, :] = val[b * rows:(b + 1) * rows, _slab(s)]


def _gather_n2_major(scr, outs, n2, pitch):
    nsl = GROUP_W // V7X_LANES

    def body(j, carry):
        for a, (ref, lead) in enumerate(outs):
            for s in range(nsl):
                ref[lead, j, :, _slab(s)] = scr[a * nsl + s, pl.ds(j, G_ROWS, stride=pitch), :]
        return carry

    lax.fori_loop(0, n2, body, 0, unroll=4)


def _conv3_kernel(prev_ref, cur_ref, next_ref, w_ref, b_ref, v_ref, x1_ref, x2_ref, scr, *, n2, pitch):
    i = pl.program_id(1)
    u = _halo_conv(prev_ref[0], cur_ref[0], next_ref[0], w_ref[...], b_ref[...], -1, i == 0,
                   i == pl.num_programs(1) - 1)
    nsl = GROUP_W // V7X_LANES
    for a in range(3):
        _scatter_rows(scr, a * nsl, u[:, a * GROUP_W:(a + 1) * GROUP_W], n2, pitch)
    _gather_n2_major(scr, ((v_ref, 0), (x1_ref, 0), (x2_ref, 0)), n2, pitch)


def _conv3_split(zd, w, b, n1, n2):
    bsz, l, _ = zd.shape
    tl = G_ROWS * n2
    pitch = _strided_pitch(n2)
    nt, rh, nbh = l // tl, tl // HALO, l // HALO
    o = jax.ShapeDtypeStruct((bsz, n2, n1 // 2, GROUP_W), F32)
    ospec = pl.BlockSpec((1, n2, G_ROWS, GROUP_W), lambda bi, i: (bi, 0, i, 0))
    return pl.pallas_call(
        functools.partial(_conv3_kernel, n2=n2, pitch=pitch),
        grid=(bsz, nt),
        in_specs=[pl.BlockSpec((1, HALO, ZD_W), lambda bi, i: (bi, jnp.maximum(i * rh - 1, 0), 0)),
                  pl.BlockSpec((1, tl, ZD_W), lambda bi, i: (bi, i, 0)),
                  pl.BlockSpec((1, HALO, ZD_W), lambda bi, i: (bi, jnp.minimum((i + 1) * rh, nbh - 1), 0)),
                  pl.BlockSpec((3, ZD_W), lambda bi, i: (0, 0)),
                  pl.BlockSpec((1, ZD_W), lambda bi, i: (0, 0))],
        out_specs=(ospec, ospec, ospec),
        out_shape=(o, o, o),
        scratch_shapes=[pltpu.VMEM((3 * GROUP_W // V7X_LANES, G_ROWS * pitch, V7X_LANES), F32)],
        compiler_params=_cparams("parallel", "parallel"),
        name="hyena_conv3",
    )(zd, zd, zd, w, b.reshape(1, ZD_W))


def _dot_hp(a, b):
    ah, bh = a.astype(BF16), b.astype(BF16)
    al = (a - ah.astype(F32)).astype(BF16)
    bl = (b - bh.astype(F32)).astype(BF16)
    return _dot(ah, bh) + _dot(al, bh) + _dot(ah, bl)


FILTER_SUB = 512


def _filter_mlp_kernel(emb_ref, w1_ref, b1_ref, f0_ref, w2_ref, b2_ref, f1_ref, w3_ref, dl_ref,
                       fwd_ref, sec_ref, s_ref, scr, *, n2, pitch):
    i = pl.program_id(0)
    tl = emb_ref.shape[0]
    nsl = GROUP_W // V7X_LANES

    @pl.when(i == 0)
    def _():
        s_ref[...] = jnp.zeros_like(s_ref)

    for sub in range(tl // FILTER_SUB):
        emb = emb_ref[sub * FILTER_SUB:(sub + 1) * FILTER_SUB, :]
        h = jnp.sin(f0_ref[...] * (_dot_hp(emb, w1_ref[...]) + b1_ref[...]))
        h = jnp.sin(f1_ref[...] * (_dot_hp(h, w2_ref[...]) + b2_ref[...]))
        h = _dot_hp(h, w3_ref[...])
        dl = dl_ref[...]
        hf = h[:, :2 * GROUP_W] * jnp.exp(-emb[:, 0:1] * dl)
        hb = h[:, 2 * GROUP_W:] * jnp.exp(-emb[:, HY_EMB:HY_EMB + 1] * dl)
        if sub == 0:
            row = _iota(hb.shape, 0) + i * tl
            hb = jnp.where(row == 0, 0.0, hb)
        s_ref[...] += jnp.sum(jnp.abs(hf) + jnp.abs(hb), axis=0, keepdims=True)
        blk0 = sub * FILTER_SUB // n2
        for o in range(2):
            _scatter_rows(scr, o * nsl, hf[:, o * GROUP_W:(o + 1) * GROUP_W], n2, pitch, blk0)
            _scatter_rows(scr, (2 + o) * nsl, hb[:, o * GROUP_W:(o + 1) * GROUP_W], n2, pitch, blk0)
    _gather_n2_major(scr, ((fwd_ref, 0), (fwd_ref, 1), (sec_ref, 0), (sec_ref, 1)), n2, pitch)


def _hyena_filter(l, w1, b1, freq, w2, b2, w3, n1, n2):
    tl = G_ROWS * n2
    bands = (HY_EMB - 1) // 2
    f = jnp.linspace(1e-4, bands - 1, bands, dtype=F32)

    def embed(pos):
        t = pos / max(l - 1, 1)
        ang = (2.0 * math.pi / l) * pos[:, None] * f[None, :]
        return jnp.concatenate([t[:, None], jnp.cos(ang), -jnp.sin(ang)], axis=-1)

    pos = jnp.arange(l, dtype=F32)
    emb = jnp.concatenate([embed(pos), embed(l - pos), jnp.zeros((l, V7X_LANES - 2 * HY_EMB), F32)], axis=-1)
    deltas = jnp.abs(jnp.linspace(math.log(HY_TARGET) / HY_SLOW, math.log(HY_TARGET) / HY_FAST, GROUP_W, dtype=F32))
    z = lambda r, c: jnp.zeros((r, c), F32)
    w1f, w2f = w1.astype(F32), w2.astype(F32)
    w3r = w3.astype(F32).reshape(HY_HID, 2, 2, GROUP_W)
    w3d = lambda d: w3r[:, :, d].reshape(HY_HID, 2 * GROUP_W)
    w1p = z(V7X_LANES, V7X_LANES).at[:HY_EMB, :HY_HID].set(w1f).at[HY_EMB:2 * HY_EMB, HY_HID:].set(w1f)
    w2p = z(V7X_LANES, V7X_LANES).at[:HY_HID, :HY_HID].set(w2f).at[HY_HID:, HY_HID:].set(w2f)
    w3p = z(V7X_LANES, 4 * GROUP_W).at[:HY_HID, :2 * GROUP_W].set(w3d(0)).at[HY_HID:, 2 * GROUP_W:].set(w3d(1))
    vec = lambda x: jnp.tile(x.astype(F32), 2).reshape(1, V7X_LANES)
    consts = (w1p, vec(b1), vec(freq[0]), w2p, vec(b2), vec(freq[1]), w3p, jnp.tile(deltas, 2).reshape(1, -1))
    pitch = _strided_pitch(n2)
    half = jax.ShapeDtypeStruct((2, n2, n1 // 2, GROUP_W), F32)
    hspec = pl.BlockSpec((2, n2, G_ROWS, GROUP_W), lambda i: (0, 0, i, 0))
    return pl.pallas_call(
        functools.partial(_filter_mlp_kernel, n2=n2, pitch=pitch),
        grid=(l // tl,),
        in_specs=[pl.BlockSpec((tl, V7X_LANES), lambda i: (i, 0))] + [_const_spec(c) for c in consts],
        out_specs=(hspec, hspec, pl.BlockSpec((1, 2 * GROUP_W), lambda i: (0, 0))),
        out_shape=(half, half, jax.ShapeDtypeStruct((1, 2 * GROUP_W), F32)),
        scratch_shapes=[pltpu.VMEM((4 * GROUP_W // V7X_LANES, G_ROWS * pitch, V7X_LANES), F32)],
        compiler_params=_cparams("arbitrary"),
        name="hyena_filter_mlp",
    )(emb, *consts)


def _fft_factors(l):
    return (64, 128) if l == 4096 else (2 * l // 256, 256)


@functools.lru_cache(maxsize=None)
def _fft_tables(n1, n2):
    n = n1 * n2
    k1 = np.arange(n1)[None, :, None]
    m2 = np.arange(n2)[:, None, None]

    def theta(n1_vals):
        idx = (k1 * (n1_vals[None, None, :] * n2 + m2)) % n
        return (2.0 * np.pi / n) * idx
    th = theta(np.arange(n1 // 2))
    fr, fi = np.cos(th), -np.sin(th)
    fwd = np.concatenate([np.concatenate([fr, -fi], -1), np.concatenate([fi, fr], -1)], -2)
    thf = theta(np.arange(n1))
    filt = np.concatenate([np.cos(thf), -np.sin(thf)], -2)
    cr = np.swapaxes(np.cos(th), 1, 2) / n
    ci = np.swapaxes(np.sin(th), 1, 2) / n
    inv = np.concatenate([np.concatenate([cr, -ci], -1), np.concatenate([ci, cr], -1)], -2)
    t2 = (2.0 * np.pi / n2) * ((np.arange(n2)[:, None] * np.arange(n2)[None, :]) % n2)
    gr, gi = np.cos(t2), -np.sin(t2)
    g = np.block([[gr, -gi], [gi, gr]])
    ginv = np.block([[gr, gi], [-gi, gr]])
    return tuple(jnp.asarray(a, BF16) for a in (fwd, filt, inv, g, ginv))


def _regroup_rows(scr, n_rows, pitch, store):
    def body(r, carry):
        for s in range(GROUP_W // V7X_LANES):
            store(r, s, scr[s, pl.ds(r, G_ROWS_BF16, stride=pitch), :])
        return carry

    lax.fori_loop(0, n_rows, body, 0, unroll=4)


def _fft_stage1_kernel(zr_ref, zi_ref, t_ref, a_ref, scr, *, n1, pitch):
    for j in range(G_ROWS_BF16):
        rhs = jnp.concatenate([zr_ref[0, j], zi_ref[0, j]], axis=0).astype(BF16)
        _scatter_rows(scr, 0, _dot(t_ref[j], rhs), 2 * n1, pitch, j)

    def store(r, s, rows):
        a_ref[0, r, :, _slab(s)] = rows.astype(BF16)

    _regroup_rows(scr, 2 * n1, pitch, store)


def _fft_stage1(zr, zi, table, n1, n2, pairs, idx_r, idx_i):
    nb = G_ROWS_BF16
    pitch = _strided_pitch(2 * n1)
    return pl.pallas_call(
        functools.partial(_fft_stage1_kernel, n1=n1, pitch=pitch),
        grid=(pairs, n2 // nb),
        in_specs=[pl.BlockSpec((1, nb, n1 // 2, GROUP_W), lambda p, j: (idx_r(p), j, 0, 0)),
                  pl.BlockSpec((1, nb, n1 // 2, GROUP_W), lambda p, j: (idx_i(p), j, 0, 0)),
                  pl.BlockSpec((nb, 2 * n1, n1), lambda p, j: (j, 0, 0))],
        out_specs=pl.BlockSpec((1, 2 * n1, nb, GROUP_W), lambda p, j: (p, 0, j, 0)),
        out_shape=jax.ShapeDtypeStruct((pairs, 2 * n1, n2, GROUP_W), BF16),
        scratch_shapes=[pltpu.VMEM((GROUP_W // V7X_LANES, nb * pitch, V7X_LANES), F32)],
        compiler_params=_cparams("parallel", "parallel"),
        name="fft_stage1",
    )(zr, zi, table)


def _a_specs(n1, n2, kb, index):
    def spec(part):
        def imap(i, p):
            pp, kk = index(i, p)
            return (pp, part * (n1 // kb) + kk, 0, 0)
        return pl.BlockSpec((1, kb, n2, GROUP_W), imap)
    return [spec(0), spec(1)]


def _fft_spec_kernel(ar_ref, ai_ref, g_ref, s_ref, h_ref, *, kb, n2):
    for k in range(kb):
        rhs = jnp.concatenate([ar_ref[0, k], ai_ref[0, k]], axis=0)
        x = _dot(g_ref[...], rhs) * s_ref[0]
        h_ref[0, k, 0] = x[:n2]
        h_ref[0, k, 1] = x[n2:]


def _fft_filter_spectrum(a, g, inv_s, n1, n2, kb=8):
    orders = a.shape[0]
    return pl.pallas_call(
        functools.partial(_fft_spec_kernel, kb=kb, n2=n2),
        grid=(orders, n1 // kb),
        in_specs=_a_specs(n1, n2, kb, lambda o, i: (o, i))
        + [_const_spec(g), pl.BlockSpec((1, 1, GROUP_W), lambda o, i: (o, 0, 0))],
        out_specs=pl.BlockSpec((1, kb, 2, n2, GROUP_W), lambda o, i: (o, i, 0, 0, 0)),
        out_shape=jax.ShapeDtypeStruct((orders, n1, 2, n2, GROUP_W), F32),
        compiler_params=_cparams("parallel", "parallel"),
        name="fft_filter_spectrum",
    )(a, a, g, inv_s)


def _fft_mid_kernel(ar_ref, ai_ref, g_ref, gi_ref, h_ref, b_ref, scr, *, n2, pitch):
    for k in range(G_ROWS_BF16):
        rhs = jnp.concatenate([ar_ref[0, k], ai_ref[0, k]], axis=0)
        x = _dot(g_ref[...], rhs)
        xr, xi = x[:n2], x[n2:]
        hr, hi = h_ref[0, k, 0], h_ref[0, k, 1]
        y = jnp.concatenate([xr * hr - xi * hi, xr * hi + xi * hr], axis=0).astype(BF16)
        _scatter_rows(scr, 0, _dot(gi_ref[...], y), 2 * n2, pitch, k)

    def store(r, s, rows):
        b_ref[0, r, :, _slab(s)] = rows.astype(BF16)

    _regroup_rows(scr, 2 * n2, pitch, store)


def _fft_mid(a, g, ginv, spec, order, n1, n2):
    pairs = a.shape[0]
    kb = G_ROWS_BF16
    pitch = _strided_pitch(2 * n2)
    return pl.pallas_call(
        functools.partial(_fft_mid_kernel, n2=n2, pitch=pitch),
        grid=(n1 // kb, pairs),
        in_specs=_a_specs(n1, n2, kb, lambda i, p: (p, i)) + [_const_spec(g), _const_spec(ginv),
                  pl.BlockSpec((1, kb, 2, n2, GROUP_W), lambda i, p: (order, i, 0, 0, 0))],
        out_specs=pl.BlockSpec((1, 2 * n2, kb, GROUP_W), lambda i, p: (p, 0, i, 0)),
        out_shape=jax.ShapeDtypeStruct((pairs, 2 * n2, n1, GROUP_W), BF16),
        scratch_shapes=[pltpu.VMEM((GROUP_W // V7X_LANES, kb * pitch, V7X_LANES), F32)],
        compiler_params=_cparams("parallel", "parallel"),
        name="fft_mid",
    )(a, a, g, ginv, spec)


def _fft_stage3_kernel(br_ref, bi_ref, t_ref, ur_ref, ui_ref, gr_ref, gi_ref, sk_ref, y_ref, *scr,
                       n1, pitch, time_major):
    h = n1 // 2
    sk = sk_ref[...]
    for j in range(G_ROWS_BF16):
        rhs = jnp.concatenate([br_ref[0, j], bi_ref[0, j]], axis=0)
        o = _dot(t_ref[j], rhs)
        yr = gr_ref[0, j] * (o[:h] + ur_ref[0, j] * sk)
        yi = gi_ref[0, j] * (o[h:] + ui_ref[0, j] * sk)
        if time_major:
            _scatter_rows(scr[0], 0, yr, h, pitch, j)
            _scatter_rows(scr[1], 0, yi, h, pitch, j)
        else:
            y_ref[0, 0, j] = yr
            y_ref[0, 1, j] = yi
    if time_major:
        for part in range(2):
            def store(r, s, rows, part=part):
                y_ref[0, part, r, :, _slab(s)] = rows.astype(y_ref.dtype)
            _regroup_rows(scr[part], h, pitch, store)


def _fft_stage3(b, table, u, gate, skip, n1, n2, time_major):
    pairs = b.shape[0]
    nb = G_ROWS_BF16
    h = n1 // 2
    pitch = _strided_pitch(h)
    bspec = lambda part: pl.BlockSpec((1, nb, n1, GROUP_W), lambda p, j: (p, part * (n2 // nb) + j, 0, 0))
    even = pl.BlockSpec((1, nb, h, GROUP_W), lambda p, j: (2 * p, j, 0, 0))
    odd = pl.BlockSpec((1, nb, h, GROUP_W), lambda p, j: (2 * p + 1, j, 0, 0))
    if time_major:
        out_spec = pl.BlockSpec((1, 2, h, nb, GROUP_W), lambda p, j: (p, 0, 0, j, 0))
        out_shape = jax.ShapeDtypeStruct((pairs, 2, h, n2, GROUP_W), BF16)
        scratch = [pltpu.VMEM((GROUP_W // V7X_LANES, nb * pitch, V7X_LANES), F32)] * 2
    else:
        out_spec = pl.BlockSpec((1, 2, nb, h, GROUP_W), lambda p, j: (p, 0, j, 0, 0))
        out_shape = jax.ShapeDtypeStruct((pairs, 2, n2, h, GROUP_W), F32)
        scratch = []
    y = pl.pallas_call(
        functools.partial(_fft_stage3_kernel, n1=n1, pitch=pitch, time_major=time_major),
        grid=(pairs, n2 // nb),
        in_specs=[bspec(0), bspec(1), pl.BlockSpec((nb, n1, 2 * n1), lambda p, j: (j, 0, 0)),
                  even, odd, even, odd, pl.BlockSpec((1, GROUP_W), lambda p, j: (0, 0))],
        out_specs=out_spec,
        out_shape=out_shape,
        scratch_shapes=scratch,
        compiler_params=_cparams("parallel", "parallel"),
        name="fft_stage3_out" if time_major else "fft_stage3",
    )(b, b, table, u, u, gate, gate, skip.reshape(1, GROUP_W).astype(F32))
    return y.reshape(2 * pairs, h * n2, GROUP_W) if time_major else y.reshape(2 * pairs, n2, h, GROUP_W)


def _hyena_spectrum(l, w1, b1, freq, w2, b2, w3):
    n1, n2 = _fft_factors(l)
    _, t_filt, _, g, _ = _fft_tables(n1, n2)
    first, second, sums = _hyena_filter(l, w1, b1, freq, w2, b2, w3, n1, n2)
    a = _fft_stage1(first, second, t_filt, n1, n2, 2, lambda p: p, lambda p: p)
    inv_s = (1.0 / sums).reshape(2, 1, GROUP_W)
    return _fft_filter_spectrum(a, g, inv_s, n1, n2)


def _hyena_long_conv(u, gate, spec, order, skip, n1, n2, time_major):
    t_fwd, _, t_inv, g, ginv = _fft_tables(n1, n2)
    a = _fft_stage1(u, u, t_fwd, n1, n2, u.shape[0] // 2, lambda p: 2 * p, lambda p: 2 * p + 1)
    b = _fft_mid(a, g, ginv, spec, order, n1, n2)
    return _fft_stage3(b, t_inv, u, gate, skip, n1, n2, time_major)


def _mixer_hyena(zd, P, l, spec):
    n1, n2 = _fft_factors(zd.shape[1])
    v, x1, x2 = _conv3_split(zd, P['d_conv_w'][l].astype(F32), P['d_conv_b'][l].astype(F32), n1, n2)
    z = _hyena_long_conv(v, x1, spec, 0, P['d_skip'][l][0], n1, n2, False)
    return _hyena_long_conv(z, x2, spec, 1, P['d_skip'][l][1], n1, n2, True)


def _gelu_tanh(x):
    return 0.5 * x * (1.0 + jnp.tanh(math.sqrt(2.0 / math.pi) * (x + 0.044715 * (x * x * x))))


def _head_rms(h, bd):
    ss = _dot((h * h).astype(BF16), bd) * (1.0 / HEAD_W)
    return h * lax.rsqrt(ss + EPS)


def _out_proj_kernel(x_ref, af_ref, ab_ref, ag_ref, bf_ref, bb_ref, br_ref, cf_ref, cb_ref, co_ref, yd_ref,
                     w_ref, bg_ref, cg_ref, bd_ref, n2_ref, rwh_ref, rwl_ref, rb_ref,
                     xo_ref, hn_ref, lg_ref):
    bd = bd_ref[...]
    f = lambda ref: ref[...].astype(F32)
    ya = (f(af_ref) + f(ab_ref)) * _gelu_tanh(f(ag_ref))
    r = f(br_ref)
    yb = _head_rms(f(bf_ref) + f(bb_ref), bd) * bg_ref[...] * (r * _sigmoid(r))
    yc = _head_rms(f(cf_ref) + f(cb_ref), bd) * cg_ref[...] * _sigmoid(f(co_ref))
    mixed = jnp.concatenate([ya.astype(BF16), yb.astype(BF16), yc.astype(BF16), yd_ref[...]], axis=-1)
    x = x_ref[...] + _dot(mixed, w_ref[...])
    xo_ref[...] = x
    hn = x * lax.rsqrt(jnp.mean(x * x, axis=-1, keepdims=True) + EPS) * n2_ref[...]
    hn_ref[...] = hn.astype(BF16)
    hi = hn.astype(BF16)
    lo = (hn - hi.astype(F32)).astype(BF16)
    lg_ref[...] = _dot(hi, rwh_ref[...]) + _dot(lo, rwh_ref[...]) + _dot(hi, rwl_ref[...]) + rb_ref[...]


def _out_proj(x2d, af, ab, za, bf, bb, zb, cf, cb, zc, yd, w_out, b_norm_g, c_norm_g, norm2_g, rw, rb, tm=512):
    t = x2d.shape[0]
    bd = _chunk_masks(False)['bd']
    rwh = rw.astype(BF16)
    rwl = (rw - rwh.astype(F32)).astype(BF16)
    row = lambda w, j=0: pl.BlockSpec((tm, w), lambda i: (i, j))
    consts = (w_out, b_norm_g.reshape(1, -1).astype(F32), c_norm_g.reshape(1, -1).astype(F32), bd,
              norm2_g.reshape(1, -1).astype(F32), rwh, rwl, rb)
    return pl.pallas_call(
        _out_proj_kernel,
        grid=(t // tm,),
        in_specs=[row(D_MODEL), row(GROUP_W), row(GROUP_W), row(GROUP_W, 1), row(GROUP_W), row(GROUP_W),
                  row(GROUP_W, 2), row(GROUP_W), row(GROUP_W), row(GROUP_W, 3), row(GROUP_W)]
        + [_const_spec(c) for c in consts],
        out_specs=(row(D_MODEL), row(D_MODEL), row(V7X_LANES)),
        out_shape=(jax.ShapeDtypeStruct((t, D_MODEL), F32), jax.ShapeDtypeStruct((t, D_MODEL), BF16),
                   jax.ShapeDtypeStruct((t, V7X_LANES), F32)),
        compiler_params=_cparams("parallel"),
        name="out_proj",
    )(x2d, af, ab, za, bf, bb, zb, cf, cb, zc, yd, *consts)


def _route(logits):
    lane = _iota(logits.shape, 1).astype(F32)
    is_g = lane < MOE_GROUPS
    gl = jnp.where(is_g, logits, NEG_BIG)
    gmax = jnp.max(gl, axis=1, keepdims=True)
    gidx = jnp.min(jnp.where(gl == gmax, lane, 1e9), axis=1, keepdims=True)
    gprob = 1.0 / jnp.sum(jnp.where(is_g, jnp.exp(gl - gmax), 0.0), axis=1, keepdims=True)
    lo = MOE_GROUPS + MOE_PER_GROUP * gidx
    el = jnp.where((lane >= lo) & (lane < lo + MOE_PER_GROUP), logits, NEG_BIG)
    v1 = jnp.max(el, axis=1, keepdims=True)
    i1 = jnp.min(jnp.where(el == v1, lane, 1e9), axis=1, keepdims=True)
    el2 = jnp.where(lane == i1, NEG_BIG, el)
    v2 = jnp.max(el2, axis=1, keepdims=True)
    i2 = jnp.min(jnp.where(el2 == v2, lane, 1e9), axis=1, keepdims=True)
    e21 = jnp.exp(v2 - v1)
    p1 = 1.0 / (1.0 + e21)
    return jnp.where(lane == i1, p1 * gprob, 0.0) + jnp.where(lane == i2, e21 * p1 * gprob, 0.0), gidx


MOE_TB = 1024
MOE_RT = 64
MOE_MM = 4 * MOE_RT
MOE_SB = MOE_TB + MOE_GROUPS * MOE_RT


def _moe_kernel(hn_ref, lg_ref, tril_ref, wg_ref, wu_ref, wd_ref, o_ref, xs_ref, ys_ref, gs_ref, pos_ref, meta_ref):
    grp = pl.program_id(1)
    tb, sb, rt = MOE_TB, MOE_SB, MOE_RT

    @pl.when(grp == 0)
    def _():
        gates, gidx = _route(lg_ref[...])
        lane = _iota((tb, V7X_LANES), 1).astype(F32)
        ohg = jnp.where(lane == gidx, 1.0, 0.0)
        rank = jnp.sum(_dot(tril_ref[...], ohg.astype(BF16)) * ohg, axis=1, keepdims=True)
        cnt = jnp.sum(ohg, axis=0, keepdims=True)
        pc = jnp.ceil(cnt * (1.0 / rt)) * rt
        lane1 = _iota((1, V7X_LANES), 1)
        off = jnp.zeros((1, V7X_LANES), F32)
        run = jnp.zeros((1, 1), F32)
        for g in range(MOE_GROUPS):
            off = jnp.where(lane1 == g, run, off)
            run = run + jnp.sum(jnp.where(lane1 == g, pc, 0.0), axis=1, keepdims=True)
        pos = jnp.sum(ohg * off, axis=1, keepdims=True) + rank
        pos_ref[...] = jnp.broadcast_to(pos, (tb, V7X_LANES))
        meta_ref[0:1, :] = off
        meta_ref[1:2, :] = pc
        hi = jnp.floor(pos * (1.0 / 64.0))
        lo = pos - 64.0 * hi
        pm = jnp.where(lane == 0.0, hi, jnp.where(lane == 1.0, lo, 0.0)).astype(BF16)
        sel = jnp.where(_iota((8, V7X_LANES), 0) == _iota((8, V7X_LANES), 1), 1.0, 0.0).astype(BF16)
        rows8 = _dot_nt(sel, pm)
        pos_row = 64.0 * rows8[0:1, :] + rows8[1:2, :]
        p = jnp.where(_iota((sb, tb), 0).astype(F32) == pos_row, 1.0, 0.0).astype(BF16)
        xs_ref[...] = _dot(p, hn_ref[...]).astype(BF16)
        gh = gates.astype(BF16)
        gl = (gates - gh.astype(F32)).astype(BF16)
        gsorted = _dot(p, jnp.concatenate([gh, gl], axis=1))
        gs_ref[...] = gsorted[:, :V7X_LANES] + gsorted[:, V7X_LANES:]
        ys_ref[...] = jnp.zeros_like(ys_ref)

    lane1 = _iota((1, V7X_LANES), 1)
    start = jnp.sum(jnp.where(lane1 == grp, meta_ref[0:1, :], 0.0)).astype(jnp.int32)
    rows = jnp.sum(jnp.where(lane1 == grp, meta_ref[1:2, :], 0.0)).astype(jnp.int32)

    def experts(r0, nrows):
        xt = xs_ref[pl.ds(r0, nrows), :]
        gsl = gs_ref[pl.ds(r0, nrows), :]
        lane_t = _iota((nrows, V7X_LANES), 1)
        acc = None
        for j in range(MOE_PER_GROUP):
            g = _dot(xt, wg_ref[j])
            a = (g * _sigmoid(g)) * _dot(xt, wu_ref[j])
            y = _dot(a.astype(BF16), wd_ref[j])
            w = jnp.sum(jnp.where(lane_t == MOE_GROUPS + MOE_PER_GROUP * grp + j, gsl, 0.0), axis=1, keepdims=True)
            acc = w * y if acc is None else acc + w * y
        ys_ref[pl.ds(r0, nrows), :] = acc.astype(BF16)

    ntiles = rows // rt
    per_mm = MOE_MM // rt

    def full(i, carry):
        experts(pl.multiple_of(start + i * MOE_MM, rt), MOE_MM)
        return carry

    lax.fori_loop(0, ntiles // per_mm, full, 0)
    done = (ntiles // per_mm) * per_mm
    size = per_mm // 2
    while size >= 1:
        take = (ntiles % (2 * size)) >= size

        @pl.when(take)
        def _(done=done, size=size):
            experts(pl.multiple_of(start + done * rt, rt), size * rt)

        done = done + jnp.where(take, size, 0)
        size //= 2

    @pl.when(grp == MOE_GROUPS - 1)
    def _():
        pt = jnp.where(_iota((tb, sb), 1).astype(F32) == pos_ref[:, 0:1], 1.0, 0.0).astype(BF16)
        o_ref[...] = _dot(pt, ys_ref[...]).astype(o_ref.dtype)


def _moe(hn, logits, w_gate, w_up, w_down):
    t = hn.shape[0]
    tm = MOE_TB
    tril = jnp.asarray(np.tril(np.ones((tm, tm), np.float32), -1), BF16)
    row = lambda w: pl.BlockSpec((tm, w), lambda i, g: (i, 0))
    wspec = lambda a, b: pl.BlockSpec((MOE_PER_GROUP, a, b), lambda i, g: (g, 0, 0))
    return pl.pallas_call(
        _moe_kernel,
        grid=(t // tm, MOE_GROUPS),
        in_specs=[row(D_MODEL), row(V7X_LANES), _const_spec(tril), wspec(D_MODEL, D_EXPERT),
                  wspec(D_MODEL, D_EXPERT), wspec(D_EXPERT, D_MODEL)],
        out_specs=row(D_MODEL),
        out_shape=jax.ShapeDtypeStruct((t, D_MODEL), BF16),
        scratch_shapes=[pltpu.VMEM((MOE_SB, D_MODEL), BF16), pltpu.VMEM((MOE_SB, D_MODEL), BF16),
                        pltpu.VMEM((MOE_SB, V7X_LANES), F32), pltpu.VMEM((tm, V7X_LANES), F32),
                        pltpu.VMEM((8, V7X_LANES), F32)],
        compiler_params=_cparams("parallel", "arbitrary"),
        name="moe",
    )(hn, logits, tril, w_gate, w_up, w_down)


def _layer(x2d, moe_prev, bsz, seq, l, P, W, spec):
    x2d, za, zb, zc, zd, zs = _in_proj(x2d, moe_prev, P['norm1_g'][l].astype(F32), W['w_in'][l])
    r3 = lambda a: a.reshape(bsz, seq, a.shape[-1])
    r2 = lambda a: a.reshape(bsz * seq, a.shape[-1])
    mp = W['mix'][l]
    cw, cb = P['a_conv_w'][l].astype(F32), P['a_conv_b'][l].reshape(1, -1).astype(F32)
    za3, zb3, zc3, zs3 = r3(za), r3(zb), r3(zc), r3(zs)
    af = _rglru(za3, cw, cb, *mp['a', 0], False)
    ab = _rglru(za3, cw, cb, *mp['a', 1], True)
    bf, bb = _gla(zb3, zs3, mp['b', 0], mp['b', 1])
    cf, cbk = _mlstm(zc3, zs3, mp['c', 0], mp['c', 1])
    yd = _mixer_hyena(r3(zd), P, l, spec)
    x_new, hn, logits = _out_proj(x2d, r2(af), r2(ab), za, r2(bf), r2(bb), zb, r2(cf), r2(cbk), zc, r2(yd),
                                  W['w_out'][l], P['b_norm_g'][l], P['c_norm_g'][l], P['norm2_g'][l],
                                  W['router_w'][l], W['router_b'][l])
    return x_new, _moe(hn, logits, W['w_gate'][l], W['w_up'][l], W['w_down'][l])


def _trunk(x, P, W):
    bsz, seq, _ = x.shape
    depth = P['w_in'].shape[0]
    x2d = x.reshape(bsz * seq, D_MODEL)
    moe_term = None
    for l in range(depth):
        spec = _hyena_spectrum(seq, P['d_ffn_w1'][l], P['d_ffn_b1'][l], P['d_sin_freq'][l], P['d_ffn_w2'][l],
                               P['d_ffn_b2'][l], P['d_ffn_w3'][l])
        x2d, moe_term = _layer(x2d, moe_term, bsz, seq, l, P, W, spec)
    return _final_norm(x2d, moe_term, P['final_norm_g']).reshape(bsz, seq, D_MODEL)


def _prep_weights(P):
    depth = P['w_in'].shape[0]
    W = {'w_in': [_permute_w_in(P['w_in'][l]) for l in range(depth)],
         'w_out': [P['w_out'][l].astype(BF16) for l in range(depth)],
         'mix': [_mixer_prep(P, l) for l in range(depth)],
         'w_gate': [P['moe_w_gate'][l].astype(BF16) for l in range(depth)],
         'w_up': [P['moe_w_up'][l].astype(BF16) for l in range(depth)],
         'w_down': [P['moe_w_down'][l].astype(BF16) for l in range(depth)],
         'router_w': [], 'router_b': []}
    for l in range(depth):
        rw = jnp.concatenate([P['moe_group_w'][l], P['moe_expert_w'][l]], axis=1).astype(F32)
        rb = jnp.concatenate([P['moe_group_b'][l], P['moe_expert_b'][l]]).astype(F32)
        padc = V7X_LANES - rw.shape[1]
        W['router_w'].append(jnp.pad(rw, ((0, 0), (0, padc))))
        W['router_b'].append(jnp.pad(rb, (0, padc)).reshape(1, V7X_LANES))
    return W


def kernel(x_prompt, x_sample, norm1_g, w_in, a_conv_w, a_conv_b, a_gate_w, a_gate_b, a_lambda, b_gate_w, b_gate_b, b_norm_g, c_gate_b, c_norm_g, d_conv_w, d_conv_b, d_ffn_w1, d_ffn_b1, d_sin_freq, d_ffn_w2, d_ffn_b2, d_ffn_w3, d_skip, w_out, norm2_g, moe_group_w, moe_group_b, moe_expert_w, moe_expert_b, moe_w_gate, moe_w_up, moe_w_down, final_norm_g):
    P = {'norm1_g': norm1_g, 'w_in': w_in, 'a_conv_w': a_conv_w, 'a_conv_b': a_conv_b,
         'a_gate_w': a_gate_w, 'a_gate_b': a_gate_b, 'a_lambda': a_lambda, 'b_gate_w': b_gate_w,
         'b_gate_b': b_gate_b, 'b_norm_g': b_norm_g, 'c_gate_b': c_gate_b, 'c_norm_g': c_norm_g,
         'd_conv_w': d_conv_w, 'd_conv_b': d_conv_b, 'd_ffn_w1': d_ffn_w1, 'd_ffn_b1': d_ffn_b1,
         'd_sin_freq': d_sin_freq, 'd_ffn_w2': d_ffn_w2, 'd_ffn_b2': d_ffn_b2, 'd_ffn_w3': d_ffn_w3,
         'd_skip': d_skip, 'w_out': w_out, 'norm2_g': norm2_g, 'moe_group_w': moe_group_w,
         'moe_group_b': moe_group_b, 'moe_expert_w': moe_expert_w, 'moe_expert_b': moe_expert_b,
         'moe_w_gate': moe_w_gate, 'moe_w_up': moe_w_up, 'moe_w_down': moe_w_down,
         'final_norm_g': final_norm_g}
    W = _prep_weights(P)
    return (_trunk(x_prompt, P, W), _trunk(x_sample, P, W))
```

```python
import functools
import math

import jax
import jax.numpy as jnp
import numpy as np
from jax import lax
from jax.experimental import pallas as pl
from jax.experimental.pallas import tpu as pltpu

F32 = jnp.float32
BF16 = jnp.bfloat16

D_MODEL = 1024
GROUP_W = 256
N_HEADS = 4
HEAD_W = GROUP_W // N_HEADS
B_DK = 32
B_RANK = 16
CHUNK = 64
RG_C = 8.0
RG_SEGMENTS = 16
RG_UNROLL = 4
GLA_TAU = 16.0
HY_EMB = 33
HY_HID = 64
HY_TARGET = 1e-2
HY_FAST = 0.3
HY_SLOW = 1.5
MOE_GROUPS = 4
MOE_PER_GROUP = 4
MOE_EXPERTS = 16
D_EXPERT = 512
EPS = 1e-6
NEG_BIG = -1e30

V7X_LANES = 128
V7X_SUBLANES = 8
VMEM_LIMIT = 56 * 1024 * 1024

ZA_W, ZB_W, ZC_W, ZD_W, ZS_W = 512, 768, 1024, 768, 128
Z_SPLITS = (ZA_W, ZB_W, ZC_W, ZD_W, ZS_W)
Z_TOTAL = sum(Z_SPLITS)


def _cparams(*sem):
    return pltpu.CompilerParams(dimension_semantics=sem, vmem_limit_bytes=VMEM_LIMIT)


def _dot(a, b):
    return jnp.dot(a, b, preferred_element_type=F32)


def _dot_nt(a, b):
    return lax.dot_general(a, b, (((1,), (1,)), ((), ())), preferred_element_type=F32)


def _dot_tn(a, b):
    return lax.dot_general(a, b, (((0,), (0,)), ((), ())), preferred_element_type=F32)


def _dot_split(m_bf16, x):
    hi = x.astype(BF16)
    lo = (x - hi.astype(F32)).astype(BF16)
    return _dot(m_bf16, hi) + _dot(m_bf16, lo)


def _split_dot(x, m_bf16):
    hi = x.astype(BF16)
    lo = (x - hi.astype(F32)).astype(BF16)
    return _dot(hi, m_bf16) + _dot(lo, m_bf16)


def _sigmoid(x):
    return 1.0 / (1.0 + jnp.exp(-x))


def _log_sigmoid(x):
    return jnp.minimum(x, 0.0) - jnp.log(1.0 + jnp.exp(-jnp.abs(x)))


def _iota(shape, dim):
    return lax.broadcasted_iota(jnp.int32, shape, dim)


def _in_proj_kernel(*refs, add):
    if add:
        x_ref, m_ref, g_ref, w_ref, xo_ref, za_ref, zb_ref, zc_ref, zd_ref, zs_ref = refs
        x = x_ref[...] + m_ref[...].astype(F32)
        xo_ref[...] = x
    else:
        x_ref, g_ref, w_ref, za_ref, zb_ref, zc_ref, zd_ref, zs_ref = refs
        x = x_ref[...]
    ms = jnp.mean(x * x, axis=-1, keepdims=True)
    xn = (x * lax.rsqrt(ms + EPS) * g_ref[...]).astype(BF16)
    off = 0
    for ref, w in zip((za_ref, zb_ref, zc_ref, zd_ref, zs_ref), Z_SPLITS):
        ref[...] = _dot(xn, w_ref[:, off:off + w]).astype(ref.dtype)
        off += w


def _in_proj(x2d, moe_prev, g, w_perm, tm=512):
    t = x2d.shape[0]
    add = moe_prev is not None
    row = lambda w: pl.BlockSpec((tm, w), lambda i: (i, 0))
    outs = tuple(jax.ShapeDtypeStruct((t, w), F32 if w == ZS_W else BF16) for w in Z_SPLITS)
    out_specs = tuple(row(w) for w in Z_SPLITS)
    consts = (g.reshape(1, D_MODEL), w_perm)
    res = pl.pallas_call(
        functools.partial(_in_proj_kernel, add=add),
        grid=(t // tm,),
        in_specs=[row(D_MODEL)] * (2 if add else 1) + [_const_spec(c) for c in consts],
        out_specs=((row(D_MODEL),) if add else ()) + out_specs,
        out_shape=((jax.ShapeDtypeStruct((t, D_MODEL), F32),) if add else ()) + outs,
        compiler_params=_cparams("parallel"),
        name="in_proj",
    )(*((x2d, moe_prev) if add else (x2d,)), *consts)
    return tuple(res) if add else (x2d,) + tuple(res)


def _final_norm_kernel(x_ref, m_ref, g_ref, o_ref):
    x = x_ref[...] + m_ref[...].astype(F32)
    o_ref[...] = x * lax.rsqrt(jnp.mean(x * x, axis=-1, keepdims=True) + EPS) * g_ref[...]


def _final_norm(x2d, moe_prev, g, tm=512):
    t = x2d.shape[0]
    row = pl.BlockSpec((tm, D_MODEL), lambda i: (i, 0))
    return pl.pallas_call(
        _final_norm_kernel,
        grid=(t // tm,),
        in_specs=[row, row, pl.BlockSpec((1, D_MODEL), lambda i: (0, 0))],
        out_specs=row,
        out_shape=jax.ShapeDtypeStruct((t, D_MODEL), F32),
        compiler_params=_cparams("parallel"),
        name="final_norm",
    )(x2d, moe_prev, g.reshape(1, D_MODEL).astype(F32))


def _permute_w_in(w_in):
    sizes = (256, 256, 128, 128, 256, 256, 16, 16, 256, 256, 256, 256, 16, 768)
    offs = np.concatenate([[0], np.cumsum(sizes)])
    seg = [w_in[:, offs[i]:offs[i + 1]] for i in range(len(sizes))]
    (a_x, a_g, b_q, b_k, b_v, b_r, b_lf, b_lb, c_q, c_k, c_v, c_o, c_g, d_u) = seg
    small = jnp.concatenate([b_lf, b_lb, c_g, jnp.zeros((w_in.shape[0], ZS_W - 48), w_in.dtype)], axis=1)
    return jnp.concatenate([a_x, a_g, b_q, b_k, b_v, b_r, c_q, c_k, c_v, c_o, d_u, small], axis=1).astype(BF16)


HALO = 16


def _strided_pitch(n):
    assert n % V7X_SUBLANES == 0
    return n if (n // V7X_SUBLANES) % 2 == 1 else n + V7X_SUBLANES


def _halo_conv(prev, cur, nxt, w, b, lo, first, last):
    tl = cur.shape[0]
    prev = jnp.where(first, 0.0, prev.astype(F32))
    nxt = jnp.where(last, 0.0, nxt.astype(F32))
    ext = jnp.concatenate([prev, cur.astype(F32), nxt], axis=0)
    n = tl + 2 * HALO
    acc = None
    for j in range(w.shape[0]):
        o = lo + j
        sh = ext if o == 0 else pltpu.roll(ext, (-o) % n, 0)
        term = sh[HALO:HALO + tl] * w[j:j + 1, :]
        acc = term if acc is None else acc + term
    return acc + b


def _rglru_kernel(prev_ref, cur_ref, next_ref, cw_ref, cb_ref, gw_ref, gb_ref, sp_ref, out_ref,
                  a_s, b_s, h_s, p_s, carry, *, tl, reverse):
    i = pl.program_id(1)
    nt = pl.num_programs(1)
    ti = (nt - 1 - i) if reverse else i
    u = _halo_conv(prev_ref[0], cur_ref[0], next_ref[0], cw_ref[...], cb_ref[...], -2, ti == 0, ti == nt - 1)
    gates = _sigmoid(_dot(u.astype(BF16), gw_ref[...]) + gb_ref[...])
    r = gates[:, :GROUP_W]
    ig = gates[:, GROUP_W:]
    log_a = (-RG_C) * r * sp_ref[...]
    a = jnp.exp(log_a)
    bt = jnp.sqrt(1.0 - a * a) * (ig * u)
    nseg = RG_SEGMENTS
    m = tl // nseg
    mp, sp = _strided_pitch(m), _strided_pitch(nseg)
    for k in range(2):
        for seg in range(nseg):
            a_s[k, seg * mp:seg * mp + m, :] = a[seg * m:(seg + 1) * m, k * V7X_LANES:(k + 1) * V7X_LANES]
            b_s[k, seg * mp:seg * mp + m, :] = bt[seg * m:(seg + 1) * m, k * V7X_LANES:(k + 1) * V7X_LANES]

    @pl.when(i == 0)
    def _():
        carry[...] = jnp.zeros_like(carry)

    def body(jj, hp):
        j = (m - 1 - jj) if reverse else jj
        out = []
        for k in range(2):
            h, p = hp[2 * k], hp[2 * k + 1]
            av = a_s[k, pl.ds(j, nseg, stride=mp), :]
            bv = b_s[k, pl.ds(j, nseg, stride=mp), :]
            h = av * h + bv
            p = av * p
            rows = pl.ds(pl.multiple_of(j * sp, V7X_SUBLANES), nseg)
            h_s[k, rows, :] = h
            p_s[k, rows, :] = p
            out += [h, p]
        return tuple(out)

    z8 = jnp.zeros((nseg, V7X_LANES), F32)
    o8 = jnp.ones((nseg, V7X_LANES), F32)
    ends = lax.fori_loop(0, m, body, (z8, o8, z8, o8), unroll=RG_UNROLL)
    for k in range(2):
        h_end, p_end = ends[2 * k], ends[2 * k + 1]
        c = carry[k]
        cin = [None] * nseg
        for seg in (range(nseg - 1, -1, -1) if reverse else range(nseg)):
            cin[seg] = c
            c = h_end[seg:seg + 1, :] + p_end[seg:seg + 1, :] * c
        carry[k] = c
        for seg in range(nseg):
            hv = h_s[k, pl.ds(seg, m, stride=sp), :]
            pv = p_s[k, pl.ds(seg, m, stride=sp), :]
            out_ref[0, seg * m:(seg + 1) * m, k * V7X_LANES:(k + 1) * V7X_LANES] = (hv + pv * cin[seg]).astype(out_ref.dtype)


def _rglru(za, conv_w, conv_b, gate_w_bd, gate_b, softplus_neg_lam, reverse, tl=512):
    b, l, _ = za.shape
    nt = l // tl
    rh = tl // HALO
    nbh = l // HALO

    def tmap(bi, i):
        return (nt - 1 - i) if reverse else i

    kern = functools.partial(_rglru_kernel, tl=tl, reverse=reverse)
    return pl.pallas_call(
        kern,
        grid=(b, nt),
        in_specs=[
            pl.BlockSpec((1, HALO, GROUP_W), lambda bi, i: (bi, jnp.maximum(tmap(bi, i) * rh - 1, 0), 0)),
            pl.BlockSpec((1, tl, GROUP_W), lambda bi, i: (bi, tmap(bi, i), 0)),
            pl.BlockSpec((1, HALO, GROUP_W), lambda bi, i: (bi, jnp.minimum((tmap(bi, i) + 1) * rh, nbh - 1), 0)),
            pl.BlockSpec((4, GROUP_W), lambda bi, i: (0, 0)),
            pl.BlockSpec((1, GROUP_W), lambda bi, i: (0, 0)),
            pl.BlockSpec((GROUP_W, 2 * GROUP_W), lambda bi, i: (0, 0)),
            pl.BlockSpec((1, 2 * GROUP_W), lambda bi, i: (0, 0)),
            pl.BlockSpec((1, GROUP_W), lambda bi, i: (0, 0)),
        ],
        out_specs=pl.BlockSpec((1, tl, GROUP_W), lambda bi, i: (bi, tmap(bi, i), 0)),
        out_shape=jax.ShapeDtypeStruct((b, l, GROUP_W), BF16),
        scratch_shapes=[pltpu.VMEM((2, RG_SEGMENTS * _strided_pitch(tl // RG_SEGMENTS), V7X_LANES), F32)] * 2
        + [pltpu.VMEM((2, (tl // RG_SEGMENTS) * _strided_pitch(RG_SEGMENTS), V7X_LANES), F32)] * 2
        + [pltpu.VMEM((2, 1, V7X_LANES), F32)],
        compiler_params=_cparams("parallel", "arbitrary"),
        name="rglru_bwd" if reverse else "rglru_fwd",
    )(za, za, za, conv_w, conv_b, gate_w_bd, gate_b, softplus_neg_lam)


def _block_diag(blocks):
    h, di, do = blocks.shape
    eye = jnp.eye(h, dtype=blocks.dtype)
    return jnp.einsum('hde,hg->hdge', blocks, eye).reshape(h * di, h * do)


def _chunk_masks(reverse):
    r = np.arange(GROUP_W)
    t = np.arange(CHUNK)
    tri = (t[None, :] >= t[:, None]) if reverse else (t[None, :] <= t[:, None])
    cmask = np.tile(tri, (1, N_HEADS))
    bd = (r[:, None] // HEAD_W) == (r[None, :] // HEAD_W)
    kd = (r[:, None] // HEAD_W) == (np.arange(N_HEADS * B_DK)[None, :] // B_DK)
    i2 = np.tile(np.eye(CHUNK, dtype=bool), (1, N_HEADS))
    hm = np.zeros((8, GROUP_W), bool)
    hm[:N_HEADS] = np.arange(N_HEADS)[:, None] == (r[None, :] // HEAD_W)
    return dict(tri=jnp.asarray(tri, BF16), cmask=jnp.asarray(cmask, F32), bd=jnp.asarray(bd, BF16),
                kd=jnp.asarray(kd, BF16), kdf=jnp.asarray(kd, F32),
                i2=jnp.asarray(i2, F32), hm=jnp.asarray(hm, F32))


def _const_spec(arr):
    nd = arr.ndim
    return pl.BlockSpec(arr.shape, lambda *_: (0,) * nd)


def _gla_chunk(zb_ref, la, tri_ref, kd_ref, kdf_ref, bd_ref, causal, out_ref, st_ref, c):
    rows = slice(c * CHUNK, (c + 1) * CHUNK)
    q = zb_ref[0, rows, 0:128].astype(F32) * (B_DK ** -0.5)
    k = zb_ref[0, rows, 128:256].astype(F32)
    vb = zb_ref[0, rows, 256:512]
    la_c = la[rows]
    bcum = _dot_split(tri_ref[...], la_c)
    btot = jnp.sum(la_c, axis=0, keepdims=True)
    q_in = (q * jnp.exp(bcum)).astype(BF16)
    k_in = (k * jnp.exp(-bcum)).astype(BF16)
    k_st = (k * jnp.exp(btot - bcum)).astype(BF16)
    kexp = jnp.concatenate([k_in] * N_HEADS, axis=0) * kd_ref[...]
    att = jnp.where(causal, _dot_nt(q_in, kexp), 0.0)
    vbd = jnp.concatenate([vb] * N_HEADS, axis=0) * bd_ref[...]
    st = st_ref[...]
    out_ref[0, rows, :] = (_dot(att.astype(BF16), vbd) + _dot_nt(q_in, st.astype(BF16))).astype(out_ref.dtype)
    st_ref[...] = st * jnp.exp(btot) + _dot_tn(vb, k_st) * kdf_ref[...]


def _gla_kernel(zbf_ref, zsf_ref, zbb_ref, zsb_ref, gwf_ref, gbf_ref, gwb_ref, gbb_ref, trif_ref, trib_ref,
                kd_ref, kdf_ref, bd_ref, cmf_ref, cmb_ref, outf_ref, outb_ref, stf_ref, stb_ref, *, tl):
    @pl.when(pl.program_id(1) == 0)
    def _():
        stf_ref[...] = jnp.zeros_like(stf_ref)
        stb_ref[...] = jnp.zeros_like(stb_ref)

    log_alpha = lambda zs_ref, gw_ref, gb_ref: _log_sigmoid(
        _dot(zs_ref[0].astype(BF16), gw_ref[...]) + gb_ref[...]) * (1.0 / GLA_TAU)
    laf = log_alpha(zsf_ref, gwf_ref, gbf_ref)
    lab = log_alpha(zsb_ref, gwb_ref, gbb_ref)
    causal_f = cmf_ref[...] > 0.0
    causal_b = cmb_ref[...] > 0.0
    nch = tl // CHUNK
    for c in range(nch):
        _gla_chunk(zbf_ref, laf, trif_ref, kd_ref, kdf_ref, bd_ref, causal_f, outf_ref, stf_ref, c)
        _gla_chunk(zbb_ref, lab, trib_ref, kd_ref, kdf_ref, bd_ref, causal_b, outb_ref, stb_ref, nch - 1 - c)


def _gla(zb, zs, prep_f, prep_b, tl=512):
    b, l, _ = zb.shape
    nt = l // tl
    mf, mb = _chunk_masks(False), _chunk_masks(True)
    consts = (*prep_f, *prep_b, mf['tri'], mb['tri'], mf['kd'], mf['kdf'], mf['bd'], mf['cmask'], mb['cmask'])
    fmap = lambda bi, i: (bi, i, 0)
    bmap = lambda bi, i: (bi, nt - 1 - i, 0)
    o = jax.ShapeDtypeStruct((b, l, GROUP_W), BF16)
    st = pltpu.VMEM((GROUP_W, N_HEADS * B_DK), F32)
    return pl.pallas_call(
        functools.partial(_gla_kernel, tl=tl),
        grid=(b, nt),
        in_specs=[pl.BlockSpec((1, tl, ZB_W), fmap), pl.BlockSpec((1, tl, ZS_W), fmap),
                  pl.BlockSpec((1, tl, ZB_W), bmap), pl.BlockSpec((1, tl, ZS_W), bmap)]
        + [_const_spec(c) for c in consts],
        out_specs=(pl.BlockSpec((1, tl, GROUP_W), fmap), pl.BlockSpec((1, tl, GROUP_W), bmap)),
        out_shape=(o, o),
        scratch_shapes=[st, st],
        compiler_params=_cparams("parallel", "arbitrary"),
        name="gla",
    )(zb, zs, zb, zs, *consts)


def _mlstm_chunk(zc_ref, ig_all, lf_all, tri_ref, i2_ref, bd_ref, causal, hm_ref, out_ref, s_ref, n_ref, m_ref, c):
    rows = slice(c * CHUNK, (c + 1) * CHUNK)
    qb = zc_ref[0, rows, 0:256]
    k = zc_ref[0, rows, 256:512].astype(F32) * (HEAD_W ** -0.5)
    kb = k.astype(BF16)
    vb = zc_ref[0, rows, 512:768]
    ig = ig_all[rows]
    lf = lf_all[rows]
    bcum = _dot_split(tri_ref[...], lf)
    blast = jnp.sum(lf, axis=0, keepdims=True)
    cc = jnp.sum((ig - bcum) * i2_ref[...], axis=0, keepdims=True)
    d2 = jnp.where(causal, bcum + cc, NEG_BIG)
    m_intra = jnp.full((CHUNK, GROUP_W), NEG_BIG, F32)
    for h in range(N_HEADS):
        hsel = hm_ref[h:h + 1, :] > 0.0
        mh = jnp.max(jnp.where(hsel, d2, NEG_BIG), axis=1, keepdims=True)
        m_intra = jnp.where(hsel, mh, m_intra)
    m_prev = m_ref[...]
    inter_log = bcum + m_prev
    m_t = jnp.maximum(inter_log, m_intra)
    w_intra = jnp.exp(d2 - m_t)
    w_inter = jnp.exp(inter_log - m_t)
    kexp = jnp.concatenate([kb] * N_HEADS, axis=0) * bd_ref[...]
    qk = (_dot_nt(qb, kexp) * w_intra).astype(BF16)
    vbd = jnp.concatenate([vb] * N_HEADS, axis=0) * bd_ref[...]
    s_prev = s_ref[...]
    n_prev = n_ref[...]
    sbd = jnp.concatenate([s_prev.astype(BF16)] * N_HEADS, axis=0) * bd_ref[...]
    nbd = jnp.concatenate([n_prev.astype(BF16)] * N_HEADS, axis=0) * bd_ref[...]
    num = _dot(qk, vbd) + w_inter * _dot(qb, sbd)
    den = _dot(qk, bd_ref[...]) + w_inter * _dot(qb, nbd)
    out_ref[0, rows, :] = (num / jnp.maximum(jnp.abs(den), jnp.exp(-m_t))).astype(out_ref.dtype)
    g_end = blast - bcum + ig
    g_max = jnp.max(g_end, axis=0, keepdims=True)
    kw = (k * jnp.exp(g_end - g_max)).astype(BF16)
    m_new = jnp.maximum(blast + m_prev, g_max)
    a = jnp.exp(blast + m_prev - m_new)
    cf = jnp.exp(g_max - m_new)
    uc_full = _dot_tn(kw, vb)
    un_full = _dot_tn(kw, jnp.ones((CHUNK, GROUP_W), BF16))
    uc = jnp.zeros((HEAD_W, GROUP_W), F32)
    un = jnp.zeros((HEAD_W, GROUP_W), F32)
    for h in range(N_HEADS):
        hsel = hm_ref[h:h + 1, :] > 0.0
        blk = slice(h * HEAD_W, (h + 1) * HEAD_W)
        uc = jnp.where(hsel, uc_full[blk], uc)
        un = jnp.where(hsel, un_full[blk], un)
    s_ref[...] = a * s_prev + cf * uc
    n_ref[...] = a * n_prev + cf * un
    m_ref[...] = m_new


def _mlstm_kernel(zcf_ref, zsf_ref, zcb_ref, zsb_ref, ef_ref, ebf_ref, eb_ref, ebb_ref, trif_ref, trib_ref,
                  i2_ref, bd_ref, cmf_ref, cmb_ref, hm_ref, outf_ref, outb_ref,
                  sf_ref, nf_ref, mf_ref, sb_ref, nb_ref, mb_ref, *, tl):
    @pl.when(pl.program_id(1) == 0)
    def _():
        for s_ref, n_ref, m_ref in ((sf_ref, nf_ref, mf_ref), (sb_ref, nb_ref, mb_ref)):
            s_ref[...] = jnp.zeros_like(s_ref)
            n_ref[...] = jnp.zeros_like(n_ref)
            m_ref[...] = jnp.full_like(m_ref, NEG_BIG)

    def gates(zs_ref, e_ref, eb_ref):
        ge = _split_dot(zs_ref[0], e_ref[...]) + eb_ref[...]
        return ge[:, :GROUP_W], _log_sigmoid(ge[:, GROUP_W:])

    igf, lff = gates(zsf_ref, ef_ref, ebf_ref)
    igb, lfb = gates(zsb_ref, eb_ref, ebb_ref)
    causal_f = cmf_ref[...] > 0.0
    causal_b = cmb_ref[...] > 0.0
    nch = tl // CHUNK
    for c in range(nch):
        _mlstm_chunk(zcf_ref, igf, lff, trif_ref, i2_ref, bd_ref, causal_f, hm_ref, outf_ref,
                     sf_ref, nf_ref, mf_ref, c)
        _mlstm_chunk(zcb_ref, igb, lfb, trib_ref, i2_ref, bd_ref, causal_b, hm_ref, outb_ref,
                     sb_ref, nb_ref, mb_ref, nch - 1 - c)


def _mlstm(zc, zs, prep_f, prep_b, tl=512):
    b, l, _ = zc.shape
    nt = l // tl
    mf, mb = _chunk_masks(False), _chunk_masks(True)
    consts = (*prep_f, *prep_b, mf['tri'], mb['tri'], mf['i2'], mf['bd'], mf['cmask'], mb['cmask'], mf['hm'])
    fmap = lambda bi, i: (bi, i, 0)
    bmap = lambda bi, i: (bi, nt - 1 - i, 0)
    o = jax.ShapeDtypeStruct((b, l, GROUP_W), BF16)
    state = [pltpu.VMEM((HEAD_W, GROUP_W), F32), pltpu.VMEM((HEAD_W, GROUP_W), F32), pltpu.VMEM((1, GROUP_W), F32)]
    return pl.pallas_call(
        functools.partial(_mlstm_kernel, tl=tl),
        grid=(b, nt),
        in_specs=[pl.BlockSpec((1, tl, ZC_W), fmap), pl.BlockSpec((1, tl, ZS_W), fmap),
                  pl.BlockSpec((1, tl, ZC_W), bmap), pl.BlockSpec((1, tl, ZS_W), bmap)]
        + [_const_spec(c) for c in consts],
        out_specs=(pl.BlockSpec((1, tl, GROUP_W), fmap), pl.BlockSpec((1, tl, GROUP_W), bmap)),
        out_shape=(o, o),
        scratch_shapes=state + state,
        compiler_params=_cparams("parallel", "arbitrary"),
        name="mlstm",
    )(zc, zs, zc, zs, *consts)


def _mixer_prep(P, l):
    out = {}
    gw, gb, lam = P['a_gate_w'][l], P['a_gate_b'][l], P['a_lambda'][l]
    for d in range(2):
        w = jnp.concatenate([_block_diag(gw[d, 0]), _block_diag(gw[d, 1])], axis=1).astype(BF16)
        bb = jnp.concatenate([gb[d, 0].reshape(1, -1), gb[d, 1].reshape(1, -1)], axis=1).astype(F32)
        nl = -lam[d].astype(F32)
        sp = (jnp.maximum(nl, 0.0) + jnp.log(1.0 + jnp.exp(-jnp.abs(nl)))).reshape(1, -1)
        out['a', d] = (w, bb, sp)
        bw = jnp.zeros((ZS_W, N_HEADS * B_DK), F32).at[B_RANK * d:B_RANK * (d + 1)].set(P['b_gate_w'][l][d])
        out['b', d] = (bw.astype(BF16), P['b_gate_b'][l][d].reshape(1, -1).astype(F32))
        e = np.zeros((ZS_W, 2 * GROUP_W), np.float32)
        for h in range(N_HEADS):
            e[32 + 8 * d + h, h * HEAD_W:(h + 1) * HEAD_W] = 1.0
            e[36 + 8 * d + h, GROUP_W + h * HEAD_W:GROUP_W + (h + 1) * HEAD_W] = 1.0
        cb = P['c_gate_b'][l].astype(F32)
        eb = jnp.concatenate([jnp.repeat(cb[2 * d], HEAD_W), jnp.repeat(cb[2 * d + 1], HEAD_W)]).reshape(1, -1)
        out['c', d] = (jnp.asarray(e, BF16), eb)
    return out


G_ROWS = 8
G_ROWS_BF16 = 16


def _slab(k):
    return slice(k * V7X_LANES, (k + 1) * V7X_LANES)


def _scatter_rows(scr, base, val, rows, pitch, blk0=0):
    for s in range(val.shape[1] // V7X_LANES):
        for b in range(val.shape[0] // rows):
            r0 = (blk0 + b) * pitch
            scr[base + s, r0:r0 + rows, :] = val[b * rows:(b + 1) * rows, _slab(s)]


def _gather_n2_major(scr, outs, n2, pitch):
    nsl = GROUP_W // V7X_LANES

    def body(j, carry):
        for a, (ref, lead) in enumerate(outs):
            for s in range(nsl):
                ref[lead, j, :, _slab(s)] = scr[a * nsl + s, pl.ds(j, G_ROWS, stride=pitch), :]
        return carry

    lax.fori_loop(0, n2, body, 0, unroll=4)


def _conv3_kernel(prev_ref, cur_ref, next_ref, w_ref, b_ref, v_ref, x1_ref, x2_ref, scr, *, n2, pitch):
    i = pl.program_id(1)
    u = _halo_conv(prev_ref[0], cur_ref[0], next_ref[0], w_ref[...], b_ref[...], -1, i == 0,
                   i == pl.num_programs(1) - 1)
    nsl = GROUP_W // V7X_LANES
    for a in range(3):
        _scatter_rows(scr, a * nsl, u[:, a * GROUP_W:(a + 1) * GROUP_W], n2, pitch)
    _gather_n2_major(scr, ((v_ref, 0), (x1_ref, 0), (x2_ref, 0)), n2, pitch)


def _conv3_split(zd, w, b, n1, n2):
    bsz, l, _ = zd.shape
    tl = G_ROWS * n2
    pitch = _strided_pitch(n2)
    nt, rh, nbh = l // tl, tl // HALO, l // HALO
    o = jax.ShapeDtypeStruct((bsz, n2, n1 // 2, GROUP_W), F32)
    ospec = pl.BlockSpec((1, n2, G_ROWS, GROUP_W), lambda bi, i: (bi, 0, i, 0))
    return pl.pallas_call(
        functools.partial(_conv3_kernel, n2=n2, pitch=pitch),
        grid=(bsz, nt),
        in_specs=[pl.BlockSpec((1, HALO, ZD_W), lambda bi, i: (bi, jnp.maximum(i * rh - 1, 0), 0)),
                  pl.BlockSpec((1, tl, ZD_W), lambda bi, i: (bi, i, 0)),
                  pl.BlockSpec((1, HALO, ZD_W), lambda bi, i: (bi, jnp.minimum((i + 1) * rh, nbh - 1), 0)),
                  pl.BlockSpec((3, ZD_W), lambda bi, i: (0, 0)),
                  pl.BlockSpec((1, ZD_W), lambda bi, i: (0, 0))],
        out_specs=(ospec, ospec, ospec),
        out_shape=(o, o, o),
        scratch_shapes=[pltpu.VMEM((3 * GROUP_W // V7X_LANES, G_ROWS * pitch, V7X_LANES), F32)],
        compiler_params=_cparams("parallel", "parallel"),
        name="hyena_conv3",
    )(zd, zd, zd, w, b.reshape(1, ZD_W))


def _dot_hp(a, b):
    ah, bh = a.astype(BF16), b.astype(BF16)
    al = (a - ah.astype(F32)).astype(BF16)
    bl = (b - bh.astype(F32)).astype(BF16)
    return _dot(ah, bh) + _dot(al, bh) + _dot(ah, bl)


FILTER_SUB = 512


def _filter_mlp_kernel(emb_ref, w1_ref, b1_ref, f0_ref, w2_ref, b2_ref, f1_ref, w3_ref, dl_ref,
                       fwd_ref, sec_ref, s_ref, scr, *, n2, pitch):
    i = pl.program_id(0)
    tl = emb_ref.shape[0]
    nsl = GROUP_W // V7X_LANES

    @pl.when(i == 0)
    def _():
        s_ref[...] = jnp.zeros_like(s_ref)

    for sub in range(tl // FILTER_SUB):
        emb = emb_ref[sub * FILTER_SUB:(sub + 1) * FILTER_SUB, :]
        h = jnp.sin(f0_ref[...] * (_dot_hp(emb, w1_ref[...]) + b1_ref[...]))
        h = jnp.sin(f1_ref[...] * (_dot_hp(h, w2_ref[...]) + b2_ref[...]))
        h = _dot_hp(h, w3_ref[...])
        dl = dl_ref[...]
        hf = h[:, :2 * GROUP_W] * jnp.exp(-emb[:, 0:1] * dl)
        hb = h[:, 2 * GROUP_W:] * jnp.exp(-emb[:, HY_EMB:HY_EMB + 1] * dl)
        if sub == 0:
            row = _iota(hb.shape, 0) + i * tl
            hb = jnp.where(row == 0, 0.0, hb)
        s_ref[...] += jnp.sum(jnp.abs(hf) + jnp.abs(hb), axis=0, keepdims=True)
        blk0 = sub * FILTER_SUB // n2
        for o in range(2):
            _scatter_rows(scr, o * nsl, hf[:, o * GROUP_W:(o + 1) * GROUP_W], n2, pitch, blk0)
            _scatter_rows(scr, (2 + o) * nsl, hb[:, o * GROUP_W:(o + 1) * GROUP_W], n2, pitch, blk0)
    _gather_n2_major(scr, ((fwd_ref, 0), (fwd_ref, 1), (sec_ref, 0), (sec_ref, 1)), n2, pitch)


def _hyena_filter(l, w1, b1, freq, w2, b2, w3, n1, n2):
    tl = G_ROWS * n2
    bands = (HY_EMB - 1) // 2
    f = jnp.linspace(1e-4, bands - 1, bands, dtype=F32)

    def embed(pos):
        t = pos / max(l - 1, 1)
        ang = (2.0 * math.pi / l) * pos[:, None] * f[None, :]
        return jnp.concatenate([t[:, None], jnp.cos(ang), -jnp.sin(ang)], axis=-1)

    pos = jnp.arange(l, dtype=F32)
    emb = jnp.concatenate([embed(pos), embed(l - pos), jnp.zeros((l, V7X_LANES - 2 * HY_EMB), F32)], axis=-1)
    deltas = jnp.abs(jnp.linspace(math.log(HY_TARGET) / HY_SLOW, math.log(HY_TARGET) / HY_FAST, GROUP_W, dtype=F32))
    z = lambda r, c: jnp.zeros((r, c), F32)
    w1f, w2f = w1.astype(F32), w2.astype(F32)
    w3r = w3.astype(F32).reshape(HY_HID, 2, 2, GROUP_W)
    w3d = lambda d: w3r[:, :, d].reshape(HY_HID, 2 * GROUP_W)
    w1p = z(V7X_LANES, V7X_LANES).at[:HY_EMB, :HY_HID].set(w1f).at[HY_EMB:2 * HY_EMB, HY_HID:].set(w1f)
    w2p = z(V7X_LANES, V7X_LANES).at[:HY_HID, :HY_HID].set(w2f).at[HY_HID:, HY_HID:].set(w2f)
    w3p = z(V7X_LANES, 4 * GROUP_W).at[:HY_HID, :2 * GROUP_W].set(w3d(0)).at[HY_HID:, 2 * GROUP_W:].set(w3d(1))
    vec = lambda x: jnp.tile(x.astype(F32), 2).reshape(1, V7X_LANES)
    consts = (w1p, vec(b1), vec(freq[0]), w2p, vec(b2), vec(freq[1]), w3p, jnp.tile(deltas, 2).reshape(1, -1))
    pitch = _strided_pitch(n2)
    half = jax.ShapeDtypeStruct((2, n2, n1 // 2, GROUP_W), F32)
    hspec = pl.BlockSpec((2, n2, G_ROWS, GROUP_W), lambda i: (0, 0, i, 0))
    return pl.pallas_call(
        functools.partial(_filter_mlp_kernel, n2=n2, pitch=pitch),
        grid=(l // tl,),
        in_specs=[pl.BlockSpec((tl, V7X_LANES), lambda i: (i, 0))] + [_const_spec(c) for c in consts],
        out_specs=(hspec, hspec, pl.BlockSpec((1, 2 * GROUP_W), lambda i: (0, 0))),
        out_shape=(half, half, jax.ShapeDtypeStruct((1, 2 * GROUP_W), F32)),
        scratch_shapes=[pltpu.VMEM((4 * GROUP_W // V7X_LANES, G_ROWS * pitch, V7X_LANES), F32)],
        compiler_params=_cparams("arbitrary"),
        name="hyena_filter_mlp",
    )(emb, *consts)


def _fft_factors(l):
    return (64, 128) if l == 4096 else (2 * l // 256, 256)


@functools.lru_cache(maxsize=None)
def _fft_tables(n1, n2):
    n = n1 * n2
    k1 = np.arange(n1)[None, :, None]
    m2 = np.arange(n2)[:, None, None]

    def theta(n1_vals):
        idx = (k1 * (n1_vals[None, None, :] * n2 + m2)) % n
        return (2.0 * np.pi / n) * idx
    th = theta(np.arange(n1 // 2))
    fr, fi = np.cos(th), -np.sin(th)
    fwd = np.concatenate([np.concatenate([fr, -fi], -1), np.concatenate([fi, fr], -1)], -2)
    thf = theta(np.arange(n1))
    filt = np.concatenate([np.cos(thf), -np.sin(thf)], -2)
    cr = np.swapaxes(np.cos(th), 1, 2) / n
    ci = np.swapaxes(np.sin(th), 1, 2) / n
    inv = np.concatenate([np.concatenate([cr, -ci], -1), np.concatenate([ci, cr], -1)], -2)
    t2 = (2.0 * np.pi / n2) * ((np.arange(n2)[:, None] * np.arange(n2)[None, :]) % n2)
    gr, gi = np.cos(t2), -np.sin(t2)
    g = np.block([[gr, -gi], [gi, gr]])
    ginv = np.block([[gr, gi], [-gi, gr]])
    return tuple(jnp.asarray(a, BF16) for a in (fwd, filt, inv, g, ginv))


def _regroup_rows(scr, n_rows, pitch, store):
    def body(r, carry):
        for s in range(GROUP_W // V7X_LANES):
            store(r, s, scr[s, pl.ds(r, G_ROWS_BF16, stride=pitch), :])
        return carry

    lax.fori_loop(0, n_rows, body, 0, unroll=4)


def _fft_stage1_kernel(zr_ref, zi_ref, t_ref, a_ref, scr, *, n1, pitch):
    for j in range(G_ROWS_BF16):
        rhs = jnp.concatenate([zr_ref[0, j], zi_ref[0, j]], axis=0).astype(BF16)
        _scatter_rows(scr, 0, _dot(t_ref[j], rhs), 2 * n1, pitch, j)

    def store(r, s, rows):
        a_ref[0, r, :, _slab(s)] = rows.astype(BF16)

    _regroup_rows(scr, 2 * n1, pitch, store)


def _fft_stage1(zr, zi, table, n1, n2, pairs, idx_r, idx_i):
    nb = G_ROWS_BF16
    pitch = _strided_pitch(2 * n1)
    return pl.pallas_call(
        functools.partial(_fft_stage1_kernel, n1=n1, pitch=pitch),
        grid=(pairs, n2 // nb),
        in_specs=[pl.BlockSpec((1, nb, n1 // 2, GROUP_W), lambda p, j: (idx_r(p), j, 0, 0)),
                  pl.BlockSpec((1, nb, n1 // 2, GROUP_W), lambda p, j: (idx_i(p), j, 0, 0)),
                  pl.BlockSpec((nb, 2 * n1, n1), lambda p, j: (j, 0, 0))],
        out_specs=pl.BlockSpec((1, 2 * n1, nb, GROUP_W), lambda p, j: (p, 0, j, 0)),
        out_shape=jax.ShapeDtypeStruct((pairs, 2 * n1, n2, GROUP_W), BF16),
        scratch_shapes=[pltpu.VMEM((GROUP_W // V7X_LANES, nb * pitch, V7X_LANES), F32)],
        compiler_params=_cparams("parallel", "parallel"),
        name="fft_stage1",
    )(zr, zi, table)


def _a_specs(n1, n2, kb, index):
    def spec(part):
        def imap(i, p):
            pp, kk = index(i, p)
            return (pp, part * (n1 // kb) + kk, 0, 0)
        return pl.BlockSpec((1, kb, n2, GROUP_W), imap)
    return [spec(0), spec(1)]


def _fft_spec_kernel(ar_ref, ai_ref, g_ref, s_ref, h_ref, *, kb, n2):
    for k in range(kb):
        rhs = jnp.concatenate([ar_ref[0, k], ai_ref[0, k]], axis=0)
        x = _dot(g_ref[...], rhs) * s_ref[0]
        h_ref[0, k, 0] = x[:n2]
        h_ref[0, k, 1] = x[n2:]


def _fft_filter_spectrum(a, g, inv_s, n1, n2, kb=8):
    orders = a.shape[0]
    return pl.pallas_call(
        functools.partial(_fft_spec_kernel, kb=kb, n2=n2),
        grid=(orders, n1 // kb),
        in_specs=_a_specs(n1, n2, kb, lambda o, i: (o, i))
        + [_const_spec(g), pl.BlockSpec((1, 1, GROUP_W), lambda o, i: (o, 0, 0))],
        out_specs=pl.BlockSpec((1, kb, 2, n2, GROUP_W), lambda o, i: (o, i, 0, 0, 0)),
        out_shape=jax.ShapeDtypeStruct((orders, n1, 2, n2, GROUP_W), F32),
        compiler_params=_cparams("parallel", "parallel"),
        name="fft_filter_spectrum",
    )(a, a, g, inv_s)


def _fft_mid_kernel(ar_ref, ai_ref, g_ref, gi_ref, h_ref, b_ref, scr, *, n2, pitch):
    for k in range(G_ROWS_BF16):
        rhs = jnp.concatenate([ar_ref[0, k], ai_ref[0, k]], axis=0)
        x = _dot(g_ref[...], rhs)
        xr, xi = x[:n2], x[n2:]
        hr, hi = h_ref[0, k, 0], h_ref[0, k, 1]
        y = jnp.concatenate([xr * hr - xi * hi, xr * hi + xi * hr], axis=0).astype(BF16)
        _scatter_rows(scr, 0, _dot(gi_ref[...], y), 2 * n2, pitch, k)

    def store(r, s, rows):
        b_ref[0, r, :, _slab(s)] = rows.astype(BF16)

    _regroup_rows(scr, 2 * n2, pitch, store)


def _fft_mid(a, g, ginv, spec, order, n1, n2):
    pairs = a.shape[0]
    kb = G_ROWS_BF16
    pitch = _strided_pitch(2 * n2)
    return pl.pallas_call(
        functools.partial(_fft_mid_kernel, n2=n2, pitch=pitch),
        grid=(n1 // kb, pairs),
        in_specs=_a_specs(n1, n2, kb, lambda i, p: (p, i)) + [_const_spec(g), _const_spec(ginv),
                  pl.BlockSpec((1, kb, 2, n2, GROUP_W), lambda i, p: (order, i, 0, 0, 0))],
        out_specs=pl.BlockSpec((1, 2 * n2, kb, GROUP_W), lambda i, p: (p, 0, i, 0)),
        out_shape=jax.ShapeDtypeStruct((pairs, 2 * n2, n1, GROUP_W), BF16),
        scratch_shapes=[pltpu.VMEM((GROUP_W // V7X_LANES, kb * pitch, V7X_LANES), F32)],
        compiler_params=_cparams("parallel", "parallel"),
        name="fft_mid",
    )(a, a, g, ginv, spec)


def _fft_stage3_kernel(br_ref, bi_ref, t_ref, ur_ref, ui_ref, gr_ref, gi_ref, sk_ref, y_ref, *scr,
                       n1, pitch, time_major):
    h = n1 // 2
    sk = sk_ref[...]
    for j in range(G_ROWS_BF16):
        rhs = jnp.concatenate([br_ref[0, j], bi_ref[0, j]], axis=0)
        o = _dot(t_ref[j], rhs)
        yr = gr_ref[0, j] * (o[:h] + ur_ref[0, j] * sk)
        yi = gi_ref[0, j] * (o[h:] + ui_ref[0, j] * sk)
        if time_major:
            _scatter_rows(scr[0], 0, yr, h, pitch, j)
            _scatter_rows(scr[1], 0, yi, h, pitch, j)
        else:
            y_ref[0, 0, j] = yr
            y_ref[0, 1, j] = yi
    if time_major:
        for part in range(2):
            def store(r, s, rows, part=part):
                y_ref[0, part, r, :, _slab(s)] = rows.astype(y_ref.dtype)
            _regroup_rows(scr[part], h, pitch, store)


def _fft_stage3(b, table, u, gate, skip, n1, n2, time_major):
    pairs = b.shape[0]
    nb = G_ROWS_BF16
    h = n1 // 2
    pitch = _strided_pitch(h)
    bspec = lambda part: pl.BlockSpec((1, nb, n1, GROUP_W), lambda p, j: (p, part * (n2 // nb) + j, 0, 0))
    even = pl.BlockSpec((1, nb, h, GROUP_W), lambda p, j: (2 * p, j, 0, 0))
    odd = pl.BlockSpec((1, nb, h, GROUP_W), lambda p, j: (2 * p + 1, j, 0, 0))
    if time_major:
        out_spec = pl.BlockSpec((1, 2, h, nb, GROUP_W), lambda p, j: (p, 0, 0, j, 0))
        out_shape = jax.ShapeDtypeStruct((pairs, 2, h, n2, GROUP_W), BF16)
        scratch = [pltpu.VMEM((GROUP_W // V7X_LANES, nb * pitch, V7X_LANES), F32)] * 2
    else:
        out_spec = pl.BlockSpec((1, 2, nb, h, GROUP_W), lambda p, j: (p, 0, j, 0, 0))
        out_shape = jax.ShapeDtypeStruct((pairs, 2, n2, h, GROUP_W), F32)
        scratch = []
    y = pl.pallas_call(
        functools.partial(_fft_stage3_kernel, n1=n1, pitch=pitch, time_major=time_major),
        grid=(pairs, n2 // nb),
        in_specs=[bspec(0), bspec(1), pl.BlockSpec((nb, n1, 2 * n1), lambda p, j: (j, 0, 0)),
                  even, odd, even, odd, pl.BlockSpec((1, GROUP_W), lambda p, j: (0, 0))],
        out_specs=out_spec,
        out_shape=out_shape,
        scratch_shapes=scratch,
        compiler_params=_cparams("parallel", "parallel"),
        name="fft_stage3_out" if time_major else "fft_stage3",
    )(b, b, table, u, u, gate, gate, skip.reshape(1, GROUP_W).astype(F32))
    return y.reshape(2 * pairs, h * n2, GROUP_W) if time_major else y.reshape(2 * pairs, n2, h, GROUP_W)


def _hyena_spectrum(l, w1, b1, freq, w2, b2, w3):
    n1, n2 = _fft_factors(l)
    _, t_filt, _, g, _ = _fft_tables(n1, n2)
    first, second, sums = _hyena_filter(l, w1, b1, freq, w2, b2, w3, n1, n2)
    a = _fft_stage1(first, second, t_filt, n1, n2, 2, lambda p: p, lambda p: p)
    inv_s = (1.0 / sums).reshape(2, 1, GROUP_W)
    return _fft_filter_spectrum(a, g, inv_s, n1, n2)


def _hyena_long_conv(u, gate, spec, order, skip, n1, n2, time_major):
    t_fwd, _, t_inv, g, ginv = _fft_tables(n1, n2)
    a = _fft_stage1(u, u, t_fwd, n1, n2, u.shape[0] // 2, lambda p: 2 * p, lambda p: 2 * p + 1)
    b = _fft_mid(a, g, ginv, spec, order, n1, n2)
    return _fft_stage3(b, t_inv, u, gate, skip, n1, n2, time_major)


def _mixer_hyena(zd, P, l, spec):
    n1, n2 = _fft_factors(zd.shape[1])
    v, x1, x2 = _conv3_split(zd, P['d_conv_w'][l].astype(F32), P['d_conv_b'][l].astype(F32), n1, n2)
    z = _hyena_long_conv(v, x1, spec, 0, P['d_skip'][l][0], n1, n2, False)
    return _hyena_long_conv(z, x2, spec, 1, P['d_skip'][l][1], n1, n2, True)


def _gelu_tanh(x):
    return 0.5 * x * (1.0 + jnp.tanh(math.sqrt(2.0 / math.pi) * (x + 0.044715 * (x * x * x))))


def _head_rms(h, bd):
    ss = _dot((h * h).astype(BF16), bd) * (1.0 / HEAD_W)
    return h * lax.rsqrt(ss + EPS)


def _out_proj_kernel(x_ref, af_ref, ab_ref, ag_ref, bf_ref, bb_ref, br_ref, cf_ref, cb_ref, co_ref, yd_ref,
                     w_ref, bg_ref, cg_ref, bd_ref, n2_ref, rwh_ref, rwl_ref, rb_ref,
                     xo_ref, hn_ref, lg_ref):
    bd = bd_ref[...]
    f = lambda ref: ref[...].astype(F32)
    ya = (f(af_ref) + f(ab_ref)) * _gelu_tanh(f(ag_ref))
    r = f(br_ref)
    yb = _head_rms(f(bf_ref) + f(bb_ref), bd) * bg_ref[...] * (r * _sigmoid(r))
    yc = _head_rms(f(cf_ref) + f(cb_ref), bd) * cg_ref[...] * _sigmoid(f(co_ref))
    mixed = jnp.concatenate([ya.astype(BF16), yb.astype(BF16), yc.astype(BF16), yd_ref[...]], axis=-1)
    x = x_ref[...] + _dot(mixed, w_ref[...])
    xo_ref[...] = x
    hn = x * lax.rsqrt(jnp.mean(x * x, axis=-1, keepdims=True) + EPS) * n2_ref[...]
    hn_ref[...] = hn.astype(BF16)
    hi = hn.astype(BF16)
    lo = (hn - hi.astype(F32)).astype(BF16)
    lg_ref[...] = _dot(hi, rwh_ref[...]) + _dot(lo, rwh_ref[...]) + _dot(hi, rwl_ref[...]) + rb_ref[...]


def _out_proj(x2d, af, ab, za, bf, bb, zb, cf, cb, zc, yd, w_out, b_norm_g, c_norm_g, norm2_g, rw, rb, tm=512):
    t = x2d.shape[0]
    bd = _chunk_masks(False)['bd']
    rwh = rw.astype(BF16)
    rwl = (rw - rwh.astype(F32)).astype(BF16)
    row = lambda w, j=0: pl.BlockSpec((tm, w), lambda i: (i, j))
    consts = (w_out, b_norm_g.reshape(1, -1).astype(F32), c_norm_g.reshape(1, -1).astype(F32), bd,
              norm2_g.reshape(1, -1).astype(F32), rwh, rwl, rb)
    return pl.pallas_call(
        _out_proj_kernel,
        grid=(t // tm,),
        in_specs=[row(D_MODEL), row(GROUP_W), row(GROUP_W), row(GROUP_W, 1), row(GROUP_W), row(GROUP_W),
                  row(GROUP_W, 2), row(GROUP_W), row(GROUP_W), row(GROUP_W, 3), row(GROUP_W)]
        + [_const_spec(c) for c in consts],
        out_specs=(row(D_MODEL), row(D_MODEL), row(V7X_LANES)),
        out_shape=(jax.ShapeDtypeStruct((t, D_MODEL), F32), jax.ShapeDtypeStruct((t, D_MODEL), BF16),
                   jax.ShapeDtypeStruct((t, V7X_LANES), F32)),
        compiler_params=_cparams("parallel"),
        name="out_proj",
    )(x2d, af, ab, za, bf, bb, zb, cf, cb, zc, yd, *consts)


def _route(logits):
    lane = _iota(logits.shape, 1).astype(F32)
    is_g = lane < MOE_GROUPS
    gl = jnp.where(is_g, logits, NEG_BIG)
    gmax = jnp.max(gl, axis=1, keepdims=True)
    gidx = jnp.min(jnp.where(gl == gmax, lane, 1e9), axis=1, keepdims=True)
    gprob = 1.0 / jnp.sum(jnp.where(is_g, jnp.exp(gl - gmax), 0.0), axis=1, keepdims=True)
    lo = MOE_GROUPS + MOE_PER_GROUP * gidx
    el = jnp.where((lane >= lo) & (lane < lo + MOE_PER_GROUP), logits, NEG_BIG)
    v1 = jnp.max(el, axis=1, keepdims=True)
    i1 = jnp.min(jnp.where(el == v1, lane, 1e9), axis=1, keepdims=True)
    el2 = jnp.where(lane == i1, NEG_BIG, el)
    v2 = jnp.max(el2, axis=1, keepdims=True)
    i2 = jnp.min(jnp.where(el2 == v2, lane, 1e9), axis=1, keepdims=True)
    e21 = jnp.exp(v2 - v1)
    p1 = 1.0 / (1.0 + e21)
    return jnp.where(lane == i1, p1 * gprob, 0.0) + jnp.where(lane == i2, e21 * p1 * gprob, 0.0), gidx


MOE_TB = 1024
MOE_RT = 64
MOE_MM = 4 * MOE_RT
MOE_SB = MOE_TB + MOE_GROUPS * MOE_RT


MOE_SUB = 2
MOE_EPS = 2


def _moe_kernel(hn_blk, lg_blk, tril_ref, wg_ref, wu_ref, wd_ref, o_blk, xs_all, ys_all, gs_all, pos_all, meta_all):
    step = pl.program_id(1)
    sub = pl.program_id(2)
    grp = step // (MOE_PER_GROUP // MOE_EPS)
    e0 = MOE_PER_GROUP * grp + MOE_EPS * (step % (MOE_PER_GROUP // MOE_EPS))
    tb, sb, rt = MOE_TB, MOE_SB, MOE_RT
    tok = pl.ds(pl.multiple_of(sub * tb, tb), tb)
    xs_ref, ys_ref, gs_ref, pos_ref, meta_ref = (r.at[sub] for r in (xs_all, ys_all, gs_all, pos_all, meta_all))
    hn_ref, lg_ref, o_ref = hn_blk.at[tok], lg_blk.at[tok], o_blk.at[tok]

    @pl.when(step == 0)
    def _():
        gates, gidx = _route(lg_ref[...])
        lane = _iota((tb, V7X_LANES), 1).astype(F32)
        ohg = jnp.where(lane == gidx, 1.0, 0.0)
        rank = jnp.sum(_dot(tril_ref[...], ohg.astype(BF16)) * ohg, axis=1, keepdims=True)
        cnt = jnp.sum(ohg, axis=0, keepdims=True)
        pc = jnp.ceil(cnt * (1.0 / rt)) * rt
        lane1 = _iota((1, V7X_LANES), 1)
        off = jnp.zeros((1, V7X_LANES), F32)
        run = jnp.zeros((1, 1), F32)
        for g in range(MOE_GROUPS):
            off = jnp.where(lane1 == g, run, off)
            run = run + jnp.sum(jnp.where(lane1 == g, pc, 0.0), axis=1, keepdims=True)
        pos = jnp.sum(ohg * off, axis=1, keepdims=True) + rank
        pos_ref[...] = jnp.broadcast_to(pos, (tb, V7X_LANES))
        meta_ref[0:1, :] = off
        meta_ref[1:2, :] = pc
        hi = jnp.floor(pos * (1.0 / 64.0))
        lo = pos - 64.0 * hi
        pm = jnp.where(lane == 0.0, hi, jnp.where(lane == 1.0, lo, 0.0)).astype(BF16)
        sel = jnp.where(_iota((8, V7X_LANES), 0) == _iota((8, V7X_LANES), 1), 1.0, 0.0).astype(BF16)
        rows8 = _dot_nt(sel, pm)
        pos_row = 64.0 * rows8[0:1, :] + rows8[1:2, :]
        p = jnp.where(_iota((sb, tb), 0).astype(F32) == pos_row, 1.0, 0.0).astype(BF16)
        xs_ref[...] = _dot(p, hn_ref[...]).astype(BF16)
        gh = gates.astype(BF16)
        gl = (gates - gh.astype(F32)).astype(BF16)
        gsorted = _dot(p, jnp.concatenate([gh, gl], axis=1))
        gs_ref[...] = gsorted[:, :V7X_LANES] + gsorted[:, V7X_LANES:]
        ys_ref[...] = jnp.zeros_like(ys_ref)

    lane1 = _iota((1, V7X_LANES), 1)
    start = jnp.sum(jnp.where(lane1 == grp, meta_ref[0:1, :], 0.0)).astype(jnp.int32)
    rows = jnp.sum(jnp.where(lane1 == grp, meta_ref[1:2, :], 0.0)).astype(jnp.int32)

    def experts(r0, nrows):
        xt = xs_ref[pl.ds(r0, nrows), :]
        gsl = gs_ref[pl.ds(r0, nrows), :]
        lane_t = _iota((nrows, V7X_LANES), 1)
        acc = ys_ref[pl.ds(r0, nrows), :].astype(F32)
        for j in range(MOE_EPS):
            g = _dot(xt, wg_ref[j])
            a = (g * _sigmoid(g)) * _dot(xt, wu_ref[j])
            y = _dot(a.astype(BF16), wd_ref[j])
            w = jnp.sum(jnp.where(lane_t == MOE_GROUPS + e0 + j, gsl, 0.0), axis=1, keepdims=True)
            acc = acc + w * y
        ys_ref[pl.ds(r0, nrows), :] = acc.astype(BF16)

    ntiles = rows // rt
    per_mm = MOE_MM // rt

    def full(i, carry):
        experts(pl.multiple_of(start + i * MOE_MM, rt), MOE_MM)
        return carry

    lax.fori_loop(0, ntiles // per_mm, full, 0)
    done = (ntiles // per_mm) * per_mm
    size = per_mm // 2
    while size >= 1:
        take = (ntiles % (2 * size)) >= size

        @pl.when(take)
        def _(done=done, size=size):
            experts(pl.multiple_of(start + done * rt, rt), size * rt)

        done = done + jnp.where(take, size, 0)
        size //= 2

    @pl.when(step == MOE_EXPERTS // MOE_EPS - 1)
    def _():
        pt = jnp.where(_iota((tb, sb), 1).astype(F32) == pos_ref[:, 0:1], 1.0, 0.0).astype(BF16)
        o_ref[...] = _dot(pt, ys_ref[...]).astype(o_ref.dtype)


def _moe(hn, logits, w_gate, w_up, w_down):
    t = hn.shape[0]
    tm = MOE_TB
    tril = jnp.asarray(np.tril(np.ones((tm, tm), np.float32), -1), BF16)
    row = lambda w: pl.BlockSpec((MOE_SUB * tm, w), lambda i, e, s: (i, 0))
    wspec = lambda a, b: pl.BlockSpec((MOE_EPS, a, b), lambda i, e, s: (e, 0, 0))
    sub = lambda rows, w, dt: pltpu.VMEM((MOE_SUB, rows, w), dt)
    return pl.pallas_call(
        _moe_kernel,
        grid=(t // (MOE_SUB * tm), MOE_EXPERTS // MOE_EPS, MOE_SUB),
        in_specs=[row(D_MODEL), row(V7X_LANES), _const_spec(tril), wspec(D_MODEL, D_EXPERT),
                  wspec(D_MODEL, D_EXPERT), wspec(D_EXPERT, D_MODEL)],
        out_specs=row(D_MODEL),
        out_shape=jax.ShapeDtypeStruct((t, D_MODEL), BF16),
        scratch_shapes=[sub(MOE_SB, D_MODEL, BF16), sub(MOE_SB, D_MODEL, BF16), sub(MOE_SB, V7X_LANES, F32),
                        sub(tm, V7X_LANES, F32), sub(8, V7X_LANES, F32)],
        compiler_params=_cparams("parallel", "arbitrary", "arbitrary"),
        name="moe",
    )(hn, logits, tril, w_gate, w_up, w_down)


def _layer(x2d, moe_prev, bsz, seq, l, P, W, spec):
    x2d, za, zb, zc, zd, zs = _in_proj(x2d, moe_prev, P['norm1_g'][l].astype(F32), W['w_in'][l])
    r3 = lambda a: a.reshape(bsz, seq, a.shape[-1])
    r2 = lambda a: a.reshape(bsz * seq, a.shape[-1])
    mp = W['mix'][l]
    cw, cb = P['a_conv_w'][l].astype(F32), P['a_conv_b'][l].reshape(1, -1).astype(F32)
    za3, zb3, zc3, zs3 = r3(za), r3(zb), r3(zc), r3(zs)
    af = _rglru(za3, cw, cb, *mp['a', 0], False)
    ab = _rglru(za3, cw, cb, *mp['a', 1], True)
    bf, bb = _gla(zb3, zs3, mp['b', 0], mp['b', 1])
    cf, cbk = _mlstm(zc3, zs3, mp['c', 0], mp['c', 1])
    yd = _mixer_hyena(r3(zd), P, l, spec)
    x_new, hn, logits = _out_proj(x2d, r2(af), r2(ab), za, r2(bf), r2(bb), zb, r2(cf), r2(cbk), zc, r2(yd),
                                  W['w_out'][l], P['b_norm_g'][l], P['c_norm_g'][l], P['norm2_g'][l],
                                  W['router_w'][l], W['router_b'][l])
    return x_new, _moe(hn, logits, W['w_gate'][l], W['w_up'][l], W['w_down'][l])


def _trunk(x, P, W):
    bsz, seq, _ = x.shape
    depth = P['w_in'].shape[0]
    x2d = x.reshape(bsz * seq, D_MODEL)
    moe_term = None
    for l in range(depth):
        spec = _hyena_spectrum(seq, P['d_ffn_w1'][l], P['d_ffn_b1'][l], P['d_sin_freq'][l], P['d_ffn_w2'][l],
                               P['d_ffn_b2'][l], P['d_ffn_w3'][l])
        x2d, moe_term = _layer(x2d, moe_term, bsz, seq, l, P, W, spec)
    return _final_norm(x2d, moe_term, P['final_norm_g']).reshape(bsz, seq, D_MODEL)


def _prep_weights(P):
    depth = P['w_in'].shape[0]
    W = {'w_in': [_permute_w_in(P['w_in'][l]) for l in range(depth)],
         'w_out': [P['w_out'][l].astype(BF16) for l in range(depth)],
         'mix': [_mixer_prep(P, l) for l in range(depth)],
         'w_gate': [P['moe_w_gate'][l].astype(BF16) for l in range(depth)],
         'w_up': [P['moe_w_up'][l].astype(BF16) for l in range(depth)],
         'w_down': [P['moe_w_down'][l].astype(BF16) for l in range(depth)],
         'router_w': [], 'router_b': []}
    for l in range(depth):
        rw = jnp.concatenate([P['moe_group_w'][l], P['moe_expert_w'][l]], axis=1).astype(F32)
        rb = jnp.concatenate([P['moe_group_b'][l], P['moe_expert_b'][l]]).astype(F32)
        padc = V7X_LANES - rw.shape[1]
        W['router_w'].append(jnp.pad(rw, ((0, 0), (0, padc))))
        W['router_b'].append(jnp.pad(rb, (0, padc)).reshape(1, V7X_LANES))
    return W


def kernel(x_prompt, x_sample, norm1_g, w_in, a_conv_w, a_conv_b, a_gate_w, a_gate_b, a_lambda, b_gate_w, b_gate_b, b_norm_g, c_gate_b, c_norm_g, d_conv_w, d_conv_b, d_ffn_w1, d_ffn_b1, d_sin_freq, d_ffn_w2, d_ffn_b2, d_ffn_w3, d_skip, w_out, norm2_g, moe_group_w, moe_group_b, moe_expert_w, moe_expert_b, moe_w_gate, moe_w_up, moe_w_down, final_norm_g):
    P = {'norm1_g': norm1_g, 'w_in': w_in, 'a_conv_w': a_conv_w, 'a_conv_b': a_conv_b,
         'a_gate_w': a_gate_w, 'a_gate_b': a_gate_b, 'a_lambda': a_lambda, 'b_gate_w': b_gate_w,
         'b_gate_b': b_gate_b, 'b_norm_g': b_norm_g, 'c_gate_b': c_gate_b, 'c_norm_g': c_norm_g,
         'd_conv_w': d_conv_w, 'd_conv_b': d_conv_b, 'd_ffn_w1': d_ffn_w1, 'd_ffn_b1': d_ffn_b1,
         'd_sin_freq': d_sin_freq, 'd_ffn_w2': d_ffn_w2, 'd_ffn_b2': d_ffn_b2, 'd_ffn_w3': d_ffn_w3,
         'd_skip': d_skip, 'w_out': w_out, 'norm2_g': norm2_g, 'moe_group_w': moe_group_w,
         'moe_group_b': moe_group_b, 'moe_expert_w': moe_expert_w, 'moe_expert_b': moe_expert_b,
         'moe_w_gate': moe_w_gate, 'moe_w_up': moe_w_up, 'moe_w_down': moe_w_down,
         'final_norm_g': final_norm_g}
    W = _prep_weights(P)
    return (_trunk(x_prompt, P, W), _trunk(x_sample, P, W))
```

```python
import functools
import math

import jax
import jax.numpy as jnp
import numpy as np
from jax import lax
from jax.experimental import pallas as pl
from jax.experimental.pallas import tpu as pltpu

F32 = jnp.float32
BF16 = jnp.bfloat16

D_MODEL = 1024
GROUP_W = 256
N_HEADS = 4
HEAD_W = GROUP_W // N_HEADS
B_DK = 32
B_RANK = 16
CHUNK = 64
RG_C = 8.0
RG_SEGMENTS = 16
RG_UNROLL = 4
GLA_TAU = 16.0
HY_EMB = 33
HY_HID = 64
HY_TARGET = 1e-2
HY_FAST = 0.3
HY_SLOW = 1.5
MOE_GROUPS = 4
MOE_PER_GROUP = 4
MOE_EXPERTS = 16
D_EXPERT = 512
EPS = 1e-6
NEG_BIG = -1e30

V7X_LANES = 128
V7X_SUBLANES = 8
VMEM_LIMIT = 56 * 1024 * 1024

ZA_W, ZB_W, ZC_W, ZD_W, ZS_W = 512, 768, 1024, 768, 128
Z_SPLITS = (ZA_W, ZB_W, ZC_W, ZD_W, ZS_W)
Z_TOTAL = sum(Z_SPLITS)


def _cparams(*sem):
    return pltpu.CompilerParams(dimension_semantics=sem, vmem_limit_bytes=VMEM_LIMIT)


def _dot(a, b):
    return jnp.dot(a, b, preferred_element_type=F32)


def _dot_nt(a, b):
    return lax.dot_general(a, b, (((1,), (1,)), ((), ())), preferred_element_type=F32)


def _dot_tn(a, b):
    return lax.dot_general(a, b, (((0,), (0,)), ((), ())), preferred_element_type=F32)


def _dot_split(m_bf16, x):
    hi = x.astype(BF16)
    lo = (x - hi.astype(F32)).astype(BF16)
    return _dot(m_bf16, hi) + _dot(m_bf16, lo)


def _split_dot(x, m_bf16):
    hi = x.astype(BF16)
    lo = (x - hi.astype(F32)).astype(BF16)
    return _dot(hi, m_bf16) + _dot(lo, m_bf16)


def _sigmoid(x):
    return 1.0 / (1.0 + jnp.exp(-x))


def _log_sigmoid(x):
    return jnp.minimum(x, 0.0) - jnp.log(1.0 + jnp.exp(-jnp.abs(x)))


def _iota(shape, dim):
    return lax.broadcasted_iota(jnp.int32, shape, dim)


def _in_proj_kernel(*refs, add):
    if add:
        x_ref, m_ref, g_ref, w_ref, xo_ref, za_ref, zb_ref, zc_ref, zd_ref, zs_ref = refs
        x = x_ref[...] + m_ref[...].astype(F32)
        xo_ref[...] = x
    else:
        x_ref, g_ref, w_ref, za_ref, zb_ref, zc_ref, zd_ref, zs_ref = refs
        x = x_ref[...]
    ms = jnp.mean(x * x, axis=-1, keepdims=True)
    xn = (x * lax.rsqrt(ms + EPS) * g_ref[...]).astype(BF16)
    off = 0
    for ref, w in zip((za_ref, zb_ref, zc_ref, zd_ref, zs_ref), Z_SPLITS):
        ref[...] = _dot(xn, w_ref[:, off:off + w]).astype(ref.dtype)
        off += w


def _in_proj(x2d, moe_prev, g, w_perm, tm=512):
    t = x2d.shape[0]
    add = moe_prev is not None
    row = lambda w: pl.BlockSpec((tm, w), lambda i: (i, 0))
    outs = tuple(jax.ShapeDtypeStruct((t, w), F32 if w == ZS_W else BF16) for w in Z_SPLITS)
    out_specs = tuple(row(w) for w in Z_SPLITS)
    consts = (g.reshape(1, D_MODEL), w_perm)
    res = pl.pallas_call(
        functools.partial(_in_proj_kernel, add=add),
        grid=(t // tm,),
        in_specs=[row(D_MODEL)] * (2 if add else 1) + [_const_spec(c) for c in consts],
        out_specs=((row(D_MODEL),) if add else ()) + out_specs,
        out_shape=((jax.ShapeDtypeStruct((t, D_MODEL), F32),) if add else ()) + outs,
        compiler_params=_cparams("parallel"),
        name="in_proj",
    )(*((x2d, moe_prev) if add else (x2d,)), *consts)
    return tuple(res) if add else (x2d,) + tuple(res)


def _final_norm_kernel(x_ref, m_ref, g_ref, o_ref):
    x = x_ref[...] + m_ref[...].astype(F32)
    o_ref[...] = x * lax.rsqrt(jnp.mean(x * x, axis=-1, keepdims=True) + EPS) * g_ref[...]


def _final_norm(x2d, moe_prev, g, tm=512):
    t = x2d.shape[0]
    row = pl.BlockSpec((tm, D_MODEL), lambda i: (i, 0))
    return pl.pallas_call(
        _final_norm_kernel,
        grid=(t // tm,),
        in_specs=[row, row, pl.BlockSpec((1, D_MODEL), lambda i: (0, 0))],
        out_specs=row,
        out_shape=jax.ShapeDtypeStruct((t, D_MODEL), F32),
        compiler_params=_cparams("parallel"),
        name="final_norm",
    )(x2d, moe_prev, g.reshape(1, D_MODEL).astype(F32))


def _permute_w_in(w_in):
    sizes = (256, 256, 128, 128, 256, 256, 16, 16, 256, 256, 256, 256, 16, 768)
    offs = np.concatenate([[0], np.cumsum(sizes)])
    seg = [w_in[:, offs[i]:offs[i + 1]] for i in range(len(sizes))]
    (a_x, a_g, b_q, b_k, b_v, b_r, b_lf, b_lb, c_q, c_k, c_v, c_o, c_g, d_u) = seg
    small = jnp.concatenate([b_lf, b_lb, c_g, jnp.zeros((w_in.shape[0], ZS_W - 48), w_in.dtype)], axis=1)
    return jnp.concatenate([a_x, a_g, b_q, b_k, b_v, b_r, c_q, c_k, c_v, c_o, d_u, small], axis=1).astype(BF16)


HALO = 16


def _strided_pitch(n):
    assert n % V7X_SUBLANES == 0
    return n if (n // V7X_SUBLANES) % 2 == 1 else n + V7X_SUBLANES


def _halo_conv(prev, cur, nxt, w, b, lo, first, last):
    tl = cur.shape[0]
    prev = jnp.where(first, 0.0, prev.astype(F32))
    nxt = jnp.where(last, 0.0, nxt.astype(F32))
    ext = jnp.concatenate([prev, cur.astype(F32), nxt], axis=0)
    n = tl + 2 * HALO
    acc = None
    for j in range(w.shape[0]):
        o = lo + j
        sh = ext if o == 0 else pltpu.roll(ext, (-o) % n, 0)
        term = sh[HALO:HALO + tl] * w[j:j + 1, :]
        acc = term if acc is None else acc + term
    return acc + b


def _rglru_kernel(prev_ref, cur_ref, next_ref, cw_ref, cb_ref, gw_ref, gb_ref, sp_ref, out_ref,
                  a_s, b_s, h_s, p_s, carry, *, tl, reverse):
    i = pl.program_id(1)
    nt = pl.num_programs(1)
    ti = (nt - 1 - i) if reverse else i
    u = _halo_conv(prev_ref[0], cur_ref[0], next_ref[0], cw_ref[...], cb_ref[...], -2, ti == 0, ti == nt - 1)
    gates = _sigmoid(_dot(u.astype(BF16), gw_ref[...]) + gb_ref[...])
    r = gates[:, :GROUP_W]
    ig = gates[:, GROUP_W:]
    log_a = (-RG_C) * r * sp_ref[...]
    a = jnp.exp(log_a)
    bt = jnp.sqrt(1.0 - a * a) * (ig * u)
    nseg = RG_SEGMENTS
    m = tl // nseg
    mp, sp = _strided_pitch(m), _strided_pitch(nseg)
    for k in range(2):
        for seg in range(nseg):
            a_s[k, seg * mp:seg * mp + m, :] = a[seg * m:(seg + 1) * m, k * V7X_LANES:(k + 1) * V7X_LANES]
            b_s[k, seg * mp:seg * mp + m, :] = bt[seg * m:(seg + 1) * m, k * V7X_LANES:(k + 1) * V7X_LANES]

    @pl.when(i == 0)
    def _():
        carry[...] = jnp.zeros_like(carry)

    def body(jj, hp):
        j = (m - 1 - jj) if reverse else jj
        out = []
        for k in range(2):
            h, p = hp[2 * k], hp[2 * k + 1]
            av = a_s[k, pl.ds(j, nseg, stride=mp), :]
            bv = b_s[k, pl.ds(j, nseg, stride=mp), :]
            h = av * h + bv
            p = av * p
            rows = pl.ds(pl.multiple_of(j * sp, V7X_SUBLANES), nseg)
            h_s[k, rows, :] = h
            p_s[k, rows, :] = p
            out += [h, p]
        return tuple(out)

    z8 = jnp.zeros((nseg, V7X_LANES), F32)
    o8 = jnp.ones((nseg, V7X_LANES), F32)
    ends = lax.fori_loop(0, m, body, (z8, o8, z8, o8), unroll=RG_UNROLL)
    for k in range(2):
        h_end, p_end = ends[2 * k], ends[2 * k + 1]
        c = carry[k]
        cin = [None] * nseg
        for seg in (range(nseg - 1, -1, -1) if reverse else range(nseg)):
            cin[seg] = c
            c = h_end[seg:seg + 1, :] + p_end[seg:seg + 1, :] * c
        carry[k] = c
        for seg in range(nseg):
            hv = h_s[k, pl.ds(seg, m, stride=sp), :]
            pv = p_s[k, pl.ds(seg, m, stride=sp), :]
            out_ref[0, seg * m:(seg + 1) * m, k * V7X_LANES:(k + 1) * V7X_LANES] = (hv + pv * cin[seg]).astype(out_ref.dtype)


def _rglru(za, conv_w, conv_b, gate_w_bd, gate_b, softplus_neg_lam, reverse, tl=512):
    b, l, _ = za.shape
    nt = l // tl
    rh = tl // HALO
    nbh = l // HALO

    def tmap(bi, i):
        return (nt - 1 - i) if reverse else i

    kern = functools.partial(_rglru_kernel, tl=tl, reverse=reverse)
    return pl.pallas_call(
        kern,
        grid=(b, nt),
        in_specs=[
            pl.BlockSpec((1, HALO, GROUP_W), lambda bi, i: (bi, jnp.maximum(tmap(bi, i) * rh - 1, 0), 0)),
            pl.BlockSpec((1, tl, GROUP_W), lambda bi, i: (bi, tmap(bi, i), 0)),
            pl.BlockSpec((1, HALO, GROUP_W), lambda bi, i: (bi, jnp.minimum((tmap(bi, i) + 1) * rh, nbh - 1), 0)),
            pl.BlockSpec((4, GROUP_W), lambda bi, i: (0, 0)),
            pl.BlockSpec((1, GROUP_W), lambda bi, i: (0, 0)),
            pl.BlockSpec((GROUP_W, 2 * GROUP_W), lambda bi, i: (0, 0)),
            pl.BlockSpec((1, 2 * GROUP_W), lambda bi, i: (0, 0)),
            pl.BlockSpec((1, GROUP_W), lambda bi, i: (0, 0)),
        ],
        out_specs=pl.BlockSpec((1, tl, GROUP_W), lambda bi, i: (bi, tmap(bi, i), 0)),
        out_shape=jax.ShapeDtypeStruct((b, l, GROUP_W), BF16),
        scratch_shapes=[pltpu.VMEM((2, RG_SEGMENTS * _strided_pitch(tl // RG_SEGMENTS), V7X_LANES), F32)] * 2
        + [pltpu.VMEM((2, (tl // RG_SEGMENTS) * _strided_pitch(RG_SEGMENTS), V7X_LANES), F32)] * 2
        + [pltpu.VMEM((2, 1, V7X_LANES), F32)],
        compiler_params=_cparams("parallel", "arbitrary"),
        name="rglru_bwd" if reverse else "rglru_fwd",
    )(za, za, za, conv_w, conv_b, gate_w_bd, gate_b, softplus_neg_lam)


def _block_diag(blocks):
    h, di, do = blocks.shape
    eye = jnp.eye(h, dtype=blocks.dtype)
    return jnp.einsum('hde,hg->hdge', blocks, eye).reshape(h * di, h * do)


def _chunk_masks(reverse):
    r = np.arange(GROUP_W)
    t = np.arange(CHUNK)
    tri = (t[None, :] >= t[:, None]) if reverse else (t[None, :] <= t[:, None])
    cmask = np.tile(tri, (1, N_HEADS))
    bd = (r[:, None] // HEAD_W) == (r[None, :] // HEAD_W)
    kd = (r[:, None] // HEAD_W) == (np.arange(N_HEADS * B_DK)[None, :] // B_DK)
    i2 = np.tile(np.eye(CHUNK, dtype=bool), (1, N_HEADS))
    hm = np.zeros((8, GROUP_W), bool)
    hm[:N_HEADS] = np.arange(N_HEADS)[:, None] == (r[None, :] // HEAD_W)
    return dict(tri=jnp.asarray(tri, BF16), cmask=jnp.asarray(cmask, F32), bd=jnp.asarray(bd, BF16),
                kd=jnp.asarray(kd, BF16), kdf=jnp.asarray(kd, F32),
                i2=jnp.asarray(i2, F32), hm=jnp.asarray(hm, F32))


def _const_spec(arr):
    nd = arr.ndim
    return pl.BlockSpec(arr.shape, lambda *_: (0,) * nd)


def _gla_chunk(zb_ref, la, tri_ref, kd_ref, kdf_ref, bd_ref, causal, out_ref, st_ref, c):
    rows = slice(c * CHUNK, (c + 1) * CHUNK)
    q = zb_ref[0, rows, 0:128].astype(F32) * (B_DK ** -0.5)
    k = zb_ref[0, rows, 128:256].astype(F32)
    vb = zb_ref[0, rows, 256:512]
    la_c = la[rows]
    bcum = _dot_split(tri_ref[...], la_c)
    btot = jnp.sum(la_c, axis=0, keepdims=True)
    q_in = (q * jnp.exp(bcum)).astype(BF16)
    k_in = (k * jnp.exp(-bcum)).astype(BF16)
    k_st = (k * jnp.exp(btot - bcum)).astype(BF16)
    kexp = jnp.concatenate([k_in] * N_HEADS, axis=0) * kd_ref[...]
    att = jnp.where(causal, _dot_nt(q_in, kexp), 0.0)
    vbd = jnp.concatenate([vb] * N_HEADS, axis=0) * bd_ref[...]
    st = st_ref[...]
    out_ref[0, rows, :] = (_dot(att.astype(BF16), vbd) + _dot_nt(q_in, st.astype(BF16))).astype(out_ref.dtype)
    st_ref[...] = st * jnp.exp(btot) + _dot_tn(vb, k_st) * kdf_ref[...]


def _gla_kernel(zbf_ref, zsf_ref, zbb_ref, zsb_ref, gwf_ref, gbf_ref, gwb_ref, gbb_ref, trif_ref, trib_ref,
                kd_ref, kdf_ref, bd_ref, cmf_ref, cmb_ref, outf_ref, outb_ref, stf_ref, stb_ref, *, tl):
    @pl.when(pl.program_id(1) == 0)
    def _():
        stf_ref[...] = jnp.zeros_like(stf_ref)
        stb_ref[...] = jnp.zeros_like(stb_ref)

    log_alpha = lambda zs_ref, gw_ref, gb_ref: _log_sigmoid(
        _dot(zs_ref[0].astype(BF16), gw_ref[...]) + gb_ref[...]) * (1.0 / GLA_TAU)
    laf = log_alpha(zsf_ref, gwf_ref, gbf_ref)
    lab = log_alpha(zsb_ref, gwb_ref, gbb_ref)
    causal_f = cmf_ref[...] > 0.0
    causal_b = cmb_ref[...] > 0.0
    nch = tl // CHUNK
    for c in range(nch):
        _gla_chunk(zbf_ref, laf, trif_ref, kd_ref, kdf_ref, bd_ref, causal_f, outf_ref, stf_ref, c)
        _gla_chunk(zbb_ref, lab, trib_ref, kd_ref, kdf_ref, bd_ref, causal_b, outb_ref, stb_ref, nch - 1 - c)


def _gla(zb, zs, prep_f, prep_b, tl=1024):
    b, l, _ = zb.shape
    nt = l // tl
    mf, mb = _chunk_masks(False), _chunk_masks(True)
    consts = (*prep_f, *prep_b, mf['tri'], mb['tri'], mf['kd'], mf['kdf'], mf['bd'], mf['cmask'], mb['cmask'])
    fmap = lambda bi, i: (bi, i, 0)
    bmap = lambda bi, i: (bi, nt - 1 - i, 0)
    o = jax.ShapeDtypeStruct((b, l, GROUP_W), BF16)
    st = pltpu.VMEM((GROUP_W, N_HEADS * B_DK), F32)
    return pl.pallas_call(
        functools.partial(_gla_kernel, tl=tl),
        grid=(b, nt),
        in_specs=[pl.BlockSpec((1, tl, ZB_W), fmap), pl.BlockSpec((1, tl, ZS_W), fmap),
                  pl.BlockSpec((1, tl, ZB_W), bmap), pl.BlockSpec((1, tl, ZS_W), bmap)]
        + [_const_spec(c) for c in consts],
        out_specs=(pl.BlockSpec((1, tl, GROUP_W), fmap), pl.BlockSpec((1, tl, GROUP_W), bmap)),
        out_shape=(o, o),
        scratch_shapes=[st, st],
        compiler_params=_cparams("parallel", "arbitrary"),
        name="gla",
    )(zb, zs, zb, zs, *consts)


def _mlstm_chunk(zc_ref, ig_all, lf_all, tri_ref, i2_ref, bd_ref, causal, hm_ref, out_ref, s_ref, n_ref, m_ref, c):
    rows = slice(c * CHUNK, (c + 1) * CHUNK)
    qb = zc_ref[0, rows, 0:256]
    k = zc_ref[0, rows, 256:512].astype(F32) * (HEAD_W ** -0.5)
    kb = k.astype(BF16)
    vb = zc_ref[0, rows, 512:768]
    ig = ig_all[rows]
    lf = lf_all[rows]
    bcum = _dot_split(tri_ref[...], lf)
    blast = jnp.sum(lf, axis=0, keepdims=True)
    cc = jnp.sum((ig - bcum) * i2_ref[...], axis=0, keepdims=True)
    d2 = jnp.where(causal, bcum + cc, NEG_BIG)
    m_intra = jnp.full((CHUNK, GROUP_W), NEG_BIG, F32)
    for h in range(N_HEADS):
        hsel = hm_ref[h:h + 1, :] > 0.0
        mh = jnp.max(jnp.where(hsel, d2, NEG_BIG), axis=1, keepdims=True)
        m_intra = jnp.where(hsel, mh, m_intra)
    m_prev = m_ref[...]
    inter_log = bcum + m_prev
    m_t = jnp.maximum(inter_log, m_intra)
    w_intra = jnp.exp(d2 - m_t)
    w_inter = jnp.exp(inter_log - m_t)
    kexp = jnp.concatenate([kb] * N_HEADS, axis=0) * bd_ref[...]
    qk = (_dot_nt(qb, kexp) * w_intra).astype(BF16)
    vbd = jnp.concatenate([vb] * N_HEADS, axis=0) * bd_ref[...]
    s_prev = s_ref[...]
    n_prev = n_ref[...]
    sbd = jnp.concatenate([s_prev.astype(BF16)] * N_HEADS, axis=0) * bd_ref[...]
    nbd = jnp.concatenate([n_prev.astype(BF16)] * N_HEADS, axis=0) * bd_ref[...]
    num = _dot(qk, vbd) + w_inter * _dot(qb, sbd)
    den = _dot(qk, bd_ref[...]) + w_inter * _dot(qb, nbd)
    out_ref[0, rows, :] = (num / jnp.maximum(jnp.abs(den), jnp.exp(-m_t))).astype(out_ref.dtype)
    g_end = blast - bcum + ig
    g_max = jnp.max(g_end, axis=0, keepdims=True)
    kw = (k * jnp.exp(g_end - g_max)).astype(BF16)
    m_new = jnp.maximum(blast + m_prev, g_max)
    a = jnp.exp(blast + m_prev - m_new)
    cf = jnp.exp(g_max - m_new)
    uc_full = _dot_tn(kw, vb)
    un_full = _dot_tn(kw, jnp.ones((CHUNK, GROUP_W), BF16))
    uc = jnp.zeros((HEAD_W, GROUP_W), F32)
    un = jnp.zeros((HEAD_W, GROUP_W), F32)
    for h in range(N_HEADS):
        hsel = hm_ref[h:h + 1, :] > 0.0
        blk = slice(h * HEAD_W, (h + 1) * HEAD_W)
        uc = jnp.where(hsel, uc_full[blk], uc)
        un = jnp.where(hsel, un_full[blk], un)
    s_ref[...] = a * s_prev + cf * uc
    n_ref[...] = a * n_prev + cf * un
    m_ref[...] = m_new


def _mlstm_kernel(zcf_ref, zsf_ref, zcb_ref, zsb_ref, ef_ref, ebf_ref, eb_ref, ebb_ref, trif_ref, trib_ref,
                  i2_ref, bd_ref, cmf_ref, cmb_ref, hm_ref, outf_ref, outb_ref,
                  sf_ref, nf_ref, mf_ref, sb_ref, nb_ref, mb_ref, *, tl):
    @pl.when(pl.program_id(1) == 0)
    def _():
        for s_ref, n_ref, m_ref in ((sf_ref, nf_ref, mf_ref), (sb_ref, nb_ref, mb_ref)):
            s_ref[...] = jnp.zeros_like(s_ref)
            n_ref[...] = jnp.zeros_like(n_ref)
            m_ref[...] = jnp.full_like(m_ref, NEG_BIG)

    def gates(zs_ref, e_ref, eb_ref):
        ge = _split_dot(zs_ref[0], e_ref[...]) + eb_ref[...]
        return ge[:, :GROUP_W], _log_sigmoid(ge[:, GROUP_W:])

    igf, lff = gates(zsf_ref, ef_ref, ebf_ref)
    igb, lfb = gates(zsb_ref, eb_ref, ebb_ref)
    causal_f = cmf_ref[...] > 0.0
    causal_b = cmb_ref[...] > 0.0
    nch = tl // CHUNK
    for c in range(nch):
        _mlstm_chunk(zcf_ref, igf, lff, trif_ref, i2_ref, bd_ref, causal_f, hm_ref, outf_ref,
                     sf_ref, nf_ref, mf_ref, c)
        _mlstm_chunk(zcb_ref, igb, lfb, trib_ref, i2_ref, bd_ref, causal_b, hm_ref, outb_ref,
                     sb_ref, nb_ref, mb_ref, nch - 1 - c)


def _mlstm(zc, zs, prep_f, prep_b, tl=1024):
    b, l, _ = zc.shape
    nt = l // tl
    mf, mb = _chunk_masks(False), _chunk_masks(True)
    consts = (*prep_f, *prep_b, mf['tri'], mb['tri'], mf['i2'], mf['bd'], mf['cmask'], mb['cmask'], mf['hm'])
    fmap = lambda bi, i: (bi, i, 0)
    bmap = lambda bi, i: (bi, nt - 1 - i, 0)
    o = jax.ShapeDtypeStruct((b, l, GROUP_W), BF16)
    state = [pltpu.VMEM((HEAD_W, GROUP_W), F32), pltpu.VMEM((HEAD_W, GROUP_W), F32), pltpu.VMEM((1, GROUP_W), F32)]
    return pl.pallas_call(
        functools.partial(_mlstm_kernel, tl=tl),
        grid=(b, nt),
        in_specs=[pl.BlockSpec((1, tl, ZC_W), fmap), pl.BlockSpec((1, tl, ZS_W), fmap),
                  pl.BlockSpec((1, tl, ZC_W), bmap), pl.BlockSpec((1, tl, ZS_W), bmap)]
        + [_const_spec(c) for c in consts],
        out_specs=(pl.BlockSpec((1, tl, GROUP_W), fmap), pl.BlockSpec((1, tl, GROUP_W), bmap)),
        out_shape=(o, o),
        scratch_shapes=state + state,
        compiler_params=_cparams("parallel", "arbitrary"),
        name="mlstm",
    )(zc, zs, zc, zs, *consts)


def _mixer_prep(P, l):
    out = {}
    gw, gb, lam = P['a_gate_w'][l], P['a_gate_b'][l], P['a_lambda'][l]
    for d in range(2):
        w = jnp.concatenate([_block_diag(gw[d, 0]), _block_diag(gw[d, 1])], axis=1).astype(BF16)
        bb = jnp.concatenate([gb[d, 0].reshape(1, -1), gb[d, 1].reshape(1, -1)], axis=1).astype(F32)
        nl = -lam[d].astype(F32)
        sp = (jnp.maximum(nl, 0.0) + jnp.log(1.0 + jnp.exp(-jnp.abs(nl)))).reshape(1, -1)
        out['a', d] = (w, bb, sp)
        bw = jnp.zeros((ZS_W, N_HEADS * B_DK), F32).at[B_RANK * d:B_RANK * (d + 1)].set(P['b_gate_w'][l][d])
        out['b', d] = (bw.astype(BF16), P['b_gate_b'][l][d].reshape(1, -1).astype(F32))
        e = np.zeros((ZS_W, 2 * GROUP_W), np.float32)
        for h in range(N_HEADS):
            e[32 + 8 * d + h, h * HEAD_W:(h + 1) * HEAD_W] = 1.0
            e[36 + 8 * d + h, GROUP_W + h * HEAD_W:GROUP_W + (h + 1) * HEAD_W] = 1.0
        cb = P['c_gate_b'][l].astype(F32)
        eb = jnp.concatenate([jnp.repeat(cb[2 * d], HEAD_W), jnp.repeat(cb[2 * d + 1], HEAD_W)]).reshape(1, -1)
        out['c', d] = (jnp.asarray(e, BF16), eb)
    return out


G_ROWS = 8
G_ROWS_BF16 = 16


def _slab(k):
    return slice(k * V7X_LANES, (k + 1) * V7X_LANES)


def _scatter_rows(scr, base, val, rows, pitch, blk0=0):
    for s in range(val.shape[1] // V7X_LANES):
        for b in range(val.shape[0] // rows):
            r0 = (blk0 + b) * pitch
            scr[base + s, r0:r0 + rows, :] = val[b * rows:(b + 1) * rows, _slab(s)]


def _gather_n2_major(scr, outs, n2, pitch):
    nsl = GROUP_W // V7X_LANES

    def body(j, carry):
        for a, (ref, lead) in enumerate(outs):
            for s in range(nsl):
                ref[lead, j, :, _slab(s)] = scr[a * nsl + s, pl.ds(j, G_ROWS, stride=pitch), :]
        return carry

    lax.fori_loop(0, n2, body, 0, unroll=8)


def _conv3_kernel(prev_ref, cur_ref, next_ref, w_ref, b_ref, v_ref, x1_ref, x2_ref, scr, *, n2, pitch):
    i = pl.program_id(1)
    u = _halo_conv(prev_ref[0], cur_ref[0], next_ref[0], w_ref[...], b_ref[...], -1, i == 0,
                   i == pl.num_programs(1) - 1)
    nsl = GROUP_W // V7X_LANES
    for a in range(3):
        _scatter_rows(scr, a * nsl, u[:, a * GROUP_W:(a + 1) * GROUP_W], n2, pitch)
    _gather_n2_major(scr, ((v_ref, 0), (x1_ref, 0), (x2_ref, 0)), n2, pitch)


def _conv3_split(zd, w, b, n1, n2):
    bsz, l, _ = zd.shape
    tl = G_ROWS * n2
    pitch = _strided_pitch(n2)
    nt, rh, nbh = l // tl, tl // HALO, l // HALO
    o = jax.ShapeDtypeStruct((bsz, n2, n1 // 2, GROUP_W), F32)
    ospec = pl.BlockSpec((1, n2, G_ROWS, GROUP_W), lambda bi, i: (bi, 0, i, 0))
    return pl.pallas_call(
        functools.partial(_conv3_kernel, n2=n2, pitch=pitch),
        grid=(bsz, nt),
        in_specs=[pl.BlockSpec((1, HALO, ZD_W), lambda bi, i: (bi, jnp.maximum(i * rh - 1, 0), 0)),
                  pl.BlockSpec((1, tl, ZD_W), lambda bi, i: (bi, i, 0)),
                  pl.BlockSpec((1, HALO, ZD_W), lambda bi, i: (bi, jnp.minimum((i + 1) * rh, nbh - 1), 0)),
                  pl.BlockSpec((3, ZD_W), lambda bi, i: (0, 0)),
                  pl.BlockSpec((1, ZD_W), lambda bi, i: (0, 0))],
        out_specs=(ospec, ospec, ospec),
        out_shape=(o, o, o),
        scratch_shapes=[pltpu.VMEM((3 * GROUP_W // V7X_LANES, G_ROWS * pitch, V7X_LANES), F32)],
        compiler_params=_cparams("parallel", "parallel"),
        name="hyena_conv3",
    )(zd, zd, zd, w, b.reshape(1, ZD_W))


def _dot_hp(a, b):
    ah, bh = a.astype(BF16), b.astype(BF16)
    al = (a - ah.astype(F32)).astype(BF16)
    bl = (b - bh.astype(F32)).astype(BF16)
    return _dot(ah, bh) + _dot(al, bh) + _dot(ah, bl)


FILTER_SUB = 512


def _filter_mlp_kernel(emb_ref, w1_ref, b1_ref, f0_ref, w2_ref, b2_ref, f1_ref, w3_ref, dl_ref,
                       fwd_ref, sec_ref, s_ref, scr, *, n2, pitch):
    i = pl.program_id(0)
    tl = emb_ref.shape[0]
    nsl = GROUP_W // V7X_LANES

    @pl.when(i == 0)
    def _():
        s_ref[...] = jnp.zeros_like(s_ref)

    for sub in range(tl // FILTER_SUB):
        emb = emb_ref[sub * FILTER_SUB:(sub + 1) * FILTER_SUB, :]
        h = jnp.sin(f0_ref[...] * (_dot_hp(emb, w1_ref[...]) + b1_ref[...]))
        h = jnp.sin(f1_ref[...] * (_dot_hp(h, w2_ref[...]) + b2_ref[...]))
        h = _dot_hp(h, w3_ref[...])
        dl = dl_ref[...]
        hf = h[:, :2 * GROUP_W] * jnp.exp(-emb[:, 0:1] * dl)
        hb = h[:, 2 * GROUP_W:] * jnp.exp(-emb[:, HY_EMB:HY_EMB + 1] * dl)
        if sub == 0:
            row = _iota(hb.shape, 0) + i * tl
            hb = jnp.where(row == 0, 0.0, hb)
        s_ref[...] += jnp.sum(jnp.abs(hf) + jnp.abs(hb), axis=0, keepdims=True)
        blk0 = sub * FILTER_SUB // n2
        for o in range(2):
            _scatter_rows(scr, o * nsl, hf[:, o * GROUP_W:(o + 1) * GROUP_W], n2, pitch, blk0)
            _scatter_rows(scr, (2 + o) * nsl, hb[:, o * GROUP_W:(o + 1) * GROUP_W], n2, pitch, blk0)
    _gather_n2_major(scr, ((fwd_ref, 0), (fwd_ref, 1), (sec_ref, 0), (sec_ref, 1)), n2, pitch)


def _hyena_filter(l, w1, b1, freq, w2, b2, w3, n1, n2):
    tl = G_ROWS * n2
    bands = (HY_EMB - 1) // 2
    f = jnp.linspace(1e-4, bands - 1, bands, dtype=F32)

    def embed(pos):
        t = pos / max(l - 1, 1)
        ang = (2.0 * math.pi / l) * pos[:, None] * f[None, :]
        return jnp.concatenate([t[:, None], jnp.cos(ang), -jnp.sin(ang)], axis=-1)

    pos = jnp.arange(l, dtype=F32)
    emb = jnp.concatenate([embed(pos), embed(l - pos), jnp.zeros((l, V7X_LANES - 2 * HY_EMB), F32)], axis=-1)
    deltas = jnp.abs(jnp.linspace(math.log(HY_TARGET) / HY_SLOW, math.log(HY_TARGET) / HY_FAST, GROUP_W, dtype=F32))
    z = lambda r, c: jnp.zeros((r, c), F32)
    w1f, w2f = w1.astype(F32), w2.astype(F32)
    w3r = w3.astype(F32).reshape(HY_HID, 2, 2, GROUP_W)
    w3d = lambda d: w3r[:, :, d].reshape(HY_HID, 2 * GROUP_W)
    w1p = z(V7X_LANES, V7X_LANES).at[:HY_EMB, :HY_HID].set(w1f).at[HY_EMB:2 * HY_EMB, HY_HID:].set(w1f)
    w2p = z(V7X_LANES, V7X_LANES).at[:HY_HID, :HY_HID].set(w2f).at[HY_HID:, HY_HID:].set(w2f)
    w3p = z(V7X_LANES, 4 * GROUP_W).at[:HY_HID, :2 * GROUP_W].set(w3d(0)).at[HY_HID:, 2 * GROUP_W:].set(w3d(1))
    vec = lambda x: jnp.tile(x.astype(F32), 2).reshape(1, V7X_LANES)
    consts = (w1p, vec(b1), vec(freq[0]), w2p, vec(b2), vec(freq[1]), w3p, jnp.tile(deltas, 2).reshape(1, -1))
    pitch = _strided_pitch(n2)
    half = jax.ShapeDtypeStruct((2, n2, n1 // 2, GROUP_W), F32)
    hspec = pl.BlockSpec((2, n2, G_ROWS, GROUP_W), lambda i: (0, 0, i, 0))
    return pl.pallas_call(
        functools.partial(_filter_mlp_kernel, n2=n2, pitch=pitch),
        grid=(l // tl,),
        in_specs=[pl.BlockSpec((tl, V7X_LANES), lambda i: (i, 0))] + [_const_spec(c) for c in consts],
        out_specs=(hspec, hspec, pl.BlockSpec((1, 2 * GROUP_W), lambda i: (0, 0))),
        out_shape=(half, half, jax.ShapeDtypeStruct((1, 2 * GROUP_W), F32)),
        scratch_shapes=[pltpu.VMEM((4 * GROUP_W // V7X_LANES, G_ROWS * pitch, V7X_LANES), F32)],
        compiler_params=_cparams("arbitrary"),
        name="hyena_filter_mlp",
    )(emb, *consts)


def _fft_factors(l):
    return (64, 128) if l == 4096 else (2 * l // 256, 256)


@functools.lru_cache(maxsize=None)
def _fft_tables(n1, n2):
    n = n1 * n2
    k1 = np.arange(n1)[None, :, None]
    m2 = np.arange(n2)[:, None, None]

    def theta(n1_vals):
        idx = (k1 * (n1_vals[None, None, :] * n2 + m2)) % n
        return (2.0 * np.pi / n) * idx
    th = theta(np.arange(n1 // 2))
    fr, fi = np.cos(th), -np.sin(th)
    fwd = np.concatenate([np.concatenate([fr, -fi], -1), np.concatenate([fi, fr], -1)], -2)
    thf = theta(np.arange(n1))
    filt = np.concatenate([np.cos(thf), -np.sin(thf)], -2)
    cr = np.swapaxes(np.cos(th), 1, 2) / n
    ci = np.swapaxes(np.sin(th), 1, 2) / n
    inv = np.concatenate([np.concatenate([cr, -ci], -1), np.concatenate([ci, cr], -1)], -2)
    t2 = (2.0 * np.pi / n2) * ((np.arange(n2)[:, None] * np.arange(n2)[None, :]) % n2)
    gr, gi = np.cos(t2), -np.sin(t2)
    g = np.block([[gr, -gi], [gi, gr]])
    ginv = np.block([[gr, gi], [-gi, gr]])
    return tuple(jnp.asarray(a, BF16) for a in (fwd, filt, inv, g, ginv))


def _regroup_rows(scr, n_rows, pitch, store):
    def body(r, carry):
        for s in range(GROUP_W // V7X_LANES):
            store(r, s, scr[s, pl.ds(r, G_ROWS_BF16, stride=pitch), :])
        return carry

    lax.fori_loop(0, n_rows, body, 0, unroll=8)


def _fft_stage1_kernel(zr_ref, zi_ref, t_ref, a_ref, scr, *, n1, pitch):
    for j in range(G_ROWS_BF16):
        rhs = jnp.concatenate([zr_ref[0, j], zi_ref[0, j]], axis=0).astype(BF16)
        _scatter_rows(scr, 0, _dot(t_ref[j], rhs), 2 * n1, pitch, j)

    def store(r, s, rows):
        a_ref[0, r, :, _slab(s)] = rows.astype(BF16)

    _regroup_rows(scr, 2 * n1, pitch, store)


def _fft_stage1(zr, zi, table, n1, n2, pairs, idx_r, idx_i):
    nb = G_ROWS_BF16
    pitch = _strided_pitch(2 * n1)
    return pl.pallas_call(
        functools.partial(_fft_stage1_kernel, n1=n1, pitch=pitch),
        grid=(pairs, n2 // nb),
        in_specs=[pl.BlockSpec((1, nb, n1 // 2, GROUP_W), lambda p, j: (idx_r(p), j, 0, 0)),
                  pl.BlockSpec((1, nb, n1 // 2, GROUP_W), lambda p, j: (idx_i(p), j, 0, 0)),
                  pl.BlockSpec((nb, 2 * n1, n1), lambda p, j: (j, 0, 0))],
        out_specs=pl.BlockSpec((1, 2 * n1, nb, GROUP_W), lambda p, j: (p, 0, j, 0)),
        out_shape=jax.ShapeDtypeStruct((pairs, 2 * n1, n2, GROUP_W), BF16),
        scratch_shapes=[pltpu.VMEM((GROUP_W // V7X_LANES, nb * pitch, V7X_LANES), F32)],
        compiler_params=_cparams("parallel", "parallel"),
        name="fft_stage1",
    )(zr, zi, table)


def _a_specs(n1, n2, kb, index):
    def spec(part):
        def imap(i, p):
            pp, kk = index(i, p)
            return (pp, part * (n1 // kb) + kk, 0, 0)
        return pl.BlockSpec((1, kb, n2, GROUP_W), imap)
    return [spec(0), spec(1)]


def _fft_spec_kernel(ar_ref, ai_ref, g_ref, s_ref, h_ref, *, kb, n2):
    for k in range(kb):
        rhs = jnp.concatenate([ar_ref[0, k], ai_ref[0, k]], axis=0)
        x = _dot(g_ref[...], rhs) * s_ref[0]
        h_ref[0, k, 0] = x[:n2]
        h_ref[0, k, 1] = x[n2:]


def _fft_filter_spectrum(a, g, inv_s, n1, n2, kb=8):
    orders = a.shape[0]
    return pl.pallas_call(
        functools.partial(_fft_spec_kernel, kb=kb, n2=n2),
        grid=(orders, n1 // kb),
        in_specs=_a_specs(n1, n2, kb, lambda o, i: (o, i))
        + [_const_spec(g), pl.BlockSpec((1, 1, GROUP_W), lambda o, i: (o, 0, 0))],
        out_specs=pl.BlockSpec((1, kb, 2, n2, GROUP_W), lambda o, i: (o, i, 0, 0, 0)),
        out_shape=jax.ShapeDtypeStruct((orders, n1, 2, n2, GROUP_W), F32),
        compiler_params=_cparams("parallel", "parallel"),
        name="fft_filter_spectrum",
    )(a, a, g, inv_s)


def _fft_mid_kernel(ar_ref, ai_ref, g_ref, gi_ref, h_ref, b_ref, scr, *, n2, pitch):
    for k in range(G_ROWS_BF16):
        rhs = jnp.concatenate([ar_ref[0, k], ai_ref[0, k]], axis=0)
        x = _dot(g_ref[...], rhs)
        xr, xi = x[:n2], x[n2:]
        hr, hi = h_ref[0, k, 0], h_ref[0, k, 1]
        y = jnp.concatenate([xr * hr - xi * hi, xr * hi + xi * hr], axis=0).astype(BF16)
        _scatter_rows(scr, 0, _dot(gi_ref[...], y), 2 * n2, pitch, k)

    def store(r, s, rows):
        b_ref[0, r, :, _slab(s)] = rows.astype(BF16)

    _regroup_rows(scr, 2 * n2, pitch, store)


def _fft_mid(a, g, ginv, spec, order, n1, n2):
    pairs = a.shape[0]
    kb = G_ROWS_BF16
    pitch = _strided_pitch(2 * n2)
    return pl.pallas_call(
        functools.partial(_fft_mid_kernel, n2=n2, pitch=pitch),
        grid=(n1 // kb, pairs),
        in_specs=_a_specs(n1, n2, kb, lambda i, p: (p, i)) + [_const_spec(g), _const_spec(ginv),
                  pl.BlockSpec((1, kb, 2, n2, GROUP_W), lambda i, p: (order, i, 0, 0, 0))],
        out_specs=pl.BlockSpec((1, 2 * n2, kb, GROUP_W), lambda i, p: (p, 0, i, 0)),
        out_shape=jax.ShapeDtypeStruct((pairs, 2 * n2, n1, GROUP_W), BF16),
        scratch_shapes=[pltpu.VMEM((GROUP_W // V7X_LANES, kb * pitch, V7X_LANES), F32)],
        compiler_params=_cparams("parallel", "parallel"),
        name="fft_mid",
    )(a, a, g, ginv, spec)


def _fft_stage3_kernel(br_ref, bi_ref, t_ref, ur_ref, ui_ref, gr_ref, gi_ref, sk_ref, y_ref, *scr,
                       n1, pitch, time_major):
    h = n1 // 2
    sk = sk_ref[...]
    for j in range(G_ROWS_BF16):
        rhs = jnp.concatenate([br_ref[0, j], bi_ref[0, j]], axis=0)
        o = _dot(t_ref[j], rhs)
        yr = gr_ref[0, j] * (o[:h] + ur_ref[0, j] * sk)
        yi = gi_ref[0, j] * (o[h:] + ui_ref[0, j] * sk)
        if time_major:
            _scatter_rows(scr[0], 0, yr, h, pitch, j)
            _scatter_rows(scr[1], 0, yi, h, pitch, j)
        else:
            y_ref[0, 0, j] = yr
            y_ref[0, 1, j] = yi
    if time_major:
        for part in range(2):
            def store(r, s, rows, part=part):
                y_ref[0, part, r, :, _slab(s)] = rows.astype(y_ref.dtype)
            _regroup_rows(scr[part], h, pitch, store)


def _fft_stage3(b, table, u, gate, skip, n1, n2, time_major):
    pairs = b.shape[0]
    nb = G_ROWS_BF16
    h = n1 // 2
    pitch = _strided_pitch(h)
    bspec = lambda part: pl.BlockSpec((1, nb, n1, GROUP_W), lambda p, j: (p, part * (n2 // nb) + j, 0, 0))
    even = pl.BlockSpec((1, nb, h, GROUP_W), lambda p, j: (2 * p, j, 0, 0))
    odd = pl.BlockSpec((1, nb, h, GROUP_W), lambda p, j: (2 * p + 1, j, 0, 0))
    if time_major:
        out_spec = pl.BlockSpec((1, 2, h, nb, GROUP_W), lambda p, j: (p, 0, 0, j, 0))
        out_shape = jax.ShapeDtypeStruct((pairs, 2, h, n2, GROUP_W), BF16)
        scratch = [pltpu.VMEM((GROUP_W // V7X_LANES, nb * pitch, V7X_LANES), F32)] * 2
    else:
        out_spec = pl.BlockSpec((1, 2, nb, h, GROUP_W), lambda p, j: (p, 0, j, 0, 0))
        out_shape = jax.ShapeDtypeStruct((pairs, 2, n2, h, GROUP_W), F32)
        scratch = []
    y = pl.pallas_call(
        functools.partial(_fft_stage3_kernel, n1=n1, pitch=pitch, time_major=time_major),
        grid=(pairs, n2 // nb),
        in_specs=[bspec(0), bspec(1), pl.BlockSpec((nb, n1, 2 * n1), lambda p, j: (j, 0, 0)),
                  even, odd, even, odd, pl.BlockSpec((1, GROUP_W), lambda p, j: (0, 0))],
        out_specs=out_spec,
        out_shape=out_shape,
        scratch_shapes=scratch,
        compiler_params=_cparams("parallel", "parallel"),
        name="fft_stage3_out" if time_major else "fft_stage3",
    )(b, b, table, u, u, gate, gate, skip.reshape(1, GROUP_W).astype(F32))
    return y.reshape(2 * pairs, h * n2, GROUP_W) if time_major else y.reshape(2 * pairs, n2, h, GROUP_W)


def _hyena_spectrum(l, w1, b1, freq, w2, b2, w3):
    n1, n2 = _fft_factors(l)
    _, t_filt, _, g, _ = _fft_tables(n1, n2)
    first, second, sums = _hyena_filter(l, w1, b1, freq, w2, b2, w3, n1, n2)
    a = _fft_stage1(first, second, t_filt, n1, n2, 2, lambda p: p, lambda p: p)
    inv_s = (1.0 / sums).reshape(2, 1, GROUP_W)
    return _fft_filter_spectrum(a, g, inv_s, n1, n2)


def _hyena_long_conv(u, gate, spec, order, skip, n1, n2, time_major):
    t_fwd, _, t_inv, g, ginv = _fft_tables(n1, n2)
    a = _fft_stage1(u, u, t_fwd, n1, n2, u.shape[0] // 2, lambda p: 2 * p, lambda p: 2 * p + 1)
    b = _fft_mid(a, g, ginv, spec, order, n1, n2)
    return _fft_stage3(b, t_inv, u, gate, skip, n1, n2, time_major)


def _mixer_hyena(zd, P, l, spec):
    n1, n2 = _fft_factors(zd.shape[1])
    v, x1, x2 = _conv3_split(zd, P['d_conv_w'][l].astype(F32), P['d_conv_b'][l].astype(F32), n1, n2)
    z = _hyena_long_conv(v, x1, spec, 0, P['d_skip'][l][0], n1, n2, False)
    return _hyena_long_conv(z, x2, spec, 1, P['d_skip'][l][1], n1, n2, True)


def _gelu_tanh(x):
    return 0.5 * x * (1.0 + jnp.tanh(math.sqrt(2.0 / math.pi) * (x + 0.044715 * (x * x * x))))


def _head_rms(h, bd):
    ss = _dot((h * h).astype(BF16), bd) * (1.0 / HEAD_W)
    return h * lax.rsqrt(ss + EPS)


def _out_proj_kernel(x_ref, af_ref, ab_ref, ag_ref, bf_ref, bb_ref, br_ref, cf_ref, cb_ref, co_ref, yd_ref,
                     w_ref, bg_ref, cg_ref, bd_ref, n2_ref, rwh_ref, rwl_ref, rb_ref,
                     xo_ref, hn_ref, lg_ref):
    bd = bd_ref[...]
    f = lambda ref: ref[...].astype(F32)
    ya = (f(af_ref) + f(ab_ref)) * _gelu_tanh(f(ag_ref))
    r = f(br_ref)
    yb = _head_rms(f(bf_ref) + f(bb_ref), bd) * bg_ref[...] * (r * _sigmoid(r))
    yc = _head_rms(f(cf_ref) + f(cb_ref), bd) * cg_ref[...] * _sigmoid(f(co_ref))
    mixed = jnp.concatenate([ya.astype(BF16), yb.astype(BF16), yc.astype(BF16), yd_ref[...]], axis=-1)
    x = x_ref[...] + _dot(mixed, w_ref[...])
    xo_ref[...] = x
    hn = x * lax.rsqrt(jnp.mean(x * x, axis=-1, keepdims=True) + EPS) * n2_ref[...]
    hn_ref[...] = hn.astype(BF16)
    hi = hn.astype(BF16)
    lo = (hn - hi.astype(F32)).astype(BF16)
    lg_ref[...] = _dot(hi, rwh_ref[...]) + _dot(lo, rwh_ref[...]) + _dot(hi, rwl_ref[...]) + rb_ref[...]


def _out_proj(x2d, af, ab, za, bf, bb, zb, cf, cb, zc, yd, w_out, b_norm_g, c_norm_g, norm2_g, rw, rb, tm=512):
    t = x2d.shape[0]
    bd = _chunk_masks(False)['bd']
    rwh = rw.astype(BF16)
    rwl = (rw - rwh.astype(F32)).astype(BF16)
    row = lambda w, j=0: pl.BlockSpec((tm, w), lambda i: (i, j))
    consts = (w_out, b_norm_g.reshape(1, -1).astype(F32), c_norm_g.reshape(1, -1).astype(F32), bd,
              norm2_g.reshape(1, -1).astype(F32), rwh, rwl, rb)
    return pl.pallas_call(
        _out_proj_kernel,
        grid=(t // tm,),
        in_specs=[row(D_MODEL), row(GROUP_W), row(GROUP_W), row(GROUP_W, 1), row(GROUP_W), row(GROUP_W),
                  row(GROUP_W, 2), row(GROUP_W), row(GROUP_W), row(GROUP_W, 3), row(GROUP_W)]
        + [_const_spec(c) for c in consts],
        out_specs=(row(D_MODEL), row(D_MODEL), row(V7X_LANES)),
        out_shape=(jax.ShapeDtypeStruct((t, D_MODEL), F32), jax.ShapeDtypeStruct((t, D_MODEL), BF16),
                   jax.ShapeDtypeStruct((t, V7X_LANES), F32)),
        compiler_params=_cparams("parallel"),
        name="out_proj",
    )(x2d, af, ab, za, bf, bb, zb, cf, cb, zc, yd, *consts)


def _route(logits):
    lane = _iota(logits.shape, 1).astype(F32)
    is_g = lane < MOE_GROUPS
    gl = jnp.where(is_g, logits, NEG_BIG)
    gmax = jnp.max(gl, axis=1, keepdims=True)
    gidx = jnp.min(jnp.where(gl == gmax, lane, 1e9), axis=1, keepdims=True)
    gprob = 1.0 / jnp.sum(jnp.where(is_g, jnp.exp(gl - gmax), 0.0), axis=1, keepdims=True)
    lo = MOE_GROUPS + MOE_PER_GROUP * gidx
    el = jnp.where((lane >= lo) & (lane < lo + MOE_PER_GROUP), logits, NEG_BIG)
    v1 = jnp.max(el, axis=1, keepdims=True)
    i1 = jnp.min(jnp.where(el == v1, lane, 1e9), axis=1, keepdims=True)
    el2 = jnp.where(lane == i1, NEG_BIG, el)
    v2 = jnp.max(el2, axis=1, keepdims=True)
    i2 = jnp.min(jnp.where(el2 == v2, lane, 1e9), axis=1, keepdims=True)
    e21 = jnp.exp(v2 - v1)
    p1 = 1.0 / (1.0 + e21)
    return jnp.where(lane == i1, p1 * gprob, 0.0) + jnp.where(lane == i2, e21 * p1 * gprob, 0.0), gidx


MOE_TB = 1024
MOE_RT = 64
MOE_MM = 4 * MOE_RT
MOE_SB = MOE_TB + MOE_GROUPS * MOE_RT


def _moe_kernel(hn_ref, lg_ref, tril_ref, wg_ref, wu_ref, wd_ref, o_ref, xs_ref, ys_ref, gs_ref, pos_ref, meta_ref):
    grp = pl.program_id(1)
    tb, sb, rt = MOE_TB, MOE_SB, MOE_RT

    @pl.when(grp == 0)
    def _():
        gates, gidx = _route(lg_ref[...])
        lane = _iota((tb, V7X_LANES), 1).astype(F32)
        ohg = jnp.where(lane == gidx, 1.0, 0.0)
        rank = jnp.sum(_dot(tril_ref[...], ohg.astype(BF16)) * ohg, axis=1, keepdims=True)
        cnt = jnp.sum(ohg, axis=0, keepdims=True)
        pc = jnp.ceil(cnt * (1.0 / rt)) * rt
        lane1 = _iota((1, V7X_LANES), 1)
        off = jnp.zeros((1, V7X_LANES), F32)
        run = jnp.zeros((1, 1), F32)
        for g in range(MOE_GROUPS):
            off = jnp.where(lane1 == g, run, off)
            run = run + jnp.sum(jnp.where(lane1 == g, pc, 0.0), axis=1, keepdims=True)
        pos = jnp.sum(ohg * off, axis=1, keepdims=True) + rank
        pos_ref[...] = jnp.broadcast_to(pos, (tb, V7X_LANES))
        meta_ref[0:1, :] = off
        meta_ref[1:2, :] = pc
        hi = jnp.floor(pos * (1.0 / 64.0))
        lo = pos - 64.0 * hi
        pm = jnp.where(lane == 0.0, hi, jnp.where(lane == 1.0, lo, 0.0)).astype(BF16)
        sel = jnp.where(_iota((8, V7X_LANES), 0) == _iota((8, V7X_LANES), 1), 1.0, 0.0).astype(BF16)
        rows8 = _dot_nt(sel, pm)
        pos_row = 64.0 * rows8[0:1, :] + rows8[1:2, :]
        p = jnp.where(_iota((sb, tb), 0).astype(F32) == pos_row, 1.0, 0.0).astype(BF16)
        xs_ref[...] = _dot(p, hn_ref[...]).astype(BF16)
        gh = gates.astype(BF16)
        gl = (gates - gh.astype(F32)).astype(BF16)
        gsorted = _dot(p, jnp.concatenate([gh, gl], axis=1))
        gs_ref[...] = gsorted[:, :V7X_LANES] + gsorted[:, V7X_LANES:]
        ys_ref[...] = jnp.zeros_like(ys_ref)

    lane1 = _iota((1, V7X_LANES), 1)
    start = jnp.sum(jnp.where(lane1 == grp, meta_ref[0:1, :], 0.0)).astype(jnp.int32)
    rows = jnp.sum(jnp.where(lane1 == grp, meta_ref[1:2, :], 0.0)).astype(jnp.int32)

    def experts(r0, nrows):
        xt = xs_ref[pl.ds(r0, nrows), :]
        gsl = gs_ref[pl.ds(r0, nrows), :]
        lane_t = _iota((nrows, V7X_LANES), 1)
        acc = None
        for j in range(MOE_PER_GROUP):
            g = _dot(xt, wg_ref[j])
            a = (g * _sigmoid(g)) * _dot(xt, wu_ref[j])
            y = _dot(a.astype(BF16), wd_ref[j])
            w = jnp.sum(jnp.where(lane_t == MOE_GROUPS + MOE_PER_GROUP * grp + j, gsl, 0.0), axis=1, keepdims=True)
            acc = w * y if acc is None else acc + w * y
        ys_ref[pl.ds(r0, nrows), :] = acc.astype(BF16)

    ntiles = rows // rt
    per_mm = MOE_MM // rt

    def full(i, carry):
        experts(pl.multiple_of(start + i * MOE_MM, rt), MOE_MM)
        return carry

    lax.fori_loop(0, ntiles // per_mm, full, 0)
    done = (ntiles // per_mm) * per_mm
    size = per_mm // 2
    while size >= 1:
        take = (ntiles % (2 * size)) >= size

        @pl.when(take)
        def _(done=done, size=size):
            experts(pl.multiple_of(start + done * rt, rt), size * rt)

        done = done + jnp.where(take, size, 0)
        size //= 2

    @pl.when(grp == MOE_GROUPS - 1)
    def _():
        pt = jnp.where(_iota((tb, sb), 1).astype(F32) == pos_ref[:, 0:1], 1.0, 0.0).astype(BF16)
        o_ref[...] = _dot(pt, ys_ref[...]).astype(o_ref.dtype)


def _moe(hn, logits, w_gate, w_up, w_down):
    t = hn.shape[0]
    tm = MOE_TB
    tril = jnp.asarray(np.tril(np.ones((tm, tm), np.float32), -1), BF16)
    row = lambda w: pl.BlockSpec((tm, w), lambda i, g: (i, 0))
    wspec = lambda a, b: pl.BlockSpec((MOE_PER_GROUP, a, b), lambda i, g: (g, 0, 0))
    return pl.pallas_call(
        _moe_kernel,
        grid=(t // tm, MOE_GROUPS),
        in_specs=[row(D_MODEL), row(V7X_LANES), _const_spec(tril), wspec(D_MODEL, D_EXPERT),
                  wspec(D_MODEL, D_EXPERT), wspec(D_EXPERT, D_MODEL)],
        out_specs=row(D_MODEL),
        out_shape=jax.ShapeDtypeStruct((t, D_MODEL), BF16),
        scratch_shapes=[pltpu.VMEM((MOE_SB, D_MODEL), BF16), pltpu.VMEM((MOE_SB, D_MODEL), BF16),
                        pltpu.VMEM((MOE_SB, V7X_LANES), F32), pltpu.VMEM((tm, V7X_LANES), F32),
                        pltpu.VMEM((8, V7X_LANES), F32)],
        compiler_params=_cparams("parallel", "arbitrary"),
        name="moe",
    )(hn, logits, tril, w_gate, w_up, w_down)


def _layer(x2d, moe_prev, bsz, seq, l, P, W, spec):
    x2d, za, zb, zc, zd, zs = _in_proj(x2d, moe_prev, P['norm1_g'][l].astype(F32), W['w_in'][l])
    r3 = lambda a: a.reshape(bsz, seq, a.shape[-1])
    r2 = lambda a: a.reshape(bsz * seq, a.shape[-1])
    mp = W['mix'][l]
    cw, cb = P['a_conv_w'][l].astype(F32), P['a_conv_b'][l].reshape(1, -1).astype(F32)
    za3, zb3, zc3, zs3 = r3(za), r3(zb), r3(zc), r3(zs)
    af = _rglru(za3, cw, cb, *mp['a', 0], False)
    ab = _rglru(za3, cw, cb, *mp['a', 1], True)
    bf, bb = _gla(zb3, zs3, mp['b', 0], mp['b', 1])
    cf, cbk = _mlstm(zc3, zs3, mp['c', 0], mp['c', 1])
    yd = _mixer_hyena(r3(zd), P, l, spec)
    x_new, hn, logits = _out_proj(x2d, r2(af), r2(ab), za, r2(bf), r2(bb), zb, r2(cf), r2(cbk), zc, r2(yd),
                                  W['w_out'][l], P['b_norm_g'][l], P['c_norm_g'][l], P['norm2_g'][l],
                                  W['router_w'][l], W['router_b'][l])
    return x_new, _moe(hn, logits, W['w_gate'][l], W['w_up'][l], W['w_down'][l])


def _trunk(x, P, W):
    bsz, seq, _ = x.shape
    depth = P['w_in'].shape[0]
    x2d = x.reshape(bsz * seq, D_MODEL)
    moe_term = None
    for l in range(depth):
        spec = _hyena_spectrum(seq, P['d_ffn_w1'][l], P['d_ffn_b1'][l], P['d_sin_freq'][l], P['d_ffn_w2'][l],
                               P['d_ffn_b2'][l], P['d_ffn_w3'][l])
        x2d, moe_term = _layer(x2d, moe_term, bsz, seq, l, P, W, spec)
    return _final_norm(x2d, moe_term, P['final_norm_g']).reshape(bsz, seq, D_MODEL)


def _prep_weights(P):
    depth = P['w_in'].shape[0]
    W = {'w_in': [_permute_w_in(P['w_in'][l]) for l in range(depth)],
         'w_out': [P['w_out'][l].astype(BF16) for l in range(depth)],
         'mix': [_mixer_prep(P, l) for l in range(depth)],
         'w_gate': [P['moe_w_gate'][l].astype(BF16) for l in range(depth)],
         'w_up': [P['moe_w_up'][l].astype(BF16) for l in range(depth)],
         'w_down': [P['moe_w_down'][l].astype(BF16) for l in range(depth)],
         'router_w': [], 'router_b': []}
    for l in range(depth):
        rw = jnp.concatenate([P['moe_group_w'][l], P['moe_expert_w'][l]], axis=1).astype(F32)
        rb = jnp.concatenate([P['moe_group_b'][l], P['moe_expert_b'][l]]).astype(F32)
        padc = V7X_LANES - rw.shape[1]
        W['router_w'].append(jnp.pad(rw, ((0, 0), (0, padc))))
        W['router_b'].append(jnp.pad(rb, (0, padc)).reshape(1, V7X_LANES))
    return W


def kernel(x_prompt, x_sample, norm1_g, w_in, a_conv_w, a_conv_b, a_gate_w, a_gate_b, a_lambda, b_gate_w, b_gate_b, b_norm_g, c_gate_b, c_norm_g, d_conv_w, d_conv_b, d_ffn_w1, d_ffn_b1, d_sin_freq, d_ffn_w2, d_ffn_b2, d_ffn_w3, d_skip, w_out, norm2_g, moe_group_w, moe_group_b, moe_expert_w, moe_expert_b, moe_w_gate, moe_w_up, moe_w_down, final_norm_g):
    P = {'norm1_g': norm1_g, 'w_in': w_in, 'a_conv_w': a_conv_w, 'a_conv_b': a_conv_b,
         'a_gate_w': a_gate_w, 'a_gate_b': a_gate_b, 'a_lambda': a_lambda, 'b_gate_w': b_gate_w,
         'b_gate_b': b_gate_b, 'b_norm_g': b_norm_g, 'c_gate_b': c_gate_b, 'c_norm_g': c_norm_g,
         'd_conv_w': d_conv_w, 'd_conv_b': d_conv_b, 'd_ffn_w1': d_ffn_w1, 'd_ffn_b1': d_ffn_b1,
         'd_sin_freq': d_sin_freq, 'd_ffn_w2': d_ffn_w2, 'd_ffn_b2': d_ffn_b2, 'd_ffn_w3': d_ffn_w3,
         'd_skip': d_skip, 'w_out': w_out, 'norm2_g': norm2_g, 'moe_group_w': moe_group_w,
         'moe_group_b': moe_group_b, 'moe_expert_w': moe_expert_w, 'moe_expert_b': moe_expert_b,
         'moe_w_gate': moe_w_gate, 'moe_w_up': moe_w_up, 'moe_w_down': moe_w_down,
         'final_norm_g': final_norm_g}
    W = _prep_weights(P)
    return (_trunk(x_prompt, P, W), _trunk(x_sample, P, W))
```

```python
import functools
import math

import jax
import jax.numpy as jnp
import numpy as np
from jax import lax
from jax.experimental import pallas as pl
from jax.experimental.pallas import tpu as pltpu

F32 = jnp.float32
BF16 = jnp.bfloat16

D_MODEL = 1024
GROUP_W = 256
N_HEADS = 4
HEAD_W = GROUP_W // N_HEADS
B_DK = 32
B_RANK = 16
CHUNK = 64
RG_C = 8.0
RG_SEGMENTS = 16
RG_UNROLL = 4
GLA_TAU = 16.0
HY_EMB = 33
HY_HID = 64
HY_TARGET = 1e-2
HY_FAST = 0.3
HY_SLOW = 1.5
MOE_GROUPS = 4
MOE_PER_GROUP = 4
MOE_EXPERTS = 16
D_EXPERT = 512
EPS = 1e-6
NEG_BIG = -1e30

V7X_LANES = 128
V7X_SUBLANES = 8
VMEM_LIMIT = 56 * 1024 * 1024

ZA_W, ZB_W, ZC_W, ZD_W, ZS_W = 512, 768, 1024, 768, 128
Z_SPLITS = (ZA_W, ZB_W, ZC_W, ZD_W, ZS_W)
Z_TOTAL = sum(Z_SPLITS)


def _cparams(*sem):
    return pltpu.CompilerParams(dimension_semantics=sem, vmem_limit_bytes=VMEM_LIMIT)


def _dot(a, b):
    return jnp.dot(a, b, preferred_element_type=F32)


def _dot_nt(a, b):
    return lax.dot_general(a, b, (((1,), (1,)), ((), ())), preferred_element_type=F32)


def _dot_tn(a, b):
    return lax.dot_general(a, b, (((0,), (0,)), ((), ())), preferred_element_type=F32)


def _dot_split(m_bf16, x):
    hi = x.astype(BF16)
    lo = (x - hi.astype(F32)).astype(BF16)
    return _dot(m_bf16, hi) + _dot(m_bf16, lo)


def _split_dot(x, m_bf16):
    hi = x.astype(BF16)
    lo = (x - hi.astype(F32)).astype(BF16)
    return _dot(hi, m_bf16) + _dot(lo, m_bf16)


def _sigmoid(x):
    return 1.0 / (1.0 + jnp.exp(-x))


def _log_sigmoid(x):
    return jnp.minimum(x, 0.0) - jnp.log(1.0 + jnp.exp(-jnp.abs(x)))


def _iota(shape, dim):
    return lax.broadcasted_iota(jnp.int32, shape, dim)


def _in_proj_kernel(*refs, add):
    if add:
        x_ref, m_ref, g_ref, w_ref, xo_ref, za_ref, zb_ref, zc_ref, zd_ref, zs_ref = refs
        x = x_ref[...] + m_ref[...].astype(F32)
        xo_ref[...] = x
    else:
        x_ref, g_ref, w_ref, za_ref, zb_ref, zc_ref, zd_ref, zs_ref = refs
        x = x_ref[...]
    ms = jnp.mean(x * x, axis=-1, keepdims=True)
    xn = (x * lax.rsqrt(ms + EPS) * g_ref[...]).astype(BF16)
    off = 0
    for ref, w in zip((za_ref, zb_ref, zc_ref, zd_ref, zs_ref), Z_SPLITS):
        ref[...] = _dot(xn, w_ref[:, off:off + w]).astype(ref.dtype)
        off += w


def _in_proj(x2d, moe_prev, g, w_perm, tm=512):
    t = x2d.shape[0]
    add = moe_prev is not None
    row = lambda w: pl.BlockSpec((tm, w), lambda i: (i, 0))
    outs = tuple(jax.ShapeDtypeStruct((t, w), F32 if w == ZS_W else BF16) for w in Z_SPLITS)
    out_specs = tuple(row(w) for w in Z_SPLITS)
    consts = (g.reshape(1, D_MODEL), w_perm)
    res = pl.pallas_call(
        functools.partial(_in_proj_kernel, add=add),
        grid=(t // tm,),
        in_specs=[row(D_MODEL)] * (2 if add else 1) + [_const_spec(c) for c in consts],
        out_specs=((row(D_MODEL),) if add else ()) + out_specs,
        out_shape=((jax.ShapeDtypeStruct((t, D_MODEL), F32),) if add else ()) + outs,
        compiler_params=_cparams("parallel"),
        name="in_proj",
    )(*((x2d, moe_prev) if add else (x2d,)), *consts)
    return tuple(res) if add else (x2d,) + tuple(res)


def _final_norm_kernel(x_ref, m_ref, g_ref, o_ref):
    x = x_ref[...] + m_ref[...].astype(F32)
    o_ref[...] = x * lax.rsqrt(jnp.mean(x * x, axis=-1, keepdims=True) + EPS) * g_ref[...]


def _final_norm(x2d, moe_prev, g, tm=512):
    t = x2d.shape[0]
    row = pl.BlockSpec((tm, D_MODEL), lambda i: (i, 0))
    return pl.pallas_call(
        _final_norm_kernel,
        grid=(t // tm,),
        in_specs=[row, row, pl.BlockSpec((1, D_MODEL), lambda i: (0, 0))],
        out_specs=row,
        out_shape=jax.ShapeDtypeStruct((t, D_MODEL), F32),
        compiler_params=_cparams("parallel"),
        name="final_norm",
    )(x2d, moe_prev, g.reshape(1, D_MODEL).astype(F32))


def _permute_w_in(w_in):
    sizes = (256, 256, 128, 128, 256, 256, 16, 16, 256, 256, 256, 256, 16, 768)
    offs = np.concatenate([[0], np.cumsum(sizes)])
    seg = [w_in[:, offs[i]:offs[i + 1]] for i in range(len(sizes))]
    (a_x, a_g, b_q, b_k, b_v, b_r, b_lf, b_lb, c_q, c_k, c_v, c_o, c_g, d_u) = seg
    small = jnp.concatenate([b_lf, b_lb, c_g, jnp.zeros((w_in.shape[0], ZS_W - 48), w_in.dtype)], axis=1)
    return jnp.concatenate([a_x, a_g, b_q, b_k, b_v, b_r, c_q, c_k, c_v, c_o, d_u, small], axis=1).astype(BF16)


HALO = 16


def _strided_pitch(n):
    assert n % V7X_SUBLANES == 0
    return n if (n // V7X_SUBLANES) % 2 == 1 else n + V7X_SUBLANES


def _halo_conv(prev, cur, nxt, w, b, lo, first, last):
    tl = cur.shape[0]
    prev = jnp.where(first, 0.0, prev.astype(F32))
    nxt = jnp.where(last, 0.0, nxt.astype(F32))
    ext = jnp.concatenate([prev, cur.astype(F32), nxt], axis=0)
    n = tl + 2 * HALO
    acc = None
    for j in range(w.shape[0]):
        o = lo + j
        sh = ext if o == 0 else pltpu.roll(ext, (-o) % n, 0)
        term = sh[HALO:HALO + tl] * w[j:j + 1, :]
        acc = term if acc is None else acc + term
    return acc + b


def _rglru_kernel(prev_ref, cur_ref, next_ref, cw_ref, cb_ref, gw_ref, gb_ref, sp_ref, out_ref,
                  a_s, b_s, h_s, p_s, carry, *, tl, reverse):
    i = pl.program_id(1)
    nt = pl.num_programs(1)
    ti = (nt - 1 - i) if reverse else i
    u = _halo_conv(prev_ref[0], cur_ref[0], next_ref[0], cw_ref[...], cb_ref[...], -2, ti == 0, ti == nt - 1)
    gates = _sigmoid(_dot(u.astype(BF16), gw_ref[...]) + gb_ref[...])
    r = gates[:, :GROUP_W]
    ig = gates[:, GROUP_W:]
    log_a = (-RG_C) * r * sp_ref[...]
    a = jnp.exp(log_a)
    bt = jnp.sqrt(1.0 - a * a) * (ig * u)
    nseg = RG_SEGMENTS
    m = tl // nseg
    mp, sp = _strided_pitch(m), _strided_pitch(nseg)
    for k in range(2):
        for seg in range(nseg):
            a_s[k, seg * mp:seg * mp + m, :] = a[seg * m:(seg + 1) * m, k * V7X_LANES:(k + 1) * V7X_LANES]
            b_s[k, seg * mp:seg * mp + m, :] = bt[seg * m:(seg + 1) * m, k * V7X_LANES:(k + 1) * V7X_LANES]

    @pl.when(i == 0)
    def _():
        carry[...] = jnp.zeros_like(carry)

    def body(jj, hp):
        j = (m - 1 - jj) if reverse else jj
        out = []
        for k in range(2):
            h, p = hp[2 * k], hp[2 * k + 1]
            av = a_s[k, pl.ds(j, nseg, stride=mp), :]
            bv = b_s[k, pl.ds(j, nseg, stride=mp), :]
            h = av * h + bv
            p = av * p
            rows = pl.ds(pl.multiple_of(j * sp, V7X_SUBLANES), nseg)
            h_s[k, rows, :] = h
            p_s[k, rows, :] = p
            out += [h, p]
        return tuple(out)

    z8 = jnp.zeros((nseg, V7X_LANES), F32)
    o8 = jnp.ones((nseg, V7X_LANES), F32)
    ends = lax.fori_loop(0, m, body, (z8, o8, z8, o8), unroll=RG_UNROLL)
    for k in range(2):
        h_end, p_end = ends[2 * k], ends[2 * k + 1]
        c = carry[k]
        cin = [None] * nseg
        for seg in (range(nseg - 1, -1, -1) if reverse else range(nseg)):
            cin[seg] = c
            c = h_end[seg:seg + 1, :] + p_end[seg:seg + 1, :] * c
        carry[k] = c
        for seg in range(nseg):
            hv = h_s[k, pl.ds(seg, m, stride=sp), :]
            pv = p_s[k, pl.ds(seg, m, stride=sp), :]
            out_ref[0, seg * m:(seg + 1) * m, k * V7X_LANES:(k + 1) * V7X_LANES] = (hv + pv * cin[seg]).astype(out_ref.dtype)


def _rglru(za, conv_w, conv_b, gate_w_bd, gate_b, softplus_neg_lam, reverse, tl=512):
    b, l, _ = za.shape
    nt = l // tl
    rh = tl // HALO
    nbh = l // HALO

    def tmap(bi, i):
        return (nt - 1 - i) if reverse else i

    kern = functools.partial(_rglru_kernel, tl=tl, reverse=reverse)
    return pl.pallas_call(
        kern,
        grid=(b, nt),
        in_specs=[
            pl.BlockSpec((1, HALO, GROUP_W), lambda bi, i: (bi, jnp.maximum(tmap(bi, i) * rh - 1, 0), 0)),
            pl.BlockSpec((1, tl, GROUP_W), lambda bi, i: (bi, tmap(bi, i), 0)),
            pl.BlockSpec((1, HALO, GROUP_W), lambda bi, i: (bi, jnp.minimum((tmap(bi, i) + 1) * rh, nbh - 1), 0)),
            pl.BlockSpec((4, GROUP_W), lambda bi, i: (0, 0)),
            pl.BlockSpec((1, GROUP_W), lambda bi, i: (0, 0)),
            pl.BlockSpec((GROUP_W, 2 * GROUP_W), lambda bi, i: (0, 0)),
            pl.BlockSpec((1, 2 * GROUP_W), lambda bi, i: (0, 0)),
            pl.BlockSpec((1, GROUP_W), lambda bi, i: (0, 0)),
        ],
        out_specs=pl.BlockSpec((1, tl, GROUP_W), lambda bi, i: (bi, tmap(bi, i), 0)),
        out_shape=jax.ShapeDtypeStruct((b, l, GROUP_W), BF16),
        scratch_shapes=[pltpu.VMEM((2, RG_SEGMENTS * _strided_pitch(tl // RG_SEGMENTS), V7X_LANES), F32)] * 2
        + [pltpu.VMEM((2, (tl // RG_SEGMENTS) * _strided_pitch(RG_SEGMENTS), V7X_LANES), F32)] * 2
        + [pltpu.VMEM((2, 1, V7X_LANES), F32)],
        compiler_params=_cparams("parallel", "arbitrary"),
        name="rglru_bwd" if reverse else "rglru_fwd",
    )(za, za, za, conv_w, conv_b, gate_w_bd, gate_b, softplus_neg_lam)


def _block_diag(blocks):
    h, di, do = blocks.shape
    eye = jnp.eye(h, dtype=blocks.dtype)
    return jnp.einsum('hde,hg->hdge', blocks, eye).reshape(h * di, h * do)


def _chunk_masks(reverse):
    r = np.arange(GROUP_W)
    t = np.arange(CHUNK)
    tri = (t[None, :] >= t[:, None]) if reverse else (t[None, :] <= t[:, None])
    cmask = np.tile(tri, (1, N_HEADS))
    bd = (r[:, None] // HEAD_W) == (r[None, :] // HEAD_W)
    kd = (r[:, None] // HEAD_W) == (np.arange(N_HEADS * B_DK)[None, :] // B_DK)
    i2 = np.tile(np.eye(CHUNK, dtype=bool), (1, N_HEADS))
    hm = np.zeros((8, GROUP_W), bool)
    hm[:N_HEADS] = np.arange(N_HEADS)[:, None] == (r[None, :] // HEAD_W)
    return dict(tri=jnp.asarray(tri, BF16), cmask=jnp.asarray(cmask, F32), bd=jnp.asarray(bd, BF16),
                kd=jnp.asarray(kd, BF16), kdf=jnp.asarray(kd, F32),
                i2=jnp.asarray(i2, F32), hm=jnp.asarray(hm, F32))


def _const_spec(arr):
    nd = arr.ndim
    return pl.BlockSpec(arr.shape, lambda *_: (0,) * nd)


def _gla_chunk(zb_ref, la, tri_ref, kd_ref, kdf_ref, bd_ref, causal, out_ref, st_ref, c):
    rows = slice(c * CHUNK, (c + 1) * CHUNK)
    q = zb_ref[0, rows, 0:128].astype(F32) * (B_DK ** -0.5)
    k = zb_ref[0, rows, 128:256].astype(F32)
    vb = zb_ref[0, rows, 256:512]
    la_c = la[rows]
    bcum = _dot_split(tri_ref[...], la_c)
    btot = jnp.sum(la_c, axis=0, keepdims=True)
    q_in = (q * jnp.exp(bcum)).astype(BF16)
    k_in = (k * jnp.exp(-bcum)).astype(BF16)
    k_st = (k * jnp.exp(btot - bcum)).astype(BF16)
    kexp = jnp.concatenate([k_in] * N_HEADS, axis=0) * kd_ref[...]
    att = jnp.where(causal, _dot_nt(q_in, kexp), 0.0)
    vbd = jnp.concatenate([vb] * N_HEADS, axis=0) * bd_ref[...]
    st = st_ref[...]
    out_ref[0, rows, :] = (_dot(att.astype(BF16), vbd) + _dot_nt(q_in, st.astype(BF16))).astype(out_ref.dtype)
    st_ref[...] = st * jnp.exp(btot) + _dot_tn(vb, k_st) * kdf_ref[...]


def _gla_kernel(zbf_ref, zsf_ref, zbb_ref, zsb_ref, gwf_ref, gbf_ref, gwb_ref, gbb_ref, trif_ref, trib_ref,
                kd_ref, kdf_ref, bd_ref, cmf_ref, cmb_ref, outf_ref, outb_ref, stf_ref, stb_ref, *, tl):
    @pl.when(pl.program_id(1) == 0)
    def _():
        stf_ref[...] = jnp.zeros_like(stf_ref)
        stb_ref[...] = jnp.zeros_like(stb_ref)

    log_alpha = lambda zs_ref, gw_ref, gb_ref: _log_sigmoid(
        _dot(zs_ref[0].astype(BF16), gw_ref[...]) + gb_ref[...]) * (1.0 / GLA_TAU)
    laf = log_alpha(zsf_ref, gwf_ref, gbf_ref)
    lab = log_alpha(zsb_ref, gwb_ref, gbb_ref)
    causal_f = cmf_ref[...] > 0.0
    causal_b = cmb_ref[...] > 0.0
    nch = tl // CHUNK
    for c in range(nch):
        _gla_chunk(zbf_ref, laf, trif_ref, kd_ref, kdf_ref, bd_ref, causal_f, outf_ref, stf_ref, c)
        _gla_chunk(zbb_ref, lab, trib_ref, kd_ref, kdf_ref, bd_ref, causal_b, outb_ref, stb_ref, nch - 1 - c)


def _gla(zb, zs, prep_f, prep_b, tl=1024):
    b, l, _ = zb.shape
    nt = l // tl
    mf, mb = _chunk_masks(False), _chunk_masks(True)
    consts = (*prep_f, *prep_b, mf['tri'], mb['tri'], mf['kd'], mf['kdf'], mf['bd'], mf['cmask'], mb['cmask'])
    fmap = lambda bi, i: (bi, i, 0)
    bmap = lambda bi, i: (bi, nt - 1 - i, 0)
    o = jax.ShapeDtypeStruct((b, l, GROUP_W), BF16)
    st = pltpu.VMEM((GROUP_W, N_HEADS * B_DK), F32)
    return pl.pallas_call(
        functools.partial(_gla_kernel, tl=tl),
        grid=(b, nt),
        in_specs=[pl.BlockSpec((1, tl, ZB_W), fmap), pl.BlockSpec((1, tl, ZS_W), fmap),
                  pl.BlockSpec((1, tl, ZB_W), bmap), pl.BlockSpec((1, tl, ZS_W), bmap)]
        + [_const_spec(c) for c in consts],
        out_specs=(pl.BlockSpec((1, tl, GROUP_W), fmap), pl.BlockSpec((1, tl, GROUP_W), bmap)),
        out_shape=(o, o),
        scratch_shapes=[st, st],
        compiler_params=_cparams("parallel", "arbitrary"),
        name="gla",
    )(zb, zs, zb, zs, *consts)


def _mlstm_chunk(zc_ref, ig_all, lf_all, tri_ref, i2_ref, bd_ref, causal, hm_ref, out_ref, s_ref, n_ref, m_ref, c):
    rows = slice(c * CHUNK, (c + 1) * CHUNK)
    qb = zc_ref[0, rows, 0:256]
    k = zc_ref[0, rows, 256:512].astype(F32) * (HEAD_W ** -0.5)
    kb = k.astype(BF16)
    vb = zc_ref[0, rows, 512:768]
    ig = ig_all[rows]
    lf = lf_all[rows]
    bcum = _dot_split(tri_ref[...], lf)
    blast = jnp.sum(lf, axis=0, keepdims=True)
    cc = jnp.sum((ig - bcum) * i2_ref[...], axis=0, keepdims=True)
    d2 = jnp.where(causal, bcum + cc, NEG_BIG)
    m_intra = jnp.full((CHUNK, GROUP_W), NEG_BIG, F32)
    for h in range(N_HEADS):
        hsel = hm_ref[h:h + 1, :] > 0.0
        mh = jnp.max(jnp.where(hsel, d2, NEG_BIG), axis=1, keepdims=True)
        m_intra = jnp.where(hsel, mh, m_intra)
    m_prev = m_ref[...]
    inter_log = bcum + m_prev
    m_t = jnp.maximum(inter_log, m_intra)
    w_intra = jnp.exp(d2 - m_t)
    w_inter = jnp.exp(inter_log - m_t)
    kexp = jnp.concatenate([kb] * N_HEADS, axis=0) * bd_ref[...]
    qk = (_dot_nt(qb, kexp) * w_intra).astype(BF16)
    vbd = jnp.concatenate([vb] * N_HEADS, axis=0) * bd_ref[...]
    s_prev = s_ref[...]
    n_prev = n_ref[...]
    sbd = jnp.concatenate([s_prev.astype(BF16)] * N_HEADS, axis=0) * bd_ref[...]
    nbd = jnp.concatenate([n_prev.astype(BF16)] * N_HEADS, axis=0) * bd_ref[...]
    num = _dot(qk, vbd) + w_inter * _dot(qb, sbd)
    den = _dot(qk, bd_ref[...]) + w_inter * _dot(qb, nbd)
    out_ref[0, rows, :] = (num / jnp.maximum(jnp.abs(den), jnp.exp(-m_t))).astype(out_ref.dtype)
    g_end = blast - bcum + ig
    g_max = jnp.max(g_end, axis=0, keepdims=True)
    kw = (k * jnp.exp(g_end - g_max)).astype(BF16)
    m_new = jnp.maximum(blast + m_prev, g_max)
    a = jnp.exp(blast + m_prev - m_new)
    cf = jnp.exp(g_max - m_new)
    uc_full = _dot_tn(kw, vb)
    un_full = _dot_tn(kw, jnp.ones((CHUNK, GROUP_W), BF16))
    uc = jnp.zeros((HEAD_W, GROUP_W), F32)
    un = jnp.zeros((HEAD_W, GROUP_W), F32)
    for h in range(N_HEADS):
        hsel = hm_ref[h:h + 1, :] > 0.0
        blk = slice(h * HEAD_W, (h + 1) * HEAD_W)
        uc = jnp.where(hsel, uc_full[blk], uc)
        un = jnp.where(hsel, un_full[blk], un)
    s_ref[...] = a * s_prev + cf * uc
    n_ref[...] = a * n_prev + cf * un
    m_ref[...] = m_new


def _mlstm_kernel(zcf_ref, zsf_ref, zcb_ref, zsb_ref, ef_ref, ebf_ref, eb_ref, ebb_ref, trif_ref, trib_ref,
                  i2_ref, bd_ref, cmf_ref, cmb_ref, hm_ref, outf_ref, outb_ref,
                  sf_ref, nf_ref, mf_ref, sb_ref, nb_ref, mb_ref, *, tl):
    @pl.when(pl.program_id(1) == 0)
    def _():
        for s_ref, n_ref, m_ref in ((sf_ref, nf_ref, mf_ref), (sb_ref, nb_ref, mb_ref)):
            s_ref[...] = jnp.zeros_like(s_ref)
            n_ref[...] = jnp.zeros_like(n_ref)
            m_ref[...] = jnp.full_like(m_ref, NEG_BIG)

    def gates(zs_ref, e_ref, eb_ref):
        ge = _split_dot(zs_ref[0], e_ref[...]) + eb_ref[...]
        return ge[:, :GROUP_W], _log_sigmoid(ge[:, GROUP_W:])

    igf, lff = gates(zsf_ref, ef_ref, ebf_ref)
    igb, lfb = gates(zsb_ref, eb_ref, ebb_ref)
    causal_f = cmf_ref[...] > 0.0
    causal_b = cmb_ref[...] > 0.0
    nch = tl // CHUNK
    for c in range(nch):
        _mlstm_chunk(zcf_ref, igf, lff, trif_ref, i2_ref, bd_ref, causal_f, hm_ref, outf_ref,
                     sf_ref, nf_ref, mf_ref, c)
        _mlstm_chunk(zcb_ref, igb, lfb, trib_ref, i2_ref, bd_ref, causal_b, hm_ref, outb_ref,
                     sb_ref, nb_ref, mb_ref, nch - 1 - c)


def _mlstm(zc, zs, prep_f, prep_b, tl=1024):
    b, l, _ = zc.shape
    nt = l // tl
    mf, mb = _chunk_masks(False), _chunk_masks(True)
    consts = (*prep_f, *prep_b, mf['tri'], mb['tri'], mf['i2'], mf['bd'], mf['cmask'], mb['cmask'], mf['hm'])
    fmap = lambda bi, i: (bi, i, 0)
    bmap = lambda bi, i: (bi, nt - 1 - i, 0)
    o = jax.ShapeDtypeStruct((b, l, GROUP_W), BF16)
    state = [pltpu.VMEM((HEAD_W, GROUP_W), F32), pltpu.VMEM((HEAD_W, GROUP_W), F32), pltpu.VMEM((1, GROUP_W), F32)]
    return pl.pallas_call(
        functools.partial(_mlstm_kernel, tl=tl),
        grid=(b, nt),
        in_specs=[pl.BlockSpec((1, tl, ZC_W), fmap), pl.BlockSpec((1, tl, ZS_W), fmap),
                  pl.BlockSpec((1, tl, ZC_W), bmap), pl.BlockSpec((1, tl, ZS_W), bmap)]
        + [_const_spec(c) for c in consts],
        out_specs=(pl.BlockSpec((1, tl, GROUP_W), fmap), pl.BlockSpec((1, tl, GROUP_W), bmap)),
        out_shape=(o, o),
        scratch_shapes=state + state,
        compiler_params=_cparams("parallel", "arbitrary"),
        name="mlstm",
    )(zc, zs, zc, zs, *consts)


def _mixer_prep(P, l):
    out = {}
    gw, gb, lam = P['a_gate_w'][l], P['a_gate_b'][l], P['a_lambda'][l]
    for d in range(2):
        w = jnp.concatenate([_block_diag(gw[d, 0]), _block_diag(gw[d, 1])], axis=1).astype(BF16)
        bb = jnp.concatenate([gb[d, 0].reshape(1, -1), gb[d, 1].reshape(1, -1)], axis=1).astype(F32)
        nl = -lam[d].astype(F32)
        sp = (jnp.maximum(nl, 0.0) + jnp.log(1.0 + jnp.exp(-jnp.abs(nl)))).reshape(1, -1)
        out['a', d] = (w, bb, sp)
        bw = jnp.zeros((ZS_W, N_HEADS * B_DK), F32).at[B_RANK * d:B_RANK * (d + 1)].set(P['b_gate_w'][l][d])
        out['b', d] = (bw.astype(BF16), P['b_gate_b'][l][d].reshape(1, -1).astype(F32))
        e = np.zeros((ZS_W, 2 * GROUP_W), np.float32)
        for h in range(N_HEADS):
            e[32 + 8 * d + h, h * HEAD_W:(h + 1) * HEAD_W] = 1.0
            e[36 + 8 * d + h, GROUP_W + h * HEAD_W:GROUP_W + (h + 1) * HEAD_W] = 1.0
        cb = P['c_gate_b'][l].astype(F32)
        eb = jnp.concatenate([jnp.repeat(cb[2 * d], HEAD_W), jnp.repeat(cb[2 * d + 1], HEAD_W)]).reshape(1, -1)
        out['c', d] = (jnp.asarray(e, BF16), eb)
    return out


G_ROWS = 8
G_ROWS_BF16 = 16


def _slab(k):
    return slice(k * V7X_LANES, (k + 1) * V7X_LANES)


def _scatter_rows(scr, base, val, rows, pitch, blk0=0):
    for s in range(val.shape[1] // V7X_LANES):
        for b in range(val.shape[0] // rows):
            r0 = (blk0 + b) * pitch
            scr[base + s, r0:r0 + rows, :] = val[b * rows:(b + 1) * rows, _slab(s)]


def _gather_n2_major(scr, outs, n2, pitch):
    nsl = GROUP_W // V7X_LANES

    def body(j, carry):
        for a, (ref, lead) in enumerate(outs):
            for s in range(nsl):
                ref[lead, j, :, _slab(s)] = scr[a * nsl + s, pl.ds(j, G_ROWS, stride=pitch), :]
        return carry

    lax.fori_loop(0, n2, body, 0, unroll=8)


def _conv3_kernel(prev_ref, cur_ref, next_ref, w_ref, b_ref, v_ref, x1_ref, x2_ref, scr, *, n2, pitch):
    i = pl.program_id(1)
    u = _halo_conv(prev_ref[0], cur_ref[0], next_ref[0], w_ref[...], b_ref[...], -1, i == 0,
                   i == pl.num_programs(1) - 1)
    nsl = GROUP_W // V7X_LANES
    for a in range(3):
        _scatter_rows(scr, a * nsl, u[:, a * GROUP_W:(a + 1) * GROUP_W], n2, pitch)
    _gather_n2_major(scr, ((v_ref, 0), (x1_ref, 0), (x2_ref, 0)), n2, pitch)


def _conv3_split(zd, w, b, n1, n2):
    bsz, l, _ = zd.shape
    tl = G_ROWS * n2
    pitch = _strided_pitch(n2)
    nt, rh, nbh = l // tl, tl // HALO, l // HALO
    o = jax.ShapeDtypeStruct((bsz, n2, n1 // 2, GROUP_W), F32)
    ospec = pl.BlockSpec((1, n2, G_ROWS, GROUP_W), lambda bi, i: (bi, 0, i, 0))
    return pl.pallas_call(
        functools.partial(_conv3_kernel, n2=n2, pitch=pitch),
        grid=(bsz, nt),
        in_specs=[pl.BlockSpec((1, HALO, ZD_W), lambda bi, i: (bi, jnp.maximum(i * rh - 1, 0), 0)),
                  pl.BlockSpec((1, tl, ZD_W), lambda bi, i: (bi, i, 0)),
                  pl.BlockSpec((1, HALO, ZD_W), lambda bi, i: (bi, jnp.minimum((i + 1) * rh, nbh - 1), 0)),
                  pl.BlockSpec((3, ZD_W), lambda bi, i: (0, 0)),
                  pl.BlockSpec((1, ZD_W), lambda bi, i: (0, 0))],
        out_specs=(ospec, ospec, ospec),
        out_shape=(o, o, o),
        scratch_shapes=[pltpu.VMEM((3 * GROUP_W // V7X_LANES, G_ROWS * pitch, V7X_LANES), F32)],
        compiler_params=_cparams("parallel", "parallel"),
        name="hyena_conv3",
    )(zd, zd, zd, w, b.reshape(1, ZD_W))


def _dot_hp(a, b):
    ah, bh = a.astype(BF16), b.astype(BF16)
    al = (a - ah.astype(F32)).astype(BF16)
    bl = (b - bh.astype(F32)).astype(BF16)
    return _dot(ah, bh) + _dot(al, bh) + _dot(ah, bl)


FILTER_SUB = 512


def _filter_mlp_kernel(emb_ref, w1_ref, b1_ref, f0_ref, w2_ref, b2_ref, f1_ref, w3_ref, dl_ref,
                       fwd_ref, sec_ref, s_ref, scr, *, n2, pitch):
    i = pl.program_id(0)
    tl = emb_ref.shape[0]
    nsl = GROUP_W // V7X_LANES

    @pl.when(i == 0)
    def _():
        s_ref[...] = jnp.zeros_like(s_ref)

    for sub in range(tl // FILTER_SUB):
        emb = emb_ref[sub * FILTER_SUB:(sub + 1) * FILTER_SUB, :]
        h = jnp.sin(f0_ref[...] * (_dot_hp(emb, w1_ref[...]) + b1_ref[...]))
        h = jnp.sin(f1_ref[...] * (_dot_hp(h, w2_ref[...]) + b2_ref[...]))
        h = _dot_hp(h, w3_ref[...])
        dl = dl_ref[...]
        hf = h[:, :2 * GROUP_W] * jnp.exp(-emb[:, 0:1] * dl)
        hb = h[:, 2 * GROUP_W:] * jnp.exp(-emb[:, HY_EMB:HY_EMB + 1] * dl)
        if sub == 0:
            row = _iota(hb.shape, 0) + i * tl
            hb = jnp.where(row == 0, 0.0, hb)
        s_ref[...] += jnp.sum(jnp.abs(hf) + jnp.abs(hb), axis=0, keepdims=True)
        blk0 = sub * FILTER_SUB // n2
        for o in range(2):
            _scatter_rows(scr, o * nsl, hf[:, o * GROUP_W:(o + 1) * GROUP_W], n2, pitch, blk0)
            _scatter_rows(scr, (2 + o) * nsl, hb[:, o * GROUP_W:(o + 1) * GROUP_W], n2, pitch, blk0)
    _gather_n2_major(scr, ((fwd_ref, 0), (fwd_ref, 1), (sec_ref, 0), (sec_ref, 1)), n2, pitch)


def _hyena_filter(l, w1, b1, freq, w2, b2, w3, n1, n2):
    tl = G_ROWS * n2
    bands = (HY_EMB - 1) // 2
    f = jnp.linspace(1e-4, bands - 1, bands, dtype=F32)

    def embed(pos):
        t = pos / max(l - 1, 1)
        ang = (2.0 * math.pi / l) * pos[:, None] * f[None, :]
        return jnp.concatenate([t[:, None], jnp.cos(ang), -jnp.sin(ang)], axis=-1)

    pos = jnp.arange(l, dtype=F32)
    emb = jnp.concatenate([embed(pos), embed(l - pos), jnp.zeros((l, V7X_LANES - 2 * HY_EMB), F32)], axis=-1)
    deltas = jnp.abs(jnp.linspace(math.log(HY_TARGET) / HY_SLOW, math.log(HY_TARGET) / HY_FAST, GROUP_W, dtype=F32))
    z = lambda r, c: jnp.zeros((r, c), F32)
    w1f, w2f = w1.astype(F32), w2.astype(F32)
    w3r = w3.astype(F32).reshape(HY_HID, 2, 2, GROUP_W)
    w3d = lambda d: w3r[:, :, d].reshape(HY_HID, 2 * GROUP_W)
    w1p = z(V7X_LANES, V7X_LANES).at[:HY_EMB, :HY_HID].set(w1f).at[HY_EMB:2 * HY_EMB, HY_HID:].set(w1f)
    w2p = z(V7X_LANES, V7X_LANES).at[:HY_HID, :HY_HID].set(w2f).at[HY_HID:, HY_HID:].set(w2f)
    w3p = z(V7X_LANES, 4 * GROUP_W).at[:HY_HID, :2 * GROUP_W].set(w3d(0)).at[HY_HID:, 2 * GROUP_W:].set(w3d(1))
    vec = lambda x: jnp.tile(x.astype(F32), 2).reshape(1, V7X_LANES)
    consts = (w1p, vec(b1), vec(freq[0]), w2p, vec(b2), vec(freq[1]), w3p, jnp.tile(deltas, 2).reshape(1, -1))
    pitch = _strided_pitch(n2)
    half = jax.ShapeDtypeStruct((2, n2, n1 // 2, GROUP_W), F32)
    hspec = pl.BlockSpec((2, n2, G_ROWS, GROUP_W), lambda i: (0, 0, i, 0))
    return pl.pallas_call(
        functools.partial(_filter_mlp_kernel, n2=n2, pitch=pitch),
        grid=(l // tl,),
        in_specs=[pl.BlockSpec((tl, V7X_LANES), lambda i: (i, 0))] + [_const_spec(c) for c in consts],
        out_specs=(hspec, hspec, pl.BlockSpec((1, 2 * GROUP_W), lambda i: (0, 0))),
        out_shape=(half, half, jax.ShapeDtypeStruct((1, 2 * GROUP_W), F32)),
        scratch_shapes=[pltpu.VMEM((4 * GROUP_W // V7X_LANES, G_ROWS * pitch, V7X_LANES), F32)],
        compiler_params=_cparams("arbitrary"),
        name="hyena_filter_mlp",
    )(emb, *consts)


def _fft_factors(l):
    return (64, 128) if l == 4096 else (2 * l // 256, 256)


@functools.lru_cache(maxsize=None)
def _fft_tables(n1, n2):
    n = n1 * n2
    k1 = np.arange(n1)[None, :, None]
    m2 = np.arange(n2)[:, None, None]

    def theta(n1_vals):
        idx = (k1 * (n1_vals[None, None, :] * n2 + m2)) % n
        return (2.0 * np.pi / n) * idx
    th = theta(np.arange(n1 // 2))
    fr, fi = np.cos(th), -np.sin(th)
    fwd = np.concatenate([np.concatenate([fr, -fi], -1), np.concatenate([fi, fr], -1)], -2)
    thf = theta(np.arange(n1))
    filt = np.concatenate([np.cos(thf), -np.sin(thf)], -2)
    cr = np.swapaxes(np.cos(th), 1, 2) / n
    ci = np.swapaxes(np.sin(th), 1, 2) / n
    inv = np.concatenate([np.concatenate([cr, -ci], -1), np.concatenate([ci, cr], -1)], -2)
    t2 = (2.0 * np.pi / n2) * ((np.arange(n2)[:, None] * np.arange(n2)[None, :]) % n2)
    gr, gi = np.cos(t2), -np.sin(t2)
    g = np.block([[gr, -gi], [gi, gr]])
    ginv = np.block([[gr, gi], [-gi, gr]])
    return tuple(jnp.asarray(a, BF16) for a in (fwd, filt, inv, g, ginv))


def _regroup_rows(scr, n_rows, pitch, store):
    def body(r, carry):
        for s in range(GROUP_W // V7X_LANES):
            store(r, s, scr[s, pl.ds(r, G_ROWS_BF16, stride=pitch), :])
        return carry

    lax.fori_loop(0, n_rows, body, 0, unroll=8)


def _fft_stage1_kernel(zr_ref, zi_ref, t_ref, a_ref, scr, *, n1, pitch):
    for j in range(G_ROWS_BF16):
        rhs = jnp.concatenate([zr_ref[0, j], zi_ref[0, j]], axis=0).astype(BF16)
        _scatter_rows(scr, 0, _dot(t_ref[j], rhs), 2 * n1, pitch, j)

    def store(r, s, rows):
        a_ref[0, r, :, _slab(s)] = rows.astype(BF16)

    _regroup_rows(scr, 2 * n1, pitch, store)


def _fft_stage1(zr, zi, table, n1, n2, pairs, idx_r, idx_i):
    nb = G_ROWS_BF16
    pitch = _strided_pitch(2 * n1)
    return pl.pallas_call(
        functools.partial(_fft_stage1_kernel, n1=n1, pitch=pitch),
        grid=(pairs, n2 // nb),
        in_specs=[pl.BlockSpec((1, nb, n1 // 2, GROUP_W), lambda p, j: (idx_r(p), j, 0, 0)),
                  pl.BlockSpec((1, nb, n1 // 2, GROUP_W), lambda p, j: (idx_i(p), j, 0, 0)),
                  pl.BlockSpec((nb, 2 * n1, n1), lambda p, j: (j, 0, 0))],
        out_specs=pl.BlockSpec((1, 2 * n1, nb, GROUP_W), lambda p, j: (p, 0, j, 0)),
        out_shape=jax.ShapeDtypeStruct((pairs, 2 * n1, n2, GROUP_W), BF16),
        scratch_shapes=[pltpu.VMEM((GROUP_W // V7X_LANES, nb * pitch, V7X_LANES), F32)],
        compiler_params=_cparams("parallel", "parallel"),
        name="fft_stage1",
    )(zr, zi, table)


def _a_specs(n1, n2, kb, index):
    def spec(part):
        def imap(i, p):
            pp, kk = index(i, p)
            return (pp, part * (n1 // kb) + kk, 0, 0)
        return pl.BlockSpec((1, kb, n2, GROUP_W), imap)
    return [spec(0), spec(1)]


def _fft_spec_kernel(ar_ref, ai_ref, g_ref, s_ref, h_ref, *, kb, n2):
    for k in range(kb):
        rhs = jnp.concatenate([ar_ref[0, k], ai_ref[0, k]], axis=0)
        x = _dot(g_ref[...], rhs) * s_ref[0]
        h_ref[0, k, 0] = x[:n2]
        h_ref[0, k, 1] = x[n2:]


def _fft_filter_spectrum(a, g, inv_s, n1, n2, kb=8):
    orders = a.shape[0]
    return pl.pallas_call(
        functools.partial(_fft_spec_kernel, kb=kb, n2=n2),
        grid=(orders, n1 // kb),
        in_specs=_a_specs(n1, n2, kb, lambda o, i: (o, i))
        + [_const_spec(g), pl.BlockSpec((1, 1, GROUP_W), lambda o, i: (o, 0, 0))],
        out_specs=pl.BlockSpec((1, kb, 2, n2, GROUP_W), lambda o, i: (o, i, 0, 0, 0)),
        out_shape=jax.ShapeDtypeStruct((orders, n1, 2, n2, GROUP_W), F32),
        compiler_params=_cparams("parallel", "parallel"),
        name="fft_filter_spectrum",
    )(a, a, g, inv_s)


def _fft_mid_kernel(ar_ref, ai_ref, g_ref, gi_ref, h_ref, b_ref, scr, *, n2, pitch):
    for k in range(G_ROWS_BF16):
        rhs = jnp.concatenate([ar_ref[0, k], ai_ref[0, k]], axis=0)
        x = _dot(g_ref[...], rhs)
        xr, xi = x[:n2], x[n2:]
        hr, hi = h_ref[0, k, 0], h_ref[0, k, 1]
        y = jnp.concatenate([xr * hr - xi * hi, xr * hi + xi * hr], axis=0).astype(BF16)
        _scatter_rows(scr, 0, _dot(gi_ref[...], y), 2 * n2, pitch, k)

    def store(r, s, rows):
        b_ref[0, r, :, _slab(s)] = rows.astype(BF16)

    _regroup_rows(scr, 2 * n2, pitch, store)


def _fft_mid(a, g, ginv, spec, order, n1, n2):
    pairs = a.shape[0]
    kb = G_ROWS_BF16
    pitch = _strided_pitch(2 * n2)
    return pl.pallas_call(
        functools.partial(_fft_mid_kernel, n2=n2, pitch=pitch),
        grid=(n1 // kb, pairs),
        in_specs=_a_specs(n1, n2, kb, lambda i, p: (p, i)) + [_const_spec(g), _const_spec(ginv),
                  pl.BlockSpec((1, kb, 2, n2, GROUP_W), lambda i, p: (order, i, 0, 0, 0))],
        out_specs=pl.BlockSpec((1, 2 * n2, kb, GROUP_W), lambda i, p: (p, 0, i, 0)),
        out_shape=jax.ShapeDtypeStruct((pairs, 2 * n2, n1, GROUP_W), BF16),
        scratch_shapes=[pltpu.VMEM((GROUP_W // V7X_LANES, kb * pitch, V7X_LANES), F32)],
        compiler_params=_cparams("parallel", "parallel"),
        name="fft_mid",
    )(a, a, g, ginv, spec)


def _fft_stage3_kernel(br_ref, bi_ref, t_ref, ur_ref, ui_ref, gr_ref, gi_ref, sk_ref, y_ref, *scr,
                       n1, pitch, time_major):
    h = n1 // 2
    sk = sk_ref[...]
    for j in range(G_ROWS_BF16):
        rhs = jnp.concatenate([br_ref[0, j], bi_ref[0, j]], axis=0)
        o = _dot(t_ref[j], rhs)
        yr = gr_ref[0, j] * (o[:h] + ur_ref[0, j] * sk)
        yi = gi_ref[0, j] * (o[h:] + ui_ref[0, j] * sk)
        if time_major:
            _scatter_rows(scr[0], 0, yr, h, pitch, j)
            _scatter_rows(scr[1], 0, yi, h, pitch, j)
        else:
            y_ref[0, 0, j] = yr
            y_ref[0, 1, j] = yi
    if time_major:
        for part in range(2):
            def store(r, s, rows, part=part):
                y_ref[0, part, r, :, _slab(s)] = rows.astype(y_ref.dtype)
            _regroup_rows(scr[part], h, pitch, store)


def _fft_stage3(b, table, u, gate, skip, n1, n2, time_major):
    pairs = b.shape[0]
    nb = G_ROWS_BF16
    h = n1 // 2
    pitch = _strided_pitch(h)
    bspec = lambda part: pl.BlockSpec((1, nb, n1, GROUP_W), lambda p, j: (p, part * (n2 // nb) + j, 0, 0))
    even = pl.BlockSpec((1, nb, h, GROUP_W), lambda p, j: (2 * p, j, 0, 0))
    odd = pl.BlockSpec((1, nb, h, GROUP_W), lambda p, j: (2 * p + 1, j, 0, 0))
    if time_major:
        out_spec = pl.BlockSpec((1, 2, h, nb, GROUP_W), lambda p, j: (p, 0, 0, j, 0))
        out_shape = jax.ShapeDtypeStruct((pairs, 2, h, n2, GROUP_W), BF16)
        scratch = [pltpu.VMEM((GROUP_W // V7X_LANES, nb * pitch, V7X_LANES), F32)] * 2
    else:
        out_spec = pl.BlockSpec((1, 2, nb, h, GROUP_W), lambda p, j: (p, 0, j, 0, 0))
        out_shape = jax.ShapeDtypeStruct((pairs, 2, n2, h, GROUP_W), F32)
        scratch = []
    y = pl.pallas_call(
        functools.partial(_fft_stage3_kernel, n1=n1, pitch=pitch, time_major=time_major),
        grid=(pairs, n2 // nb),
        in_specs=[bspec(0), bspec(1), pl.BlockSpec((nb, n1, 2 * n1), lambda p, j: (j, 0, 0)),
                  even, odd, even, odd, pl.BlockSpec((1, GROUP_W), lambda p, j: (0, 0))],
        out_specs=out_spec,
        out_shape=out_shape,
        scratch_shapes=scratch,
        compiler_params=_cparams("parallel", "parallel"),
        name="fft_stage3_out" if time_major else "fft_stage3",
    )(b, b, table, u, u, gate, gate, skip.reshape(1, GROUP_W).astype(F32))
    return y.reshape(2 * pairs, h * n2, GROUP_W) if time_major else y.reshape(2 * pairs, n2, h, GROUP_W)


def _hyena_spectrum(l, w1, b1, freq, w2, b2, w3):
    n1, n2 = _fft_factors(l)
    _, t_filt, _, g, _ = _fft_tables(n1, n2)
    first, second, sums = _hyena_filter(l, w1, b1, freq, w2, b2, w3, n1, n2)
    a = _fft_stage1(first, second, t_filt, n1, n2, 2, lambda p: p, lambda p: p)
    inv_s = (1.0 / sums).reshape(2, 1, GROUP_W)
    return _fft_filter_spectrum(a, g, inv_s, n1, n2)


def _hyena_long_conv(u, gate, spec, order, skip, n1, n2, time_major):
    t_fwd, _, t_inv, g, ginv = _fft_tables(n1, n2)
    a = _fft_stage1(u, u, t_fwd, n1, n2, u.shape[0] // 2, lambda p: 2 * p, lambda p: 2 * p + 1)
    b = _fft_mid(a, g, ginv, spec, order, n1, n2)
    return _fft_stage3(b, t_inv, u, gate, skip, n1, n2, time_major)


def _mixer_hyena(zd, P, l, spec):
    n1, n2 = _fft_factors(zd.shape[1])
    v, x1, x2 = _conv3_split(zd, P['d_conv_w'][l].astype(F32), P['d_conv_b'][l].astype(F32), n1, n2)
    z = _hyena_long_conv(v, x1, spec, 0, P['d_skip'][l][0], n1, n2, False)
    return _hyena_long_conv(z, x2, spec, 1, P['d_skip'][l][1], n1, n2, True)


def _gelu_tanh(x):
    return 0.5 * x * (1.0 + jnp.tanh(math.sqrt(2.0 / math.pi) * (x + 0.044715 * (x * x * x))))


def _head_rms(h, bd):
    ss = _dot((h * h).astype(BF16), bd) * (1.0 / HEAD_W)
    return h * lax.rsqrt(ss + EPS)


def _out_proj_kernel(x_ref, af_ref, ab_ref, ag_ref, bf_ref, bb_ref, br_ref, cf_ref, cb_ref, co_ref, yd_ref,
                     w_ref, bg_ref, cg_ref, bd_ref, n2_ref, rwh_ref, rwl_ref, rb_ref,
                     xo_ref, hn_ref, lg_ref):
    bd = bd_ref[...]
    f = lambda ref: ref[...].astype(F32)
    ya = (f(af_ref) + f(ab_ref)) * _gelu_tanh(f(ag_ref))
    r = f(br_ref)
    yb = _head_rms(f(bf_ref) + f(bb_ref), bd) * bg_ref[...] * (r * _sigmoid(r))
    yc = _head_rms(f(cf_ref) + f(cb_ref), bd) * cg_ref[...] * _sigmoid(f(co_ref))
    mixed = jnp.concatenate([ya.astype(BF16), yb.astype(BF16), yc.astype(BF16), yd_ref[...]], axis=-1)
    x = x_ref[...] + _dot(mixed, w_ref[...])
    xo_ref[...] = x
    hn = x * lax.rsqrt(jnp.mean(x * x, axis=-1, keepdims=True) + EPS) * n2_ref[...]
    hn_ref[...] = hn.astype(BF16)
    hi = hn.astype(BF16)
    lo = (hn - hi.astype(F32)).astype(BF16)
    lg_ref[...] = _dot(hi, rwh_ref[...]) + _dot(lo, rwh_ref[...]) + _dot(hi, rwl_ref[...]) + rb_ref[...]


def _out_proj(x2d, af, ab, za, bf, bb, zb, cf, cb, zc, yd, w_out, b_norm_g, c_norm_g, norm2_g, rw, rb, tm=512):
    t = x2d.shape[0]
    bd = _chunk_masks(False)['bd']
    rwh = rw.astype(BF16)
    rwl = (rw - rwh.astype(F32)).astype(BF16)
    row = lambda w, j=0: pl.BlockSpec((tm, w), lambda i: (i, j))
    consts = (w_out, b_norm_g.reshape(1, -1).astype(F32), c_norm_g.reshape(1, -1).astype(F32), bd,
              norm2_g.reshape(1, -1).astype(F32), rwh, rwl, rb)
    return pl.pallas_call(
        _out_proj_kernel,
        grid=(t // tm,),
        in_specs=[row(D_MODEL), row(GROUP_W), row(GROUP_W), row(GROUP_W, 1), row(GROUP_W), row(GROUP_W),
                  row(GROUP_W, 2), row(GROUP_W), row(GROUP_W), row(GROUP_W, 3), row(GROUP_W)]
        + [_const_spec(c) for c in consts],
        out_specs=(row(D_MODEL), row(D_MODEL), row(V7X_LANES)),
        out_shape=(jax.ShapeDtypeStruct((t, D_MODEL), F32), jax.ShapeDtypeStruct((t, D_MODEL), BF16),
                   jax.ShapeDtypeStruct((t, V7X_LANES), F32)),
        compiler_params=_cparams("parallel"),
        name="out_proj",
    )(x2d, af, ab, za, bf, bb, zb, cf, cb, zc, yd, *consts)


def _route(logits):
    lane = _iota(logits.shape, 1).astype(F32)
    is_g = lane < MOE_GROUPS
    gl = jnp.where(is_g, logits, NEG_BIG)
    gmax = jnp.max(gl, axis=1, keepdims=True)
    gidx = jnp.min(jnp.where(gl == gmax, lane, 1e9), axis=1, keepdims=True)
    gprob = 1.0 / jnp.sum(jnp.where(is_g, jnp.exp(gl - gmax), 0.0), axis=1, keepdims=True)
    lo = MOE_GROUPS + MOE_PER_GROUP * gidx
    el = jnp.where((lane >= lo) & (lane < lo + MOE_PER_GROUP), logits, NEG_BIG)
    v1 = jnp.max(el, axis=1, keepdims=True)
    i1 = jnp.min(jnp.where(el == v1, lane, 1e9), axis=1, keepdims=True)
    el2 = jnp.where(lane == i1, NEG_BIG, el)
    v2 = jnp.max(el2, axis=1, keepdims=True)
    i2 = jnp.min(jnp.where(el2 == v2, lane, 1e9), axis=1, keepdims=True)
    e21 = jnp.exp(v2 - v1)
    p1 = 1.0 / (1.0 + e21)
    return jnp.where(lane == i1, p1 * gprob, 0.0) + jnp.where(lane == i2, e21 * p1 * gprob, 0.0), gidx


MOE_TB = 1024
MOE_RT = 64
MOE_MM = 4 * MOE_RT
MOE_SB = MOE_TB + MOE_GROUPS * MOE_RT


def _moe_kernel(hn_ref, lg_ref, tril_ref, wg_ref, wu_ref, wd_ref, o_ref, xs_ref, ys_ref, gs_ref, pos_ref, meta_ref):
    grp = pl.program_id(1)
    tb, sb, rt = MOE_TB, MOE_SB, MOE_RT

    @pl.when(grp == 0)
    def _():
        gates, gidx = _route(lg_ref[...])
        lane = _iota((tb, V7X_LANES), 1).astype(F32)
        ohg = jnp.where(lane == gidx, 1.0, 0.0)
        rank = jnp.sum(_dot(tril_ref[...], ohg.astype(BF16)) * ohg, axis=1, keepdims=True)
        cnt = jnp.sum(ohg, axis=0, keepdims=True)
        pc = jnp.ceil(cnt * (1.0 / rt)) * rt
        lane1 = _iota((1, V7X_LANES), 1)
        off = jnp.zeros((1, V7X_LANES), F32)
        run = jnp.zeros((1, 1), F32)
        for g in range(MOE_GROUPS):
            off = jnp.where(lane1 == g, run, off)
            run = run + jnp.sum(jnp.where(lane1 == g, pc, 0.0), axis=1, keepdims=True)
        pos = jnp.sum(ohg * off, axis=1, keepdims=True) + rank
        pos_ref[...] = jnp.broadcast_to(pos, (tb, V7X_LANES))
        meta_ref[0:1, :] = off
        meta_ref[1:2, :] = pc
        hi = jnp.floor(pos * (1.0 / 64.0))
        lo = pos - 64.0 * hi
        pm = jnp.where(lane == 0.0, hi, jnp.where(lane == 1.0, lo, 0.0)).astype(BF16)
        sel = jnp.where(_iota((8, V7X_LANES), 0) == _iota((8, V7X_LANES), 1), 1.0, 0.0).astype(BF16)
        rows8 = _dot_nt(sel, pm)
        pos_row = 64.0 * rows8[0:1, :] + rows8[1:2, :]
        p = jnp.where(_iota((sb, tb), 0).astype(F32) == pos_row, 1.0, 0.0).astype(BF16)
        xs_ref[...] = _dot(p, hn_ref[...]).astype(BF16)
        gh = gates.astype(BF16)
        gl = (gates - gh.astype(F32)).astype(BF16)
        gsorted = _dot(p, jnp.concatenate([gh, gl], axis=1))
        gs_ref[...] = gsorted[:, :V7X_LANES] + gsorted[:, V7X_LANES:]
        ys_ref[...] = jnp.zeros_like(ys_ref)

    lane1 = _iota((1, V7X_LANES), 1)
    start = jnp.sum(jnp.where(lane1 == grp, meta_ref[0:1, :], 0.0)).astype(jnp.int32)
    rows = jnp.sum(jnp.where(lane1 == grp, meta_ref[1:2, :], 0.0)).astype(jnp.int32)

    def experts(r0, nrows):
        xt = xs_ref[pl.ds(r0, nrows), :]
        gsl = gs_ref[pl.ds(r0, nrows), :]
        lane_t = _iota((nrows, V7X_LANES), 1)
        acc = None
        for j in range(MOE_PER_GROUP):
            g = _dot(xt, wg_ref[j])
            a = (g * _sigmoid(g)) * _dot(xt, wu_ref[j])
            y = _dot(a.astype(BF16), wd_ref[j])
            w = jnp.sum(jnp.where(lane_t == MOE_GROUPS + MOE_PER_GROUP * grp + j, gsl, 0.0), axis=1, keepdims=True)
            acc = w * y if acc is None else acc + w * y
        ys_ref[pl.ds(r0, nrows), :] = acc.astype(BF16)

    ntiles = rows // rt
    per_mm = MOE_MM // rt

    def full(i, carry):
        experts(pl.multiple_of(start + i * MOE_MM, rt), MOE_MM)
        return carry

    nfull = jnp.maximum(ntiles // per_mm - 1, 0)
    lax.fori_loop(0, nfull, full, 0)
    last = ntiles - nfull * per_mm
    for k in range(1, 2 * per_mm):
        @pl.when(last == k)
        def _(k=k):
            experts(pl.multiple_of(start + nfull * MOE_MM, rt), k * rt)

    @pl.when(grp == MOE_GROUPS - 1)
    def _():
        pt = jnp.where(_iota((tb, sb), 1).astype(F32) == pos_ref[:, 0:1], 1.0, 0.0).astype(BF16)
        o_ref[...] = _dot(pt, ys_ref[...]).astype(o_ref.dtype)


def _moe(hn, logits, w_gate, w_up, w_down):
    t = hn.shape[0]
    tm = MOE_TB
    tril = jnp.asarray(np.tril(np.ones((tm, tm), np.float32), -1), BF16)
    row = lambda w: pl.BlockSpec((tm, w), lambda i, g: (i, 0))
    wspec = lambda a, b: pl.BlockSpec((MOE_PER_GROUP, a, b), lambda i, g: (g, 0, 0))
    return pl.pallas_call(
        _moe_kernel,
        grid=(t // tm, MOE_GROUPS),
        in_specs=[row(D_MODEL), row(V7X_LANES), _const_spec(tril), wspec(D_MODEL, D_EXPERT),
                  wspec(D_MODEL, D_EXPERT), wspec(D_EXPERT, D_MODEL)],
        out_specs=row(D_MODEL),
        out_shape=jax.ShapeDtypeStruct((t, D_MODEL), BF16),
        scratch_shapes=[pltpu.VMEM((MOE_SB, D_MODEL), BF16), pltpu.VMEM((MOE_SB, D_MODEL), BF16),
                        pltpu.VMEM((MOE_SB, V7X_LANES), F32), pltpu.VMEM((tm, V7X_LANES), F32),
                        pltpu.VMEM((8, V7X_LANES), F32)],
        compiler_params=_cparams("parallel", "arbitrary"),
        name="moe",
    )(hn, logits, tril, w_gate, w_up, w_down)


def _layer(x2d, moe_prev, bsz, seq, l, P, W, spec):
    x2d, za, zb, zc, zd, zs = _in_proj(x2d, moe_prev, P['norm1_g'][l].astype(F32), W['w_in'][l])
    r3 = lambda a: a.reshape(bsz, seq, a.shape[-1])
    r2 = lambda a: a.reshape(bsz * seq, a.shape[-1])
    mp = W['mix'][l]
    cw, cb = P['a_conv_w'][l].astype(F32), P['a_conv_b'][l].reshape(1, -1).astype(F32)
    za3, zb3, zc3, zs3 = r3(za), r3(zb), r3(zc), r3(zs)
    af = _rglru(za3, cw, cb, *mp['a', 0], False)
    ab = _rglru(za3, cw, cb, *mp['a', 1], True)
    bf, bb = _gla(zb3, zs3, mp['b', 0], mp['b', 1])
    cf, cbk = _mlstm(zc3, zs3, mp['c', 0], mp['c', 1])
    yd = _mixer_hyena(r3(zd), P, l, spec)
    x_new, hn, logits = _out_proj(x2d, r2(af), r2(ab), za, r2(bf), r2(bb), zb, r2(cf), r2(cbk), zc, r2(yd),
                                  W['w_out'][l], P['b_norm_g'][l], P['c_norm_g'][l], P['norm2_g'][l],
                                  W['router_w'][l], W['router_b'][l])
    return x_new, _moe(hn, logits, W['w_gate'][l], W['w_up'][l], W['w_down'][l])


def _trunk(x, P, W):
    bsz, seq, _ = x.shape
    depth = P['w_in'].shape[0]
    x2d = x.reshape(bsz * seq, D_MODEL)
    moe_term = None
    for l in range(depth):
        spec = _hyena_spectrum(seq, P['d_ffn_w1'][l], P['d_ffn_b1'][l], P['d_sin_freq'][l], P['d_ffn_w2'][l],
                               P['d_ffn_b2'][l], P['d_ffn_w3'][l])
        x2d, moe_term = _layer(x2d, moe_term, bsz, seq, l, P, W, spec)
    return _final_norm(x2d, moe_term, P['final_norm_g']).reshape(bsz, seq, D_MODEL)


def _prep_weights(P):
    depth = P['w_in'].shape[0]
    W = {'w_in': [_permute_w_in(P['w_in'][l]) for l in range(depth)],
         'w_out': [P['w_out'][l].astype(BF16) for l in range(depth)],
         'mix': [_mixer_prep(P, l) for l in range(depth)],
         'w_gate': [P['moe_w_gate'][l].astype(BF16) for l in range(depth)],
         'w_up': [P['moe_w_up'][l].astype(BF16) for l in range(depth)],
         'w_down': [P['moe_w_down'][l].astype(BF16) for l in range(depth)],
         'router_w': [], 'router_b': []}
    for l in range(depth):
        rw = jnp.concatenate([P['moe_group_w'][l], P['moe_expert_w'][l]], axis=1).astype(F32)
        rb = jnp.concatenate([P['moe_group_b'][l], P['moe_expert_b'][l]]).astype(F32)
        padc = V7X_LANES - rw.shape[1]
        W['router_w'].append(jnp.pad(rw, ((0, 0), (0, padc))))
        W['router_b'].append(jnp.pad(rb, (0, padc)).reshape(1, V7X_LANES))
    return W


def kernel(x_prompt, x_sample, norm1_g, w_in, a_conv_w, a_conv_b, a_gate_w, a_gate_b, a_lambda, b_gate_w, b_gate_b, b_norm_g, c_gate_b, c_norm_g, d_conv_w, d_conv_b, d_ffn_w1, d_ffn_b1, d_sin_freq, d_ffn_w2, d_ffn_b2, d_ffn_w3, d_skip, w_out, norm2_g, moe_group_w, moe_group_b, moe_expert_w, moe_expert_b, moe_w_gate, moe_w_up, moe_w_down, final_norm_g):
    P = {'norm1_g': norm1_g, 'w_in': w_in, 'a_conv_w': a_conv_w, 'a_conv_b': a_conv_b,
         'a_gate_w': a_gate_w, 'a_gate_b': a_gate_b, 'a_lambda': a_lambda, 'b_gate_w': b_gate_w,
         'b_gate_b': b_gate_b, 'b_norm_g': b_norm_g, 'c_gate_b': c_gate_b, 'c_norm_g': c_norm_g,
         'd_conv_w': d_conv_w, 'd_conv_b': d_conv_b, 'd_ffn_w1': d_ffn_w1, 'd_ffn_b1': d_ffn_b1,
         'd_sin_freq': d_sin_freq, 'd_ffn_w2': d_ffn_w2, 'd_ffn_b2': d_ffn_b2, 'd_ffn_w3': d_ffn_w3,
         'd_skip': d_skip, 'w_out': w_out, 'norm2_g': norm2_g, 'moe_group_w': moe_group_w,
         'moe_group_b': moe_group_b, 'moe_expert_w': moe_expert_w, 'moe_expert_b': moe_expert_b,
         'moe_w_gate': moe_w_gate, 'moe_w_up': moe_w_up, 'moe_w_down': moe_w_down,
         'final_norm_g': final_norm_g}
    W = _prep_weights(P)
    return (_trunk(x_prompt, P, W), _trunk(x_sample, P, W))
```

```python
import functools
import math

import jax
import jax.numpy as jnp
import numpy as np
from jax import lax
from jax.experimental import pallas as pl
from jax.experimental.pallas import tpu as pltpu

F32 = jnp.float32
BF16 = jnp.bfloat16

D_MODEL = 1024
GROUP_W = 256
N_HEADS = 4
HEAD_W = GROUP_W // N_HEADS
B_DK = 32
B_RANK = 16
CHUNK = 64
RG_C = 8.0
RG_SEGMENTS = 16
RG_UNROLL = 4
GLA_TAU = 16.0
HY_EMB = 33
HY_HID = 64
HY_TARGET = 1e-2
HY_FAST = 0.3
HY_SLOW = 1.5
MOE_GROUPS = 4
MOE_PER_GROUP = 4
MOE_EXPERTS = 16
D_EXPERT = 512
EPS = 1e-6
NEG_BIG = -1e30

V7X_LANES = 128
V7X_SUBLANES = 8
VMEM_LIMIT = 56 * 1024 * 1024

ZA_W, ZB_W, ZC_W, ZD_W, ZS_W = 512, 768, 1024, 768, 128
Z_SPLITS = (ZA_W, ZB_W, ZC_W, ZD_W, ZS_W)
Z_TOTAL = sum(Z_SPLITS)


def _cparams(*sem):
    return pltpu.CompilerParams(dimension_semantics=sem, vmem_limit_bytes=VMEM_LIMIT)


def _dot(a, b):
    return jnp.dot(a, b, preferred_element_type=F32)


def _dot_nt(a, b):
    return lax.dot_general(a, b, (((1,), (1,)), ((), ())), preferred_element_type=F32)


def _dot_tn(a, b):
    return lax.dot_general(a, b, (((0,), (0,)), ((), ())), preferred_element_type=F32)


def _dot_split(m_bf16, x):
    hi = x.astype(BF16)
    lo = (x - hi.astype(F32)).astype(BF16)
    return _dot(m_bf16, hi) + _dot(m_bf16, lo)


def _split_dot(x, m_bf16):
    hi = x.astype(BF16)
    lo = (x - hi.astype(F32)).astype(BF16)
    return _dot(hi, m_bf16) + _dot(lo, m_bf16)


def _sigmoid(x):
    return 1.0 / (1.0 + jnp.exp(-x))


def _log_sigmoid(x):
    return jnp.minimum(x, 0.0) - jnp.log(1.0 + jnp.exp(-jnp.abs(x)))


def _iota(shape, dim):
    return lax.broadcasted_iota(jnp.int32, shape, dim)


def _in_proj_kernel(*refs, add):
    if add:
        x_ref, m_ref, g_ref, w_ref, xo_ref, za_ref, zb_ref, zc_ref, zd_ref, zs_ref = refs
        x = x_ref[...] + m_ref[...].astype(F32)
        xo_ref[...] = x
    else:
        x_ref, g_ref, w_ref, za_ref, zb_ref, zc_ref, zd_ref, zs_ref = refs
        x = x_ref[...]
    ms = jnp.mean(x * x, axis=-1, keepdims=True)
    xn = (x * lax.rsqrt(ms + EPS) * g_ref[...]).astype(BF16)
    off = 0
    for ref, w in zip((za_ref, zb_ref, zc_ref, zd_ref, zs_ref), Z_SPLITS):
        ref[...] = _dot(xn, w_ref[:, off:off + w]).astype(ref.dtype)
        off += w


def _in_proj(x2d, moe_prev, g, w_perm, tm=512):
    t = x2d.shape[0]
    add = moe_prev is not None
    row = lambda w: pl.BlockSpec((tm, w), lambda i: (i, 0))
    outs = tuple(jax.ShapeDtypeStruct((t, w), F32 if w == ZS_W else BF16) for w in Z_SPLITS)
    out_specs = tuple(row(w) for w in Z_SPLITS)
    consts = (g.reshape(1, D_MODEL), w_perm)
    res = pl.pallas_call(
        functools.partial(_in_proj_kernel, add=add),
        grid=(t // tm,),
        in_specs=[row(D_MODEL)] * (2 if add else 1) + [_const_spec(c) for c in consts],
        out_specs=((row(D_MODEL),) if add else ()) + out_specs,
        out_shape=((jax.ShapeDtypeStruct((t, D_MODEL), F32),) if add else ()) + outs,
        compiler_params=_cparams("parallel"),
        name="in_proj",
    )(*((x2d, moe_prev) if add else (x2d,)), *consts)
    return tuple(res) if add else (x2d,) + tuple(res)


def _final_norm_kernel(x_ref, m_ref, g_ref, o_ref):
    x = x_ref[...] + m_ref[...].astype(F32)
    o_ref[...] = x * lax.rsqrt(jnp.mean(x * x, axis=-1, keepdims=True) + EPS) * g_ref[...]


def _final_norm(x2d, moe_prev, g, tm=512):
    t = x2d.shape[0]
    row = pl.BlockSpec((tm, D_MODEL), lambda i: (i, 0))
    return pl.pallas_call(
        _final_norm_kernel,
        grid=(t // tm,),
        in_specs=[row, row, pl.BlockSpec((1, D_MODEL), lambda i: (0, 0))],
        out_specs=row,
        out_shape=jax.ShapeDtypeStruct((t, D_MODEL), F32),
        compiler_params=_cparams("parallel"),
        name="final_norm",
    )(x2d, moe_prev, g.reshape(1, D_MODEL).astype(F32))


def _permute_w_in(w_in):
    sizes = (256, 256, 128, 128, 256, 256, 16, 16, 256, 256, 256, 256, 16, 768)
    offs = np.concatenate([[0], np.cumsum(sizes)])
    seg = [w_in[:, offs[i]:offs[i + 1]] for i in range(len(sizes))]
    (a_x, a_g, b_q, b_k, b_v, b_r, b_lf, b_lb, c_q, c_k, c_v, c_o, c_g, d_u) = seg
    small = jnp.concatenate([b_lf, b_lb, c_g, jnp.zeros((w_in.shape[0], ZS_W - 48), w_in.dtype)], axis=1)
    return jnp.concatenate([a_x, a_g, b_q, b_k, b_v, b_r, c_q, c_k, c_v, c_o, d_u, small], axis=1).astype(BF16)


HALO = 16


def _strided_pitch(n):
    assert n % V7X_SUBLANES == 0
    return n if (n // V7X_SUBLANES) % 2 == 1 else n + V7X_SUBLANES


def _halo_conv(prev, cur, nxt, w, b, lo, first, last):
    tl = cur.shape[0]
    prev = jnp.where(first, 0.0, prev.astype(F32))
    nxt = jnp.where(last, 0.0, nxt.astype(F32))
    ext = jnp.concatenate([prev, cur.astype(F32), nxt], axis=0)
    n = tl + 2 * HALO
    acc = None
    for j in range(w.shape[0]):
        o = lo + j
        sh = ext if o == 0 else pltpu.roll(ext, (-o) % n, 0)
        term = sh[HALO:HALO + tl] * w[j:j + 1, :]
        acc = term if acc is None else acc + term
    return acc + b


def _rglru_kernel(prev_ref, cur_ref, next_ref, cw_ref, cb_ref, gw_ref, gb_ref, sp_ref, out_ref,
                  a_s, b_s, h_s, p_s, carry, *, tl, reverse):
    i = pl.program_id(1)
    nt = pl.num_programs(1)
    ti = (nt - 1 - i) if reverse else i
    u = _halo_conv(prev_ref[0], cur_ref[0], next_ref[0], cw_ref[...], cb_ref[...], -2, ti == 0, ti == nt - 1)
    gates = _sigmoid(_dot(u.astype(BF16), gw_ref[...]) + gb_ref[...])
    r = gates[:, :GROUP_W]
    ig = gates[:, GROUP_W:]
    log_a = (-RG_C) * r * sp_ref[...]
    a = jnp.exp(log_a)
    bt = jnp.sqrt(1.0 - a * a) * (ig * u)
    nseg = RG_SEGMENTS
    m = tl // nseg
    mp, sp = _strided_pitch(m), _strided_pitch(nseg)
    for k in range(2):
        for seg in range(nseg):
            a_s[k, seg * mp:seg * mp + m, :] = a[seg * m:(seg + 1) * m, k * V7X_LANES:(k + 1) * V7X_LANES]
            b_s[k, seg * mp:seg * mp + m, :] = bt[seg * m:(seg + 1) * m, k * V7X_LANES:(k + 1) * V7X_LANES]

    @pl.when(i == 0)
    def _():
        carry[...] = jnp.zeros_like(carry)

    def body(jj, hp):
        j = (m - 1 - jj) if reverse else jj
        out = []
        for k in range(2):
            h, p = hp[2 * k], hp[2 * k + 1]
            av = a_s[k, pl.ds(j, nseg, stride=mp), :]
            bv = b_s[k, pl.ds(j, nseg, stride=mp), :]
            h = av * h + bv
            p = av * p
            rows = pl.ds(pl.multiple_of(j * sp, V7X_SUBLANES), nseg)
            h_s[k, rows, :] = h
            p_s[k, rows, :] = p
            out += [h, p]
        return tuple(out)

    z8 = jnp.zeros((nseg, V7X_LANES), F32)
    o8 = jnp.ones((nseg, V7X_LANES), F32)
    ends = lax.fori_loop(0, m, body, (z8, o8, z8, o8), unroll=RG_UNROLL)
    for k in range(2):
        h_end, p_end = ends[2 * k], ends[2 * k + 1]
        c = carry[k]
        cin = [None] * nseg
        for seg in (range(nseg - 1, -1, -1) if reverse else range(nseg)):
            cin[seg] = c
            c = h_end[seg:seg + 1, :] + p_end[seg:seg + 1, :] * c
        carry[k] = c
        for seg in range(nseg):
            hv = h_s[k, pl.ds(seg, m, stride=sp), :]
            pv = p_s[k, pl.ds(seg, m, stride=sp), :]
            out_ref[0, seg * m:(seg + 1) * m, k * V7X_LANES:(k + 1) * V7X_LANES] = (hv + pv * cin[seg]).astype(out_ref.dtype)


def _rglru(za, conv_w, conv_b, gate_w_bd, gate_b, softplus_neg_lam, reverse, tl=512):
    b, l, _ = za.shape
    nt = l // tl
    rh = tl // HALO
    nbh = l // HALO

    def tmap(bi, i):
        return (nt - 1 - i) if reverse else i

    kern = functools.partial(_rglru_kernel, tl=tl, reverse=reverse)
    return pl.pallas_call(
        kern,
        grid=(b, nt),
        in_specs=[
            pl.BlockSpec((1, HALO, GROUP_W), lambda bi, i: (bi, jnp.maximum(tmap(bi, i) * rh - 1, 0), 0)),
            pl.BlockSpec((1, tl, GROUP_W), lambda bi, i: (bi, tmap(bi, i), 0)),
            pl.BlockSpec((1, HALO, GROUP_W), lambda bi, i: (bi, jnp.minimum((tmap(bi, i) + 1) * rh, nbh - 1), 0)),
            pl.BlockSpec((4, GROUP_W), lambda bi, i: (0, 0)),
            pl.BlockSpec((1, GROUP_W), lambda bi, i: (0, 0)),
            pl.BlockSpec((GROUP_W, 2 * GROUP_W), lambda bi, i: (0, 0)),
            pl.BlockSpec((1, 2 * GROUP_W), lambda bi, i: (0, 0)),
            pl.BlockSpec((1, GROUP_W), lambda bi, i: (0, 0)),
        ],
        out_specs=pl.BlockSpec((1, tl, GROUP_W), lambda bi, i: (bi, tmap(bi, i), 0)),
        out_shape=jax.ShapeDtypeStruct((b, l, GROUP_W), BF16),
        scratch_shapes=[pltpu.VMEM((2, RG_SEGMENTS * _strided_pitch(tl // RG_SEGMENTS), V7X_LANES), F32)] * 2
        + [pltpu.VMEM((2, (tl // RG_SEGMENTS) * _strided_pitch(RG_SEGMENTS), V7X_LANES), F32)] * 2
        + [pltpu.VMEM((2, 1, V7X_LANES), F32)],
        compiler_params=_cparams("parallel", "arbitrary"),
        name="rglru_bwd" if reverse else "rglru_fwd",
    )(za, za, za, conv_w, conv_b, gate_w_bd, gate_b, softplus_neg_lam)


def _block_diag(blocks):
    h, di, do = blocks.shape
    eye = jnp.eye(h, dtype=blocks.dtype)
    return jnp.einsum('hde,hg->hdge', blocks, eye).reshape(h * di, h * do)


def _chunk_masks(reverse):
    r = np.arange(GROUP_W)
    t = np.arange(CHUNK)
    tri = (t[None, :] >= t[:, None]) if reverse else (t[None, :] <= t[:, None])
    cmask = np.tile(tri, (1, N_HEADS))
    bd = (r[:, None] // HEAD_W) == (r[None, :] // HEAD_W)
    kd = (r[:, None] // HEAD_W) == (np.arange(N_HEADS * B_DK)[None, :] // B_DK)
    i2 = np.tile(np.eye(CHUNK, dtype=bool), (1, N_HEADS))
    hm = np.zeros((8, GROUP_W), bool)
    hm[:N_HEADS] = np.arange(N_HEADS)[:, None] == (r[None, :] // HEAD_W)
    return dict(tri=jnp.asarray(tri, BF16), cmask=jnp.asarray(cmask, F32), bd=jnp.asarray(bd, BF16),
                kd=jnp.asarray(kd, BF16), kdf=jnp.asarray(kd, F32),
                i2=jnp.asarray(i2, F32), hm=jnp.asarray(hm, F32))


def _const_spec(arr):
    nd = arr.ndim
    return pl.BlockSpec(arr.shape, lambda *_: (0,) * nd)


def _gla_chunk(zb_ref, la, tri_ref, kd_ref, kdf_ref, bd_ref, causal, out_ref, st_ref, c):
    rows = slice(c * CHUNK, (c + 1) * CHUNK)
    q = zb_ref[0, rows, 0:128].astype(F32) * (B_DK ** -0.5)
    k = zb_ref[0, rows, 128:256].astype(F32)
    vb = zb_ref[0, rows, 256:512]
    la_c = la[rows]
    bcum = _dot_split(tri_ref[...], la_c)
    btot = jnp.sum(la_c, axis=0, keepdims=True)
    q_in = (q * jnp.exp(bcum)).astype(BF16)
    k_in = (k * jnp.exp(-bcum)).astype(BF16)
    k_st = (k * jnp.exp(btot - bcum)).astype(BF16)
    kexp = jnp.concatenate([k_in] * N_HEADS, axis=0) * kd_ref[...]
    att = jnp.where(causal, _dot_nt(q_in, kexp), 0.0)
    vbd = jnp.concatenate([vb] * N_HEADS, axis=0) * bd_ref[...]
    st = st_ref[...]
    out_ref[0, rows, :] = (_dot(att.astype(BF16), vbd) + _dot_nt(q_in, st.astype(BF16))).astype(out_ref.dtype)
    st_ref[...] = st * jnp.exp(btot) + _dot_tn(vb, k_st) * kdf_ref[...]


def _gla_kernel(zbf_ref, zsf_ref, zbb_ref, zsb_ref, gwf_ref, gbf_ref, gwb_ref, gbb_ref, trif_ref, trib_ref,
                kd_ref, kdf_ref, bd_ref, cmf_ref, cmb_ref, outf_ref, outb_ref, stf_ref, stb_ref, *, tl):
    @pl.when(pl.program_id(1) == 0)
    def _():
        stf_ref[...] = jnp.zeros_like(stf_ref)
        stb_ref[...] = jnp.zeros_like(stb_ref)

    log_alpha = lambda zs_ref, gw_ref, gb_ref: _log_sigmoid(
        _dot(zs_ref[0].astype(BF16), gw_ref[...]) + gb_ref[...]) * (1.0 / GLA_TAU)
    laf = log_alpha(zsf_ref, gwf_ref, gbf_ref)
    lab = log_alpha(zsb_ref, gwb_ref, gbb_ref)
    causal_f = cmf_ref[...] > 0.0
    causal_b = cmb_ref[...] > 0.0
    nch = tl // CHUNK
    for c in range(nch):
        _gla_chunk(zbf_ref, laf, trif_ref, kd_ref, kdf_ref, bd_ref, causal_f, outf_ref, stf_ref, c)
        _gla_chunk(zbb_ref, lab, trib_ref, kd_ref, kdf_ref, bd_ref, causal_b, outb_ref, stb_ref, nch - 1 - c)


def _gla(zb, zs, prep_f, prep_b, tl=1024):
    b, l, _ = zb.shape
    nt = l // tl
    mf, mb = _chunk_masks(False), _chunk_masks(True)
    consts = (*prep_f, *prep_b, mf['tri'], mb['tri'], mf['kd'], mf['kdf'], mf['bd'], mf['cmask'], mb['cmask'])
    fmap = lambda bi, i: (bi, i, 0)
    bmap = lambda bi, i: (bi, nt - 1 - i, 0)
    o = jax.ShapeDtypeStruct((b, l, GROUP_W), BF16)
    st = pltpu.VMEM((GROUP_W, N_HEADS * B_DK), F32)
    return pl.pallas_call(
        functools.partial(_gla_kernel, tl=tl),
        grid=(b, nt),
        in_specs=[pl.BlockSpec((1, tl, ZB_W), fmap), pl.BlockSpec((1, tl, ZS_W), fmap),
                  pl.BlockSpec((1, tl, ZB_W), bmap), pl.BlockSpec((1, tl, ZS_W), bmap)]
        + [_const_spec(c) for c in consts],
        out_specs=(pl.BlockSpec((1, tl, GROUP_W), fmap), pl.BlockSpec((1, tl, GROUP_W), bmap)),
        out_shape=(o, o),
        scratch_shapes=[st, st],
        compiler_params=_cparams("parallel", "arbitrary"),
        name="gla",
    )(zb, zs, zb, zs, *consts)


def _mlstm_chunk(zc_ref, ig_all, lf_all, tri_ref, i2_ref, bd_ref, causal, hm_ref, out_ref, s_ref, n_ref, m_ref, c):
    rows = slice(c * CHUNK, (c + 1) * CHUNK)
    qb = zc_ref[0, rows, 0:256]
    k = zc_ref[0, rows, 256:512].astype(F32) * (HEAD_W ** -0.5)
    kb = k.astype(BF16)
    vb = zc_ref[0, rows, 512:768]
    ig = ig_all[rows]
    lf = lf_all[rows]
    bcum = _dot_split(tri_ref[...], lf)
    blast = jnp.sum(lf, axis=0, keepdims=True)
    cc = jnp.sum((ig - bcum) * i2_ref[...], axis=0, keepdims=True)
    d2 = jnp.where(causal, bcum + cc, NEG_BIG)
    m_intra = jnp.full((CHUNK, GROUP_W), NEG_BIG, F32)
    for h in range(N_HEADS):
        hsel = hm_ref[h:h + 1, :] > 0.0
        mh = jnp.max(jnp.where(hsel, d2, NEG_BIG), axis=1, keepdims=True)
        m_intra = jnp.where(hsel, mh, m_intra)
    m_prev = m_ref[...]
    inter_log = bcum + m_prev
    m_t = jnp.maximum(inter_log, m_intra)
    w_intra = jnp.exp(d2 - m_t)
    w_inter = jnp.exp(inter_log - m_t)
    kexp = jnp.concatenate([kb] * N_HEADS, axis=0) * bd_ref[...]
    qk = (_dot_nt(qb, kexp) * w_intra).astype(BF16)
    vbd = jnp.concatenate([vb] * N_HEADS, axis=0) * bd_ref[...]
    s_prev = s_ref[...]
    n_prev = n_ref[...]
    sbd = jnp.concatenate([s_prev.astype(BF16)] * N_HEADS, axis=0) * bd_ref[...]
    nbd = jnp.concatenate([n_prev.astype(BF16)] * N_HEADS, axis=0) * bd_ref[...]
    num = _dot(qk, vbd) + w_inter * _dot(qb, sbd)
    den = _dot(qk, bd_ref[...]) + w_inter * _dot(qb, nbd)
    out_ref[0, rows, :] = (num / jnp.maximum(jnp.abs(den), jnp.exp(-m_t))).astype(out_ref.dtype)
    g_end = blast - bcum + ig
    g_max = jnp.max(g_end, axis=0, keepdims=True)
    kw = (k * jnp.exp(g_end - g_max)).astype(BF16)
    m_new = jnp.maximum(blast + m_prev, g_max)
    a = jnp.exp(blast + m_prev - m_new)
    cf = jnp.exp(g_max - m_new)
    uc_full = _dot_tn(kw, vb)
    un_full = _dot_tn(kw, jnp.ones((CHUNK, GROUP_W), BF16))
    uc = jnp.zeros((HEAD_W, GROUP_W), F32)
    un = jnp.zeros((HEAD_W, GROUP_W), F32)
    for h in range(N_HEADS):
        hsel = hm_ref[h:h + 1, :] > 0.0
        blk = slice(h * HEAD_W, (h + 1) * HEAD_W)
        uc = jnp.where(hsel, uc_full[blk], uc)
        un = jnp.where(hsel, un_full[blk], un)
    s_ref[...] = a * s_prev + cf * uc
    n_ref[...] = a * n_prev + cf * un
    m_ref[...] = m_new


def _mlstm_kernel(zcf_ref, zsf_ref, zcb_ref, zsb_ref, ef_ref, ebf_ref, eb_ref, ebb_ref, trif_ref, trib_ref,
                  i2_ref, bd_ref, cmf_ref, cmb_ref, hm_ref, outf_ref, outb_ref,
                  sf_ref, nf_ref, mf_ref, sb_ref, nb_ref, mb_ref, *, tl):
    @pl.when(pl.program_id(1) == 0)
    def _():
        for s_ref, n_ref, m_ref in ((sf_ref, nf_ref, mf_ref), (sb_ref, nb_ref, mb_ref)):
            s_ref[...] = jnp.zeros_like(s_ref)
            n_ref[...] = jnp.zeros_like(n_ref)
            m_ref[...] = jnp.full_like(m_ref, NEG_BIG)

    def gates(zs_ref, e_ref, eb_ref):
        ge = _split_dot(zs_ref[0], e_ref[...]) + eb_ref[...]
        return ge[:, :GROUP_W], _log_sigmoid(ge[:, GROUP_W:])

    igf, lff = gates(zsf_ref, ef_ref, ebf_ref)
    igb, lfb = gates(zsb_ref, eb_ref, ebb_ref)
    causal_f = cmf_ref[...] > 0.0
    causal_b = cmb_ref[...] > 0.0
    nch = tl // CHUNK
    for c in range(nch):
        _mlstm_chunk(zcf_ref, igf, lff, trif_ref, i2_ref, bd_ref, causal_f, hm_ref, outf_ref,
                     sf_ref, nf_ref, mf_ref, c)
        _mlstm_chunk(zcb_ref, igb, lfb, trib_ref, i2_ref, bd_ref, causal_b, hm_ref, outb_ref,
                     sb_ref, nb_ref, mb_ref, nch - 1 - c)


def _mlstm(zc, zs, prep_f, prep_b, tl=1024):
    b, l, _ = zc.shape
    nt = l // tl
    mf, mb = _chunk_masks(False), _chunk_masks(True)
    consts = (*prep_f, *prep_b, mf['tri'], mb['tri'], mf['i2'], mf['bd'], mf['cmask'], mb['cmask'], mf['hm'])
    fmap = lambda bi, i: (bi, i, 0)
    bmap = lambda bi, i: (bi, nt - 1 - i, 0)
    o = jax.ShapeDtypeStruct((b, l, GROUP_W), BF16)
    state = [pltpu.VMEM((HEAD_W, GROUP_W), F32), pltpu.VMEM((HEAD_W, GROUP_W), F32), pltpu.VMEM((1, GROUP_W), F32)]
    return pl.pallas_call(
        functools.partial(_mlstm_kernel, tl=tl),
        grid=(b, nt),
        in_specs=[pl.BlockSpec((1, tl, ZC_W), fmap), pl.BlockSpec((1, tl, ZS_W), fmap),
                  pl.BlockSpec((1, tl, ZC_W), bmap), pl.BlockSpec((1, tl, ZS_W), bmap)]
        + [_const_spec(c) for c in consts],
        out_specs=(pl.BlockSpec((1, tl, GROUP_W), fmap), pl.BlockSpec((1, tl, GROUP_W), bmap)),
        out_shape=(o, o),
        scratch_shapes=state + state,
        compiler_params=_cparams("parallel", "arbitrary"),
        name="mlstm",
    )(zc, zs, zc, zs, *consts)


def _mixer_prep(P, l):
    out = {}
    gw, gb, lam = P['a_gate_w'][l], P['a_gate_b'][l], P['a_lambda'][l]
    for d in range(2):
        w = jnp.concatenate([_block_diag(gw[d, 0]), _block_diag(gw[d, 1])], axis=1).astype(BF16)
        bb = jnp.concatenate([gb[d, 0].reshape(1, -1), gb[d, 1].reshape(1, -1)], axis=1).astype(F32)
        nl = -lam[d].astype(F32)
        sp = (jnp.maximum(nl, 0.0) + jnp.log(1.0 + jnp.exp(-jnp.abs(nl)))).reshape(1, -1)
        out['a', d] = (w, bb, sp)
        bw = jnp.zeros((ZS_W, N_HEADS * B_DK), F32).at[B_RANK * d:B_RANK * (d + 1)].set(P['b_gate_w'][l][d])
        out['b', d] = (bw.astype(BF16), P['b_gate_b'][l][d].reshape(1, -1).astype(F32))
        e = np.zeros((ZS_W, 2 * GROUP_W), np.float32)
        for h in range(N_HEADS):
            e[32 + 8 * d + h, h * HEAD_W:(h + 1) * HEAD_W] = 1.0
            e[36 + 8 * d + h, GROUP_W + h * HEAD_W:GROUP_W + (h + 1) * HEAD_W] = 1.0
        cb = P['c_gate_b'][l].astype(F32)
        eb = jnp.concatenate([jnp.repeat(cb[2 * d], HEAD_W), jnp.repeat(cb[2 * d + 1], HEAD_W)]).reshape(1, -1)
        out['c', d] = (jnp.asarray(e, BF16), eb)
    return out


G_ROWS = 8
G_ROWS_BF16 = 16


def _slab(k):
    return slice(k * V7X_LANES, (k + 1) * V7X_LANES)


def _scatter_rows(scr, base, val, rows, pitch, blk0=0):
    for s in range(val.shape[1] // V7X_LANES):
        for b in range(val.shape[0] // rows):
            r0 = (blk0 + b) * pitch
            scr[base + s, r0:r0 + rows, :] = val[b * rows:(b + 1) * rows, _slab(s)]


def _gather_n2_major(scr, outs, n2, pitch):
    nsl = GROUP_W // V7X_LANES

    def body(j, carry):
        for a, (ref, lead) in enumerate(outs):
            for s in range(nsl):
                ref[lead, j, :, _slab(s)] = scr[a * nsl + s, pl.ds(j, G_ROWS, stride=pitch), :]
        return carry

    lax.fori_loop(0, n2, body, 0, unroll=8)


def _conv3_kernel(prev_ref, cur_ref, next_ref, w_ref, b_ref, v_ref, x1_ref, x2_ref, scr, *, n2, pitch):
    i = pl.program_id(1)
    u = _halo_conv(prev_ref[0], cur_ref[0], next_ref[0], w_ref[...], b_ref[...], -1, i == 0,
                   i == pl.num_programs(1) - 1)
    nsl = GROUP_W // V7X_LANES
    for a in range(3):
        _scatter_rows(scr, a * nsl, u[:, a * GROUP_W:(a + 1) * GROUP_W], n2, pitch)
    _gather_n2_major(scr, ((v_ref, 0), (x1_ref, 0), (x2_ref, 0)), n2, pitch)


def _conv3_split(zd, w, b, n1, n2):
    bsz, l, _ = zd.shape
    tl = G_ROWS * n2
    pitch = _strided_pitch(n2)
    nt, rh, nbh = l // tl, tl // HALO, l // HALO
    o = jax.ShapeDtypeStruct((bsz, n2, n1 // 2, GROUP_W), F32)
    ospec = pl.BlockSpec((1, n2, G_ROWS, GROUP_W), lambda bi, i: (bi, 0, i, 0))
    return pl.pallas_call(
        functools.partial(_conv3_kernel, n2=n2, pitch=pitch),
        grid=(bsz, nt),
        in_specs=[pl.BlockSpec((1, HALO, ZD_W), lambda bi, i: (bi, jnp.maximum(i * rh - 1, 0), 0)),
                  pl.BlockSpec((1, tl, ZD_W), lambda bi, i: (bi, i, 0)),
                  pl.BlockSpec((1, HALO, ZD_W), lambda bi, i: (bi, jnp.minimum((i + 1) * rh, nbh - 1), 0)),
                  pl.BlockSpec((3, ZD_W), lambda bi, i: (0, 0)),
                  pl.BlockSpec((1, ZD_W), lambda bi, i: (0, 0))],
        out_specs=(ospec, ospec, ospec),
        out_shape=(o, o, o),
        scratch_shapes=[pltpu.VMEM((3 * GROUP_W // V7X_LANES, G_ROWS * pitch, V7X_LANES), F32)],
        compiler_params=_cparams("parallel", "parallel"),
        name="hyena_conv3",
    )(zd, zd, zd, w, b.reshape(1, ZD_W))


def _dot_hp(a, b):
    ah, bh = a.astype(BF16), b.astype(BF16)
    al = (a - ah.astype(F32)).astype(BF16)
    bl = (b - bh.astype(F32)).astype(BF16)
    return _dot(ah, bh) + _dot(al, bh) + _dot(ah, bl)


FILTER_SUB = 512


def _filter_mlp_kernel(emb_ref, w1_ref, b1_ref, f0_ref, w2_ref, b2_ref, f1_ref, w3_ref, dl_ref,
                       fwd_ref, sec_ref, s_ref, scr, *, n2, pitch):
    i = pl.program_id(0)
    tl = emb_ref.shape[0]
    nsl = GROUP_W // V7X_LANES

    @pl.when(i == 0)
    def _():
        s_ref[...] = jnp.zeros_like(s_ref)

    for sub in range(tl // FILTER_SUB):
        emb = emb_ref[sub * FILTER_SUB:(sub + 1) * FILTER_SUB, :]
        h = jnp.sin(f0_ref[...] * (_dot_hp(emb, w1_ref[...]) + b1_ref[...]))
        h = jnp.sin(f1_ref[...] * (_dot_hp(h, w2_ref[...]) + b2_ref[...]))
        h = _dot_hp(h, w3_ref[...])
        dl = dl_ref[...]
        hf = h[:, :2 * GROUP_W] * jnp.exp(-emb[:, 0:1] * dl)
        hb = h[:, 2 * GROUP_W:] * jnp.exp(-emb[:, HY_EMB:HY_EMB + 1] * dl)
        if sub == 0:
            row = _iota(hb.shape, 0) + i * tl
            hb = jnp.where(row == 0, 0.0, hb)
        s_ref[...] += jnp.sum(jnp.abs(hf) + jnp.abs(hb), axis=0, keepdims=True)
        blk0 = sub * FILTER_SUB // n2
        for o in range(2):
            _scatter_rows(scr, o * nsl, hf[:, o * GROUP_W:(o + 1) * GROUP_W], n2, pitch, blk0)
            _scatter_rows(scr, (2 + o) * nsl, hb[:, o * GROUP_W:(o + 1) * GROUP_W], n2, pitch, blk0)
    _gather_n2_major(scr, ((fwd_ref, 0), (fwd_ref, 1), (sec_ref, 0), (sec_ref, 1)), n2, pitch)


def _hyena_filter(l, w1, b1, freq, w2, b2, w3, n1, n2):
    tl = G_ROWS * n2
    bands = (HY_EMB - 1) // 2
    f = jnp.linspace(1e-4, bands - 1, bands, dtype=F32)

    def embed(pos):
        t = pos / max(l - 1, 1)
        ang = (2.0 * math.pi / l) * pos[:, None] * f[None, :]
        return jnp.concatenate([t[:, None], jnp.cos(ang), -jnp.sin(ang)], axis=-1)

    pos = jnp.arange(l, dtype=F32)
    emb = jnp.concatenate([embed(pos), embed(l - pos), jnp.zeros((l, V7X_LANES - 2 * HY_EMB), F32)], axis=-1)
    deltas = jnp.abs(jnp.linspace(math.log(HY_TARGET) / HY_SLOW, math.log(HY_TARGET) / HY_FAST, GROUP_W, dtype=F32))
    z = lambda r, c: jnp.zeros((r, c), F32)
    w1f, w2f = w1.astype(F32), w2.astype(F32)
    w3r = w3.astype(F32).reshape(HY_HID, 2, 2, GROUP_W)
    w3d = lambda d: w3r[:, :, d].reshape(HY_HID, 2 * GROUP_W)
    w1p = z(V7X_LANES, V7X_LANES).at[:HY_EMB, :HY_HID].set(w1f).at[HY_EMB:2 * HY_EMB, HY_HID:].set(w1f)
    w2p = z(V7X_LANES, V7X_LANES).at[:HY_HID, :HY_HID].set(w2f).at[HY_HID:, HY_HID:].set(w2f)
    w3p = z(V7X_LANES, 4 * GROUP_W).at[:HY_HID, :2 * GROUP_W].set(w3d(0)).at[HY_HID:, 2 * GROUP_W:].set(w3d(1))
    vec = lambda x: jnp.tile(x.astype(F32), 2).reshape(1, V7X_LANES)
    consts = (w1p, vec(b1), vec(freq[0]), w2p, vec(b2), vec(freq[1]), w3p, jnp.tile(deltas, 2).reshape(1, -1))
    pitch = _strided_pitch(n2)
    half = jax.ShapeDtypeStruct((2, n2, n1 // 2, GROUP_W), F32)
    hspec = pl.BlockSpec((2, n2, G_ROWS, GROUP_W), lambda i: (0, 0, i, 0))
    return pl.pallas_call(
        functools.partial(_filter_mlp_kernel, n2=n2, pitch=pitch),
        grid=(l // tl,),
        in_specs=[pl.BlockSpec((tl, V7X_LANES), lambda i: (i, 0))] + [_const_spec(c) for c in consts],
        out_specs=(hspec, hspec, pl.BlockSpec((1, 2 * GROUP_W), lambda i: (0, 0))),
        out_shape=(half, half, jax.ShapeDtypeStruct((1, 2 * GROUP_W), F32)),
        scratch_shapes=[pltpu.VMEM((4 * GROUP_W // V7X_LANES, G_ROWS * pitch, V7X_LANES), F32)],
        compiler_params=_cparams("arbitrary"),
        name="hyena_filter_mlp",
    )(emb, *consts)


def _fft_factors(l):
    return (64, 128) if l == 4096 else (2 * l // 256, 256)


@functools.lru_cache(maxsize=None)
def _fft_tables(n1, n2):
    n = n1 * n2
    k1 = np.arange(n1)[None, :, None]
    m2 = np.arange(n2)[:, None, None]

    def theta(n1_vals):
        idx = (k1 * (n1_vals[None, None, :] * n2 + m2)) % n
        return (2.0 * np.pi / n) * idx
    th = theta(np.arange(n1 // 2))
    fr, fi = np.cos(th), -np.sin(th)
    fwd = np.concatenate([np.concatenate([fr, -fi], -1), np.concatenate([fi, fr], -1)], -2)
    thf = theta(np.arange(n1))
    filt = np.concatenate([np.cos(thf), -np.sin(thf)], -2)
    cr = np.swapaxes(np.cos(th), 1, 2) / n
    ci = np.swapaxes(np.sin(th), 1, 2) / n
    inv = np.concatenate([np.concatenate([cr, -ci], -1), np.concatenate([ci, cr], -1)], -2)
    t2 = (2.0 * np.pi / n2) * ((np.arange(n2)[:, None] * np.arange(n2)[None, :]) % n2)
    gr, gi = np.cos(t2), -np.sin(t2)
    g = np.block([[gr, -gi], [gi, gr]])
    ginv = np.block([[gr, gi], [-gi, gr]])
    return tuple(jnp.asarray(a, BF16) for a in (fwd, filt, inv, g, ginv))


def _regroup_rows(scr, n_rows, pitch, store):
    def body(r, carry):
        for s in range(GROUP_W // V7X_LANES):
            store(r, s, scr[s, pl.ds(r, G_ROWS_BF16, stride=pitch), :])
        return carry

    lax.fori_loop(0, n_rows, body, 0, unroll=8)


def _fft_stage1_kernel(zr_ref, zi_ref, t_ref, a_ref, scr, *, n1, pitch):
    for j in range(G_ROWS_BF16):
        rhs = jnp.concatenate([zr_ref[0, j], zi_ref[0, j]], axis=0).astype(BF16)
        _scatter_rows(scr, 0, _dot(t_ref[j], rhs), 2 * n1, pitch, j)

    def store(r, s, rows):
        a_ref[0, r, :, _slab(s)] = rows.astype(BF16)

    _regroup_rows(scr, 2 * n1, pitch, store)


def _fft_stage1(zr, zi, table, n1, n2, pairs, idx_r, idx_i):
    nb = G_ROWS_BF16
    pitch = _strided_pitch(2 * n1)
    return pl.pallas_call(
        functools.partial(_fft_stage1_kernel, n1=n1, pitch=pitch),
        grid=(pairs, n2 // nb),
        in_specs=[pl.BlockSpec((1, nb, n1 // 2, GROUP_W), lambda p, j: (idx_r(p), j, 0, 0)),
                  pl.BlockSpec((1, nb, n1 // 2, GROUP_W), lambda p, j: (idx_i(p), j, 0, 0)),
                  pl.BlockSpec((nb, 2 * n1, n1), lambda p, j: (j, 0, 0))],
        out_specs=pl.BlockSpec((1, 2 * n1, nb, GROUP_W), lambda p, j: (p, 0, j, 0)),
        out_shape=jax.ShapeDtypeStruct((pairs, 2 * n1, n2, GROUP_W), BF16),
        scratch_shapes=[pltpu.VMEM((GROUP_W // V7X_LANES, nb * pitch, V7X_LANES), F32)],
        compiler_params=_cparams("parallel", "parallel"),
        name="fft_stage1",
    )(zr, zi, table)


def _a_specs(n1, n2, kb, index):
    def spec(part):
        def imap(i, p):
            pp, kk = index(i, p)
            return (pp, part * (n1 // kb) + kk, 0, 0)
        return pl.BlockSpec((1, kb, n2, GROUP_W), imap)
    return [spec(0), spec(1)]


def _fft_spec_kernel(ar_ref, ai_ref, g_ref, s_ref, h_ref, *, kb, n2):
    for k in range(kb):
        rhs = jnp.concatenate([ar_ref[0, k], ai_ref[0, k]], axis=0)
        x = _dot(g_ref[...], rhs) * s_ref[0]
        h_ref[0, k, 0] = x[:n2]
        h_ref[0, k, 1] = x[n2:]


def _fft_filter_spectrum(a, g, inv_s, n1, n2, kb=8):
    orders = a.shape[0]
    return pl.pallas_call(
        functools.partial(_fft_spec_kernel, kb=kb, n2=n2),
        grid=(orders, n1 // kb),
        in_specs=_a_specs(n1, n2, kb, lambda o, i: (o, i))
        + [_const_spec(g), pl.BlockSpec((1, 1, GROUP_W), lambda o, i: (o, 0, 0))],
        out_specs=pl.BlockSpec((1, kb, 2, n2, GROUP_W), lambda o, i: (o, i, 0, 0, 0)),
        out_shape=jax.ShapeDtypeStruct((orders, n1, 2, n2, GROUP_W), F32),
        compiler_params=_cparams("parallel", "parallel"),
        name="fft_filter_spectrum",
    )(a, a, g, inv_s)


def _fft_mid_kernel(ar_ref, ai_ref, g_ref, gi_ref, h_ref, b_ref, scr, *, n2, pitch):
    for k in range(G_ROWS_BF16):
        rhs = jnp.concatenate([ar_ref[0, k], ai_ref[0, k]], axis=0)
        x = _dot(g_ref[...], rhs)
        xr, xi = x[:n2], x[n2:]
        hr, hi = h_ref[0, k, 0], h_ref[0, k, 1]
        y = jnp.concatenate([xr * hr - xi * hi, xr * hi + xi * hr], axis=0).astype(BF16)
        _scatter_rows(scr, 0, _dot(gi_ref[...], y), 2 * n2, pitch, k)

    def store(r, s, rows):
        b_ref[0, r, :, _slab(s)] = rows.astype(BF16)

    _regroup_rows(scr, 2 * n2, pitch, store)


def _fft_mid(a, g, ginv, spec, order, n1, n2):
    pairs = a.shape[0]
    kb = G_ROWS_BF16
    pitch = _strided_pitch(2 * n2)
    return pl.pallas_call(
        functools.partial(_fft_mid_kernel, n2=n2, pitch=pitch),
        grid=(n1 // kb, pairs),
        in_specs=_a_specs(n1, n2, kb, lambda i, p: (p, i)) + [_const_spec(g), _const_spec(ginv),
                  pl.BlockSpec((1, kb, 2, n2, GROUP_W), lambda i, p: (order, i, 0, 0, 0))],
        out_specs=pl.BlockSpec((1, 2 * n2, kb, GROUP_W), lambda i, p: (p, 0, i, 0)),
        out_shape=jax.ShapeDtypeStruct((pairs, 2 * n2, n1, GROUP_W), BF16),
        scratch_shapes=[pltpu.VMEM((GROUP_W // V7X_LANES, kb * pitch, V7X_LANES), F32)],
        compiler_params=_cparams("parallel", "parallel"),
        name="fft_mid",
    )(a, a, g, ginv, spec)


def _fft_stage3_kernel(br_ref, bi_ref, t_ref, ur_ref, ui_ref, gr_ref, gi_ref, sk_ref, y_ref, *scr,
                       n1, pitch, time_major):
    h = n1 // 2
    sk = sk_ref[...]
    for j in range(G_ROWS_BF16):
        rhs = jnp.concatenate([br_ref[0, j], bi_ref[0, j]], axis=0)
        o = _dot(t_ref[j], rhs)
        yr = gr_ref[0, j] * (o[:h] + ur_ref[0, j] * sk)
        yi = gi_ref[0, j] * (o[h:] + ui_ref[0, j] * sk)
        if time_major:
            _scatter_rows(scr[0], 0, yr, h, pitch, j)
            _scatter_rows(scr[1], 0, yi, h, pitch, j)
        else:
            y_ref[0, 0, j] = yr
            y_ref[0, 1, j] = yi
    if time_major:
        for part in range(2):
            def store(r, s, rows, part=part):
                y_ref[0, part, r, :, _slab(s)] = rows.astype(y_ref.dtype)
            _regroup_rows(scr[part], h, pitch, store)


def _fft_stage3(b, table, u, gate, skip, n1, n2, time_major):
    pairs = b.shape[0]
    nb = G_ROWS_BF16
    h = n1 // 2
    pitch = _strided_pitch(h)
    bspec = lambda part: pl.BlockSpec((1, nb, n1, GROUP_W), lambda p, j: (p, part * (n2 // nb) + j, 0, 0))
    even = pl.BlockSpec((1, nb, h, GROUP_W), lambda p, j: (2 * p, j, 0, 0))
    odd = pl.BlockSpec((1, nb, h, GROUP_W), lambda p, j: (2 * p + 1, j, 0, 0))
    if time_major:
        out_spec = pl.BlockSpec((1, 2, h, nb, GROUP_W), lambda p, j: (p, 0, 0, j, 0))
        out_shape = jax.ShapeDtypeStruct((pairs, 2, h, n2, GROUP_W), BF16)
        scratch = [pltpu.VMEM((GROUP_W // V7X_LANES, nb * pitch, V7X_LANES), F32)] * 2
    else:
        out_spec = pl.BlockSpec((1, 2, nb, h, GROUP_W), lambda p, j: (p, 0, j, 0, 0))
        out_shape = jax.ShapeDtypeStruct((pairs, 2, n2, h, GROUP_W), F32)
        scratch = []
    y = pl.pallas_call(
        functools.partial(_fft_stage3_kernel, n1=n1, pitch=pitch, time_major=time_major),
        grid=(pairs, n2 // nb),
        in_specs=[bspec(0), bspec(1), pl.BlockSpec((nb, n1, 2 * n1), lambda p, j: (j, 0, 0)),
                  even, odd, even, odd, pl.BlockSpec((1, GROUP_W), lambda p, j: (0, 0))],
        out_specs=out_spec,
        out_shape=out_shape,
        scratch_shapes=scratch,
        compiler_params=_cparams("parallel", "parallel"),
        name="fft_stage3_out" if time_major else "fft_stage3",
    )(b, b, table, u, u, gate, gate, skip.reshape(1, GROUP_W).astype(F32))
    return y.reshape(2 * pairs, h * n2, GROUP_W) if time_major else y.reshape(2 * pairs, n2, h, GROUP_W)


def _hyena_spectrum(l, w1, b1, freq, w2, b2, w3):
    n1, n2 = _fft_factors(l)
    _, t_filt, _, g, _ = _fft_tables(n1, n2)
    first, second, sums = _hyena_filter(l, w1, b1, freq, w2, b2, w3, n1, n2)
    a = _fft_stage1(first, second, t_filt, n1, n2, 2, lambda p: p, lambda p: p)
    inv_s = (1.0 / sums).reshape(2, 1, GROUP_W)
    return _fft_filter_spectrum(a, g, inv_s, n1, n2)


def _hyena_long_conv(u, gate, spec, order, skip, n1, n2, time_major):
    t_fwd, _, t_inv, g, ginv = _fft_tables(n1, n2)
    a = _fft_stage1(u, u, t_fwd, n1, n2, u.shape[0] // 2, lambda p: 2 * p, lambda p: 2 * p + 1)
    b = _fft_mid(a, g, ginv, spec, order, n1, n2)
    return _fft_stage3(b, t_inv, u, gate, skip, n1, n2, time_major)


def _mixer_hyena(zd, P, l, spec):
    n1, n2 = _fft_factors(zd.shape[1])
    v, x1, x2 = _conv3_split(zd, P['d_conv_w'][l].astype(F32), P['d_conv_b'][l].astype(F32), n1, n2)
    z = _hyena_long_conv(v, x1, spec, 0, P['d_skip'][l][0], n1, n2, False)
    return _hyena_long_conv(z, x2, spec, 1, P['d_skip'][l][1], n1, n2, True)


def _gelu_tanh(x):
    return 0.5 * x * (1.0 + jnp.tanh(math.sqrt(2.0 / math.pi) * (x + 0.044715 * (x * x * x))))


def _head_rms(h, bd):
    ss = _dot((h * h).astype(BF16), bd) * (1.0 / HEAD_W)
    return h * lax.rsqrt(ss + EPS)


def _out_proj_kernel(x_ref, af_ref, ab_ref, ag_ref, bf_ref, bb_ref, br_ref, cf_ref, cb_ref, co_ref, yd_ref,
                     w_ref, bg_ref, cg_ref, bd_ref, n2_ref, rwh_ref, rwl_ref, rb_ref,
                     xo_ref, hn_ref, lg_ref):
    bd = bd_ref[...]
    f = lambda ref: ref[...].astype(F32)
    ya = (f(af_ref) + f(ab_ref)) * _gelu_tanh(f(ag_ref))
    r = f(br_ref)
    yb = _head_rms(f(bf_ref) + f(bb_ref), bd) * bg_ref[...] * (r * _sigmoid(r))
    yc = _head_rms(f(cf_ref) + f(cb_ref), bd) * cg_ref[...] * _sigmoid(f(co_ref))
    mixed = jnp.concatenate([ya.astype(BF16), yb.astype(BF16), yc.astype(BF16), yd_ref[...]], axis=-1)
    x = x_ref[...] + _dot(mixed, w_ref[...])
    xo_ref[...] = x
    hn = x * lax.rsqrt(jnp.mean(x * x, axis=-1, keepdims=True) + EPS) * n2_ref[...]
    hn_ref[...] = hn.astype(BF16)
    hi = hn.astype(BF16)
    lo = (hn - hi.astype(F32)).astype(BF16)
    lg_ref[...] = _dot(hi, rwh_ref[...]) + _dot(lo, rwh_ref[...]) + _dot(hi, rwl_ref[...]) + rb_ref[...]


def _out_proj(x2d, af, ab, za, bf, bb, zb, cf, cb, zc, yd, w_out, b_norm_g, c_norm_g, norm2_g, rw, rb, tm=512):
    t = x2d.shape[0]
    bd = _chunk_masks(False)['bd']
    rwh = rw.astype(BF16)
    rwl = (rw - rwh.astype(F32)).astype(BF16)
    row = lambda w, j=0: pl.BlockSpec((tm, w), lambda i: (i, j))
    consts = (w_out, b_norm_g.reshape(1, -1).astype(F32), c_norm_g.reshape(1, -1).astype(F32), bd,
              norm2_g.reshape(1, -1).astype(F32), rwh, rwl, rb)
    return pl.pallas_call(
        _out_proj_kernel,
        grid=(t // tm,),
        in_specs=[row(D_MODEL), row(GROUP_W), row(GROUP_W), row(GROUP_W, 1), row(GROUP_W), row(GROUP_W),
                  row(GROUP_W, 2), row(GROUP_W), row(GROUP_W), row(GROUP_W, 3), row(GROUP_W)]
        + [_const_spec(c) for c in consts],
        out_specs=(row(D_MODEL), row(D_MODEL), row(V7X_LANES)),
        out_shape=(jax.ShapeDtypeStruct((t, D_MODEL), F32), jax.ShapeDtypeStruct((t, D_MODEL), BF16),
                   jax.ShapeDtypeStruct((t, V7X_LANES), F32)),
        compiler_params=_cparams("parallel"),
        name="out_proj",
    )(x2d, af, ab, za, bf, bb, zb, cf, cb, zc, yd, *consts)


def _route(logits):
    lane = _iota(logits.shape, 1).astype(F32)
    is_g = lane < MOE_GROUPS
    gl = jnp.where(is_g, logits, NEG_BIG)
    gmax = jnp.max(gl, axis=1, keepdims=True)
    gidx = jnp.min(jnp.where(gl == gmax, lane, 1e9), axis=1, keepdims=True)
    gprob = 1.0 / jnp.sum(jnp.where(is_g, jnp.exp(gl - gmax), 0.0), axis=1, keepdims=True)
    lo = MOE_GROUPS + MOE_PER_GROUP * gidx
    el = jnp.where((lane >= lo) & (lane < lo + MOE_PER_GROUP), logits, NEG_BIG)
    v1 = jnp.max(el, axis=1, keepdims=True)
    i1 = jnp.min(jnp.where(el == v1, lane, 1e9), axis=1, keepdims=True)
    el2 = jnp.where(lane == i1, NEG_BIG, el)
    v2 = jnp.max(el2, axis=1, keepdims=True)
    i2 = jnp.min(jnp.where(el2 == v2, lane, 1e9), axis=1, keepdims=True)
    e21 = jnp.exp(v2 - v1)
    p1 = 1.0 / (1.0 + e21)
    return jnp.where(lane == i1, p1 * gprob, 0.0) + jnp.where(lane == i2, e21 * p1 * gprob, 0.0), gidx


MOE_TB = 1024
MOE_RT = 32
MOE_MM = 8 * MOE_RT
MOE_SB = MOE_TB + MOE_GROUPS * MOE_RT


def _moe_kernel(hn_ref, lg_ref, tril_ref, wg_ref, wu_ref, wd_ref, o_ref, xs_ref, ys_ref, gs_ref, pos_ref, meta_ref):
    grp = pl.program_id(1)
    tb, sb, rt = MOE_TB, MOE_SB, MOE_RT

    @pl.when(grp == 0)
    def _():
        gates, gidx = _route(lg_ref[...])
        lane = _iota((tb, V7X_LANES), 1).astype(F32)
        ohg = jnp.where(lane == gidx, 1.0, 0.0)
        rank = jnp.sum(_dot(tril_ref[...], ohg.astype(BF16)) * ohg, axis=1, keepdims=True)
        cnt = jnp.sum(ohg, axis=0, keepdims=True)
        pc = jnp.ceil(cnt * (1.0 / rt)) * rt
        lane1 = _iota((1, V7X_LANES), 1)
        off = jnp.zeros((1, V7X_LANES), F32)
        run = jnp.zeros((1, 1), F32)
        for g in range(MOE_GROUPS):
            off = jnp.where(lane1 == g, run, off)
            run = run + jnp.sum(jnp.where(lane1 == g, pc, 0.0), axis=1, keepdims=True)
        pos = jnp.sum(ohg * off, axis=1, keepdims=True) + rank
        pos_ref[...] = jnp.broadcast_to(pos, (tb, V7X_LANES))
        meta_ref[0:1, :] = off
        meta_ref[1:2, :] = pc
        hi = jnp.floor(pos * (1.0 / 64.0))
        lo = pos - 64.0 * hi
        pm = jnp.where(lane == 0.0, hi, jnp.where(lane == 1.0, lo, 0.0)).astype(BF16)
        sel = jnp.where(_iota((8, V7X_LANES), 0) == _iota((8, V7X_LANES), 1), 1.0, 0.0).astype(BF16)
        rows8 = _dot_nt(sel, pm)
        pos_row = 64.0 * rows8[0:1, :] + rows8[1:2, :]
        p = jnp.where(_iota((sb, tb), 0).astype(F32) == pos_row, 1.0, 0.0).astype(BF16)
        xs_ref[...] = _dot(p, hn_ref[...]).astype(BF16)
        gh = gates.astype(BF16)
        gl = (gates - gh.astype(F32)).astype(BF16)
        gsorted = _dot(p, jnp.concatenate([gh, gl], axis=1))
        gs_ref[...] = gsorted[:, :V7X_LANES] + gsorted[:, V7X_LANES:]
        ys_ref[...] = jnp.zeros_like(ys_ref)

    lane1 = _iota((1, V7X_LANES), 1)
    start = jnp.sum(jnp.where(lane1 == grp, meta_ref[0:1, :], 0.0)).astype(jnp.int32)
    rows = jnp.sum(jnp.where(lane1 == grp, meta_ref[1:2, :], 0.0)).astype(jnp.int32)

    def experts(r0, nrows):
        xt = xs_ref[pl.ds(r0, nrows), :]
        gsl = gs_ref[pl.ds(r0, nrows), :]
        lane_t = _iota((nrows, V7X_LANES), 1)
        acc = None
        for j in range(MOE_PER_GROUP):
            g = _dot(xt, wg_ref[j])
            a = (g * _sigmoid(g)) * _dot(xt, wu_ref[j])
            y = _dot(a.astype(BF16), wd_ref[j])
            w = jnp.sum(jnp.where(lane_t == MOE_GROUPS + MOE_PER_GROUP * grp + j, gsl, 0.0), axis=1, keepdims=True)
            acc = w * y if acc is None else acc + w * y
        ys_ref[pl.ds(r0, nrows), :] = acc.astype(BF16)

    ntiles = rows // rt
    per_mm = MOE_MM // rt

    def full(i, carry):
        experts(pl.multiple_of(start + i * MOE_MM, rt), MOE_MM)
        return carry

    nfull = jnp.maximum(ntiles // per_mm - 1, 0)
    lax.fori_loop(0, nfull, full, 0)
    last = ntiles - nfull * per_mm
    for k in range(1, 2 * per_mm):
        @pl.when(last == k)
        def _(k=k):
            experts(pl.multiple_of(start + nfull * MOE_MM, rt), k * rt)

    @pl.when(grp == MOE_GROUPS - 1)
    def _():
        pt = jnp.where(_iota((tb, sb), 1).astype(F32) == pos_ref[:, 0:1], 1.0, 0.0).astype(BF16)
        o_ref[...] = _dot(pt, ys_ref[...]).astype(o_ref.dtype)


def _moe(hn, logits, w_gate, w_up, w_down):
    t = hn.shape[0]
    tm = MOE_TB
    tril = jnp.asarray(np.tril(np.ones((tm, tm), np.float32), -1), BF16)
    row = lambda w: pl.BlockSpec((tm, w), lambda i, g: (i, 0))
    wspec = lambda a, b: pl.BlockSpec((MOE_PER_GROUP, a, b), lambda i, g: (g, 0, 0))
    return pl.pallas_call(
        _moe_kernel,
        grid=(t // tm, MOE_GROUPS),
        in_specs=[row(D_MODEL), row(V7X_LANES), _const_spec(tril), wspec(D_MODEL, D_EXPERT),
                  wspec(D_MODEL, D_EXPERT), wspec(D_EXPERT, D_MODEL)],
        out_specs=row(D_MODEL),
        out_shape=jax.ShapeDtypeStruct((t, D_MODEL), BF16),
        scratch_shapes=[pltpu.VMEM((MOE_SB, D_MODEL), BF16), pltpu.VMEM((MOE_SB, D_MODEL), BF16),
                        pltpu.VMEM((MOE_SB, V7X_LANES), F32), pltpu.VMEM((tm, V7X_LANES), F32),
                        pltpu.VMEM((8, V7X_LANES), F32)],
        compiler_params=_cparams("parallel", "arbitrary"),
        name="moe",
    )(hn, logits, tril, w_gate, w_up, w_down)


def _layer(x2d, moe_prev, bsz, seq, l, P, W, spec):
    x2d, za, zb, zc, zd, zs = _in_proj(x2d, moe_prev, P['norm1_g'][l].astype(F32), W['w_in'][l])
    r3 = lambda a: a.reshape(bsz, seq, a.shape[-1])
    r2 = lambda a: a.reshape(bsz * seq, a.shape[-1])
    mp = W['mix'][l]
    cw, cb = P['a_conv_w'][l].astype(F32), P['a_conv_b'][l].reshape(1, -1).astype(F32)
    za3, zb3, zc3, zs3 = r3(za), r3(zb), r3(zc), r3(zs)
    af = _rglru(za3, cw, cb, *mp['a', 0], False)
    ab = _rglru(za3, cw, cb, *mp['a', 1], True)
    bf, bb = _gla(zb3, zs3, mp['b', 0], mp['b', 1])
    cf, cbk = _mlstm(zc3, zs3, mp['c', 0], mp['c', 1])
    yd = _mixer_hyena(r3(zd), P, l, spec)
    x_new, hn, logits = _out_proj(x2d, r2(af), r2(ab), za, r2(bf), r2(bb), zb, r2(cf), r2(cbk), zc, r2(yd),
                                  W['w_out'][l], P['b_norm_g'][l], P['c_norm_g'][l], P['norm2_g'][l],
                                  W['router_w'][l], W['router_b'][l])
    return x_new, _moe(hn, logits, W['w_gate'][l], W['w_up'][l], W['w_down'][l])


def _trunk(x, P, W):
    bsz, seq, _ = x.shape
    depth = P['w_in'].shape[0]
    x2d = x.reshape(bsz * seq, D_MODEL)
    moe_term = None
    for l in range(depth):
        spec = _hyena_spectrum(seq, P['d_ffn_w1'][l], P['d_ffn_b1'][l], P['d_sin_freq'][l], P['d_ffn_w2'][l],
                               P['d_ffn_b2'][l], P['d_ffn_w3'][l])
        x2d, moe_term = _layer(x2d, moe_term, bsz, seq, l, P, W, spec)
    return _final_norm(x2d, moe_term, P['final_norm_g']).reshape(bsz, seq, D_MODEL)


def _prep_weights(P):
    depth = P['w_in'].shape[0]
    W = {'w_in': [_permute_w_in(P['w_in'][l]) for l in range(depth)],
         'w_out': [P['w_out'][l].astype(BF16) for l in range(depth)],
         'mix': [_mixer_prep(P, l) for l in range(depth)],
         'w_gate': [P['moe_w_gate'][l].astype(BF16) for l in range(depth)],
         'w_up': [P['moe_w_up'][l].astype(BF16) for l in range(depth)],
         'w_down': [P['moe_w_down'][l].astype(BF16) for l in range(depth)],
         'router_w': [], 'router_b': []}
    for l in range(depth):
        rw = jnp.concatenate([P['moe_group_w'][l], P['moe_expert_w'][l]], axis=1).astype(F32)
        rb = jnp.concatenate([P['moe_group_b'][l], P['moe_expert_b'][l]]).astype(F32)
        padc = V7X_LANES - rw.shape[1]
        W['router_w'].append(jnp.pad(rw, ((0, 0), (0, padc))))
        W['router_b'].append(jnp.pad(rb, (0, padc)).reshape(1, V7X_LANES))
    return W


def kernel(x_prompt, x_sample, norm1_g, w_in, a_conv_w, a_conv_b, a_gate_w, a_gate_b, a_lambda, b_gate_w, b_gate_b, b_norm_g, c_gate_b, c_norm_g, d_conv_w, d_conv_b, d_ffn_w1, d_ffn_b1, d_sin_freq, d_ffn_w2, d_ffn_b2, d_ffn_w3, d_skip, w_out, norm2_g, moe_group_w, moe_group_b, moe_expert_w, moe_expert_b, moe_w_gate, moe_w_up, moe_w_down, final_norm_g):
    P = {'norm1_g': norm1_g, 'w_in': w_in, 'a_conv_w': a_conv_w, 'a_conv_b': a_conv_b,
         'a_gate_w': a_gate_w, 'a_gate_b': a_gate_b, 'a_lambda': a_lambda, 'b_gate_w': b_gate_w,
         'b_gate_b': b_gate_b, 'b_norm_g': b_norm_g, 'c_gate_b': c_gate_b, 'c_norm_g': c_norm_g,
         'd_conv_w': d_conv_w, 'd_conv_b': d_conv_b, 'd_ffn_w1': d_ffn_w1, 'd_ffn_b1': d_ffn_b1,
         'd_sin_freq': d_sin_freq, 'd_ffn_w2': d_ffn_w2, 'd_ffn_b2': d_ffn_b2, 'd_ffn_w3': d_ffn_w3,
         'd_skip': d_skip, 'w_out': w_out, 'norm2_g': norm2_g, 'moe_group_w': moe_group_w,
         'moe_group_b': moe_group_b, 'moe_expert_w': moe_expert_w, 'moe_expert_b': moe_expert_b,
         'moe_w_gate': moe_w_gate, 'moe_w_up': moe_w_up, 'moe_w_down': moe_w_down,
         'final_norm_g': final_norm_g}
    W = _prep_weights(P)
    return (_trunk(x_prompt, P, W), _trunk(x_sample, P, W))
```

```python
import functools
import math

import jax
import jax.numpy as jnp
import numpy as np
from jax import lax
from jax.experimental import pallas as pl
from jax.experimental.pallas import tpu as pltpu

F32 = jnp.float32
BF16 = jnp.bfloat16

D_MODEL = 1024
GROUP_W = 256
N_HEADS = 4
HEAD_W = GROUP_W // N_HEADS
B_DK = 32
B_RANK = 16
CHUNK = 64
RG_C = 8.0
RG_SEGMENTS = 16
RG_UNROLL = 4
GLA_TAU = 16.0
HY_EMB = 33
HY_HID = 64
HY_TARGET = 1e-2
HY_FAST = 0.3
HY_SLOW = 1.5
MOE_GROUPS = 4
MOE_PER_GROUP = 4
MOE_EXPERTS = 16
D_EXPERT = 512
EPS = 1e-6
NEG_BIG = -1e30

V7X_LANES = 128
V7X_SUBLANES = 8
VMEM_LIMIT = 56 * 1024 * 1024

ZA_W, ZB_W, ZC_W, ZD_W, ZS_W = 512, 768, 1024, 768, 128
Z_SPLITS = (ZA_W, ZB_W, ZC_W, ZD_W, ZS_W)
Z_TOTAL = sum(Z_SPLITS)


def _cparams(*sem):
    return pltpu.CompilerParams(dimension_semantics=sem, vmem_limit_bytes=VMEM_LIMIT)


def _dot(a, b):
    return jnp.dot(a, b, preferred_element_type=F32)


def _dot_nt(a, b):
    return lax.dot_general(a, b, (((1,), (1,)), ((), ())), preferred_element_type=F32)


def _dot_tn(a, b):
    return lax.dot_general(a, b, (((0,), (0,)), ((), ())), preferred_element_type=F32)


def _dot_split(m_bf16, x):
    hi = x.astype(BF16)
    lo = (x - hi.astype(F32)).astype(BF16)
    return _dot(m_bf16, hi) + _dot(m_bf16, lo)


def _split_dot(x, m_bf16):
    hi = x.astype(BF16)
    lo = (x - hi.astype(F32)).astype(BF16)
    return _dot(hi, m_bf16) + _dot(lo, m_bf16)


def _sigmoid(x):
    return 1.0 / (1.0 + jnp.exp(-x))


def _log_sigmoid(x):
    return jnp.minimum(x, 0.0) - jnp.log(1.0 + jnp.exp(-jnp.abs(x)))


def _iota(shape, dim):
    return lax.broadcasted_iota(jnp.int32, shape, dim)


def _in_proj_kernel(*refs, add):
    if add:
        x_ref, m_ref, g_ref, w_ref, xo_ref, za_ref, zb_ref, zc_ref, zd_ref, zs_ref = refs
        x = x_ref[...] + m_ref[...].astype(F32)
        xo_ref[...] = x
    else:
        x_ref, g_ref, w_ref, za_ref, zb_ref, zc_ref, zd_ref, zs_ref = refs
        x = x_ref[...]
    ms = jnp.mean(x * x, axis=-1, keepdims=True)
    xn = (x * lax.rsqrt(ms + EPS) * g_ref[...]).astype(BF16)
    off = 0
    for ref, w in zip((za_ref, zb_ref, zc_ref, zd_ref, zs_ref), Z_SPLITS):
        ref[...] = _dot(xn, w_ref[:, off:off + w]).astype(ref.dtype)
        off += w


def _in_proj(x2d, moe_prev, g, w_perm, tm=512):
    t = x2d.shape[0]
    add = moe_prev is not None
    row = lambda w: pl.BlockSpec((tm, w), lambda i: (i, 0))
    outs = tuple(jax.ShapeDtypeStruct((t, w), F32 if w == ZS_W else BF16) for w in Z_SPLITS)
    out_specs = tuple(row(w) for w in Z_SPLITS)
    consts = (g.reshape(1, D_MODEL), w_perm)
    res = pl.pallas_call(
        functools.partial(_in_proj_kernel, add=add),
        grid=(t // tm,),
        in_specs=[row(D_MODEL)] * (2 if add else 1) + [_const_spec(c) for c in consts],
        out_specs=((row(D_MODEL),) if add else ()) + out_specs,
        out_shape=((jax.ShapeDtypeStruct((t, D_MODEL), F32),) if add else ()) + outs,
        compiler_params=_cparams("parallel"),
        name="in_proj",
    )(*((x2d, moe_prev) if add else (x2d,)), *consts)
    return tuple(res) if add else (x2d,) + tuple(res)


def _final_norm_kernel(x_ref, m_ref, g_ref, o_ref):
    x = x_ref[...] + m_ref[...].astype(F32)
    o_ref[...] = x * lax.rsqrt(jnp.mean(x * x, axis=-1, keepdims=True) + EPS) * g_ref[...]


def _final_norm(x2d, moe_prev, g, tm=512):
    t = x2d.shape[0]
    row = pl.BlockSpec((tm, D_MODEL), lambda i: (i, 0))
    return pl.pallas_call(
        _final_norm_kernel,
        grid=(t // tm,),
        in_specs=[row, row, pl.BlockSpec((1, D_MODEL), lambda i: (0, 0))],
        out_specs=row,
        out_shape=jax.ShapeDtypeStruct((t, D_MODEL), F32),
        compiler_params=_cparams("parallel"),
        name="final_norm",
    )(x2d, moe_prev, g.reshape(1, D_MODEL).astype(F32))


def _permute_w_in(w_in):
    sizes = (256, 256, 128, 128, 256, 256, 16, 16, 256, 256, 256, 256, 16, 768)
    offs = np.concatenate([[0], np.cumsum(sizes)])
    seg = [w_in[:, offs[i]:offs[i + 1]] for i in range(len(sizes))]
    (a_x, a_g, b_q, b_k, b_v, b_r, b_lf, b_lb, c_q, c_k, c_v, c_o, c_g, d_u) = seg
    small = jnp.concatenate([b_lf, b_lb, c_g, jnp.zeros((w_in.shape[0], ZS_W - 48), w_in.dtype)], axis=1)
    return jnp.concatenate([a_x, a_g, b_q, b_k, b_v, b_r, c_q, c_k, c_v, c_o, d_u, small], axis=1).astype(BF16)


HALO = 16


def _strided_pitch(n):
    assert n % V7X_SUBLANES == 0
    return n if (n // V7X_SUBLANES) % 2 == 1 else n + V7X_SUBLANES


def _halo_conv(prev, cur, nxt, w, b, lo, first, last):
    tl = cur.shape[0]
    prev = jnp.where(first, 0.0, prev.astype(F32))
    nxt = jnp.where(last, 0.0, nxt.astype(F32))
    ext = jnp.concatenate([prev, cur.astype(F32), nxt], axis=0)
    n = tl + 2 * HALO
    acc = None
    for j in range(w.shape[0]):
        o = lo + j
        sh = ext if o == 0 else pltpu.roll(ext, (-o) % n, 0)
        term = sh[HALO:HALO + tl] * w[j:j + 1, :]
        acc = term if acc is None else acc + term
    return acc + b


def _rglru_kernel(prev_ref, cur_ref, next_ref, cw_ref, cb_ref, gw_ref, gb_ref, sp_ref, out_ref,
                  a_s, b_s, h_s, p_s, carry, *, tl, reverse):
    i = pl.program_id(1)
    nt = pl.num_programs(1)
    ti = (nt - 1 - i) if reverse else i
    u = _halo_conv(prev_ref[0], cur_ref[0], next_ref[0], cw_ref[...], cb_ref[...], -2, ti == 0, ti == nt - 1)
    gates = _sigmoid(_dot(u.astype(BF16), gw_ref[...]) + gb_ref[...])
    r = gates[:, :GROUP_W]
    ig = gates[:, GROUP_W:]
    log_a = (-RG_C) * r * sp_ref[...]
    a = jnp.exp(log_a)
    bt = jnp.sqrt(1.0 - a * a) * (ig * u)
    nseg = RG_SEGMENTS
    m = tl // nseg
    mp, sp = _strided_pitch(m), _strided_pitch(nseg)
    for k in range(2):
        for seg in range(nseg):
            a_s[k, seg * mp:seg * mp + m, :] = a[seg * m:(seg + 1) * m, k * V7X_LANES:(k + 1) * V7X_LANES]
            b_s[k, seg * mp:seg * mp + m, :] = bt[seg * m:(seg + 1) * m, k * V7X_LANES:(k + 1) * V7X_LANES]

    @pl.when(i == 0)
    def _():
        carry[...] = jnp.zeros_like(carry)

    def body(jj, hp):
        j = (m - 1 - jj) if reverse else jj
        out = []
        for k in range(2):
            h, p = hp[2 * k], hp[2 * k + 1]
            av = a_s[k, pl.ds(j, nseg, stride=mp), :]
            bv = b_s[k, pl.ds(j, nseg, stride=mp), :]
            h = av * h + bv
            p = av * p
            rows = pl.ds(pl.multiple_of(j * sp, V7X_SUBLANES), nseg)
            h_s[k, rows, :] = h
            p_s[k, rows, :] = p
            out += [h, p]
        return tuple(out)

    z8 = jnp.zeros((nseg, V7X_LANES), F32)
    o8 = jnp.ones((nseg, V7X_LANES), F32)
    ends = lax.fori_loop(0, m, body, (z8, o8, z8, o8), unroll=RG_UNROLL)
    for k in range(2):
        h_end, p_end = ends[2 * k], ends[2 * k + 1]
        c = carry[k]
        cin = [None] * nseg
        for seg in (range(nseg - 1, -1, -1) if reverse else range(nseg)):
            cin[seg] = c
            c = h_end[seg:seg + 1, :] + p_end[seg:seg + 1, :] * c
        carry[k] = c
        for seg in range(nseg):
            hv = h_s[k, pl.ds(seg, m, stride=sp), :]
            pv = p_s[k, pl.ds(seg, m, stride=sp), :]
            out_ref[0, seg * m:(seg + 1) * m, k * V7X_LANES:(k + 1) * V7X_LANES] = (hv + pv * cin[seg]).astype(out_ref.dtype)


def _rglru(za, conv_w, conv_b, gate_w_bd, gate_b, softplus_neg_lam, reverse, tl=512):
    b, l, _ = za.shape
    nt = l // tl
    rh = tl // HALO
    nbh = l // HALO

    def tmap(bi, i):
        return (nt - 1 - i) if reverse else i

    kern = functools.partial(_rglru_kernel, tl=tl, reverse=reverse)
    return pl.pallas_call(
        kern,
        grid=(b, nt),
        in_specs=[
            pl.BlockSpec((1, HALO, GROUP_W), lambda bi, i: (bi, jnp.maximum(tmap(bi, i) * rh - 1, 0), 0)),
            pl.BlockSpec((1, tl, GROUP_W), lambda bi, i: (bi, tmap(bi, i), 0)),
            pl.BlockSpec((1, HALO, GROUP_W), lambda bi, i: (bi, jnp.minimum((tmap(bi, i) + 1) * rh, nbh - 1), 0)),
            pl.BlockSpec((4, GROUP_W), lambda bi, i: (0, 0)),
            pl.BlockSpec((1, GROUP_W), lambda bi, i: (0, 0)),
            pl.BlockSpec((GROUP_W, 2 * GROUP_W), lambda bi, i: (0, 0)),
            pl.BlockSpec((1, 2 * GROUP_W), lambda bi, i: (0, 0)),
            pl.BlockSpec((1, GROUP_W), lambda bi, i: (0, 0)),
        ],
        out_specs=pl.BlockSpec((1, tl, GROUP_W), lambda bi, i: (bi, tmap(bi, i), 0)),
        out_shape=jax.ShapeDtypeStruct((b, l, GROUP_W), BF16),
        scratch_shapes=[pltpu.VMEM((2, RG_SEGMENTS * _strided_pitch(tl // RG_SEGMENTS), V7X_LANES), F32)] * 2
        + [pltpu.VMEM((2, (tl // RG_SEGMENTS) * _strided_pitch(RG_SEGMENTS), V7X_LANES), F32)] * 2
        + [pltpu.VMEM((2, 1, V7X_LANES), F32)],
        compiler_params=_cparams("parallel", "arbitrary"),
        name="rglru_bwd" if reverse else "rglru_fwd",
    )(za, za, za, conv_w, conv_b, gate_w_bd, gate_b, softplus_neg_lam)


def _block_diag(blocks):
    h, di, do = blocks.shape
    eye = jnp.eye(h, dtype=blocks.dtype)
    return jnp.einsum('hde,hg->hdge', blocks, eye).reshape(h * di, h * do)


def _chunk_masks(reverse):
    r = np.arange(GROUP_W)
    t = np.arange(CHUNK)
    tri = (t[None, :] >= t[:, None]) if reverse else (t[None, :] <= t[:, None])
    cmask = np.tile(tri, (1, N_HEADS))
    bd = (r[:, None] // HEAD_W) == (r[None, :] // HEAD_W)
    kd = (r[:, None] // HEAD_W) == (np.arange(N_HEADS * B_DK)[None, :] // B_DK)
    i2 = np.tile(np.eye(CHUNK, dtype=bool), (1, N_HEADS))
    hm = np.zeros((8, GROUP_W), bool)
    hm[:N_HEADS] = np.arange(N_HEADS)[:, None] == (r[None, :] // HEAD_W)
    return dict(tri=jnp.asarray(tri, BF16), cmask=jnp.asarray(cmask, F32), bd=jnp.asarray(bd, BF16),
                kd=jnp.asarray(kd, BF16), kdf=jnp.asarray(kd, F32),
                i2=jnp.asarray(i2, F32), hm=jnp.asarray(hm, F32))


def _const_spec(arr):
    nd = arr.ndim
    return pl.BlockSpec(arr.shape, lambda *_: (0,) * nd)


def _gla_chunk(zb_ref, la, tri_ref, kd_ref, kdf_ref, bd_ref, causal, out_ref, st_ref, c):
    rows = slice(c * CHUNK, (c + 1) * CHUNK)
    q = zb_ref[0, rows, 0:128].astype(F32) * (B_DK ** -0.5)
    k = zb_ref[0, rows, 128:256].astype(F32)
    vb = zb_ref[0, rows, 256:512]
    la_c = la[rows]
    bcum = _dot_split(tri_ref[...], la_c)
    btot = jnp.sum(la_c, axis=0, keepdims=True)
    q_in = (q * jnp.exp(bcum)).astype(BF16)
    k_in = (k * jnp.exp(-bcum)).astype(BF16)
    k_st = (k * jnp.exp(btot - bcum)).astype(BF16)
    kexp = jnp.concatenate([k_in] * N_HEADS, axis=0) * kd_ref[...]
    att = jnp.where(causal, _dot_nt(q_in, kexp), 0.0)
    vbd = jnp.concatenate([vb] * N_HEADS, axis=0) * bd_ref[...]
    st = st_ref[...]
    out_ref[0, rows, :] = (_dot(att.astype(BF16), vbd) + _dot_nt(q_in, st.astype(BF16))).astype(out_ref.dtype)
    st_ref[...] = st * jnp.exp(btot) + _dot_tn(vb, k_st) * kdf_ref[...]


def _gla_kernel(zbf_ref, zsf_ref, zbb_ref, zsb_ref, gwf_ref, gbf_ref, gwb_ref, gbb_ref, trif_ref, trib_ref,
                kd_ref, kdf_ref, bd_ref, cmf_ref, cmb_ref, outf_ref, outb_ref, stf_ref, stb_ref, *, tl):
    @pl.when(pl.program_id(1) == 0)
    def _():
        stf_ref[...] = jnp.zeros_like(stf_ref)
        stb_ref[...] = jnp.zeros_like(stb_ref)

    log_alpha = lambda zs_ref, gw_ref, gb_ref: _log_sigmoid(
        _dot(zs_ref[0].astype(BF16), gw_ref[...]) + gb_ref[...]) * (1.0 / GLA_TAU)
    laf = log_alpha(zsf_ref, gwf_ref, gbf_ref)
    lab = log_alpha(zsb_ref, gwb_ref, gbb_ref)
    causal_f = cmf_ref[...] > 0.0
    causal_b = cmb_ref[...] > 0.0
    nch = tl // CHUNK
    for c in range(nch):
        _gla_chunk(zbf_ref, laf, trif_ref, kd_ref, kdf_ref, bd_ref, causal_f, outf_ref, stf_ref, c)
        _gla_chunk(zbb_ref, lab, trib_ref, kd_ref, kdf_ref, bd_ref, causal_b, outb_ref, stb_ref, nch - 1 - c)


def _gla(zb, zs, prep_f, prep_b, tl=1024):
    b, l, _ = zb.shape
    nt = l // tl
    mf, mb = _chunk_masks(False), _chunk_masks(True)
    consts = (*prep_f, *prep_b, mf['tri'], mb['tri'], mf['kd'], mf['kdf'], mf['bd'], mf['cmask'], mb['cmask'])
    fmap = lambda bi, i: (bi, i, 0)
    bmap = lambda bi, i: (bi, nt - 1 - i, 0)
    o = jax.ShapeDtypeStruct((b, l, GROUP_W), BF16)
    st = pltpu.VMEM((GROUP_W, N_HEADS * B_DK), F32)
    return pl.pallas_call(
        functools.partial(_gla_kernel, tl=tl),
        grid=(b, nt),
        in_specs=[pl.BlockSpec((1, tl, ZB_W), fmap), pl.BlockSpec((1, tl, ZS_W), fmap),
                  pl.BlockSpec((1, tl, ZB_W), bmap), pl.BlockSpec((1, tl, ZS_W), bmap)]
        + [_const_spec(c) for c in consts],
        out_specs=(pl.BlockSpec((1, tl, GROUP_W), fmap), pl.BlockSpec((1, tl, GROUP_W), bmap)),
        out_shape=(o, o),
        scratch_shapes=[st, st],
        compiler_params=_cparams("parallel", "arbitrary"),
        name="gla",
    )(zb, zs, zb, zs, *consts)


def _mlstm_chunk(zc_ref, ig_all, lf_all, tri_ref, i2_ref, bd_ref, causal, hm_ref, out_ref, s_ref, n_ref, m_ref, c):
    rows = slice(c * CHUNK, (c + 1) * CHUNK)
    qb = zc_ref[0, rows, 0:256]
    k = zc_ref[0, rows, 256:512].astype(F32) * (HEAD_W ** -0.5)
    kb = k.astype(BF16)
    vb = zc_ref[0, rows, 512:768]
    ig = ig_all[rows]
    lf = lf_all[rows]
    bcum = _dot_split(tri_ref[...], lf)
    blast = jnp.sum(lf, axis=0, keepdims=True)
    cc = jnp.sum((ig - bcum) * i2_ref[...], axis=0, keepdims=True)
    d2 = jnp.where(causal, bcum + cc, NEG_BIG)
    m_intra = jnp.full((CHUNK, GROUP_W), NEG_BIG, F32)
    for h in range(N_HEADS):
        hsel = hm_ref[h:h + 1, :] > 0.0
        mh = jnp.max(jnp.where(hsel, d2, NEG_BIG), axis=1, keepdims=True)
        m_intra = jnp.where(hsel, mh, m_intra)
    m_prev = m_ref[...]
    inter_log = bcum + m_prev
    m_t = jnp.maximum(inter_log, m_intra)
    w_intra = jnp.exp(d2 - m_t)
    w_inter = jnp.exp(inter_log - m_t)
    kexp = jnp.concatenate([kb] * N_HEADS, axis=0) * bd_ref[...]
    qk = (_dot_nt(qb, kexp) * w_intra).astype(BF16)
    vbd = jnp.concatenate([vb] * N_HEADS, axis=0) * bd_ref[...]
    s_prev = s_ref[...]
    n_prev = n_ref[...]
    sbd = jnp.concatenate([s_prev.astype(BF16)] * N_HEADS, axis=0) * bd_ref[...]
    nbd = jnp.concatenate([n_prev.astype(BF16)] * N_HEADS, axis=0) * bd_ref[...]
    num = _dot(qk, vbd) + w_inter * _dot(qb, sbd)
    den = _dot(qk, bd_ref[...]) + w_inter * _dot(qb, nbd)
    out_ref[0, rows, :] = (num / jnp.maximum(jnp.abs(den), jnp.exp(-m_t))).astype(out_ref.dtype)
    g_end = blast - bcum + ig
    g_max = jnp.max(g_end, axis=0, keepdims=True)
    kw = (k * jnp.exp(g_end - g_max)).astype(BF16)
    m_new = jnp.maximum(blast + m_prev, g_max)
    a = jnp.exp(blast + m_prev - m_new)
    cf = jnp.exp(g_max - m_new)
    uc_full = _dot_tn(kw, vb)
    un_full = _dot_tn(kw, jnp.ones((CHUNK, GROUP_W), BF16))
    uc = jnp.zeros((HEAD_W, GROUP_W), F32)
    un = jnp.zeros((HEAD_W, GROUP_W), F32)
    for h in range(N_HEADS):
        hsel = hm_ref[h:h + 1, :] > 0.0
        blk = slice(h * HEAD_W, (h + 1) * HEAD_W)
        uc = jnp.where(hsel, uc_full[blk], uc)
        un = jnp.where(hsel, un_full[blk], un)
    s_ref[...] = a * s_prev + cf * uc
    n_ref[...] = a * n_prev + cf * un
    m_ref[...] = m_new


def _mlstm_kernel(zcf_ref, zsf_ref, zcb_ref, zsb_ref, ef_ref, ebf_ref, eb_ref, ebb_ref, trif_ref, trib_ref,
                  i2_ref, bd_ref, cmf_ref, cmb_ref, hm_ref, outf_ref, outb_ref,
                  sf_ref, nf_ref, mf_ref, sb_ref, nb_ref, mb_ref, *, tl):
    @pl.when(pl.program_id(1) == 0)
    def _():
        for s_ref, n_ref, m_ref in ((sf_ref, nf_ref, mf_ref), (sb_ref, nb_ref, mb_ref)):
            s_ref[...] = jnp.zeros_like(s_ref)
            n_ref[...] = jnp.zeros_like(n_ref)
            m_ref[...] = jnp.full_like(m_ref, NEG_BIG)

    def gates(zs_ref, e_ref, eb_ref):
        ge = _split_dot(zs_ref[0], e_ref[...]) + eb_ref[...]
        return ge[:, :GROUP_W], _log_sigmoid(ge[:, GROUP_W:])

    igf, lff = gates(zsf_ref, ef_ref, ebf_ref)
    igb, lfb = gates(zsb_ref, eb_ref, ebb_ref)
    causal_f = cmf_ref[...] > 0.0
    causal_b = cmb_ref[...] > 0.0
    nch = tl // CHUNK
    for c in range(nch):
        _mlstm_chunk(zcf_ref, igf, lff, trif_ref, i2_ref, bd_ref, causal_f, hm_ref, outf_ref,
                     sf_ref, nf_ref, mf_ref, c)
        _mlstm_chunk(zcb_ref, igb, lfb, trib_ref, i2_ref, bd_ref, causal_b, hm_ref, outb_ref,
                     sb_ref, nb_ref, mb_ref, nch - 1 - c)


def _mlstm(zc, zs, prep_f, prep_b, tl=1024):
    b, l, _ = zc.shape
    nt = l // tl
    mf, mb = _chunk_masks(False), _chunk_masks(True)
    consts = (*prep_f, *prep_b, mf['tri'], mb['tri'], mf['i2'], mf['bd'], mf['cmask'], mb['cmask'], mf['hm'])
    fmap = lambda bi, i: (bi, i, 0)
    bmap = lambda bi, i: (bi, nt - 1 - i, 0)
    o = jax.ShapeDtypeStruct((b, l, GROUP_W), BF16)
    state = [pltpu.VMEM((HEAD_W, GROUP_W), F32), pltpu.VMEM((HEAD_W, GROUP_W), F32), pltpu.VMEM((1, GROUP_W), F32)]
    return pl.pallas_call(
        functools.partial(_mlstm_kernel, tl=tl),
        grid=(b, nt),
        in_specs=[pl.BlockSpec((1, tl, ZC_W), fmap), pl.BlockSpec((1, tl, ZS_W), fmap),
                  pl.BlockSpec((1, tl, ZC_W), bmap), pl.BlockSpec((1, tl, ZS_W), bmap)]
        + [_const_spec(c) for c in consts],
        out_specs=(pl.BlockSpec((1, tl, GROUP_W), fmap), pl.BlockSpec((1, tl, GROUP_W), bmap)),
        out_shape=(o, o),
        scratch_shapes=state + state,
        compiler_params=_cparams("parallel", "arbitrary"),
        name="mlstm",
    )(zc, zs, zc, zs, *consts)


def _mixer_prep(P, l):
    out = {}
    gw, gb, lam = P['a_gate_w'][l], P['a_gate_b'][l], P['a_lambda'][l]
    for d in range(2):
        w = jnp.concatenate([_block_diag(gw[d, 0]), _block_diag(gw[d, 1])], axis=1).astype(BF16)
        bb = jnp.concatenate([gb[d, 0].reshape(1, -1), gb[d, 1].reshape(1, -1)], axis=1).astype(F32)
        nl = -lam[d].astype(F32)
        sp = (jnp.maximum(nl, 0.0) + jnp.log(1.0 + jnp.exp(-jnp.abs(nl)))).reshape(1, -1)
        out['a', d] = (w, bb, sp)
        bw = jnp.zeros((ZS_W, N_HEADS * B_DK), F32).at[B_RANK * d:B_RANK * (d + 1)].set(P['b_gate_w'][l][d])
        out['b', d] = (bw.astype(BF16), P['b_gate_b'][l][d].reshape(1, -1).astype(F32))
        e = np.zeros((ZS_W, 2 * GROUP_W), np.float32)
        for h in range(N_HEADS):
            e[32 + 8 * d + h, h * HEAD_W:(h + 1) * HEAD_W] = 1.0
            e[36 + 8 * d + h, GROUP_W + h * HEAD_W:GROUP_W + (h + 1) * HEAD_W] = 1.0
        cb = P['c_gate_b'][l].astype(F32)
        eb = jnp.concatenate([jnp.repeat(cb[2 * d], HEAD_W), jnp.repeat(cb[2 * d + 1], HEAD_W)]).reshape(1, -1)
        out['c', d] = (jnp.asarray(e, BF16), eb)
    return out


G_ROWS = 8
G_ROWS_BF16 = 16


def _slab(k):
    return slice(k * V7X_LANES, (k + 1) * V7X_LANES)


def _scatter_rows(scr, base, val, rows, pitch, blk0=0):
    for s in range(val.shape[1] // V7X_LANES):
        for b in range(val.shape[0] // rows):
            r0 = (blk0 + b) * pitch
            scr[base + s, r0:r0 + rows, :] = val[b * rows:(b + 1) * rows, _slab(s)]


def _gather_n2_major(scr, outs, n2, pitch):
    nsl = GROUP_W // V7X_LANES

    def body(j, carry):
        for a, (ref, lead) in enumerate(outs):
            for s in range(nsl):
                ref[lead, j, :, _slab(s)] = scr[a * nsl + s, pl.ds(j, G_ROWS, stride=pitch), :]
        return carry

    lax.fori_loop(0, n2, body, 0, unroll=8)


def _conv3_kernel(prev_ref, cur_ref, next_ref, w_ref, b_ref, v_ref, x1_ref, x2_ref, scr, *, n2, pitch):
    i = pl.program_id(1)
    u = _halo_conv(prev_ref[0], cur_ref[0], next_ref[0], w_ref[...], b_ref[...], -1, i == 0,
                   i == pl.num_programs(1) - 1)
    nsl = GROUP_W // V7X_LANES
    for a in range(3):
        _scatter_rows(scr, a * nsl, u[:, a * GROUP_W:(a + 1) * GROUP_W], n2, pitch)
    _gather_n2_major(scr, ((v_ref, 0), (x1_ref, 0), (x2_ref, 0)), n2, pitch)


def _conv3_split(zd, w, b, n1, n2):
    bsz, l, _ = zd.shape
    tl = G_ROWS * n2
    pitch = _strided_pitch(n2)
    nt, rh, nbh = l // tl, tl // HALO, l // HALO
    o = jax.ShapeDtypeStruct((bsz, n2, n1 // 2, GROUP_W), F32)
    ospec = pl.BlockSpec((1, n2, G_ROWS, GROUP_W), lambda bi, i: (bi, 0, i, 0))
    return pl.pallas_call(
        functools.partial(_conv3_kernel, n2=n2, pitch=pitch),
        grid=(bsz, nt),
        in_specs=[pl.BlockSpec((1, HALO, ZD_W), lambda bi, i: (bi, jnp.maximum(i * rh - 1, 0), 0)),
                  pl.BlockSpec((1, tl, ZD_W), lambda bi, i: (bi, i, 0)),
                  pl.BlockSpec((1, HALO, ZD_W), lambda bi, i: (bi, jnp.minimum((i + 1) * rh, nbh - 1), 0)),
                  pl.BlockSpec((3, ZD_W), lambda bi, i: (0, 0)),
                  pl.BlockSpec((1, ZD_W), lambda bi, i: (0, 0))],
        out_specs=(ospec, ospec, ospec),
        out_shape=(o, o, o),
        scratch_shapes=[pltpu.VMEM((3 * GROUP_W // V7X_LANES, G_ROWS * pitch, V7X_LANES), F32)],
        compiler_params=_cparams("parallel", "parallel"),
        name="hyena_conv3",
    )(zd, zd, zd, w, b.reshape(1, ZD_W))


def _dot_hp(a, b):
    ah, bh = a.astype(BF16), b.astype(BF16)
    al = (a - ah.astype(F32)).astype(BF16)
    bl = (b - bh.astype(F32)).astype(BF16)
    return _dot(ah, bh) + _dot(al, bh) + _dot(ah, bl)


FILTER_SUB = 512


def _filter_mlp_kernel(emb_ref, w1_ref, b1_ref, f0_ref, w2_ref, b2_ref, f1_ref, w3_ref, dl_ref,
                       fwd_ref, sec_ref, s_ref, scr, *, n2, pitch):
    i = pl.program_id(0)
    tl = emb_ref.shape[0]
    nsl = GROUP_W // V7X_LANES

    @pl.when(i == 0)
    def _():
        s_ref[...] = jnp.zeros_like(s_ref)

    for sub in range(tl // FILTER_SUB):
        emb = emb_ref[sub * FILTER_SUB:(sub + 1) * FILTER_SUB, :]
        h = jnp.sin(f0_ref[...] * (_dot_hp(emb, w1_ref[...]) + b1_ref[...]))
        h = jnp.sin(f1_ref[...] * (_dot_hp(h, w2_ref[...]) + b2_ref[...]))
        h = _dot_hp(h, w3_ref[...])
        dl = dl_ref[...]
        hf = h[:, :2 * GROUP_W] * jnp.exp(-emb[:, 0:1] * dl)
        hb = h[:, 2 * GROUP_W:] * jnp.exp(-emb[:, HY_EMB:HY_EMB + 1] * dl)
        if sub == 0:
            row = _iota(hb.shape, 0) + i * tl
            hb = jnp.where(row == 0, 0.0, hb)
        s_ref[...] += jnp.sum(jnp.abs(hf) + jnp.abs(hb), axis=0, keepdims=True)
        blk0 = sub * FILTER_SUB // n2
        for o in range(2):
            _scatter_rows(scr, o * nsl, hf[:, o * GROUP_W:(o + 1) * GROUP_W], n2, pitch, blk0)
            _scatter_rows(scr, (2 + o) * nsl, hb[:, o * GROUP_W:(o + 1) * GROUP_W], n2, pitch, blk0)
    _gather_n2_major(scr, ((fwd_ref, 0), (fwd_ref, 1), (sec_ref, 0), (sec_ref, 1)), n2, pitch)


def _hyena_filter(l, w1, b1, freq, w2, b2, w3, n1, n2):
    tl = G_ROWS * n2
    bands = (HY_EMB - 1) // 2
    f = jnp.linspace(1e-4, bands - 1, bands, dtype=F32)

    def embed(pos):
        t = pos / max(l - 1, 1)
        ang = (2.0 * math.pi / l) * pos[:, None] * f[None, :]
        return jnp.concatenate([t[:, None], jnp.cos(ang), -jnp.sin(ang)], axis=-1)

    pos = jnp.arange(l, dtype=F32)
    emb = jnp.concatenate([embed(pos), embed(l - pos), jnp.zeros((l, V7X_LANES - 2 * HY_EMB), F32)], axis=-1)
    deltas = jnp.abs(jnp.linspace(math.log(HY_TARGET) / HY_SLOW, math.log(HY_TARGET) / HY_FAST, GROUP_W, dtype=F32))
    z = lambda r, c: jnp.zeros((r, c), F32)
    w1f, w2f = w1.astype(F32), w2.astype(F32)
    w3r = w3.astype(F32).reshape(HY_HID, 2, 2, GROUP_W)
    w3d = lambda d: w3r[:, :, d].reshape(HY_HID, 2 * GROUP_W)
    w1p = z(V7X_LANES, V7X_LANES).at[:HY_EMB, :HY_HID].set(w1f).at[HY_EMB:2 * HY_EMB, HY_HID:].set(w1f)
    w2p = z(V7X_LANES, V7X_LANES).at[:HY_HID, :HY_HID].set(w2f).at[HY_HID:, HY_HID:].set(w2f)
    w3p = z(V7X_LANES, 4 * GROUP_W).at[:HY_HID, :2 * GROUP_W].set(w3d(0)).at[HY_HID:, 2 * GROUP_W:].set(w3d(1))
    vec = lambda x: jnp.tile(x.astype(F32), 2).reshape(1, V7X_LANES)
    consts = (w1p, vec(b1), vec(freq[0]), w2p, vec(b2), vec(freq[1]), w3p, jnp.tile(deltas, 2).reshape(1, -1))
    pitch = _strided_pitch(n2)
    half = jax.ShapeDtypeStruct((2, n2, n1 // 2, GROUP_W), F32)
    hspec = pl.BlockSpec((2, n2, G_ROWS, GROUP_W), lambda i: (0, 0, i, 0))
    return pl.pallas_call(
        functools.partial(_filter_mlp_kernel, n2=n2, pitch=pitch),
        grid=(l // tl,),
        in_specs=[pl.BlockSpec((tl, V7X_LANES), lambda i: (i, 0))] + [_const_spec(c) for c in consts],
        out_specs=(hspec, hspec, pl.BlockSpec((1, 2 * GROUP_W), lambda i: (0, 0))),
        out_shape=(half, half, jax.ShapeDtypeStruct((1, 2 * GROUP_W), F32)),
        scratch_shapes=[pltpu.VMEM((4 * GROUP_W // V7X_LANES, G_ROWS * pitch, V7X_LANES), F32)],
        compiler_params=_cparams("arbitrary"),
        name="hyena_filter_mlp",
    )(emb, *consts)


def _fft_factors(l):
    return (64, 128) if l == 4096 else (2 * l // 256, 256)


@functools.lru_cache(maxsize=None)
def _fft_tables(n1, n2):
    n = n1 * n2
    k1 = np.arange(n1)[None, :, None]
    m2 = np.arange(n2)[:, None, None]

    def theta(n1_vals):
        idx = (k1 * (n1_vals[None, None, :] * n2 + m2)) % n
        return (2.0 * np.pi / n) * idx
    th = theta(np.arange(n1 // 2))
    fr, fi = np.cos(th), -np.sin(th)
    fwd = np.concatenate([np.concatenate([fr, -fi], -1), np.concatenate([fi, fr], -1)], -2)
    thf = theta(np.arange(n1))
    filt = np.concatenate([np.cos(thf), -np.sin(thf)], -2)
    cr = np.swapaxes(np.cos(th), 1, 2) / n
    ci = np.swapaxes(np.sin(th), 1, 2) / n
    inv = np.concatenate([np.concatenate([cr, -ci], -1), np.concatenate([ci, cr], -1)], -2)
    t2 = (2.0 * np.pi / n2) * ((np.arange(n2)[:, None] * np.arange(n2)[None, :]) % n2)
    gr, gi = np.cos(t2), -np.sin(t2)
    g = np.block([[gr, -gi], [gi, gr]])
    ginv = np.block([[gr, gi], [-gi, gr]])
    return tuple(jnp.asarray(a, BF16) for a in (fwd, filt, inv, g, ginv))


def _regroup_rows(scr, n_rows, pitch, store):
    def body(r, carry):
        for s in range(GROUP_W // V7X_LANES):
            store(r, s, scr[s, pl.ds(r, G_ROWS_BF16, stride=pitch), :])
        return carry

    lax.fori_loop(0, n_rows, body, 0, unroll=8)


def _fft_stage1_kernel(zr_ref, zi_ref, t_ref, a_ref, scr, *, n1, pitch):
    for j in range(G_ROWS_BF16):
        rhs = jnp.concatenate([zr_ref[0, j], zi_ref[0, j]], axis=0).astype(BF16)
        _scatter_rows(scr, 0, _dot(t_ref[j], rhs), 2 * n1, pitch, j)

    def store(r, s, rows):
        a_ref[0, r, :, _slab(s)] = rows.astype(BF16)

    _regroup_rows(scr, 2 * n1, pitch, store)


def _fft_stage1(zr, zi, table, n1, n2, pairs, idx_r, idx_i):
    nb = G_ROWS_BF16
    pitch = _strided_pitch(2 * n1)
    return pl.pallas_call(
        functools.partial(_fft_stage1_kernel, n1=n1, pitch=pitch),
        grid=(pairs, n2 // nb),
        in_specs=[pl.BlockSpec((1, nb, n1 // 2, GROUP_W), lambda p, j: (idx_r(p), j, 0, 0)),
                  pl.BlockSpec((1, nb, n1 // 2, GROUP_W), lambda p, j: (idx_i(p), j, 0, 0)),
                  pl.BlockSpec((nb, 2 * n1, n1), lambda p, j: (j, 0, 0))],
        out_specs=pl.BlockSpec((1, 2 * n1, nb, GROUP_W), lambda p, j: (p, 0, j, 0)),
        out_shape=jax.ShapeDtypeStruct((pairs, 2 * n1, n2, GROUP_W), BF16),
        scratch_shapes=[pltpu.VMEM((GROUP_W // V7X_LANES, nb * pitch, V7X_LANES), F32)],
        compiler_params=_cparams("parallel", "parallel"),
        name="fft_stage1",
    )(zr, zi, table)


def _a_specs(n1, n2, kb, index):
    def spec(part):
        def imap(i, p):
            pp, kk = index(i, p)
            return (pp, part * (n1 // kb) + kk, 0, 0)
        return pl.BlockSpec((1, kb, n2, GROUP_W), imap)
    return [spec(0), spec(1)]


def _fft_spec_kernel(ar_ref, ai_ref, g_ref, s_ref, h_ref, *, kb, n2):
    for k in range(kb):
        rhs = jnp.concatenate([ar_ref[0, k], ai_ref[0, k]], axis=0)
        x = _dot(g_ref[...], rhs) * s_ref[0]
        h_ref[0, k, 0] = x[:n2]
        h_ref[0, k, 1] = x[n2:]


def _fft_filter_spectrum(a, g, inv_s, n1, n2, kb=8):
    orders = a.shape[0]
    return pl.pallas_call(
        functools.partial(_fft_spec_kernel, kb=kb, n2=n2),
        grid=(orders, n1 // kb),
        in_specs=_a_specs(n1, n2, kb, lambda o, i: (o, i))
        + [_const_spec(g), pl.BlockSpec((1, 1, GROUP_W), lambda o, i: (o, 0, 0))],
        out_specs=pl.BlockSpec((1, kb, 2, n2, GROUP_W), lambda o, i: (o, i, 0, 0, 0)),
        out_shape=jax.ShapeDtypeStruct((orders, n1, 2, n2, GROUP_W), F32),
        compiler_params=_cparams("parallel", "parallel"),
        name="fft_filter_spectrum",
    )(a, a, g, inv_s)


def _fft_mid_kernel(ar_ref, ai_ref, g_ref, gi_ref, h_ref, b_ref, scr, *, n2, pitch):
    for k in range(G_ROWS_BF16):
        rhs = jnp.concatenate([ar_ref[0, k], ai_ref[0, k]], axis=0)
        x = _dot(g_ref[...], rhs)
        xr, xi = x[:n2], x[n2:]
        hr, hi = h_ref[0, k, 0], h_ref[0, k, 1]
        y = jnp.concatenate([xr * hr - xi * hi, xr * hi + xi * hr], axis=0).astype(BF16)
        _scatter_rows(scr, 0, _dot(gi_ref[...], y), 2 * n2, pitch, k)

    def store(r, s, rows):
        b_ref[0, r, :, _slab(s)] = rows.astype(BF16)

    _regroup_rows(scr, 2 * n2, pitch, store)


def _fft_mid(a, g, ginv, spec, order, n1, n2):
    pairs = a.shape[0]
    kb = G_ROWS_BF16
    pitch = _strided_pitch(2 * n2)
    return pl.pallas_call(
        functools.partial(_fft_mid_kernel, n2=n2, pitch=pitch),
        grid=(n1 // kb, pairs),
        in_specs=_a_specs(n1, n2, kb, lambda i, p: (p, i)) + [_const_spec(g), _const_spec(ginv),
                  pl.BlockSpec((1, kb, 2, n2, GROUP_W), lambda i, p: (order, i, 0, 0, 0))],
        out_specs=pl.BlockSpec((1, 2 * n2, kb, GROUP_W), lambda i, p: (p, 0, i, 0)),
        out_shape=jax.ShapeDtypeStruct((pairs, 2 * n2, n1, GROUP_W), BF16),
        scratch_shapes=[pltpu.VMEM((GROUP_W // V7X_LANES, kb * pitch, V7X_LANES), F32)],
        compiler_params=_cparams("parallel", "parallel"),
        name="fft_mid",
    )(a, a, g, ginv, spec)


def _fft_stage3_kernel(br_ref, bi_ref, t_ref, ur_ref, ui_ref, gr_ref, gi_ref, sk_ref, y_ref, *scr,
                       n1, pitch, time_major):
    h = n1 // 2
    sk = sk_ref[...]
    for j in range(G_ROWS_BF16):
        rhs = jnp.concatenate([br_ref[0, j], bi_ref[0, j]], axis=0)
        o = _dot(t_ref[j], rhs)
        yr = gr_ref[0, j] * (o[:h] + ur_ref[0, j] * sk)
        yi = gi_ref[0, j] * (o[h:] + ui_ref[0, j] * sk)
        if time_major:
            _scatter_rows(scr[0], 0, yr, h, pitch, j)
            _scatter_rows(scr[1], 0, yi, h, pitch, j)
        else:
            y_ref[0, 0, j] = yr
            y_ref[0, 1, j] = yi
    if time_major:
        for part in range(2):
            def store(r, s, rows, part=part):
                y_ref[0, part, r, :, _slab(s)] = rows.astype(y_ref.dtype)
            _regroup_rows(scr[part], h, pitch, store)


def _fft_stage3(b, table, u, gate, skip, n1, n2, time_major):
    pairs = b.shape[0]
    nb = G_ROWS_BF16
    h = n1 // 2
    pitch = _strided_pitch(h)
    bspec = lambda part: pl.BlockSpec((1, nb, n1, GROUP_W), lambda p, j: (p, part * (n2 // nb) + j, 0, 0))
    even = pl.BlockSpec((1, nb, h, GROUP_W), lambda p, j: (2 * p, j, 0, 0))
    odd = pl.BlockSpec((1, nb, h, GROUP_W), lambda p, j: (2 * p + 1, j, 0, 0))
    if time_major:
        out_spec = pl.BlockSpec((1, 2, h, nb, GROUP_W), lambda p, j: (p, 0, 0, j, 0))
        out_shape = jax.ShapeDtypeStruct((pairs, 2, h, n2, GROUP_W), BF16)
        scratch = [pltpu.VMEM((GROUP_W // V7X_LANES, nb * pitch, V7X_LANES), F32)] * 2
    else:
        out_spec = pl.BlockSpec((1, 2, nb, h, GROUP_W), lambda p, j: (p, 0, j, 0, 0))
        out_shape = jax.ShapeDtypeStruct((pairs, 2, n2, h, GROUP_W), F32)
        scratch = []
    y = pl.pallas_call(
        functools.partial(_fft_stage3_kernel, n1=n1, pitch=pitch, time_major=time_major),
        grid=(pairs, n2 // nb),
        in_specs=[bspec(0), bspec(1), pl.BlockSpec((nb, n1, 2 * n1), lambda p, j: (j, 0, 0)),
                  even, odd, even, odd, pl.BlockSpec((1, GROUP_W), lambda p, j: (0, 0))],
        out_specs=out_spec,
        out_shape=out_shape,
        scratch_shapes=scratch,
        compiler_params=_cparams("parallel", "parallel"),
        name="fft_stage3_out" if time_major else "fft_stage3",
    )(b, b, table, u, u, gate, gate, skip.reshape(1, GROUP_W).astype(F32))
    return y.reshape(2 * pairs, h * n2, GROUP_W) if time_major else y.reshape(2 * pairs, n2, h, GROUP_W)


def _hyena_spectrum(l, w1, b1, freq, w2, b2, w3):
    n1, n2 = _fft_factors(l)
    _, t_filt, _, g, _ = _fft_tables(n1, n2)
    first, second, sums = _hyena_filter(l, w1, b1, freq, w2, b2, w3, n1, n2)
    a = _fft_stage1(first, second, t_filt, n1, n2, 2, lambda p: p, lambda p: p)
    inv_s = (1.0 / sums).reshape(2, 1, GROUP_W)
    return _fft_filter_spectrum(a, g, inv_s, n1, n2)


def _hyena_long_conv(u, gate, spec, order, skip, n1, n2, time_major):
    t_fwd, _, t_inv, g, ginv = _fft_tables(n1, n2)
    a = _fft_stage1(u, u, t_fwd, n1, n2, u.shape[0] // 2, lambda p: 2 * p, lambda p: 2 * p + 1)
    b = _fft_mid(a, g, ginv, spec, order, n1, n2)
    return _fft_stage3(b, t_inv, u, gate, skip, n1, n2, time_major)


def _mixer_hyena(zd, P, l, spec):
    n1, n2 = _fft_factors(zd.shape[1])
    v, x1, x2 = _conv3_split(zd, P['d_conv_w'][l].astype(F32), P['d_conv_b'][l].astype(F32), n1, n2)
    z = _hyena_long_conv(v, x1, spec, 0, P['d_skip'][l][0], n1, n2, False)
    return _hyena_long_conv(z, x2, spec, 1, P['d_skip'][l][1], n1, n2, True)


def _gelu_tanh(x):
    return 0.5 * x * (1.0 + jnp.tanh(math.sqrt(2.0 / math.pi) * (x + 0.044715 * (x * x * x))))


def _head_rms(h, bd):
    ss = _dot((h * h).astype(BF16), bd) * (1.0 / HEAD_W)
    return h * lax.rsqrt(ss + EPS)


def _out_proj_kernel(x_ref, af_ref, ab_ref, ag_ref, bf_ref, bb_ref, br_ref, cf_ref, cb_ref, co_ref, yd_ref,
                     w_ref, bg_ref, cg_ref, bd_ref, n2_ref, rwh_ref, rwl_ref, rb_ref,
                     xo_ref, hn_ref, lg_ref):
    bd = bd_ref[...]
    f = lambda ref: ref[...].astype(F32)
    ya = (f(af_ref) + f(ab_ref)) * _gelu_tanh(f(ag_ref))
    r = f(br_ref)
    yb = _head_rms(f(bf_ref) + f(bb_ref), bd) * bg_ref[...] * (r * _sigmoid(r))
    yc = _head_rms(f(cf_ref) + f(cb_ref), bd) * cg_ref[...] * _sigmoid(f(co_ref))
    mixed = jnp.concatenate([ya.astype(BF16), yb.astype(BF16), yc.astype(BF16), yd_ref[...]], axis=-1)
    x = x_ref[...] + _dot(mixed, w_ref[...])
    xo_ref[...] = x
    hn = x * lax.rsqrt(jnp.mean(x * x, axis=-1, keepdims=True) + EPS) * n2_ref[...]
    hn_ref[...] = hn.astype(BF16)
    hi = hn.astype(BF16)
    lo = (hn - hi.astype(F32)).astype(BF16)
    lg_ref[...] = _dot(hi, rwh_ref[...]) + _dot(lo, rwh_ref[...]) + _dot(hi, rwl_ref[...]) + rb_ref[...]


def _out_proj(x2d, af, ab, za, bf, bb, zb, cf, cb, zc, yd, w_out, b_norm_g, c_norm_g, norm2_g, rw, rb, tm=512):
    t = x2d.shape[0]
    bd = _chunk_masks(False)['bd']
    rwh = rw.astype(BF16)
    rwl = (rw - rwh.astype(F32)).astype(BF16)
    row = lambda w, j=0: pl.BlockSpec((tm, w), lambda i: (i, j))
    consts = (w_out, b_norm_g.reshape(1, -1).astype(F32), c_norm_g.reshape(1, -1).astype(F32), bd,
              norm2_g.reshape(1, -1).astype(F32), rwh, rwl, rb)
    return pl.pallas_call(
        _out_proj_kernel,
        grid=(t // tm,),
        in_specs=[row(D_MODEL), row(GROUP_W), row(GROUP_W), row(GROUP_W, 1), row(GROUP_W), row(GROUP_W),
                  row(GROUP_W, 2), row(GROUP_W), row(GROUP_W), row(GROUP_W, 3), row(GROUP_W)]
        + [_const_spec(c) for c in consts],
        out_specs=(row(D_MODEL), row(D_MODEL), row(V7X_LANES)),
        out_shape=(jax.ShapeDtypeStruct((t, D_MODEL), F32), jax.ShapeDtypeStruct((t, D_MODEL), BF16),
                   jax.ShapeDtypeStruct((t, V7X_LANES), F32)),
        compiler_params=_cparams("parallel"),
        name="out_proj",
    )(x2d, af, ab, za, bf, bb, zb, cf, cb, zc, yd, *consts)


def _route(logits):
    lane = _iota(logits.shape, 1).astype(F32)
    is_g = lane < MOE_GROUPS
    gl = jnp.where(is_g, logits, NEG_BIG)
    gmax = jnp.max(gl, axis=1, keepdims=True)
    gidx = jnp.min(jnp.where(gl == gmax, lane, 1e9), axis=1, keepdims=True)
    gprob = 1.0 / jnp.sum(jnp.where(is_g, jnp.exp(gl - gmax), 0.0), axis=1, keepdims=True)
    lo = MOE_GROUPS + MOE_PER_GROUP * gidx
    el = jnp.where((lane >= lo) & (lane < lo + MOE_PER_GROUP), logits, NEG_BIG)
    v1 = jnp.max(el, axis=1, keepdims=True)
    i1 = jnp.min(jnp.where(el == v1, lane, 1e9), axis=1, keepdims=True)
    el2 = jnp.where(lane == i1, NEG_BIG, el)
    v2 = jnp.max(el2, axis=1, keepdims=True)
    i2 = jnp.min(jnp.where(el2 == v2, lane, 1e9), axis=1, keepdims=True)
    e21 = jnp.exp(v2 - v1)
    p1 = 1.0 / (1.0 + e21)
    return jnp.where(lane == i1, p1 * gprob, 0.0) + jnp.where(lane == i2, e21 * p1 * gprob, 0.0), gidx


MOE_TB = 1024
MOE_RT = 64
MOE_MM = 4 * MOE_RT
MOE_SB = MOE_TB + MOE_GROUPS * MOE_RT


def _moe_kernel(hn_ref, lg_ref, tril_ref, wgu_ref, wd_ref, o_ref, xs_ref, ys_ref, gs_ref, pos_ref, meta_ref):
    grp = pl.program_id(1)
    tb, sb, rt = MOE_TB, MOE_SB, MOE_RT

    @pl.when(grp == 0)
    def _():
        gates, gidx = _route(lg_ref[...])
        lane = _iota((tb, V7X_LANES), 1).astype(F32)
        ohg = jnp.where(lane == gidx, 1.0, 0.0)
        rank = jnp.sum(_dot(tril_ref[...], ohg.astype(BF16)) * ohg, axis=1, keepdims=True)
        cnt = jnp.sum(ohg, axis=0, keepdims=True)
        pc = jnp.ceil(cnt * (1.0 / rt)) * rt
        lane1 = _iota((1, V7X_LANES), 1)
        off = jnp.zeros((1, V7X_LANES), F32)
        run = jnp.zeros((1, 1), F32)
        for g in range(MOE_GROUPS):
            off = jnp.where(lane1 == g, run, off)
            run = run + jnp.sum(jnp.where(lane1 == g, pc, 0.0), axis=1, keepdims=True)
        pos = jnp.sum(ohg * off, axis=1, keepdims=True) + rank
        pos_ref[...] = jnp.broadcast_to(pos, (tb, V7X_LANES))
        meta_ref[0:1, :] = off
        meta_ref[1:2, :] = pc
        hi = jnp.floor(pos * (1.0 / 64.0))
        lo = pos - 64.0 * hi
        pm = jnp.where(lane == 0.0, hi, jnp.where(lane == 1.0, lo, 0.0)).astype(BF16)
        sel = jnp.where(_iota((8, V7X_LANES), 0) == _iota((8, V7X_LANES), 1), 1.0, 0.0).astype(BF16)
        rows8 = _dot_nt(sel, pm)
        pos_row = 64.0 * rows8[0:1, :] + rows8[1:2, :]
        p = jnp.where(_iota((sb, tb), 0).astype(F32) == pos_row, 1.0, 0.0).astype(BF16)
        xs_ref[...] = _dot(p, hn_ref[...]).astype(BF16)
        gh = gates.astype(BF16)
        gl = (gates - gh.astype(F32)).astype(BF16)
        gsorted = _dot(p, jnp.concatenate([gh, gl], axis=1))
        gs_ref[...] = gsorted[:, :V7X_LANES] + gsorted[:, V7X_LANES:]
        ys_ref[...] = jnp.zeros_like(ys_ref)

    lane1 = _iota((1, V7X_LANES), 1)
    start = jnp.sum(jnp.where(lane1 == grp, meta_ref[0:1, :], 0.0)).astype(jnp.int32)
    rows = jnp.sum(jnp.where(lane1 == grp, meta_ref[1:2, :], 0.0)).astype(jnp.int32)

    def experts(r0, nrows):
        xt = xs_ref[pl.ds(r0, nrows), :]
        gsl = gs_ref[pl.ds(r0, nrows), :]
        lane_t = _iota((nrows, V7X_LANES), 1)
        acc = None
        for j in range(MOE_PER_GROUP):
            gu = _dot(xt, wgu_ref[j])
            g = gu[:, :D_EXPERT]
            a = (g * _sigmoid(g)) * gu[:, D_EXPERT:]
            y = _dot(a.astype(BF16), wd_ref[j])
            w = jnp.sum(jnp.where(lane_t == MOE_GROUPS + MOE_PER_GROUP * grp + j, gsl, 0.0), axis=1, keepdims=True)
            acc = w * y if acc is None else acc + w * y
        ys_ref[pl.ds(r0, nrows), :] = acc.astype(BF16)

    ntiles = rows // rt
    per_mm = MOE_MM // rt

    def full(i, carry):
        experts(pl.multiple_of(start + i * MOE_MM, rt), MOE_MM)
        return carry

    nfull = jnp.maximum(ntiles // per_mm - 1, 0)
    lax.fori_loop(0, nfull, full, 0)
    last = ntiles - nfull * per_mm
    for k in range(1, 2 * per_mm):
        @pl.when(last == k)
        def _(k=k):
            experts(pl.multiple_of(start + nfull * MOE_MM, rt), k * rt)

    @pl.when(grp == MOE_GROUPS - 1)
    def _():
        pt = jnp.where(_iota((tb, sb), 1).astype(F32) == pos_ref[:, 0:1], 1.0, 0.0).astype(BF16)
        o_ref[...] = _dot(pt, ys_ref[...]).astype(o_ref.dtype)


def _moe(hn, logits, w_gate, w_up, w_down):
    t = hn.shape[0]
    tm = MOE_TB
    tril = jnp.asarray(np.tril(np.ones((tm, tm), np.float32), -1), BF16)
    row = lambda w: pl.BlockSpec((tm, w), lambda i, g: (i, 0))
    wspec = lambda a, b: pl.BlockSpec((MOE_PER_GROUP, a, b), lambda i, g: (g, 0, 0))
    return pl.pallas_call(
        _moe_kernel,
        grid=(t // tm, MOE_GROUPS),
        in_specs=[row(D_MODEL), row(V7X_LANES), _const_spec(tril), wspec(D_MODEL, 2 * D_EXPERT),
                  wspec(D_EXPERT, D_MODEL)],
        out_specs=row(D_MODEL),
        out_shape=jax.ShapeDtypeStruct((t, D_MODEL), BF16),
        scratch_shapes=[pltpu.VMEM((MOE_SB, D_MODEL), BF16), pltpu.VMEM((MOE_SB, D_MODEL), BF16),
                        pltpu.VMEM((MOE_SB, V7X_LANES), F32), pltpu.VMEM((tm, V7X_LANES), F32),
                        pltpu.VMEM((8, V7X_LANES), F32)],
        compiler_params=_cparams("parallel", "arbitrary"),
        name="moe",
    )(hn, logits, tril, jnp.concatenate([w_gate, w_up], axis=-1), w_down)


def _layer(x2d, moe_prev, bsz, seq, l, P, W, spec):
    x2d, za, zb, zc, zd, zs = _in_proj(x2d, moe_prev, P['norm1_g'][l].astype(F32), W['w_in'][l])
    r3 = lambda a: a.reshape(bsz, seq, a.shape[-1])
    r2 = lambda a: a.reshape(bsz * seq, a.shape[-1])
    mp = W['mix'][l]
    cw, cb = P['a_conv_w'][l].astype(F32), P['a_conv_b'][l].reshape(1, -1).astype(F32)
    za3, zb3, zc3, zs3 = r3(za), r3(zb), r3(zc), r3(zs)
    af = _rglru(za3, cw, cb, *mp['a', 0], False)
    ab = _rglru(za3, cw, cb, *mp['a', 1], True)
    bf, bb = _gla(zb3, zs3, mp['b', 0], mp['b', 1])
    cf, cbk = _mlstm(zc3, zs3, mp['c', 0], mp['c', 1])
    yd = _mixer_hyena(r3(zd), P, l, spec)
    x_new, hn, logits = _out_proj(x2d, r2(af), r2(ab), za, r2(bf), r2(bb), zb, r2(cf), r2(cbk), zc, r2(yd),
                                  W['w_out'][l], P['b_norm_g'][l], P['c_norm_g'][l], P['norm2_g'][l],
                                  W['router_w'][l], W['router_b'][l])
    return x_new, _moe(hn, logits, W['w_gate'][l], W['w_up'][l], W['w_down'][l])


def _trunk(x, P, W):
    bsz, seq, _ = x.shape
    depth = P['w_in'].shape[0]
    x2d = x.reshape(bsz * seq, D_MODEL)
    moe_term = None
    for l in range(depth):
        spec = _hyena_spectrum(seq, P['d_ffn_w1'][l], P['d_ffn_b1'][l], P['d_sin_freq'][l], P['d_ffn_w2'][l],
                               P['d_ffn_b2'][l], P['d_ffn_w3'][l])
        x2d, moe_term = _layer(x2d, moe_term, bsz, seq, l, P, W, spec)
    return _final_norm(x2d, moe_term, P['final_norm_g']).reshape(bsz, seq, D_MODEL)


def _prep_weights(P):
    depth = P['w_in'].shape[0]
    W = {'w_in': [_permute_w_in(P['w_in'][l]) for l in range(depth)],
         'w_out': [P['w_out'][l].astype(BF16) for l in range(depth)],
         'mix': [_mixer_prep(P, l) for l in range(depth)],
         'w_gate': [P['moe_w_gate'][l].astype(BF16) for l in range(depth)],
         'w_up': [P['moe_w_up'][l].astype(BF16) for l in range(depth)],
         'w_down': [P['moe_w_down'][l].astype(BF16) for l in range(depth)],
         'router_w': [], 'router_b': []}
    for l in range(depth):
        rw = jnp.concatenate([P['moe_group_w'][l], P['moe_expert_w'][l]], axis=1).astype(F32)
        rb = jnp.concatenate([P['moe_group_b'][l], P['moe_expert_b'][l]]).astype(F32)
        padc = V7X_LANES - rw.shape[1]
        W['router_w'].append(jnp.pad(rw, ((0, 0), (0, padc))))
        W['router_b'].append(jnp.pad(rb, (0, padc)).reshape(1, V7X_LANES))
    return W


def kernel(x_prompt, x_sample, norm1_g, w_in, a_conv_w, a_conv_b, a_gate_w, a_gate_b, a_lambda, b_gate_w, b_gate_b, b_norm_g, c_gate_b, c_norm_g, d_conv_w, d_conv_b, d_ffn_w1, d_ffn_b1, d_sin_freq, d_ffn_w2, d_ffn_b2, d_ffn_w3, d_skip, w_out, norm2_g, moe_group_w, moe_group_b, moe_expert_w, moe_expert_b, moe_w_gate, moe_w_up, moe_w_down, final_norm_g):
    P = {'norm1_g': norm1_g, 'w_in': w_in, 'a_conv_w': a_conv_w, 'a_conv_b': a_conv_b,
         'a_gate_w': a_gate_w, 'a_gate_b': a_gate_b, 'a_lambda': a_lambda, 'b_gate_w': b_gate_w,
         'b_gate_b': b_gate_b, 'b_norm_g': b_norm_g, 'c_gate_b': c_gate_b, 'c_norm_g': c_norm_g,
         'd_conv_w': d_conv_w, 'd_conv_b': d_conv_b, 'd_ffn_w1': d_ffn_w1, 'd_ffn_b1': d_ffn_b1,
         'd_sin_freq': d_sin_freq, 'd_ffn_w2': d_ffn_w2, 'd_ffn_b2': d_ffn_b2, 'd_ffn_w3': d_ffn_w3,
         'd_skip': d_skip, 'w_out': w_out, 'norm2_g': norm2_g, 'moe_group_w': moe_group_w,
         'moe_group_b': moe_group_b, 'moe_expert_w': moe_expert_w, 'moe_expert_b': moe_expert_b,
         'moe_w_gate': moe_w_gate, 'moe_w_up': moe_w_up, 'moe_w_down': moe_w_down,
         'final_norm_g': final_norm_g}
    W = _prep_weights(P)
    return (_trunk(x_prompt, P, W), _trunk(x_sample, P, W))
```
